```python
import math
import jax, jax.numpy as jnp
from jax import lax
import numpy as np

D_MODEL = 1024
BATCH = 32
SEQ = 256
DEPTH = 4
DEC_BATCH = 4
DEC_SEQ = 1024
PAST_LEN = 512

GRID_W = 64
MIX_W = 512
HG_HEADS = 4
HG_DK = 128
HG_DV = MIX_W // HG_HEADS
HG_CHUNK = 16
ML_HEADS = 4
ML_DK = 128
ML_DV = MIX_W // ML_HEADS
ML_CHUNK = 64
MB_HEADS = 8
MB_HEADDIM = MIX_W // MB_HEADS
MB_DINNER = MIX_W
MB_GROUPS = 2
MB_DSTATE = 64
MB_CONV = 3
MB_XBC = MB_DINNER + 2 * MB_GROUPS * MB_DSTATE
MB_CHUNK = 64
D_FF = 2816
N_BRANCH = 3
N_MOD = 9
ALPHA = (2 * DEPTH) ** 0.25
BETA = (8 * DEPTH) ** -0.25
LN_EPS = 1e-5
RMS_EPS = 1e-6
IN_SIZES = (HG_HEADS * HG_DK, HG_HEADS * HG_DV, HG_HEADS * HG_DV, 2 * HG_HEADS * HG_DK,
            ML_HEADS * ML_DK, ML_HEADS * ML_DK, ML_HEADS * ML_DV, ML_HEADS * ML_DV, 2 * ML_HEADS, 2 * ML_HEADS,
            MB_DINNER, MB_XBC, 2 * MB_HEADS, N_BRANCH * D_MODEL)
D_IN = sum(IN_SIZES)

kernel_name = 'hybrid_diffusion_prefix_trunk'


def _split_points():
    pts, acc = [], 0
    for s in IN_SIZES[:-1]:
        acc += s
        pts.append(acc)
    return pts


def _causal(t):
    return jnp.tril(jnp.ones((t, t), dtype=bool))


def _layernorm(x, g, b):
    xf = x.astype(jnp.float32)
    mu = jnp.mean(xf, -1, keepdims=True)
    var = jnp.mean(jnp.square(xf - mu), -1, keepdims=True)
    return ((xf - mu) * lax.rsqrt(var + LN_EPS) * g + b).astype(x.dtype)


def _head_rms(x, g):
    xf = x.astype(jnp.float32)
    y = xf * lax.rsqrt(jnp.mean(xf * xf, -1, keepdims=True) + RMS_EPS)
    return y.reshape(x.shape[0], x.shape[1], -1) * g


def _modulate(x, shift, scale):
    return x * (1 + scale) + shift


def _swiglu(x, w_gu, w_down):
    a, b = jnp.split(jnp.einsum('bld,df->blf', x, w_gu), 2, axis=-1)
    return jnp.einsum('blf,fd->bld', jax.nn.silu(a) * b, w_down)


def _dwconv(x, w, b):
    out = lax.conv_general_dilated(x, w[:, None, :], window_strides=(1,), padding='SAME',
                                   dimension_numbers=('NWC', 'WIO', 'NWC'),
                                   feature_group_count=x.shape[-1])
    return out + b


def _state_scan(decay, contrib, s0):
    def step(s, dc):
        d, cn = dc
        return d * s + cn, s
    s_fin, s_in = lax.scan(step, s0, (jnp.moveaxis(decay, 1, 0), jnp.moveaxis(contrib, 1, 0)))
    return jnp.moveaxis(s_in, 0, 1), s_fin


def _bidir(scan_fn, args_f, args_b, state):
    y_f, s_f = scan_fn(*args_f, tuple(s[:, 0] for s in state))
    y_b, s_b = scan_fn(*(jnp.flip(a, 1) for a in args_b), tuple(s[:, 1] for s in state))
    return y_f + jnp.flip(y_b, 1), tuple(jnp.stack([a, b], axis=1) for a, b in zip(s_f, s_b))


def _hgrn2_scan(q, k, v, logf, state):
    (s0,) = state
    bsz, seq, h, dk = q.shape
    t = HG_CHUNK
    n = seq // t
    q, k, logf = [a.astype(jnp.float32).reshape(bsz, n, t, h, dk) for a in (q, k, logf)]
    v = v.astype(jnp.float32).reshape(bsz, n, t, h, HG_DV)
    b = jnp.cumsum(logf, axis=2)
    b_last = b[:, :, -1]
    e = b[:, :, :, None] - b[:, :, None, :]
    e = jnp.where(_causal(t)[:, :, None, None], e, -jnp.inf)
    att = jnp.einsum('bnthk,bnshk,bntshk->bntsh', q, k, jnp.exp(e))
    o = jnp.einsum('bntsh,bnshv->bnthv', att, v)
    contrib = jnp.einsum('bnshk,bnshv->bnhkv', k * jnp.exp(b_last[:, :, None] - b), v)
    s_in, s_fin = _state_scan(jnp.exp(b_last)[..., None], contrib, s0.astype(jnp.float32))
    o = o + jnp.einsum('bnthk,bnhkv->bnthv', q * jnp.exp(b), s_in)
    return o.reshape(bsz, seq, h, HG_DV), (s_fin,)


def _mlstm_scan(q, k, v, ig, logf, state):
    c0, n0, m0 = state
    bsz, seq, h, dk = q.shape
    t = ML_CHUNK
    n = seq // t
    q, k = [a.astype(jnp.float32).reshape(bsz, n, t, h, dk) for a in (q, k)]
    v = v.astype(jnp.float32).reshape(bsz, n, t, h, ML_DV)
    ig, logf = [a.astype(jnp.float32).reshape(bsz, n, t, h) for a in (ig, logf)]
    b = jnp.cumsum(logf, axis=2)
    b_last = b[:, :, -1]
    g = b_last[:, :, None] - b + ig
    g_max = jnp.max(g, axis=2)
    wg = jnp.exp(g - g_max[:, :, None])
    c_loc = jnp.einsum('bnsh,bnshk,bnshv->bnhkv', wg, k, v)
    n_loc = jnp.einsum('bnsh,bnshk->bnhk', wg, k)

    def step(carry, inp):
        cm, nm, mm = carry
        bl, gm, cl, nl = inp
        m_new = jnp.maximum(bl + mm, gm)
        a = jnp.exp(bl + mm - m_new)
        s = jnp.exp(gm - m_new)
        return (a[..., None, None] * cm + s[..., None, None] * cl, a[..., None] * nm + s[..., None] * nl, m_new), (cm, nm, mm)

    xs = tuple(jnp.moveaxis(a, 1, 0) for a in (b_last, g_max, c_loc, n_loc))
    init = (c0.astype(jnp.float32), n0.astype(jnp.float32), m0.astype(jnp.float32))
    (c_f, n_f, m_f), (c_in, n_in, m_in) = lax.scan(step, init, xs)
    c_in, n_in, m_in = [jnp.moveaxis(a, 0, 1) for a in (c_in, n_in, m_in)]
    bt = jnp.swapaxes(b, 2, 3)
    it = jnp.swapaxes(ig, 2, 3)
    dlog = jnp.where(_causal(t), bt[..., :, None] - bt[..., None, :] + it[..., None, :], -jnp.inf)
    m_inter = bt + m_in[..., None]
    m_t = jnp.maximum(m_inter, jnp.max(dlog, -1))
    s = jnp.exp(dlog - m_t[..., None]) * jnp.einsum('bnthk,bnshk->bnhts', q, k)
    w_inter = jnp.exp(m_inter - m_t)
    num = (jnp.einsum('bnhts,bnshv->bnthv', s, v)
           + jnp.swapaxes(w_inter, 2, 3)[..., None] * jnp.einsum('bnthk,bnhkv->bnthv', q, c_in))
    den = jnp.sum(s, -1) + w_inter * jnp.einsum('bnthk,bnhk->bnht', q, n_in)
    den = jnp.maximum(jnp.abs(den), jnp.exp(-m_t))
    hout = num / jnp.swapaxes(den, 2, 3)[..., None]
    return hout.reshape(bsz, seq, h, ML_DV), (c_f, n_f, m_f)


def _ssd_scan(x, dt, la, bm, cm, state):
    (h0,) = state
    bsz, seq, h, p = x.shape
    t = MB_CHUNK
    n = seq // t
    x = x.astype(jnp.float32).reshape(bsz, n, t, h, p)
    dt = dt.reshape(bsz, n, t, h)
    bm = bm.astype(jnp.float32).reshape(bsz, n, t, h, MB_DSTATE)
    cm = cm.astype(jnp.float32).reshape(bsz, n, t, h, MB_DSTATE)
    a = jnp.cumsum(la.reshape(bsz, n, t, h), axis=2)
    a_last = a[:, :, -1]
    at = jnp.swapaxes(a, 2, 3)
    seg = jnp.where(_causal(t), at[..., :, None] - at[..., None, :], -jnp.inf)
    xdt = x * dt[..., None]
    cb = jnp.einsum('bnthd,bnshd->bnhts', cm, bm)
    y = jnp.einsum('bnhts,bnshp->bnthp', jnp.exp(seg) * cb, xdt)
    contrib = jnp.einsum('bnshd,bnshp->bnhpd', bm * jnp.exp(a_last[:, :, None] - a)[..., None], xdt)
    h_in, h_fin = _state_scan(jnp.exp(a_last)[..., None, None], contrib, h0.astype(jnp.float32))
    y = y + jnp.einsum('bnthd,bnhpd->bnthp', cm * jnp.exp(a)[..., None], h_in)
    return y.reshape(bsz, seq, h, p), (h_fin,)


def _token_mixer(hin, p, states):
    bsz, seq = hin.shape[0], hin.shape[1]
    u = jnp.einsum('bld,de->ble', hin, p['w_in'])
    (hq, hi, hg, hf, mq, mk, mv, mo, mi, mf, mz, mxbc, mdt, bg) = jnp.split(u, _split_points(), axis=-1)
    hg_s, ml_c, ml_n, ml_m, mb_s = states

    lb = p['hg_lb']
    q = jax.nn.silu(hq).reshape(bsz, seq, HG_HEADS, HG_DK)
    iv = hi.reshape(bsz, seq, HG_HEADS, HG_DV)
    fpre = hf.astype(jnp.float32).reshape(bsz, seq, 2, HG_HEADS, HG_DK)
    logf = jnp.logaddexp(jnp.log(lb), jnp.log1p(-lb) + jax.nn.log_sigmoid(fpre))
    kk = (1.0 - lb) * jax.nn.sigmoid(-fpre)
    o_hg, hg_new = _bidir(_hgrn2_scan, (q, kk[:, :, 0], iv, logf[:, :, 0]),
                          (q, kk[:, :, 1], iv, logf[:, :, 1]), (hg_s,))
    y_hg = _head_rms(o_hg, p['hg_norm_g']) * jax.nn.silu(hg.astype(jnp.float32))

    qm = mq.reshape(bsz, seq, ML_HEADS, ML_DK)
    km = mk.reshape(bsz, seq, ML_HEADS, ML_DK) * (ML_DK ** -0.5)
    vm = mv.reshape(bsz, seq, ML_HEADS, ML_DV)
    gb = p['ml_gate_b']
    ig = mi.astype(jnp.float32).reshape(bsz, seq, 2, ML_HEADS) + gb[0]
    lf = jax.nn.log_sigmoid(mf.astype(jnp.float32).reshape(bsz, seq, 2, ML_HEADS) + gb[1])
    h_ml, ml_new = _bidir(_mlstm_scan, (qm, km, vm, ig[:, :, 0], lf[:, :, 0]),
                          (qm, km, vm, ig[:, :, 1], lf[:, :, 1]), (ml_c, ml_n, ml_m))
    y_ml = _head_rms(h_ml, p['ml_norm_g']) * jax.nn.sigmoid(mo.astype(jnp.float32))

    xbc = jax.nn.silu(_dwconv(mxbc, p['mb_conv_w'], p['mb_conv_b']))
    mx, mb_b, mb_c = jnp.split(xbc, [MB_DINNER, MB_DINNER + MB_GROUPS * MB_DSTATE], axis=-1)
    mx = mx.reshape(bsz, seq, MB_HEADS, MB_HEADDIM)
    rep = MB_HEADS // MB_GROUPS
    mb_b = jnp.repeat(mb_b.reshape(bsz, seq, MB_GROUPS, MB_DSTATE), rep, axis=2)
    mb_c = jnp.repeat(mb_c.reshape(bsz, seq, MB_GROUPS, MB_DSTATE), rep, axis=2)
    dt = jax.nn.softplus(mdt.astype(jnp.float32).reshape(bsz, seq, 2, MB_HEADS) + p['mb_dt_bias'])
    la = dt * -jnp.exp(p['mb_a_log'].astype(jnp.float32))
    y_mb, mb_new = _bidir(_ssd_scan, (mx, dt[:, :, 0], la[:, :, 0], mb_b, mb_c),
                          (mx, dt[:, :, 1], la[:, :, 1], mb_b, mb_c), (mb_s,))
    y_mb = y_mb + p['mb_d'][:, None] * mx
    y_mb = _head_rms((y_mb.reshape(bsz, seq, MB_DINNER) * jax.nn.silu(mz.astype(jnp.float32)))[:, :, None], p['mb_norm_g'])

    br = jnp.stack([y_hg, y_ml, y_mb], axis=2).astype(hin.dtype)
    proj = jnp.einsum('blcr,crd->blcd', br, p['w_branch'])
    gates = jax.nn.sigmoid(bg.reshape(bsz, seq, N_BRANCH, D_MODEL))
    y = jnp.einsum('blcd,de->ble', gates * proj, p['w_out'])
    return y, (hg_new[0], ml_new[0], ml_new[1], ml_new[2], mb_new[0])


def _layer(x, mod, p, states):
    sh1, sc1, g1, sh2, sc2, g2, sh3, sc3, g3 = jnp.split(mod, N_MOD, axis=-1)
    x = _layernorm(ALPHA * x + 0.5 * g1 * _swiglu(_modulate(x, sh1, sc1), p['ffn_w_gu'][0], p['ffn_w_down'][0]),
                   p['ln_g'][0], p['ln_b'][0])
    y, new_states = _token_mixer(_modulate(x, sh2, sc2), p, states)
    x = _layernorm(ALPHA * x + g2 * y, p['ln_g'][1], p['ln_b'][1])
    x = _layernorm(ALPHA * x + 0.5 * g3 * _swiglu(_modulate(x, sh3, sc3), p['ffn_w_gu'][1], p['ffn_w_down'][1]),
                   p['ln_g'][2], p['ln_b'][2])
    return x, new_states


def _grid_pos_embed(n_tok):
    rows = n_tok // GRID_W
    r, cidx = jnp.meshgrid(jnp.arange(rows, dtype=jnp.float32), jnp.arange(GRID_W, dtype=jnp.float32), indexing='ij')
    quarter = D_MODEL // 4
    freq = jnp.exp(-math.log(10000.0) * jnp.arange(quarter, dtype=jnp.float32) / quarter)
    ar = r.reshape(-1, 1) * freq
    ac = cidx.reshape(-1, 1) * freq
    return jnp.concatenate([jnp.sin(ar), jnp.cos(ar), jnp.sin(ac), jnp.cos(ac)], axis=-1)


def _zero_states(bsz):
    z = lambda *s: jnp.zeros((bsz, 2) + s, jnp.float32)
    return (z(HG_HEADS, HG_DK, HG_DV), z(ML_HEADS, ML_DK, ML_DV), z(ML_HEADS, ML_DK), z(ML_HEADS),
            z(MB_HEADS, MB_HEADDIM, MB_DSTATE))


def setup_inputs(seed: int = 0) -> dict:
    key = jax.random.key(seed)
    ks = iter(jax.random.split(key, 40))
    nrm = lambda shape, s=1.0: s * jax.random.normal(next(ks), shape, jnp.float32)
    unif = lambda shape, lo, hi: jax.random.uniform(next(ks), shape, jnp.float32, lo, hi)
    d = D_MODEL
    st = (DEC_BATCH, DEPTH, 2)
    dt0 = jnp.exp(unif((DEPTH, 2, MB_HEADS), math.log(1e-3), math.log(1e-1)))
    return {
        'x_prompt': nrm((BATCH, SEQ, d)),
        'x_sample': nrm((DEC_BATCH, DEC_SEQ, d)),
        'state_hgrn': nrm(st + (HG_HEADS, HG_DK, HG_DV), 0.5),
        'state_mlstm_C': nrm(st + (ML_HEADS, ML_DK, ML_DV), 0.3),
        'state_mlstm_n': nrm(st + (ML_HEADS, ML_DK), 0.3),
        'state_mlstm_m': nrm(st + (ML_HEADS,), 1.0),
        'state_ssd': nrm(st + (MB_HEADS, MB_HEADDIM, MB_DSTATE), 0.5),
        'c': nrm((DEC_BATCH, d)),
        'c_ctx': nrm((d,)),
        'w_mod': nrm((DEPTH, d, N_MOD * d), 0.5 * d ** -0.5),
        'b_mod': nrm((DEPTH, N_MOD * d), 0.02),
        'ln_g': 1.0 + nrm((DEPTH, 3, d), 0.02),
        'ln_b': nrm((DEPTH, 3, d), 0.02),
        'ffn_w_gu': nrm((DEPTH, 2, d, 2 * D_FF), d ** -0.5),
        'ffn_w_down': nrm((DEPTH, 2, D_FF, d), BETA * D_FF ** -0.5),
        'w_in': nrm((DEPTH, d, D_IN), d ** -0.5),
        'hg_lb': nrm((DEPTH, 2, HG_HEADS * HG_DK)),
        'hg_norm_g': 1.0 + nrm((DEPTH, HG_HEADS * HG_DV), 0.02),
        'ml_gate_b': jnp.stack([nrm((DEPTH, 2, ML_HEADS), 0.1), unif((DEPTH, 2, ML_HEADS), 3.0, 6.0)], axis=1),
        'ml_norm_g': 1.0 + nrm((DEPTH, ML_HEADS * ML_DV), 0.02),
        'mb_conv_w': nrm((DEPTH, MB_CONV, MB_XBC), MB_CONV ** -0.5),
        'mb_conv_b': nrm((DEPTH, MB_XBC), 0.02),
        'mb_dt_bias': dt0 + jnp.log(-jnp.expm1(-dt0)),
        'mb_a_log': jnp.log(unif((DEPTH, 2, MB_HEADS), 1.0, 16.0)),
        'mb_d': 1.0 + nrm((DEPTH, MB_HEADS), 0.02),
        'mb_norm_g': 1.0 + nrm((DEPTH, MB_DINNER), 0.02),
        'w_branch': nrm((DEPTH, N_BRANCH, MIX_W, d), MIX_W ** -0.5),
        'w_out': nrm((DEPTH, d, d), BETA * d ** -0.5),
    }


def reference(x_prompt, x_sample, state_hgrn, state_mlstm_C, state_mlstm_n, state_mlstm_m, state_ssd, c, c_ctx,
              w_mod, b_mod, ln_g, ln_b, ffn_w_gu, ffn_w_down, w_in, hg_lb, hg_norm_g, ml_gate_b, ml_norm_g,
              mb_conv_w, mb_conv_b, mb_dt_bias, mb_a_log, mb_d, mb_norm_g, w_branch, w_out):
    lbs = jnp.cumsum(jax.nn.softmax(hg_lb.astype(jnp.float32), axis=0), axis=0)
    lbs = (lbs - lbs[0]).reshape(DEPTH, 2, HG_HEADS, HG_DK)
    params = [{'w_in': w_in[l], 'hg_lb': lbs[l], 'hg_norm_g': hg_norm_g[l], 'ml_gate_b': ml_gate_b[l],
               'ml_norm_g': ml_norm_g[l], 'mb_conv_w': mb_conv_w[l], 'mb_conv_b': mb_conv_b[l],
               'mb_dt_bias': mb_dt_bias[l], 'mb_a_log': mb_a_log[l], 'mb_d': mb_d[l], 'mb_norm_g': mb_norm_g[l],
               'w_branch': w_branch[l], 'w_out': w_out[l], 'ln_g': ln_g[l], 'ln_b': ln_b[l],
               'ffn_w_gu': ffn_w_gu[l], 'ffn_w_down': ffn_w_down[l]} for l in range(DEPTH)]
    mod_ctx = jnp.einsum('d,lde->le', jax.nn.silu(c_ctx), w_mod) + b_mod
    mod_lat = jnp.einsum('bd,lde->lbe', jax.nn.silu(c), w_mod) + b_mod[:, None]

    xc = x_prompt
    collected = ([], [], [], [], [])
    for l in range(DEPTH):
        xc, st = _layer(xc, mod_ctx[l][None, None], params[l], _zero_states(x_prompt.shape[0]))
        for lst, s in zip(collected, st):
            lst.append(s)
    y_prompt = xc
    new_state_hgrn = jnp.stack(collected[0], axis=1)
    new_state_mlstm_C = jnp.stack(collected[1], axis=1)
    new_state_mlstm_n = jnp.stack(collected[2], axis=1)
    new_state_mlstm_m = jnp.stack(collected[3], axis=1)
    new_state_ssd = jnp.stack(collected[4], axis=1)

    xs = x_sample + _grid_pos_embed(x_sample.shape[1]).astype(x_sample.dtype)[None]
    for l in range(DEPTH):
        st = (state_hgrn[:, l], state_mlstm_C[:, l], state_mlstm_n[:, l], state_mlstm_m[:, l], state_ssd[:, l])
        xs, _ = _layer(xs, mod_lat[l][:, None], params[l], st)
    y_sample = xs
    return (y_prompt, y_sample, new_state_hgrn, new_state_mlstm_C, new_state_mlstm_n, new_state_mlstm_m, new_state_ssd)
```

```python
import functools
import math

import jax
import jax.numpy as jnp
from jax import lax
from jax.experimental import pallas as pl
from jax.experimental.pallas import tpu as pltpu

F32 = jnp.float32
BF16 = jnp.bfloat16

D_MODEL = 1024
DEPTH = 4
GRID_W = 64
MIX_W = 512
HG_HEADS = 4
HG_DK = 128
ML_HEADS = 4
ML_DK = 128
MB_HEADS = 8
MB_HEADDIM = 64
MB_GROUPS = 2
MB_DSTATE = 64
MB_XBC = MIX_W + 2 * MB_GROUPS * MB_DSTATE
D_FF = 2816
N_MOD = 9
ALPHA = (2 * DEPTH) ** 0.25
LN_EPS = 1e-5
RMS_EPS = 1e-6

LANES = 128
CH = 256
ROW_TILE = 1024
FF_TILE = 256
VMEM_LIMIT = 56 * 1024 * 1024

UB_W = 7168
UF_W = 1920
GATE_BLK = 14


def _dot(a, b):
    return jnp.dot(a, b, preferred_element_type=F32)


def _dot_nt(a, b):
    return lax.dot_general(a, b, (((1,), (1,)), ((), ())), preferred_element_type=F32)


def _dot_tn(a, b):
    return lax.dot_general(a, b, (((0,), (0,)), ((), ())), preferred_element_type=F32)


def _silu(x):
    return x * jax.nn.sigmoid(x)


def _log_sigmoid(x):
    return jnp.minimum(x, 0.0) - jnp.log1p(jnp.exp(-jnp.abs(x)))


def _softplus(x):
    return jnp.maximum(x, 0.0) + jnp.log1p(jnp.exp(-jnp.abs(x)))


def _ln(z, g, b):
    mu = jnp.mean(z, -1, keepdims=True)
    d = z - mu
    var = jnp.mean(d * d, -1, keepdims=True)
    return d * lax.rsqrt(var + LN_EPS) * g + b


def _split3(x):
    h = x.astype(BF16)
    r = x - h.astype(F32)
    m = r.astype(BF16)
    l = (r - m.astype(F32)).astype(BF16)
    return h, m, l


def _cumsum_rows(tril, x):
    h, m, l = _split3(x)
    return _dot(tril, h) + _dot(tril, m) + _dot(tril, l)


def _cumsum_lanes(x, triu):
    h, m, l = _split3(x)
    return _dot(h, triu) + _dot(m, triu) + _dot(l, triu)


def _tri(c, lower):
    r = lax.broadcasted_iota(jnp.int32, (c, c), 0)
    s = lax.broadcasted_iota(jnp.int32, (c, c), 1)
    return jnp.where((s <= r) if lower else (r <= s), 1.0, 0.0).astype(BF16)


def _col(x, idx):
    lane = lax.broadcasted_iota(jnp.int32, x.shape, 1)
    return jnp.sum(jnp.where(lane == idx, x, 0.0), axis=1, keepdims=True)


def _cparams(sem):
    return pltpu.CompilerParams(dimension_semantics=sem, vmem_limit_bytes=VMEM_LIMIT)


def _mod_kernel(c_ref, w_ref, b_ref, o_ref):
    a = _silu(c_ref[...]).astype(BF16)
    o_ref[...] = _dot(a, w_ref[...].astype(BF16)) + b_ref[...]


def _mod_call(cv, w_mod, b_mod):
    depth, d, e = w_mod.shape
    tn = 1024
    return pl.pallas_call(
        _mod_kernel,
        grid=(depth, e // tn),
        in_specs=[pl.BlockSpec((8, d), lambda l, j: (0, 0)),
                  pl.BlockSpec((None, d, tn), lambda l, j: (l, 0, j)),
                  pl.BlockSpec((None, 1, tn), lambda l, j: (l, 0, j))],
        out_specs=pl.BlockSpec((None, 8, tn), lambda l, j: (l, 0, j)),
        out_shape=jax.ShapeDtypeStruct((depth, 8, e), F32),
        compiler_params=_cparams(("arbitrary", "arbitrary")),
        name="mod",
    )(cv, w_mod, b_mod.reshape(depth, 1, e))


def _mod_row(i, n_ctx_tiles):
    return jnp.maximum(i - (n_ctx_tiles - 1), 0)


def _ffn_kernel(x_ref, mod_ref, wa_ref, wb_ref, wd_ref, g_ref, b_ref, o_ref, xm_s, acc_s, *, nf):
    j = pl.program_id(1)
    d = x_ref.shape[1]

    @pl.when(j == 0)
    def _():
        sh = mod_ref[:, 0:d]
        sc = mod_ref[:, d:2 * d]
        xm_s[...] = (x_ref[...] * (1.0 + sc) + sh).astype(BF16)
        acc_s[...] = jnp.zeros_like(acc_s)

    xm = xm_s[...]
    a = _dot(xm, wa_ref[...])
    b = _dot(xm, wb_ref[...])
    h = (_silu(a) * b).astype(BF16)
    acc_s[...] += _dot(h, wd_ref[...])

    @pl.when(j == nf - 1)
    def _():
        gate = mod_ref[:, 2 * d:3 * d]
        z = ALPHA * x_ref[...] + 0.5 * gate * acc_s[...]
        o_ref[...] = _ln(z, g_ref[...], b_ref[...])


def _ffn_call(x, mod4, w_gu, w_down, ln_g, ln_b, layer, which, n_ctx_tiles):
    t, d = x.shape
    nf = D_FF // FF_TILE
    sub = 0 if which == 0 else 2
    return pl.pallas_call(
        functools.partial(_ffn_kernel, nf=nf),
        grid=(t // ROW_TILE, nf),
        in_specs=[pl.BlockSpec((ROW_TILE, d), lambda i, j: (i, 0)),
                  pl.BlockSpec((None, None, 1, 3 * d), lambda i, j: (layer, _mod_row(i, n_ctx_tiles), 0, sub)),
                  pl.BlockSpec((None, None, d, FF_TILE), lambda i, j: (layer, which, 0, j)),
                  pl.BlockSpec((None, None, d, FF_TILE), lambda i, j: (layer, which, 0, j + nf)),
                  pl.BlockSpec((None, None, FF_TILE, d), lambda i, j: (layer, which, j, 0)),
                  pl.BlockSpec((None, None, 1, d), lambda i, j: (layer, sub, 0, 0)),
                  pl.BlockSpec((None, None, 1, d), lambda i, j: (layer, sub, 0, 0))],
        out_specs=pl.BlockSpec((ROW_TILE, d), lambda i, j: (i, 0)),
        out_shape=jax.ShapeDtypeStruct((t, d), F32),
        scratch_shapes=[pltpu.VMEM((ROW_TILE, d), BF16), pltpu.VMEM((ROW_TILE, d), F32)],
        compiler_params=_cparams(("arbitrary", "arbitrary")),
        name=f"ffn{which}",
    )(x, mod4, w_gu, w_gu, w_down, ln_g, ln_b)


def _inproj_kernel(x_ref, mod_ref, w_ref, o_ref, xm_s):
    d = x_ref.shape[1]

    @pl.when(pl.program_id(1) == 0)
    def _():
        sh = mod_ref[:, 0:d]
        sc = mod_ref[:, d:2 * d]
        xm_s[...] = (x_ref[...] * (1.0 + sc) + sh).astype(BF16)

    o_ref[...] = _dot(xm_s[...], w_ref[...]).astype(o_ref.dtype)


def _inproj_call(x, mod4, w, layer, tn, out_dtype, n_ctx_tiles, name):
    t, d = x.shape
    n = w.shape[2]
    return pl.pallas_call(
        _inproj_kernel,
        grid=(t // ROW_TILE, n // tn),
        in_specs=[pl.BlockSpec((ROW_TILE, d), lambda i, j: (i, 0)),
                  pl.BlockSpec((None, None, 1, 3 * d), lambda i, j: (layer, _mod_row(i, n_ctx_tiles), 0, 1)),
                  pl.BlockSpec((None, d, tn), lambda i, j: (layer, 0, j))],
        out_specs=pl.BlockSpec((ROW_TILE, tn), lambda i, j: (i, j)),
        out_shape=jax.ShapeDtypeStruct((t, n), out_dtype),
        scratch_shapes=[pltpu.VMEM((ROW_TILE, d), BF16)],
        compiler_params=_cparams(("arbitrary", "arbitrary")),
        name=name,
    )(x, mod4, w)


def _mixout_kernel(x_ref, mod_ref, yh_ref, ym_ref, ys_ref, g0_ref, g1_ref, g2_ref, wb_ref, wo_ref,
                   lg_ref, lb_ref, o_ref):
    d = x_ref.shape[1]
    p = jax.nn.sigmoid(g0_ref[...].astype(F32)) * _dot(yh_ref[...], wb_ref[0])
    p += jax.nn.sigmoid(g1_ref[...].astype(F32)) * _dot(ym_ref[...], wb_ref[1])
    p += jax.nn.sigmoid(g2_ref[...].astype(F32)) * _dot(ys_ref[...], wb_ref[2])
    y = _dot(p.astype(BF16), wo_ref[...])
    gate = mod_ref[:, 2 * d:3 * d]
    z = ALPHA * x_ref[...] + gate * y
    o_ref[...] = _ln(z, lg_ref[...], lb_ref[...])


def _mixout_call(x, mod4, y_hg, y_ml, y_mb, ub, w_branch, w_out, ln_g, ln_b, layer, n_ctx_tiles):
    t, d = x.shape
    tm = 512
    per = ROW_TILE // tm
    gb = (UB_W - 3 * d) // d
    return pl.pallas_call(
        _mixout_kernel,
        grid=(t // tm,),
        in_specs=[pl.BlockSpec((tm, d), lambda i: (i, 0)),
                  pl.BlockSpec((None, None, 1, 3 * d), lambda i: (layer, _mod_row(i // per, n_ctx_tiles), 0, 1)),
                  pl.BlockSpec((tm, MIX_W), lambda i: (i, 0)),
                  pl.BlockSpec((tm, MIX_W), lambda i: (i, 0)),
                  pl.BlockSpec((tm, MIX_W), lambda i: (i, 0)),
                  pl.BlockSpec((tm, d), lambda i: (i, gb)),
                  pl.BlockSpec((tm, d), lambda i: (i, gb + 1)),
                  pl.BlockSpec((tm, d), lambda i: (i, gb + 2)),
                  pl.BlockSpec((None, 3, MIX_W, d), lambda i: (layer, 0, 0, 0)),
                  pl.BlockSpec((None, d, d), lambda i: (layer, 0, 0)),
                  pl.BlockSpec((None, None, 1, d), lambda i: (layer, 1, 0, 0)),
                  pl.BlockSpec((None, None, 1, d), lambda i: (layer, 1, 0, 0))],
        out_specs=pl.BlockSpec((tm, d), lambda i: (i, 0)),
        out_shape=jax.ShapeDtypeStruct((t, d), F32),
        compiler_params=_cparams(("arbitrary",)),
        name="mixout",
    )(x, mod4, y_hg, y_ml, y_mb, ub, ub, ub, w_branch, w_out, ln_g, ln_b)


def _block_ref_rows(c, w):
    n_rows = c.shape[0]
    if 2 * w == n_rows:
        return jnp.broadcast_to(c[w - 1:w, :], c.shape)
    if w >= 4:
        n = n_rows // (2 * w)
        c3 = c.reshape(n, 2 * w, LANES)
        return jnp.broadcast_to(c3[:, w - 1:w, :], c3.shape).reshape(n_rows, LANES)
    c8 = c.reshape(n_rows // 8, 8, LANES)
    sub = lax.broadcasted_iota(jnp.int32, c8.shape, 1)
    if w == 2:
        r = jnp.where(sub < 4, c8[:, 1:2, :], c8[:, 5:6, :])
    else:
        r = jnp.where(sub < 2, c8[:, 0:1, :],
                      jnp.where(sub < 4, c8[:, 2:3, :], jnp.where(sub < 6, c8[:, 4:5, :], c8[:, 6:7, :])))
    return r.reshape(n_rows, LANES)


def _hgrn_chunk(q, v, fpre, loglb, log1mlb, onemlb, s_t, tril, backward):
    c_len = q.shape[0]
    x2 = log1mlb + _log_sigmoid(fpre)
    logf = jnp.maximum(loglb, x2) + jnp.log1p(jnp.exp(-jnp.abs(loglb - x2)))
    kk = onemlb * jax.nn.sigmoid(-fpre)
    cum = _cumsum_rows(tril, logf)
    pos = (cum - logf) if backward else cum
    row = lax.broadcasted_iota(jnp.int32, (c_len, LANES), 0)
    rr = lax.broadcasted_iota(jnp.int32, (c_len, c_len), 0)
    ss = lax.broadcasted_iota(jnp.int32, (c_len, c_len), 1)
    att = jnp.zeros((c_len, c_len), F32)
    w = c_len // 2
    while w >= 1:
        e = jnp.exp(-jnp.abs(pos - _block_ref_rows(cum, w)))
        upper = (row & w) != 0
        if backward:
            q_side = jnp.where(upper, 0.0, q * e).astype(BF16)
            k_side = jnp.where(upper, kk * e, 0.0).astype(BF16)
        else:
            q_side = jnp.where(upper, q * e, 0.0).astype(BF16)
            k_side = jnp.where(upper, 0.0, kk * e).astype(BF16)
        a_w = _dot_nt(q_side, k_side)
        if 2 * w == c_len:
            att = att + a_w
        else:
            sh = int(math.log2(2 * w))
            att = att + jnp.where((rr >> sh) == (ss >> sh), a_w, 0.0)
        w //= 2
    diag = jnp.sum(q * kk, axis=1, keepdims=True)
    c_last = cum[c_len - 1:c_len, :]
    if backward:
        q_in = q * jnp.exp(c_last - pos)
        k_out = kk * jnp.exp(pos)
    else:
        q_in = q * jnp.exp(cum)
        k_out = kk * jnp.exp(c_last - cum)
    o = _dot(att.astype(BF16), v) + diag * v.astype(F32) + _dot_nt(q_in.astype(BF16), s_t.astype(BF16))
    s_new = s_t * jnp.exp(c_last) + _dot_tn(v, k_out.astype(BF16))
    return o, s_new


def _hgrn_kernel(q_ref, v_ref, g_ref, f0_ref, f1_ref, lbp_ref, ng_ref, s0_ref, y_ref, *rest,
                 seq_len, emit_state):
    if emit_state:
        sfin_ref, oacc, st = rest
    else:
        oacc, st = rest
    h = pl.program_id(1)
    nchunk = seq_len // CH
    tril = _tri(CH, True)
    f_refs = (f0_ref, f1_ref)
    for d in range(2):
        st[d] = s0_ref[d].T

    def run(ci, first):
        for d in range(2):
            cc = ci if d == 0 else nchunk - 1 - ci
            r0 = cc * CH
            if not isinstance(r0, int):
                r0 = pl.multiple_of(r0, CH)
            rows = pl.ds(r0, CH)
            lrow = pl.ds(d * HG_HEADS + h, 1)
            o, s_new = _hgrn_chunk(_silu(q_ref[rows, :].astype(F32)), v_ref[rows, :], f_refs[d][rows, :],
                                   lbp_ref[0, lrow, :], lbp_ref[1, lrow, :], lbp_ref[2, lrow, :],
                                   st[d], tril, d == 1)
            st[d] = s_new
            if first and d == 0:
                oacc[rows, :] = o
            else:
                oacc[rows, :] += o

    if nchunk == 1:
        run(0, True)
    else:
        oacc[...] = jnp.zeros_like(oacc)
        lax.fori_loop(0, nchunk, lambda ci, c: (run(ci, False), c)[1], 0)

    o = oacc[...]
    y = o * lax.rsqrt(jnp.mean(o * o, -1, keepdims=True) + RMS_EPS) * ng_ref[...]
    y_ref[...] = (y * _silu(g_ref[...].astype(F32))).astype(y_ref.dtype)
    if emit_state:
        for d in range(2):
            sfin_ref[d] = st[d].T


def _hgrn_call(ub, uf, lbp, ng, s0, y_prev, *, seq_len, row0, emit_state):
    t = ub.shape[0]
    nseq = s0.shape[0]
    bo = row0 // seq_len
    nh = HG_HEADS
    in_specs = [pl.BlockSpec((seq_len, LANES), lambda i, h: (bo + i, h)),
                pl.BlockSpec((seq_len, LANES), lambda i, h: (bo + i, nh + h)),
                pl.BlockSpec((seq_len, LANES), lambda i, h: (bo + i, 2 * nh + h)),
                pl.BlockSpec((seq_len, LANES), lambda i, h: (bo + i, h)),
                pl.BlockSpec((seq_len, LANES), lambda i, h: (bo + i, nh + h)),
                pl.BlockSpec((3, 8, LANES), lambda i, h: (0, 0, 0)),
                pl.BlockSpec((1, LANES), lambda i, h: (0, h)),
                pl.BlockSpec((None, 2, None, HG_DK, LANES), lambda i, h: (i, 0, h, 0, 0))]
    args = [ub, ub, ub, uf, uf, lbp, ng, s0]
    out_specs = [pl.BlockSpec((seq_len, LANES), lambda i, h: (bo + i, h))]
    out_shape = [jax.ShapeDtypeStruct((t, MIX_W), BF16)]
    aliases = {}
    if y_prev is not None:
        in_specs.append(pl.BlockSpec(memory_space=pl.ANY))
        args.append(y_prev)
        aliases = {len(args) - 1: 0}
    if emit_state:
        out_specs.append(pl.BlockSpec((None, 2, None, HG_DK, LANES), lambda i, h: (i, 0, h, 0, 0)))
        out_shape.append(jax.ShapeDtypeStruct(s0.shape, F32))

    def body(*refs):
        n_in = len(args)
        ins = refs[:8]
        outs = refs[n_in:]
        _hgrn_kernel(*ins, *outs, seq_len=seq_len, emit_state=emit_state)

    res = pl.pallas_call(
        body,
        grid=(nseq, nh),
        in_specs=in_specs,
        out_specs=out_specs,
        out_shape=out_shape,
        input_output_aliases=aliases,
        scratch_shapes=[pltpu.VMEM((seq_len, LANES), F32), pltpu.VMEM((2, LANES, HG_DK), F32)],
        compiler_params=_cparams(("arbitrary", "arbitrary")),
        name="hgrn_ctx" if emit_state else "hgrn_lat",
    )(*args)
    return res if emit_state else (res[0], None)


def _mlstm_chunk(q, k, v, gates, r, rows_s, c_in, n_in, m_in, tril, triu, backward):
    c_len = q.shape[0]
    ls = _log_sigmoid(gates)
    cum_cols = _cumsum_rows(tril, ls)
    g_t = gates.T
    ls_rows = _log_sigmoid(g_t[8:16, :])
    rows_s[0:8, :] = g_t[0:8, :]
    rows_s[8:16, :] = ls_rows
    rows_s[16:24, :] = _cumsum_lanes(ls_rows, triu)
    i_row = rows_s[pl.ds(r, 1), :]
    lf_row = rows_s[pl.ds(8 + r, 1), :]
    c_row = rows_s[pl.ds(16 + r, 1), :]
    i_col = _col(gates, r)
    lf_col = _col(ls, 8 + r)
    c_col = _col(cum_cols, 8 + r)
    c_last = c_col[c_len - 1:c_len, :]
    rr = lax.broadcasted_iota(jnp.int32, (c_len, c_len), 0)
    ss = lax.broadcasted_iota(jnp.int32, (c_len, c_len), 1)
    if backward:
        e_col = c_col - lf_col
        dlog = jnp.where(ss >= rr, (c_row - lf_row) + i_row - e_col, -jnp.inf)
        m_inter = (c_last - e_col) + m_in
        g_col = e_col + i_col
    else:
        dlog = jnp.where(ss <= rr, c_col - c_row + i_row, -jnp.inf)
        m_inter = c_col + m_in
        g_col = (c_last - c_col) + i_col
    m_t = jnp.maximum(m_inter, jnp.max(dlog, axis=1, keepdims=True))
    qb = q.astype(BF16)
    s = jnp.exp(dlog - m_t) * _dot_nt(qb, k.astype(BF16))
    w_inter = jnp.exp(m_inter - m_t)
    num = _dot(s.astype(BF16), v) + w_inter * _dot(qb, c_in.astype(BF16))
    den = jnp.sum(s, axis=1, keepdims=True) + w_inter * jnp.sum(q * n_in, axis=1, keepdims=True)
    den = jnp.maximum(jnp.abs(den), jnp.exp(-m_t))
    hout = num / den
    m_new = jnp.maximum(c_last + m_in, jnp.max(g_col, axis=0, keepdims=True))
    kw = k * jnp.exp(g_col - m_new)
    a = jnp.exp(c_last + m_in - m_new)
    c_new = a * c_in + _dot_tn(kw.astype(BF16), v)
    n_new = a * n_in + jnp.sum(kw, axis=0, keepdims=True)
    return hout, c_new, n_new, m_new


def _mlstm_kernel(q_ref, k_ref, v_ref, og_ref, gt_ref, bias_ref, ng_ref, c0_ref, n0_ref, m0_ref, y_ref, *rest,
                  seq_len, emit_state):
    if emit_state:
        cfin_ref, nfin_ref, mfin_ref, oacc, rows_s, cst, nst, mst = rest
    else:
        oacc, rows_s, cst, nst, mst = rest
    h = pl.program_id(1)
    nchunk = seq_len // CH
    tril = _tri(CH, True)
    triu = _tri(CH, False)
    scale = ML_DK ** -0.5
    for d in range(2):
        cst[d] = c0_ref[d]
        nst[d] = n0_ref[d]
        mst[d] = m0_ref[d]

    def run(ci, first):
        for d in range(2):
            cc = ci if d == 0 else nchunk - 1 - ci
            r0 = cc * CH
            if not isinstance(r0, int):
                r0 = pl.multiple_of(r0, CH)
            rows = pl.ds(r0, CH)
            gates = gt_ref[rows, :] + bias_ref[...]
            hout, c_new, n_new, m_new = _mlstm_chunk(
                q_ref[rows, :].astype(F32), k_ref[rows, :].astype(F32) * scale, v_ref[rows, :], gates,
                d * ML_HEADS + h, rows_s, cst[d], nst[d], mst[d][:, 0:1], tril, triu, d == 1)
            cst[d] = c_new
            nst[d] = n_new
            mst[d] = jnp.broadcast_to(m_new, (1, LANES))
            if first and d == 0:
                oacc[rows, :] = hout
            else:
                oacc[rows, :] += hout

    if nchunk == 1:
        run(0, True)
    else:
        oacc[...] = jnp.zeros_like(oacc)
        lax.fori_loop(0, nchunk, lambda ci, c: (run(ci, False), c)[1], 0)

    o = oacc[...]
    y = o * lax.rsqrt(jnp.mean(o * o, -1, keepdims=True) + RMS_EPS) * ng_ref[...]
    y_ref[...] = (y * jax.nn.sigmoid(og_ref[...].astype(F32))).astype(y_ref.dtype)
    if emit_state:
        for d in range(2):
            cfin_ref[d] = cst[d]
            nfin_ref[d] = nst[d]
            mfin_ref[d] = mst[d]


def _mlstm_call(ub, uf, bias_row, ng, c0, n0, m0, y_prev, *, seq_len, row0, emit_state):
    t = ub.shape[0]
    nseq = c0.shape[0]
    bo = row0 // seq_len
    nh = ML_HEADS
    base = 3 * HG_HEADS
    st_spec = pl.BlockSpec((None, 2, None, ML_DK, LANES), lambda i, h: (i, 0, h, 0, 0))
    vec_spec = pl.BlockSpec((None, 2, None, 1, LANES), lambda i, h: (i, 0, h, 0, 0))
    in_specs = [pl.BlockSpec((seq_len, LANES), lambda i, h: (bo + i, base + h)),
                pl.BlockSpec((seq_len, LANES), lambda i, h: (bo + i, base + nh + h)),
                pl.BlockSpec((seq_len, LANES), lambda i, h: (bo + i, base + 2 * nh + h)),
                pl.BlockSpec((seq_len, LANES), lambda i, h: (bo + i, base + 3 * nh + h)),
                pl.BlockSpec((seq_len, LANES), lambda i, h: (bo + i, GATE_BLK)),
                pl.BlockSpec((1, LANES), lambda i, h: (0, 0)),
                pl.BlockSpec((1, LANES), lambda i, h: (0, h)),
                st_spec, vec_spec, vec_spec]
    args = [ub, ub, ub, ub, uf, bias_row, ng, c0, n0, m0]
    out_specs = [pl.BlockSpec((seq_len, LANES), lambda i, h: (bo + i, h))]
    out_shape = [jax.ShapeDtypeStruct((t, MIX_W), BF16)]
    aliases = {}
    if y_prev is not None:
        in_specs.append(pl.BlockSpec(memory_space=pl.ANY))
        args.append(y_prev)
        aliases = {len(args) - 1: 0}
    if emit_state:
        out_specs += [st_spec, vec_spec, vec_spec]
        out_shape += [jax.ShapeDtypeStruct(c0.shape, F32), jax.ShapeDtypeStruct(n0.shape, F32),
                      jax.ShapeDtypeStruct(m0.shape, F32)]

    def body(*refs):
        n_in = len(args)
        _mlstm_kernel(*refs[:10], *refs[n_in:], seq_len=seq_len, emit_state=emit_state)

    res = pl.pallas_call(
        body,
        grid=(nseq, nh),
        in_specs=in_specs,
        out_specs=out_specs,
        out_shape=out_shape,
        input_output_aliases=aliases,
        scratch_shapes=[pltpu.VMEM((seq_len, LANES), F32), pltpu.VMEM((24, CH), F32),
                        pltpu.VMEM((2, ML_DK, LANES), F32), pltpu.VMEM((2, 1, LANES), F32),
                        pltpu.VMEM((2, 1, LANES), F32)],
        compiler_params=_cparams(("arbitrary", "arbitrary")),
        name="mlstm_ctx" if emit_state else "mlstm_lat",
    )(*args)
    return res if emit_state else (res[0], None, None, None)


N_PAIR = MB_HEADS // 2


def _ssd_chunk(xs, bcs, gates, nega_row, ht_refs, tril, triu, backward):
    c_len = xs.shape[0]
    dt = _softplus(gates)
    la = dt * nega_row
    cum_cols = _cumsum_rows(tril, la)
    la_rows = la.T[16:32, :]
    cum_rows = _cumsum_lanes(la_rows, triu)
    rr = lax.broadcasted_iota(jnp.int32, (c_len, c_len), 0)
    ss = lax.broadcasted_iota(jnp.int32, (c_len, c_len), 1)
    lane = lax.broadcasted_iota(jnp.int32, (c_len, LANES), 1)
    lo = lane < MB_HEADDIM
    lo_state = lax.broadcasted_iota(jnp.int32, (LANES, LANES), 1) < MB_HEADDIM
    bblk = bcs[:, 0:LANES]
    cblk = bcs[:, LANES:2 * LANES]
    off = 8 if backward else 0
    ys, new_states = [], []
    gm = None
    for j in range(N_PAIR):
        grp = j // (N_PAIR // MB_GROUPS)
        in_grp = (lane >= grp * MB_DSTATE) & (lane < (grp + 1) * MB_DSTATE)
        if j % (N_PAIR // MB_GROUPS) == 0:
            c_g = jnp.where(in_grp, cblk, 0.0)
            b_g = jnp.where(in_grp, bblk, 0.0)
            gm = _dot_nt(c_g.astype(BF16), b_g.astype(BF16))
        xpair = xs[:, j * LANES:(j + 1) * LANES]
        ht = ht_refs[j]
        y = jnp.zeros((c_len, LANES), F32)
        decay = None
        upd = None
        for half in range(2):
            idx = 16 + off + 2 * j + half
            c_col = cum_cols[:, idx:idx + 1]
            c_row = cum_rows[idx - 16:idx - 15, :]
            dt_col = dt[:, idx:idx + 1]
            c_last = c_col[c_len - 1:c_len, :]
            sel = lo if half == 0 else jnp.logical_not(lo)
            xh = jnp.where(sel, xpair * dt_col, 0.0).astype(BF16)
            if backward:
                la_col = la[:, idx:idx + 1]
                la_row = la_rows[idx - 16:idx - 15, :]
                e_col = c_col - la_col
                m = jnp.where(ss >= rr, jnp.exp((c_row - la_row) - e_col), 0.0) * gm
                c_in = c_g * jnp.exp(c_last - e_col)
                b_out = b_g * jnp.exp(e_col)
            else:
                m = jnp.where(ss <= rr, jnp.exp(c_col - c_row), 0.0) * gm
                c_in = c_g * jnp.exp(c_col)
                b_out = b_g * jnp.exp(c_last - c_col)
            ht_half = jnp.where(lo_state, ht, 0.0) if half == 0 else jnp.where(lo_state, 0.0, ht)
            y = y + _dot(m.astype(BF16), xh) + _dot(c_in.astype(BF16), ht_half.astype(BF16))
            u = _dot_tn(b_out.astype(BF16), xh)
            upd = u if upd is None else upd + u
            dl = jnp.exp(c_last)
            decay = dl if decay is None else jnp.where(lane[0:1, :] < MB_HEADDIM, decay, dl)
        new_states.append(ht * decay + upd)
        ys.append(y)
    return jnp.concatenate(ys, axis=1), new_states


def _ssd_kernel(x_ref, bc_ref, z_ref, gt_ref, cw_ref, bias_ref, nega_ref, dskip_ref, ng_ref, h0_ref, y_ref, *rest,
                seq_len, emit_state):
    if emit_state:
        hfin_ref, xs_s, bcs_s, yacc, hst = rest
    else:
        xs_s, bcs_s, yacc, hst = rest
    nchunk = seq_len // CH
    tril = _tri(CH, True)
    triu = _tri(CH, False)

    def conv(v, lo_col, width):
        row = lax.broadcasted_iota(jnp.int32, v.shape, 0)
        prev = jnp.where(row == 0, 0.0, pltpu.roll(v, 1, 0))
        nxt = jnp.where(row == seq_len - 1, 0.0, pltpu.roll(v, seq_len - 1, 0))
        cs = slice(lo_col, lo_col + width)
        return _silu(cw_ref[0:1, cs] * prev + cw_ref[1:2, cs] * v + cw_ref[2:3, cs] * nxt + cw_ref[3:4, cs])

    xs_s[...] = conv(x_ref[...], 0, MIX_W)
    bcs_s[...] = conv(bc_ref[...], MIX_W, 2 * LANES)
    hst[...] = h0_ref[...]

    def run(ci, first):
        for d in range(2):
            cc = ci if d == 0 else nchunk - 1 - ci
            r0 = cc * CH
            if not isinstance(r0, int):
                r0 = pl.multiple_of(r0, CH)
            rows = pl.ds(r0, CH)
            gates = gt_ref[rows, :] + bias_ref[...]
            y, new_states = _ssd_chunk(xs_s[rows, :], bcs_s[rows, :], gates, nega_ref[...],
                                       [hst[d, j] for j in range(N_PAIR)], tril, triu, d == 1)
            for j in range(N_PAIR):
                hst[d, j] = new_states[j]
            if first and d == 0:
                yacc[rows, :] = y
            else:
                yacc[rows, :] += y

    if nchunk == 1:
        run(0, True)
    else:
        yacc[...] = jnp.zeros_like(yacc)
        lax.fori_loop(0, nchunk, lambda ci, c: (run(ci, False), c)[1], 0)

    y = (yacc[...] + dskip_ref[...] * xs_s[...]) * _silu(z_ref[...].astype(F32))
    y = y * lax.rsqrt(jnp.mean(y * y, -1, keepdims=True) + RMS_EPS) * ng_ref[...]
    y_ref[...] = y.astype(y_ref.dtype)
    if emit_state:
        hfin_ref[...] = hst[...]


def _ssd_call(ub, uf, cw, bias_row, nega_row, dskip, ng, h0, y_prev, *, seq_len, row0, emit_state):
    t = ub.shape[0]
    nseq = h0.shape[0]
    bo = row0 // seq_len
    st_spec = pl.BlockSpec((None, 2, N_PAIR, LANES, LANES), lambda i: (i, 0, 0, 0, 0))
    row_spec = lambda w: pl.BlockSpec((1, w), lambda i: (0, 0))
    in_specs = [pl.BlockSpec((seq_len, MIX_W), lambda i: (bo + i, 2)),
                pl.BlockSpec((seq_len, 2 * LANES), lambda i: (bo + i, 6)),
                pl.BlockSpec((seq_len, MIX_W), lambda i: (bo + i, 7)),
                pl.BlockSpec((seq_len, LANES), lambda i: (bo + i, GATE_BLK)),
                pl.BlockSpec((8, MB_XBC), lambda i: (0, 0)),
                row_spec(LANES), row_spec(LANES), row_spec(MIX_W), row_spec(MIX_W),
                st_spec]
    args = [uf, uf, ub, uf, cw, bias_row, nega_row, dskip, ng, h0]
    out_specs = [pl.BlockSpec((seq_len, MIX_W), lambda i: (bo + i, 0))]
    out_shape = [jax.ShapeDtypeStruct((t, MIX_W), BF16)]
    aliases = {}
    if y_prev is not None:
        in_specs.append(pl.BlockSpec(memory_space=pl.ANY))
        args.append(y_prev)
        aliases = {len(args) - 1: 0}
    if emit_state:
        out_specs.append(st_spec)
        out_shape.append(jax.ShapeDtypeStruct(h0.shape, F32))

    def body(*refs):
        n_in = len(args)
        _ssd_kernel(*refs[:10], *refs[n_in:], seq_len=seq_len, emit_state=emit_state)

    res = pl.pallas_call(
        body,
        grid=(nseq,),
        in_specs=in_specs,
        out_specs=out_specs,
        out_shape=out_shape,
        input_output_aliases=aliases,
        scratch_shapes=[pltpu.VMEM((seq_len, MIX_W), F32), pltpu.VMEM((seq_len, 2 * LANES), F32),
                        pltpu.VMEM((seq_len, MIX_W), F32), pltpu.VMEM((2, N_PAIR, LANES, LANES), F32)],
        compiler_params=_cparams(("arbitrary",)),
        name="ssd_ctx" if emit_state else "ssd_lat",
    )(*args)
    return res if emit_state else (res[0], None)


def _ssd_state_to_pairs(s):
    bsz = s.shape[0]
    st = jnp.swapaxes(s, -1, -2).reshape(bsz, 2, N_PAIR, 2, MB_DSTATE, MB_HEADDIM)
    st = jnp.moveaxis(st, 3, 4).reshape(bsz, 2, N_PAIR, MB_DSTATE, 2 * MB_HEADDIM)
    zero = jnp.zeros_like(st)
    grp = (jnp.arange(N_PAIR) // (N_PAIR // MB_GROUPS)).reshape(1, 1, N_PAIR, 1, 1)
    return jnp.concatenate([jnp.where(grp == 0, st, zero), jnp.where(grp == 1, st, zero)], axis=3)


def _ssd_pairs_to_state(hp):
    bsz = hp.shape[0]
    halves = hp.reshape(bsz, 2, N_PAIR, MB_GROUPS, MB_DSTATE, 2 * MB_HEADDIM)
    grp = (jnp.arange(N_PAIR) // (N_PAIR // MB_GROUPS)).reshape(1, 1, N_PAIR, 1, 1)
    st = jnp.where(grp == 0, halves[:, :, :, 0], halves[:, :, :, 1])
    st = st.reshape(bsz, 2, N_PAIR, MB_DSTATE, 2, MB_HEADDIM)
    st = jnp.moveaxis(st, 4, 3).reshape(bsz, 2, MB_HEADS, MB_DSTATE, MB_HEADDIM)
    return jnp.swapaxes(st, -1, -2)


def _grid_pos_embed(n_tok, d_model):
    rows = n_tok // GRID_W
    r, cidx = jnp.meshgrid(jnp.arange(rows, dtype=F32), jnp.arange(GRID_W, dtype=F32), indexing='ij')
    quarter = d_model // 4
    freq = jnp.exp(-math.log(10000.0) * jnp.arange(quarter, dtype=F32) / quarter)
    ar = r.reshape(-1, 1) * freq
    ac = cidx.reshape(-1, 1) * freq
    return jnp.concatenate([jnp.sin(ar), jnp.cos(ar), jnp.sin(ac), jnp.cos(ac)], axis=-1)


def _pad_row(pieces, width):
    v = jnp.concatenate([p.reshape(-1).astype(F32) for p in pieces])
    return jnp.pad(v, (0, width - v.shape[0])).reshape(1, width)


def kernel(x_prompt, x_sample, state_hgrn, state_mlstm_C, state_mlstm_n, state_mlstm_m, state_ssd, c, c_ctx,
           w_mod, b_mod, ln_g, ln_b, ffn_w_gu, ffn_w_down, w_in, hg_lb, hg_norm_g, ml_gate_b, ml_norm_g,
           mb_conv_w, mb_conv_b, mb_dt_bias, mb_a_log, mb_d, mb_norm_g, w_branch, w_out):
    bsz, seq, d = x_prompt.shape
    dbsz, dseq, _ = x_sample.shape
    depth = w_mod.shape[0]
    t_ctx = bsz * seq
    n_ctx_tiles = t_ctx // ROW_TILE
    assert dseq == ROW_TILE and t_ctx % ROW_TILE == 0 and seq == CH and dseq % CH == 0

    xs0 = x_sample + _grid_pos_embed(dseq, d).astype(x_sample.dtype)[None]
    x = jnp.concatenate([x_prompt.reshape(t_ctx, d), xs0.reshape(dbsz * dseq, d)], axis=0)

    cv = jnp.concatenate([c_ctx[None], c, jnp.zeros((8 - 1 - dbsz, d), F32)], axis=0)
    mod4 = _mod_call(cv, w_mod, b_mod).reshape(depth, 8, 1, N_MOD * d)

    w_gu_b = ffn_w_gu.astype(BF16)
    w_down_b = ffn_w_down.astype(BF16)
    w_branch_b = w_branch.astype(BF16)
    w_out_b = w_out.astype(BF16)
    w_a = jnp.concatenate([w_in[:, :, 0:1536], w_in[:, :, 2560:4608], w_in[:, :, 4624:5136],
                           w_in[:, :, 5920:8992]], axis=2).astype(BF16)
    w_b = jnp.concatenate([w_in[:, :, 1536:2560], w_in[:, :, 5136:5904], w_in[:, :, 4608:4624],
                           w_in[:, :, 5904:5920], jnp.zeros((depth, d, UF_W - 1824), F32)], axis=2).astype(BF16)
    ln_g4 = ln_g.reshape(depth, 3, 1, d)
    ln_b4 = ln_b.reshape(depth, 3, 1, d)

    lbs = jnp.cumsum(jax.nn.softmax(hg_lb.astype(F32), axis=0), axis=0)
    lbs = (lbs - lbs[0]).reshape(depth, 2 * HG_HEADS, HG_DK)
    lbp = jnp.stack([jnp.log(lbs), jnp.log1p(-lbs), 1.0 - lbs], axis=1)
    cw = jnp.concatenate([mb_conv_w, mb_conv_b[:, None, :], jnp.zeros((depth, 4, MB_XBC), F32)], axis=1)

    zeros_ctx = lambda *s: jnp.zeros((bsz, 2) + s, F32)
    lat_h0 = _ssd_state_to_pairs(state_ssd.reshape((dbsz * depth,) + state_ssd.shape[2:])).reshape(
        (dbsz, depth, 2, N_PAIR, LANES, LANES))

    st_hg, st_c, st_n, st_m, st_h = [], [], [], [], []
    for l in range(depth):
        x = _ffn_call(x, mod4, w_gu_b, w_down_b, ln_g4, ln_b4, l, 0, n_ctx_tiles)
        ub = _inproj_call(x, mod4, w_a, l, 512, BF16, n_ctx_tiles, "inproj_a")
        uf = _inproj_call(x, mod4, w_b, l, 640, F32, n_ctx_tiles, "inproj_b")

        ng_h = hg_norm_g[l].reshape(1, MIX_W)
        y_hg, s_fin = _hgrn_call(ub, uf, lbp[l], ng_h, zeros_ctx(HG_HEADS, HG_DK, LANES), None,
                                 seq_len=seq, row0=0, emit_state=True)
        y_hg, _ = _hgrn_call(ub, uf, lbp[l], ng_h, state_hgrn[:, l], y_hg,
                             seq_len=dseq, row0=t_ctx, emit_state=False)
        st_hg.append(s_fin)

        ml_bias = _pad_row([ml_gate_b[l, 0], ml_gate_b[l, 1], mb_dt_bias[l]], LANES)
        ng_m = ml_norm_g[l].reshape(1, MIX_W)
        y_ml, c_fin, n_fin, m_fin = _mlstm_call(
            ub, uf, ml_bias, ng_m, zeros_ctx(ML_HEADS, ML_DK, LANES), zeros_ctx(ML_HEADS, 1, LANES),
            zeros_ctx(ML_HEADS, 1, LANES), None, seq_len=seq, row0=0, emit_state=True)
        y_ml, _, _, _ = _mlstm_call(
            ub, uf, ml_bias, ng_m, state_mlstm_C[:, l], state_mlstm_n[:, l][:, :, :, None, :],
            jnp.broadcast_to(state_mlstm_m[:, l][:, :, :, None, None], (dbsz, 2, ML_HEADS, 1, LANES)), y_ml,
            seq_len=dseq, row0=t_ctx, emit_state=False)
        st_c.append(c_fin)
        st_n.append(n_fin[:, :, :, 0, :])
        st_m.append(m_fin[:, :, :, 0, 0])

        nega = _pad_row([jnp.zeros((16,), F32), -jnp.exp(mb_a_log[l].astype(F32))], LANES)
        dskip = jnp.repeat(mb_d[l], MB_HEADDIM).reshape(1, MIX_W)
        ng_s = mb_norm_g[l].reshape(1, MIX_W)
        y_mb, h_fin = _ssd_call(ub, uf, cw[l], ml_bias, nega, dskip, ng_s, zeros_ctx(N_PAIR, LANES, LANES), None,
                                seq_len=seq, row0=0, emit_state=True)
        y_mb, _ = _ssd_call(ub, uf, cw[l], ml_bias, nega, dskip, ng_s, lat_h0[:, l], y_mb,
                            seq_len=dseq, row0=t_ctx, emit_state=False)
        st_h.append(_ssd_pairs_to_state(h_fin))

        x = _mixout_call(x, mod4, y_hg, y_ml, y_mb, ub, w_branch_b, w_out_b, ln_g4, ln_b4, l, n_ctx_tiles)
        x = _ffn_call(x, mod4, w_gu_b, w_down_b, ln_g4, ln_b4, l, 1, n_ctx_tiles)

    y_prompt = x[:t_ctx].reshape(bsz, seq, d)
    y_sample = x[t_ctx:].reshape(dbsz, dseq, d)
    return (y_prompt, y_sample, jnp.stack(st_hg, axis=1), jnp.stack(st_c, axis=1), jnp.stack(st_n, axis=1),
            jnp.stack(st_m, axis=1), jnp.stack(st_h, axis=1))
```

```python
import functools
import math

import jax
import jax.numpy as jnp
from jax import lax
from jax.experimental import pallas as pl
from jax.experimental.pallas import tpu as pltpu

F32 = jnp.float32
BF16 = jnp.bfloat16

D_MODEL = 1024
DEPTH = 4
GRID_W = 64
MIX_W = 512
HG_HEADS = 4
HG_DK = 128
ML_HEADS = 4
ML_DK = 128
MB_HEADS = 8
MB_HEADDIM = 64
MB_GROUPS = 2
MB_DSTATE = 64
MB_XBC = MIX_W + 2 * MB_GROUPS * MB_DSTATE
D_FF = 2816
N_MOD = 9
ALPHA = (2 * DEPTH) ** 0.25
LN_EPS = 1e-5
RMS_EPS = 1e-6
LOG2E = 1.4426950408889634

LANES = 128
CH = 256
HALF = CH // 2
ROW_TILE = 1024
FF_TILE = 256
VMEM_LIMIT = 56 * 1024 * 1024

UB_SLABS = 56
UF_SLABS = 15
GATE_SLAB = 14
N_SEL = 32
NEG_BIG = -1e30


def _dot(a, b):
    return jnp.dot(a, b, preferred_element_type=F32)


def _dot_nt(a, b):
    return lax.dot_general(a, b, (((1,), (1,)), ((), ())), preferred_element_type=F32)


def _dot_tn(a, b):
    return lax.dot_general(a, b, (((0,), (0,)), ((), ())), preferred_element_type=F32)


def _sigmoid(x):
    return 1.0 / (1.0 + jnp.exp(-x))


def _silu(x):
    return x * _sigmoid(x)


def _log_sigmoid(x):
    return jnp.minimum(x, 0.0) - jnp.log(1.0 + jnp.exp(-jnp.abs(x)))


def _softplus(x):
    return jnp.maximum(x, 0.0) + jnp.log(1.0 + jnp.exp(-jnp.abs(x)))


def _neg_abs(x):
    return lax.bitcast_convert_type(lax.bitcast_convert_type(x, jnp.uint32) | jnp.uint32(0x80000000), F32)


def _ln(z, g, b):
    mu = jnp.mean(z, -1, keepdims=True)
    d = z - mu
    var = jnp.mean(d * d, -1, keepdims=True)
    return d * lax.rsqrt(var + LN_EPS) * g + b


def _split3(x):
    h = x.astype(BF16)
    r = x - h.astype(F32)
    m = r.astype(BF16)
    l = (r - m.astype(F32)).astype(BF16)
    return h, m, l


def _cumsum_rows(tril, x):
    h, m, l = _split3(x)
    return _dot(tril, h) + _dot(tril, m) + _dot(tril, l)


def _cumsum_lanes(x, triu):
    h, m, l = _split3(x)
    return _dot(h, triu) + _dot(m, triu) + _dot(l, triu)


def _cummax_lanes(x, reverse):
    n = x.shape[1]
    lane = lax.broadcasted_iota(jnp.int32, x.shape, 1)
    k = 1
    while k < n:
        if reverse:
            sh = jnp.where(lane < n - k, pltpu.roll(x, n - k, 1), -jnp.inf)
        else:
            sh = jnp.where(lane >= k, pltpu.roll(x, k, 1), -jnp.inf)
        x = jnp.maximum(x, sh)
        k *= 2
    return x


def _tri(c, lower):
    r = lax.broadcasted_iota(jnp.int32, (c, c), 0)
    s = lax.broadcasted_iota(jnp.int32, (c, c), 1)
    return jnp.where((s <= r) if lower else (r <= s), 1.0, 0.0).astype(BF16)


def _sel_lhs(x_rows):
    h = x_rows.astype(BF16).astype(F32)
    r = x_rows - h
    m = r.astype(BF16).astype(F32)
    rows = jnp.concatenate([h, m, r - m, jnp.ones_like(x_rows)], axis=0)
    return rows.T.astype(BF16)


def _selectors():
    r = jnp.arange(LANES)
    ch = jnp.arange(N_SEL)
    hit = ((r[None, :] % N_SEL) == ch[:, None]) & (r[None, :] < 3 * N_SEL)
    return jnp.broadcast_to(hit[:, :, None], (N_SEL, LANES, LANES)).astype(BF16)


def _cparams(sem):
    return pltpu.CompilerParams(dimension_semantics=sem, vmem_limit_bytes=VMEM_LIMIT)


def _chunk_rows(ci):
    r0 = ci * CH
    if not isinstance(r0, int):
        r0 = pl.multiple_of(r0, CH)
    return pl.ds(r0, CH)


def _chunk_loop(nchunk, run):
    if nchunk == 1:
        run(0, (0, 1))
    else:
        def body(ci, carry):
            run(ci, (0,))
            run(nchunk - 1 - ci, (1,))
            return carry
        lax.fori_loop(0, nchunk, body, 0)


def _mod_kernel(c_ref, w_ref, b_ref, o_ref):
    a = _silu(c_ref[...]).astype(BF16)
    o_ref[...] = _dot(a, w_ref[...].astype(BF16)) + b_ref[...]


def _mod_call(cv, w_mod, b_mod):
    depth, d, e = w_mod.shape
    tn = 1024
    return pl.pallas_call(
        _mod_kernel,
        grid=(depth, e // tn),
        in_specs=[pl.BlockSpec((8, d), lambda l, j: (0, 0)),
                  pl.BlockSpec((None, d, tn), lambda l, j: (l, 0, j)),
                  pl.BlockSpec((None, 1, tn), lambda l, j: (l, 0, j))],
        out_specs=pl.BlockSpec((None, 8, tn), lambda l, j: (l, 0, j)),
        out_shape=jax.ShapeDtypeStruct((depth, 8, e), F32),
        compiler_params=_cparams(("arbitrary", "arbitrary")),
        name="mod",
    )(cv, w_mod, b_mod.reshape(depth, 1, e))


def _mod_row(i, n_ctx_tiles):
    return jnp.maximum(i - (n_ctx_tiles - 1), 0)


def _ffn_kernel(x_ref, mod_ref, wa_ref, wb_ref, wd_ref, g_ref, b_ref, o_ref, xm_s, acc_s, *, nf):
    j = pl.program_id(1)
    d = x_ref.shape[1]

    @pl.when(j == 0)
    def _():
        sh = mod_ref[:, 0:d]
        sc = mod_ref[:, d:2 * d]
        xm_s[...] = (x_ref[...] * (1.0 + sc) + sh).astype(BF16)
        acc_s[...] = jnp.zeros_like(acc_s)

    xm = xm_s[...]
    a = _dot(xm, wa_ref[...])
    b = _dot(xm, wb_ref[...])
    h = (_silu(a) * b).astype(BF16)
    acc_s[...] += _dot(h, wd_ref[...])

    @pl.when(j == nf - 1)
    def _():
        gate = mod_ref[:, 2 * d:3 * d]
        z = ALPHA * x_ref[...] + 0.5 * gate * acc_s[...]
        o_ref[...] = _ln(z, g_ref[...], b_ref[...])


def _ffn_call(x, mod4, w_gu, w_down, ln_g, ln_b, layer, which, n_ctx_tiles):
    t, d = x.shape
    nf = D_FF // FF_TILE
    sub = 0 if which == 0 else 2
    return pl.pallas_call(
        functools.partial(_ffn_kernel, nf=nf),
        grid=(t // ROW_TILE, nf),
        in_specs=[pl.BlockSpec((ROW_TILE, d), lambda i, j: (i, 0)),
                  pl.BlockSpec((None, None, 1, 3 * d), lambda i, j: (layer, _mod_row(i, n_ctx_tiles), 0, sub)),
                  pl.BlockSpec((None, None, d, FF_TILE), lambda i, j: (layer, which, 0, j)),
                  pl.BlockSpec((None, None, d, FF_TILE), lambda i, j: (layer, which, 0, j + nf)),
                  pl.BlockSpec((None, None, FF_TILE, d), lambda i, j: (layer, which, j, 0)),
                  pl.BlockSpec((None, None, 1, d), lambda i, j: (layer, sub, 0, 0)),
                  pl.BlockSpec((None, None, 1, d), lambda i, j: (layer, sub, 0, 0))],
        out_specs=pl.BlockSpec((ROW_TILE, d), lambda i, j: (i, 0)),
        out_shape=jax.ShapeDtypeStruct((t, d), F32),
        scratch_shapes=[pltpu.VMEM((ROW_TILE, d), BF16), pltpu.VMEM((ROW_TILE, d), F32)],
        compiler_params=_cparams(("arbitrary", "arbitrary")),
        name=f"ffn{which}",
    )(x, mod4, w_gu, w_gu, w_down, ln_g, ln_b)


def _inproj_kernel(x_ref, mod_ref, w_ref, o_ref, xm_s):
    d = x_ref.shape[1]

    @pl.when(pl.program_id(1) == 0)
    def _():
        sh = mod_ref[:, 0:d]
        sc = mod_ref[:, d:2 * d]
        xm_s[...] = (x_ref[...] * (1.0 + sc) + sh).astype(BF16)

    res = _dot(xm_s[...], w_ref[...]).astype(o_ref.dtype)
    for k in range(o_ref.shape[0]):
        o_ref[k] = res[:, k * LANES:(k + 1) * LANES]


def _inproj_call(x, mod4, w, layer, tn, out_dtype, n_ctx_tiles, name):
    t, d = x.shape
    n = w.shape[2]
    return pl.pallas_call(
        _inproj_kernel,
        grid=(t // ROW_TILE, n // tn),
        in_specs=[pl.BlockSpec((ROW_TILE, d), lambda i, j: (i, 0)),
                  pl.BlockSpec((None, None, 1, 3 * d), lambda i, j: (layer, _mod_row(i, n_ctx_tiles), 0, 1)),
                  pl.BlockSpec((None, d, tn), lambda i, j: (layer, 0, j))],
        out_specs=pl.BlockSpec((tn // LANES, ROW_TILE, LANES), lambda i, j: (j, i, 0)),
        out_shape=jax.ShapeDtypeStruct((n // LANES, t, LANES), out_dtype),
        scratch_shapes=[pltpu.VMEM((ROW_TILE, d), BF16)],
        compiler_params=_cparams(("arbitrary", "arbitrary")),
        name=name,
    )(x, mod4, w)


def _slabs(ref):
    return jnp.concatenate([ref[k] for k in range(ref.shape[0])], axis=1)


def _mixout_kernel(x_ref, mod_ref, yh_ref, ym_ref, ys_ref, g0_ref, g1_ref, g2_ref, wb_ref, wo_ref,
                   lg_ref, lb_ref, o_ref):
    d = x_ref.shape[1]
    p = _sigmoid(_slabs(g0_ref).astype(F32)) * _dot(_slabs(yh_ref), wb_ref[0])
    p += _sigmoid(_slabs(g1_ref).astype(F32)) * _dot(_slabs(ym_ref), wb_ref[1])
    p += _sigmoid(_slabs(g2_ref).astype(F32)) * _dot(_slabs(ys_ref), wb_ref[2])
    y = _dot(p.astype(BF16), wo_ref[...])
    gate = mod_ref[:, 2 * d:3 * d]
    z = ALPHA * x_ref[...] + gate * y
    o_ref[...] = _ln(z, lg_ref[...], lb_ref[...])


def _mixout_call(x, mod4, y_hg, y_ml, y_mb, ub, w_branch, w_out, ln_g, ln_b, layer, n_ctx_tiles):
    t, d = x.shape
    tm = 512
    per = ROW_TILE // tm
    ns = MIX_W // LANES
    ng = d // LANES
    g0 = (UB_SLABS - 3 * ng) // ng
    y_spec = pl.BlockSpec((ns, tm, LANES), lambda i: (0, i, 0))
    return pl.pallas_call(
        _mixout_kernel,
        grid=(t // tm,),
        in_specs=[pl.BlockSpec((tm, d), lambda i: (i, 0)),
                  pl.BlockSpec((None, None, 1, 3 * d), lambda i: (layer, _mod_row(i // per, n_ctx_tiles), 0, 1)),
                  y_spec, y_spec, y_spec,
                  pl.BlockSpec((ng, tm, LANES), lambda i: (g0, i, 0)),
                  pl.BlockSpec((ng, tm, LANES), lambda i: (g0 + 1, i, 0)),
                  pl.BlockSpec((ng, tm, LANES), lambda i: (g0 + 2, i, 0)),
                  pl.BlockSpec((None, 3, MIX_W, d), lambda i: (layer, 0, 0, 0)),
                  pl.BlockSpec((None, d, d), lambda i: (layer, 0, 0)),
                  pl.BlockSpec((None, None, 1, d), lambda i: (layer, 1, 0, 0)),
                  pl.BlockSpec((None, None, 1, d), lambda i: (layer, 1, 0, 0))],
        out_specs=pl.BlockSpec((tm, d), lambda i: (i, 0)),
        out_shape=jax.ShapeDtypeStruct((t, d), F32),
        compiler_params=_cparams(("arbitrary",)),
        name="mixout",
    )(x, mod4, y_hg, y_ml, y_mb, ub, ub, ub, w_branch, w_out, ln_g, ln_b)


N_LEVEL = int(math.log2(HALF))


def _block_ref_rows(c, w):
    n_rows = c.shape[0]
    if 2 * w == n_rows:
        return jnp.broadcast_to(c[w - 1:w, :], c.shape)
    if w >= 4:
        n = n_rows // (2 * w)
        c3 = c.reshape(n, 2 * w, LANES)
        return jnp.broadcast_to(c3[:, w - 1:w, :], c3.shape).reshape(n_rows, LANES)
    c8 = c.reshape(n_rows // 8, 8, LANES)
    sub = lax.broadcasted_iota(jnp.int32, c8.shape, 1)
    if w == 2:
        r = jnp.where(sub < 4, c8[:, 1:2, :], c8[:, 5:6, :])
    else:
        r = jnp.where(sub < 2, c8[:, 0:1, :],
                      jnp.where(sub < 4, c8[:, 2:3, :], jnp.where(sub < 6, c8[:, 4:5, :], c8[:, 6:7, :])))
    return r.reshape(n_rows, LANES)


def _level_ids():
    r = lax.broadcasted_iota(jnp.int32, (HALF, HALF), 0)
    s = lax.broadcasted_iota(jnp.int32, (HALF, HALF), 1)
    lv = 31 - lax.clz(r ^ s)
    lv = jnp.where(r == s, N_LEVEL, lv)
    return jnp.where(r >= s, lv, -1), jnp.where(r <= s, lv, -1)


def _hgrn_chunk(q, v, fpre, loglb, log1mlb, onemlb, s_t, tril, lvl, backward, has_init):
    u = jnp.exp(-jnp.abs(fpre))
    x2 = log1mlb + (jnp.minimum(fpre, 0.0) - jnp.log(1.0 + u))
    logf = jnp.maximum(loglb, x2) + jnp.log(1.0 + jnp.exp(-jnp.abs(loglb - x2)))
    kk = onemlb * (jnp.where(fpre >= 0.0, u, 1.0) / (1.0 + u))
    lf2 = logf * LOG2E
    cum = _cumsum_rows(tril, lf2)
    pos = (cum - lf2) if backward else cum

    def sides(w):
        e = jnp.exp2(_neg_abs(pos - _block_ref_rows(cum, w)))
        return (q * e).astype(BF16), (kk * e).astype(BF16)

    zq, zk = sides(HALF)
    if backward:
        cross = _dot_nt(zq[0:HALF], zk[HALF:CH])
    else:
        cross = _dot_nt(zq[HALF:CH], zk[0:HALF])
    qb, kb = q.astype(BF16), kk.astype(BF16)
    diag = []
    for b in range(2):
        rows = slice(b * HALF, (b + 1) * HALF)
        a = jnp.where(lvl == N_LEVEL, _dot_nt(qb[rows], kb[rows]), 0.0)
        diag.append(a)
    w = HALF // 2
    while w >= 1:
        zq, zk = sides(w)
        lv = int(math.log2(w))
        for b in range(2):
            rows = slice(b * HALF, (b + 1) * HALF)
            diag[b] = jnp.where(lvl == lv, _dot_nt(zq[rows], zk[rows]), diag[b])
        w //= 2
    d0, d1, cr = diag[0].astype(BF16), diag[1].astype(BF16), cross.astype(BF16)
    if backward:
        o_lo = _dot(jnp.concatenate([d0, cr], axis=1), v)
        o_hi = _dot(d1, v[HALF:CH])
    else:
        o_lo = _dot(d0, v[0:HALF])
        o_hi = _dot(jnp.concatenate([cr, d1], axis=1), v)
    o = jnp.concatenate([o_lo, o_hi], axis=0)
    c_last = cum[CH - 1:CH, :]
    k_out = kk * jnp.exp2((pos) if backward else (c_last - cum))
    s_new = _dot_tn(v, k_out.astype(BF16))
    if has_init:
        q_in = q * jnp.exp2((c_last - pos) if backward else cum)
        o = o + _dot_nt(q_in.astype(BF16), s_t.astype(BF16))
        s_new = s_new + s_t * jnp.exp2(c_last)
    return o, s_new


def _hgrn_kernel(q_ref, v_ref, g_ref, f_ref, lbp_ref, ng_ref, *rest, seq_len, is_ctx):
    if is_ctx:
        y_ref, sfin_ref, oacc, st = rest
    else:
        s0_ref, y_ref, oacc, st = rest
    nchunk = seq_len // CH
    tril = _tri(CH, True)
    lvl_f, lvl_b = _level_ids()

    def head(h, carry):
        if not is_ctx:
            for d in range(2):
                st[d] = s0_ref[d, h].T
            if nchunk > 1:
                oacc[...] = jnp.zeros_like(oacc)

        def run(ci, dirs):
            rows = _chunk_rows(ci)
            q = _silu(q_ref[h, rows, :].astype(F32))
            v = v_ref[h, rows, :]
            for d in dirs:
                lrow = pl.ds(d * HG_HEADS + h, 1)
                o, s_new = _hgrn_chunk(q, v, f_ref[d * HG_HEADS + h, rows, :],
                                       lbp_ref[0, lrow, :], lbp_ref[1, lrow, :], lbp_ref[2, lrow, :],
                                       None if is_ctx else st[d], tril, lvl_b if d else lvl_f, d == 1, not is_ctx)
                st[d] = s_new
                if nchunk == 1 and d == 0:
                    oacc[rows, :] = o
                else:
                    oacc[rows, :] += o

        _chunk_loop(nchunk, run)
        o = oacc[...]
        y = o * lax.rsqrt(jnp.mean(o * o, -1, keepdims=True) + RMS_EPS) * ng_ref[pl.ds(h, 1), :]
        y_ref[h] = (y * _silu(g_ref[h].astype(F32))).astype(y_ref.dtype)
        if is_ctx:
            for d in range(2):
                sfin_ref[d, h] = st[d].T
        return carry

    lax.fori_loop(0, HG_HEADS, head, 0)


def _hgrn_call(ub, uf, lbp, ng, s0, y_prev, *, seq_len, row0, nseq):
    is_ctx = s0 is None
    t = ub.shape[1]
    bo = row0 // seq_len
    nh = HG_HEADS
    st_spec = pl.BlockSpec((None, 2, nh, HG_DK, LANES), lambda i: (i, 0, 0, 0, 0))
    in_specs = [pl.BlockSpec((nh, seq_len, LANES), lambda i: (0, bo + i, 0)),
                pl.BlockSpec((nh, seq_len, LANES), lambda i: (1, bo + i, 0)),
                pl.BlockSpec((nh, seq_len, LANES), lambda i: (2, bo + i, 0)),
                pl.BlockSpec((2 * nh, seq_len, LANES), lambda i: (0, bo + i, 0)),
                pl.BlockSpec((3, 2 * nh, LANES), lambda i: (0, 0, 0)),
                pl.BlockSpec((nh, LANES), lambda i: (0, 0))]
    args = [ub, ub, ub, uf, lbp, ng]
    y_spec = pl.BlockSpec((nh, seq_len, LANES), lambda i: (0, bo + i, 0))
    y_shape = jax.ShapeDtypeStruct((nh, t, LANES), BF16)
    aliases = {}
    if is_ctx:
        out_specs, out_shape = [y_spec, st_spec], [y_shape, jax.ShapeDtypeStruct((nseq, 2, nh, HG_DK, LANES), F32)]
    else:
        in_specs += [st_spec, pl.BlockSpec(memory_space=pl.ANY)]
        args += [s0, y_prev]
        aliases = {len(args) - 1: 0}
        out_specs, out_shape = [y_spec], [y_shape]
    n_used = len(args) - (0 if is_ctx else 1)

    def body(*refs):
        _hgrn_kernel(*refs[:n_used], *refs[len(args):], seq_len=seq_len, is_ctx=is_ctx)

    res = pl.pallas_call(
        body, grid=(nseq,), in_specs=in_specs, out_specs=out_specs, out_shape=out_shape,
        input_output_aliases=aliases,
        scratch_shapes=[pltpu.VMEM((seq_len, LANES), F32), pltpu.VMEM((2, LANES, HG_DK), F32)],
        compiler_params=_cparams(("arbitrary",)),
        name="hgrn_ctx" if is_ctx else "hgrn_lat",
    )(*args)
    return res if is_ctx else (res[0], None)


def _mlstm_chunk(q_ref, k_ref, v_ref, gt_ref, bias_ref, sel_ref, rows, dirs, cn_s, ms_s, oacc, triu, masks,
                 has_init, first_write):
    gates = gt_ref[rows, :] + bias_ref[...]
    g_t = gates.T
    i_rows = g_t[0:8, :]
    lf_rows = _log_sigmoid(g_t[8:16, :])
    c_rows = _cumsum_lanes(lf_rows, triu)
    is_b = lax.broadcasted_iota(jnp.int32, (8, CH), 0) >= ML_HEADS
    is_b1 = is_b[:, 0:1]
    m_in = ms_s[:, 0:1]
    c_last = c_rows[:, CH - 1:CH]
    pos = jnp.where(is_b, c_rows - lf_rows, c_rows)
    u = jnp.where(is_b, pos + i_rows, i_rows - pos)
    nu = jnp.where(is_b1, c_last + m_in, m_in)
    mu = jnp.maximum(jnp.where(is_b, _cummax_lanes(u, True), _cummax_lanes(u, False)), nu)
    m_t = jnp.where(is_b, mu - pos, pos + mu)
    mu_end = jnp.where(is_b1, mu[:, 0:1], mu[:, CH - 1:CH])
    m_new = jnp.where(is_b1, mu[:, 0:1], m_t[:, CH - 1:CH])
    x_rows = jnp.concatenate([-mu, -m_t, u, jnp.zeros_like(u)], axis=0) * LOG2E
    lhs = _sel_lhs(x_rows)
    scale = ML_DK ** -0.5
    ones_blk = jnp.ones((CH, LANES), BF16)
    for h in range(ML_HEADS):
        q = q_ref[h, rows, :]
        kf = k_ref[h, rows, :].astype(F32) * scale
        vaug = jnp.concatenate([v_ref[h, rows, :], ones_blk], axis=1)
        qk = _dot_nt(q, kf.astype(BF16))
        hsum = None
        for d in dirs:
            hd = d * ML_HEADS + h
            b_mu = _dot(lhs, sel_ref[hd])
            b_m = _dot(lhs, sel_ref[8 + hd])
            b_u = _dot(lhs, sel_ref[16 + hd])
            xe = jnp.concatenate([b_mu, b_mu], axis=1) + x_rows[16 + hd:17 + hd, :]
            s = jnp.where(masks[d], jnp.exp2(xe), 0.0) * qk
            num = _dot(s.astype(BF16), vaug)
            kw = kf * jnp.exp2(b_u - mu_end[hd:hd + 1, :] * LOG2E)
            upd = _dot_tn(kw.astype(BF16), vaug)
            if has_init:
                cn = cn_s[d, h]
                w_int = jnp.exp2(nu[hd:hd + 1, :] * LOG2E + b_mu)
                num = num + jnp.concatenate([w_int, w_int], axis=1) * _dot(q, cn.astype(BF16))
                upd = upd + jnp.exp2((nu[hd:hd + 1, :] - mu_end[hd:hd + 1, :]) * LOG2E) * cn
            cn_s[d, h] = upd
            den = jnp.maximum(jnp.abs(num[:, LANES:]), jnp.exp2(b_m))
            hout = num[:, :LANES] / den
            hsum = hout if hsum is None else hsum + hout
        if first_write:
            oacc[h, rows, :] = hsum
        else:
            oacc[h, rows, :] += hsum
    m_new = jnp.broadcast_to(m_new, (8, LANES))
    if len(dirs) == 2:
        ms_s[...] = m_new
    else:
        row = lax.broadcasted_iota(jnp.int32, (8, LANES), 0)
        mine = (row >= ML_HEADS) if dirs[0] == 1 else (row < ML_HEADS)
        ms_s[...] = jnp.where(mine, m_new, ms_s[...])


def _mlstm_kernel(q_ref, k_ref, v_ref, og_ref, gt_ref, bias_ref, ng_ref, sel_ref, *rest, seq_len, is_ctx):
    if is_ctx:
        y_ref, cfin_ref, nfin_ref, mfin_ref, oacc, cn_s, ms_s = rest
    else:
        c0_ref, n0_ref, m0_ref, y_ref, oacc, cn_s, ms_s = rest
    nchunk = seq_len // CH
    triu = _tri(CH, False)
    rr = lax.broadcasted_iota(jnp.int32, (CH, CH), 0)
    ss = lax.broadcasted_iota(jnp.int32, (CH, CH), 1)
    masks = (ss <= rr, ss >= rr)
    if is_ctx:
        ms_s[...] = jnp.zeros_like(ms_s)
    else:
        ms_s[...] = m0_ref[...]
        for d in range(2):
            for h in range(ML_HEADS):
                nb = jnp.broadcast_to(n0_ref[d * ML_HEADS + h:d * ML_HEADS + h + 1, :], (ML_DK, LANES)).T
                cn_s[d, h] = jnp.concatenate([c0_ref[d, h], nb], axis=1)
        if nchunk > 1:
            oacc[...] = jnp.zeros_like(oacc)

    def run(ci, dirs):
        _mlstm_chunk(q_ref, k_ref, v_ref, gt_ref, bias_ref, sel_ref, _chunk_rows(ci), dirs, cn_s, ms_s, oacc, triu,
                     masks, not is_ctx, nchunk == 1)

    _chunk_loop(nchunk, run)
    for h in range(ML_HEADS):
        o = oacc[h]
        y = o * lax.rsqrt(jnp.mean(o * o, -1, keepdims=True) + RMS_EPS) * ng_ref[h:h + 1, :]
        y_ref[h] = (y * _sigmoid(og_ref[h].astype(F32))).astype(y_ref.dtype)
    if is_ctx:
        for d in range(2):
            for h in range(ML_HEADS):
                cn = cn_s[d, h]
                cfin_ref[d, h] = cn[:, :LANES]
                nfin_ref[pl.ds(d * ML_HEADS + h, 1), :] = cn[:, LANES:].T[0:1, :]
        mfin_ref[...] = ms_s[...]


def _mlstm_call(ub, uf, bias_row, ng, sel, c0, n0, m0, y_prev, *, seq_len, row0, nseq):
    is_ctx = c0 is None
    t = ub.shape[1]
    bo = row0 // seq_len
    nh = ML_HEADS
    st_spec = pl.BlockSpec((None, 2, nh, ML_DK, LANES), lambda i: (i, 0, 0, 0, 0))
    vec_spec = pl.BlockSpec((None, 2 * nh, LANES), lambda i: (i, 0, 0))
    slab = lambda k: pl.BlockSpec((nh, seq_len, LANES), lambda i: (k, bo + i, 0))
    in_specs = [slab(3), slab(4), slab(5), slab(6),
                pl.BlockSpec((None, seq_len, LANES), lambda i: (GATE_SLAB, bo + i, 0)),
                pl.BlockSpec((1, LANES), lambda i: (0, 0)),
                pl.BlockSpec((nh, LANES), lambda i: (0, 0)),
                pl.BlockSpec((N_SEL, LANES, LANES), lambda i: (0, 0, 0))]
    args = [ub, ub, ub, ub, uf, bias_row, ng, sel]
    y_spec = pl.BlockSpec((nh, seq_len, LANES), lambda i: (0, bo + i, 0))
    y_shape = jax.ShapeDtypeStruct((nh, t, LANES), BF16)
    aliases = {}
    if is_ctx:
        out_specs = [y_spec, st_spec, vec_spec, vec_spec]
        out_shape = [y_shape, jax.ShapeDtypeStruct((nseq, 2, nh, ML_DK, LANES), F32),
                     jax.ShapeDtypeStruct((nseq, 2 * nh, LANES), F32), jax.ShapeDtypeStruct((nseq, 2 * nh, LANES), F32)]
    else:
        in_specs += [st_spec, vec_spec, vec_spec, pl.BlockSpec(memory_space=pl.ANY)]
        args += [c0, n0, m0, y_prev]
        aliases = {len(args) - 1: 0}
        out_specs, out_shape = [y_spec], [y_shape]
    n_used = len(args) - (0 if is_ctx else 1)

    def body(*refs):
        _mlstm_kernel(*refs[:n_used], *refs[len(args):], seq_len=seq_len, is_ctx=is_ctx)

    res = pl.pallas_call(
        body, grid=(nseq,), in_specs=in_specs, out_specs=out_specs, out_shape=out_shape,
        input_output_aliases=aliases,
        scratch_shapes=[pltpu.VMEM((nh, seq_len, LANES), F32), pltpu.VMEM((2, nh, ML_DK, 2 * LANES), F32),
                        pltpu.VMEM((2 * nh, LANES), F32)],
        compiler_params=_cparams(("arbitrary",)),
        name="mlstm_ctx" if is_ctx else "mlstm_lat",
    )(*args)
    return res if is_ctx else (res[0], None, None, None)


N_PAIR = MB_HEADS // 2
PAIRS_PER_GROUP = N_PAIR // MB_GROUPS


def _ssd_chunk(xs_s, bcs_s, gt_ref, bias_ref, nega_ref, sel_ref, rows, dirs, hst, yacc, triu, masks,
               has_init, first_write):
    gates = gt_ref[rows, :] + bias_ref[...]
    g_t = gates.T
    dt_rows = _softplus(g_t[16:32, :])
    la_rows = dt_rows * nega_ref[...]
    c_rows = _cumsum_lanes(la_rows, triu)
    ldt = jnp.log(dt_rows)
    is_b = lax.broadcasted_iota(jnp.int32, (16, CH), 0) >= MB_HEADS
    pos = jnp.where(is_b, c_rows - la_rows, c_rows)
    a_col = jnp.where(is_b, -pos, pos) * LOG2E
    r_row = jnp.where(is_b, pos + ldt, ldt - pos) * LOG2E
    c_last = c_rows[:, CH - 1:CH] * LOG2E
    lhs = _sel_lhs(jnp.concatenate([a_col, jnp.maximum(r_row, NEG_BIG)], axis=0))
    lane = lax.broadcasted_iota(jnp.int32, (CH, LANES), 1)
    lo = lane < MB_HEADDIM
    lo_state = lax.broadcasted_iota(jnp.int32, (LANES, LANES), 1) < MB_HEADDIM
    lo_row = lo_state[0:1, :]
    bblk = bcs_s[rows, 0:LANES]
    cblk = bcs_s[rows, LANES:2 * LANES]
    for j in range(N_PAIR):
        grp = j // PAIRS_PER_GROUP
        if j % PAIRS_PER_GROUP == 0:
            in_grp = (lane >= grp * MB_DSTATE) & (lane < (grp + 1) * MB_DSTATE)
            c_g = jnp.where(in_grp, cblk, 0.0)
            b_g = jnp.where(in_grp, bblk, 0.0)
            gm = _dot_nt(c_g.astype(BF16), b_g.astype(BF16))
        xpair = xs_s[rows, j * LANES:(j + 1) * LANES]
        xh = (jnp.where(lo, xpair, 0.0).astype(BF16), jnp.where(lo, 0.0, xpair).astype(BF16))
        ysum = None
        for d in dirs:
            ht = hst[d, j]
            y = None
            upd = None
            decay = None
            for half in range(2):
                hd = d * MB_HEADS + 2 * j + half
                cl = c_last[hd:hd + 1, :]
                b_a = _dot(lhs, sel_ref[hd])
                b_r = _dot(lhs, sel_ref[16 + hd])
                xe = jnp.concatenate([b_a, b_a], axis=1) + r_row[hd:hd + 1, :]
                m = jnp.where(masks[d], jnp.exp2(xe), 0.0) * gm
                yy = _dot(m.astype(BF16), xh[half])
                b_out = b_g * jnp.exp2(b_r if d else b_r + cl)
                u = _dot_tn(b_out.astype(BF16), xh[half])
                if has_init:
                    c_in = c_g * jnp.exp2(b_a + cl if d else b_a)
                    ht_half = jnp.where(lo_state, ht, 0.0) if half == 0 else jnp.where(lo_state, 0.0, ht)
                    yy = yy + _dot(c_in.astype(BF16), ht_half.astype(BF16))
                    dl = jnp.exp2(cl)
                    decay = dl if decay is None else jnp.where(lo_row, decay, dl)
                y = yy if y is None else y + yy
                upd = u if upd is None else upd + u
            hst[d, j] = (upd + ht * decay) if has_init else upd
            ysum = y if ysum is None else ysum + y
        if first_write:
            yacc[rows, j * LANES:(j + 1) * LANES] = ysum
        else:
            yacc[rows, j * LANES:(j + 1) * LANES] += ysum


def _ssd_kernel(x_ref, bc_ref, z_ref, gt_ref, cw_ref, bias_ref, nega_ref, dskip_ref, ng_ref, sel_ref, *rest,
                seq_len, is_ctx):
    if is_ctx:
        y_ref, hfin_ref, xs_s, bcs_s, yacc, hst = rest
    else:
        h0_ref, y_ref, xs_s, bcs_s, yacc, hst = rest
    nchunk = seq_len // CH
    triu = _tri(CH, False)
    rr = lax.broadcasted_iota(jnp.int32, (CH, CH), 0)
    ss = lax.broadcasted_iota(jnp.int32, (CH, CH), 1)
    masks = (ss <= rr, ss >= rr)

    def conv(v, lo_col):
        row = lax.broadcasted_iota(jnp.int32, v.shape, 0)
        prev = jnp.where(row == 0, 0.0, pltpu.roll(v, 1, 0))
        nxt = jnp.where(row == seq_len - 1, 0.0, pltpu.roll(v, seq_len - 1, 0))
        cs = slice(lo_col, lo_col + v.shape[1])
        return _silu(cw_ref[0:1, cs] * prev + cw_ref[1:2, cs] * v + cw_ref[2:3, cs] * nxt + cw_ref[3:4, cs])

    for k in range(MIX_W // LANES):
        xs_s[:, k * LANES:(k + 1) * LANES] = conv(x_ref[k], k * LANES)
    for k in range(2):
        bcs_s[:, k * LANES:(k + 1) * LANES] = conv(bc_ref[k], MIX_W + k * LANES)
    if not is_ctx:
        hst[...] = h0_ref[...]
        if nchunk > 1:
            yacc[...] = jnp.zeros_like(yacc)

    def run(ci, dirs):
        _ssd_chunk(xs_s, bcs_s, gt_ref, bias_ref, nega_ref, sel_ref, _chunk_rows(ci), dirs, hst, yacc, triu, masks,
                   not is_ctx, nchunk == 1)

    _chunk_loop(nchunk, run)
    y = (yacc[...] + dskip_ref[...] * xs_s[...]) * _silu(_slabs(z_ref).astype(F32))
    y = (y * lax.rsqrt(jnp.mean(y * y, -1, keepdims=True) + RMS_EPS) * ng_ref[...]).astype(y_ref.dtype)
    for k in range(MIX_W // LANES):
        y_ref[k] = y[:, k * LANES:(k + 1) * LANES]
    if is_ctx:
        hfin_ref[...] = hst[...]


def _ssd_call(ub, uf, cw, bias_row, nega_rows, dskip, ng, sel, h0, y_prev, *, seq_len, row0, nseq):
    is_ctx = h0 is None
    t = ub.shape[1]
    bo = row0 // seq_len
    ns = MIX_W // LANES
    st_spec = pl.BlockSpec((None, 2, N_PAIR, LANES, LANES), lambda i: (i, 0, 0, 0, 0))
    row_spec = lambda w: pl.BlockSpec((1, w), lambda i: (0, 0))
    in_specs = [pl.BlockSpec((ns, seq_len, LANES), lambda i: (2, bo + i, 0)),
                pl.BlockSpec((2, seq_len, LANES), lambda i: (6, bo + i, 0)),
                pl.BlockSpec((ns, seq_len, LANES), lambda i: (7, bo + i, 0)),
                pl.BlockSpec((None, seq_len, LANES), lambda i: (GATE_SLAB, bo + i, 0)),
                pl.BlockSpec((8, MB_XBC), lambda i: (0, 0)),
                row_spec(LANES),
                pl.BlockSpec((2 * MB_HEADS, CH), lambda i: (0, 0)),
                row_spec(MIX_W), row_spec(MIX_W),
                pl.BlockSpec((N_SEL, LANES, LANES), lambda i: (0, 0, 0))]
    args = [uf, uf, ub, uf, cw, bias_row, nega_rows, dskip, ng, sel]
    y_spec = pl.BlockSpec((ns, seq_len, LANES), lambda i: (0, bo + i, 0))
    y_shape = jax.ShapeDtypeStruct((ns, t, LANES), BF16)
    aliases = {}
    if is_ctx:
        out_specs = [y_spec, st_spec]
        out_shape = [y_shape, jax.ShapeDtypeStruct((nseq, 2, N_PAIR, LANES, LANES), F32)]
    else:
        in_specs += [st_spec, pl.BlockSpec(memory_space=pl.ANY)]
        args += [h0, y_prev]
        aliases = {len(args) - 1: 0}
        out_specs, out_shape = [y_spec], [y_shape]
    n_used = len(args) - (0 if is_ctx else 1)

    def body(*refs):
        _ssd_kernel(*refs[:n_used], *refs[len(args):], seq_len=seq_len, is_ctx=is_ctx)

    res = pl.pallas_call(
        body, grid=(nseq,), in_specs=in_specs, out_specs=out_specs, out_shape=out_shape,
        input_output_aliases=aliases,
        scratch_shapes=[pltpu.VMEM((seq_len, MIX_W), F32), pltpu.VMEM((seq_len, 2 * LANES), F32),
                        pltpu.VMEM((seq_len, MIX_W), F32), pltpu.VMEM((2, N_PAIR, LANES, LANES), F32)],
        compiler_params=_cparams(("arbitrary",)),
        name="ssd_ctx" if is_ctx else "ssd_lat",
    )(*args)
    return res if is_ctx else (res[0], None)


def _ssd_state_to_pairs(s):
    bsz = s.shape[0]
    st = jnp.swapaxes(s, -1, -2).reshape(bsz, 2, N_PAIR, 2, MB_DSTATE, MB_HEADDIM)
    st = jnp.moveaxis(st, 3, 4).reshape(bsz, 2, N_PAIR, MB_DSTATE, 2 * MB_HEADDIM)
    zero = jnp.zeros_like(st)
    grp = (jnp.arange(N_PAIR) // PAIRS_PER_GROUP).reshape(1, 1, N_PAIR, 1, 1)
    return jnp.concatenate([jnp.where(grp == 0, st, zero), jnp.where(grp == 1, st, zero)], axis=3)


def _ssd_pairs_to_state(hp):
    bsz = hp.shape[0]
    halves = hp.reshape(bsz, 2, N_PAIR, MB_GROUPS, MB_DSTATE, 2 * MB_HEADDIM)
    grp = (jnp.arange(N_PAIR) // PAIRS_PER_GROUP).reshape(1, 1, N_PAIR, 1, 1)
    st = jnp.where(grp == 0, halves[:, :, :, 0], halves[:, :, :, 1])
    st = st.reshape(bsz, 2, N_PAIR, MB_DSTATE, 2, MB_HEADDIM)
    st = jnp.moveaxis(st, 4, 3).reshape(bsz, 2, MB_HEADS, MB_DSTATE, MB_HEADDIM)
    return jnp.swapaxes(st, -1, -2)


def _grid_pos_embed(n_tok, d_model):
    rows = n_tok // GRID_W
    r, cidx = jnp.meshgrid(jnp.arange(rows, dtype=F32), jnp.arange(GRID_W, dtype=F32), indexing='ij')
    quarter = d_model // 4
    freq = jnp.exp(-math.log(10000.0) * jnp.arange(quarter, dtype=F32) / quarter)
    ar = r.reshape(-1, 1) * freq
    ac = cidx.reshape(-1, 1) * freq
    return jnp.concatenate([jnp.sin(ar), jnp.cos(ar), jnp.sin(ac), jnp.cos(ac)], axis=-1)


def _gate_row(pieces):
    v = jnp.concatenate([p.reshape(-1).astype(F32) for p in pieces])
    return jnp.concatenate([v, v, v, jnp.zeros((LANES - 3 * N_SEL,), F32)]).reshape(1, LANES)


def kernel(x_prompt, x_sample, state_hgrn, state_mlstm_C, state_mlstm_n, state_mlstm_m, state_ssd, c, c_ctx,
           w_mod, b_mod, ln_g, ln_b, ffn_w_gu, ffn_w_down, w_in, hg_lb, hg_norm_g, ml_gate_b, ml_norm_g,
           mb_conv_w, mb_conv_b, mb_dt_bias, mb_a_log, mb_d, mb_norm_g, w_branch, w_out):
    bsz, seq, d = x_prompt.shape
    dbsz, dseq, _ = x_sample.shape
    depth = w_mod.shape[0]
    t_ctx = bsz * seq
    n_ctx_tiles = t_ctx // ROW_TILE
    assert dseq == ROW_TILE and t_ctx % ROW_TILE == 0 and seq == CH and dseq % CH == 0

    xs0 = x_sample + _grid_pos_embed(dseq, d).astype(x_sample.dtype)[None]
    x = jnp.concatenate([x_prompt.reshape(t_ctx, d), xs0.reshape(dbsz * dseq, d)], axis=0)

    cv = jnp.concatenate([c_ctx[None], c, jnp.zeros((8 - 1 - dbsz, d), F32)], axis=0)
    mod4 = _mod_call(cv, w_mod, b_mod).reshape(depth, 8, 1, N_MOD * d)

    w_gu_b = ffn_w_gu.astype(BF16)
    w_down_b = ffn_w_down.astype(BF16)
    w_branch_b = w_branch.astype(BF16)
    w_out_b = w_out.astype(BF16)
    w_a = jnp.concatenate([w_in[:, :, 0:1536], w_in[:, :, 2560:4608], w_in[:, :, 4624:5136],
                           w_in[:, :, 5920:8992]], axis=2).astype(BF16)
    w_gate = jnp.concatenate([w_in[:, :, 4608:4624], w_in[:, :, 5904:5920]], axis=2)
    w_b = jnp.concatenate([w_in[:, :, 1536:2560], w_in[:, :, 5136:5904], w_gate, w_gate, w_gate,
                           jnp.zeros((depth, d, LANES - 3 * N_SEL), F32)], axis=2).astype(BF16)
    ln_g4 = ln_g.reshape(depth, 3, 1, d)
    ln_b4 = ln_b.reshape(depth, 3, 1, d)

    lbs = jnp.cumsum(jax.nn.softmax(hg_lb.astype(F32), axis=0), axis=0)
    lbs = (lbs - lbs[0]).reshape(depth, 2 * HG_HEADS, HG_DK)
    lbp = jnp.stack([jnp.log(lbs), jnp.log1p(-lbs), 1.0 - lbs], axis=1)
    cw = jnp.concatenate([mb_conv_w, mb_conv_b[:, None, :], jnp.zeros((depth, 4, MB_XBC), F32)], axis=1)
    sel = _selectors()

    lat_h0 = _ssd_state_to_pairs(state_ssd.reshape((dbsz * depth,) + state_ssd.shape[2:])).reshape(
        (dbsz, depth, 2, N_PAIR, LANES, LANES))
    lat_n0 = state_mlstm_n.reshape(dbsz, depth, 2 * ML_HEADS, ML_DK)
    lat_m0 = jnp.broadcast_to(state_mlstm_m.reshape(dbsz, depth, 2 * ML_HEADS, 1), (dbsz, depth, 2 * ML_HEADS, LANES))

    st_hg, st_c, st_n, st_m, st_h = [], [], [], [], []
    for l in range(depth):
        x = _ffn_call(x, mod4, w_gu_b, w_down_b, ln_g4, ln_b4, l, 0, n_ctx_tiles)
        ub = _inproj_call(x, mod4, w_a, l, 512, BF16, n_ctx_tiles, "inproj_a")
        uf = _inproj_call(x, mod4, w_b, l, 640, F32, n_ctx_tiles, "inproj_b")

        ng_h = hg_norm_g[l].reshape(HG_HEADS, LANES)
        y_hg, s_fin = _hgrn_call(ub, uf, lbp[l], ng_h, None, None, seq_len=seq, row0=0, nseq=bsz)
        y_hg, _ = _hgrn_call(ub, uf, lbp[l], ng_h, state_hgrn[:, l], y_hg, seq_len=dseq, row0=t_ctx, nseq=dbsz)
        st_hg.append(s_fin)

        gate_bias = _gate_row([ml_gate_b[l, 0], ml_gate_b[l, 1], mb_dt_bias[l]])
        ng_m = ml_norm_g[l].reshape(ML_HEADS, LANES)
        y_ml, c_fin, n_fin, m_fin = _mlstm_call(ub, uf, gate_bias, ng_m, sel, None, None, None, None,
                                                seq_len=seq, row0=0, nseq=bsz)
        y_ml, _, _, _ = _mlstm_call(ub, uf, gate_bias, ng_m, sel, state_mlstm_C[:, l], lat_n0[:, l], lat_m0[:, l],
                                    y_ml, seq_len=dseq, row0=t_ctx, nseq=dbsz)
        st_c.append(c_fin)
        st_n.append(n_fin.reshape(bsz, 2, ML_HEADS, ML_DK))
        st_m.append(m_fin[:, :, 0].reshape(bsz, 2, ML_HEADS))

        nega = jnp.broadcast_to(-jnp.exp(mb_a_log[l].astype(F32)).reshape(2 * MB_HEADS, 1), (2 * MB_HEADS, CH))
        dskip = jnp.repeat(mb_d[l], MB_HEADDIM).reshape(1, MIX_W)
        ng_s = mb_norm_g[l].reshape(1, MIX_W)
        y_mb, h_fin = _ssd_call(ub, uf, cw[l], gate_bias, nega, dskip, ng_s, sel, None, None,
                                seq_len=seq, row0=0, nseq=bsz)
        y_mb, _ = _ssd_call(ub, uf, cw[l], gate_bias, nega, dskip, ng_s, sel, lat_h0[:, l], y_mb,
                            seq_len=dseq, row0=t_ctx, nseq=dbsz)
        st_h.append(_ssd_pairs_to_state(h_fin))

        x = _mixout_call(x, mod4, y_hg, y_ml, y_mb, ub, w_branch_b, w_out_b, ln_g4, ln_b4, l, n_ctx_tiles)
        x = _ffn_call(x, mod4, w_gu_b, w_down_b, ln_g4, ln_b4, l, 1, n_ctx_tiles)

    y_prompt = x[:t_ctx].reshape(bsz, seq, d)
    y_sample = x[t_ctx:].reshape(dbsz, dseq, d)
    return (y_prompt, y_sample, jnp.stack(st_hg, axis=1), jnp.stack(st_c, axis=1), jnp.stack(st_n, axis=1),
            jnp.stack(st_m, axis=1), jnp.stack(st_h, axis=1))
```

```python
import functools
import math

import jax
import jax.numpy as jnp
from jax import lax
from jax.experimental import pallas as pl
from jax.experimental.pallas import tpu as pltpu

F32 = jnp.float32
BF16 = jnp.bfloat16

D_MODEL = 1024
DEPTH = 4
GRID_W = 64
MIX_W = 512
HG_HEADS = 4
HG_DK = 128
ML_HEADS = 4
ML_DK = 128
MB_HEADS = 8
MB_HEADDIM = 64
MB_GROUPS = 2
MB_DSTATE = 64
MB_XBC = MIX_W + 2 * MB_GROUPS * MB_DSTATE
D_FF = 2816
N_MOD = 9
ALPHA = (2 * DEPTH) ** 0.25
LN_EPS = 1e-5
RMS_EPS = 1e-6
LOG2E = 1.4426950408889634

LANES = 128
CH = 256
HALF = CH // 2
ROW_TILE = 1024
FF_TILE = 256
INPROJ_TILE_A = 1792
INPROJ_TILE_B = 640
CTX_SEQS_PER_STEP = 4
VMEM_LIMIT = 56 * 1024 * 1024

UB_SLABS = 56
UF_SLABS = 15
GATE_SLAB = 14
N_SEL = 32
LOG2_TINY = -150.0


def _dot(a, b):
    return jnp.dot(a, b, preferred_element_type=F32)


def _dot_nt(a, b):
    return lax.dot_general(a, b, (((1,), (1,)), ((), ())), preferred_element_type=F32)


def _dot_tn(a, b):
    return lax.dot_general(a, b, (((0,), (0,)), ((), ())), preferred_element_type=F32)


def _sigmoid(x):
    return 1.0 / (1.0 + jnp.exp(-x))


def _silu(x):
    return x * _sigmoid(x)


def _log_sigmoid(x):
    return jnp.minimum(x, 0.0) - jnp.log(1.0 + jnp.exp(-jnp.abs(x)))


def _softplus(x):
    return jnp.maximum(x, 0.0) + jnp.log(1.0 + jnp.exp(-jnp.abs(x)))


def _neg_abs(x):
    return -jnp.abs(x)


def _ln(z, g, b):
    mu = jnp.mean(z, -1, keepdims=True)
    d = z - mu
    var = jnp.mean(d * d, -1, keepdims=True)
    return d * lax.rsqrt(var + LN_EPS) * g + b


def _split3(x):
    h = x.astype(BF16)
    r = x - h.astype(F32)
    m = r.astype(BF16)
    l = (r - m.astype(F32)).astype(BF16)
    return h, m, l


def _cumsum_rows(tril, x):
    h, m, l = _split3(x)
    return _dot(tril, h) + _dot(tril, m) + _dot(tril, l)


def _cumsum_lanes(x, triu):
    h, m, l = _split3(x)
    return _dot(h, triu) + _dot(m, triu) + _dot(l, triu)


def _cummax_lanes(x, reverse):
    n = x.shape[1]
    lane = lax.broadcasted_iota(jnp.int32, x.shape, 1)
    k = 1
    while k < n:
        if reverse:
            sh = jnp.where(lane < n - k, pltpu.roll(x, n - k, 1), -jnp.inf)
        else:
            sh = jnp.where(lane >= k, pltpu.roll(x, k, 1), -jnp.inf)
        x = jnp.maximum(x, sh)
        k *= 2
    return x


def _tri(c, lower):
    r = lax.broadcasted_iota(jnp.int32, (c, c), 0)
    s = lax.broadcasted_iota(jnp.int32, (c, c), 1)
    return jnp.where((s <= r) if lower else (r <= s), 1.0, 0.0).astype(BF16)


def _sel_lhs(x_rows):
    h = x_rows.astype(BF16).astype(F32)
    r = x_rows - h
    m = r.astype(BF16).astype(F32)
    rows = jnp.concatenate([h, m, r - m, jnp.ones_like(x_rows)], axis=0)
    return rows.T.astype(BF16)


def _selectors():
    r = jnp.arange(LANES)
    ch = jnp.arange(N_SEL)
    hit = ((r[None, :] % N_SEL) == ch[:, None]) & (r[None, :] < 3 * N_SEL)
    return jnp.broadcast_to(hit[:, :, None], (N_SEL, LANES, LANES)).astype(BF16)


def _cparams(sem):
    return pltpu.CompilerParams(dimension_semantics=sem, vmem_limit_bytes=VMEM_LIMIT)


def _chunk_rows(ci, base=0):
    r0 = ci * CH + base
    if not isinstance(r0, int):
        r0 = pl.multiple_of(r0, CH)
    return pl.ds(r0, CH)


def _state_io(shape_tail, nseq, nsq, layer, depth):
    nz = (0,) * len(shape_tail)
    spec = pl.BlockSpec((nsq, None) + shape_tail, lambda i: (i, layer) + nz)
    shape = jax.ShapeDtypeStruct((nseq, depth) + shape_tail, F32)
    return spec, shape


def _chunk_loop(nchunk, run):
    if nchunk == 1:
        run(0, (0, 1))
    else:
        def body(ci, carry):
            run(ci, (0,))
            run(nchunk - 1 - ci, (1,))
            return carry
        lax.fori_loop(0, nchunk, body, 0)


def _mod_kernel(c_ref, w_ref, b_ref, o_ref):
    a = _silu(c_ref[...]).astype(BF16)
    o_ref[...] = _dot(a, w_ref[...].astype(BF16)) + b_ref[...]


def _mod_call(cv, w_mod, b_mod):
    depth, d, e = w_mod.shape
    tn = 1024
    return pl.pallas_call(
        _mod_kernel,
        grid=(depth, e // tn),
        in_specs=[pl.BlockSpec((8, d), lambda l, j: (0, 0)),
                  pl.BlockSpec((None, d, tn), lambda l, j: (l, 0, j)),
                  pl.BlockSpec((None, 1, tn), lambda l, j: (l, 0, j))],
        out_specs=pl.BlockSpec((None, 8, tn), lambda l, j: (l, 0, j)),
        out_shape=jax.ShapeDtypeStruct((depth, 8, e), F32),
        compiler_params=_cparams(("arbitrary", "arbitrary")),
        name="mod",
    )(cv, w_mod, b_mod.reshape(depth, 1, e))


def _mod_row(i, n_ctx_tiles):
    return jnp.maximum(i - (n_ctx_tiles - 1), 0)


def _ffn_kernel(x_ref, mod_ref, wa_ref, wb_ref, wd_ref, g_ref, b_ref, o_ref, xm_s, acc_s, *, nf):
    j = pl.program_id(1)
    d = x_ref.shape[1]

    @pl.when(j == 0)
    def _():
        sh = mod_ref[:, 0:d]
        sc = mod_ref[:, d:2 * d]
        xm_s[...] = (x_ref[...] * (1.0 + sc) + sh).astype(BF16)
        acc_s[...] = jnp.zeros_like(acc_s)

    xm = xm_s[...]
    a = _dot(xm, wa_ref[...])
    b = _dot(xm, wb_ref[...])
    h = (_silu(a) * b).astype(BF16)
    acc_s[...] += _dot(h, wd_ref[...])

    @pl.when(j == nf - 1)
    def _():
        gate = mod_ref[:, 2 * d:3 * d]
        z = ALPHA * x_ref[...] + 0.5 * gate * acc_s[...]
        o_ref[...] = _ln(z, g_ref[...], b_ref[...])


def _ffn_call(x, mod4, w_gu, w_down, ln_g, ln_b, layer, which, n_ctx_tiles):
    t, d = x.shape
    nf = D_FF // FF_TILE
    assert nf * FF_TILE == D_FF
    sub = 0 if which == 0 else 2
    return pl.pallas_call(
        functools.partial(_ffn_kernel, nf=nf),
        grid=(t // ROW_TILE, nf),
        in_specs=[pl.BlockSpec((ROW_TILE, d), lambda i, j: (i, 0)),
                  pl.BlockSpec((None, None, 1, 3 * d), lambda i, j: (layer, _mod_row(i, n_ctx_tiles), 0, sub)),
                  pl.BlockSpec((None, None, d, FF_TILE), lambda i, j: (layer, which, 0, j)),
                  pl.BlockSpec((None, None, d, FF_TILE), lambda i, j: (layer, which, 0, j + nf)),
                  pl.BlockSpec((None, None, FF_TILE, d), lambda i, j: (layer, which, j, 0)),
                  pl.BlockSpec((None, None, 1, d), lambda i, j: (layer, sub, 0, 0)),
                  pl.BlockSpec((None, None, 1, d), lambda i, j: (layer, sub, 0, 0))],
        out_specs=pl.BlockSpec((ROW_TILE, d), lambda i, j: (i, 0)),
        out_shape=jax.ShapeDtypeStruct((t, d), F32),
        scratch_shapes=[pltpu.VMEM((ROW_TILE, d), BF16), pltpu.VMEM((ROW_TILE, d), F32)],
        compiler_params=_cparams(("arbitrary", "arbitrary")),
        name=f"ffn{which}",
    )(x, mod4, w_gu, w_gu, w_down, ln_g, ln_b)


def _inproj_kernel(x_ref, mod_ref, w_ref, o_ref, xm_s):
    d = x_ref.shape[1]

    @pl.when(pl.program_id(1) == 0)
    def _():
        sh = mod_ref[:, 0:d]
        sc = mod_ref[:, d:2 * d]
        xm_s[...] = (x_ref[...] * (1.0 + sc) + sh).astype(BF16)

    res = _dot(xm_s[...], w_ref[...]).astype(o_ref.dtype)
    for k in range(o_ref.shape[0]):
        o_ref[k] = res[:, k * LANES:(k + 1) * LANES]


def _inproj_call(x, mod4, w, layer, tn, out_dtype, n_ctx_tiles, name):
    t, d = x.shape
    n = w.shape[2]
    return pl.pallas_call(
        _inproj_kernel,
        grid=(t // ROW_TILE, n // tn),
        in_specs=[pl.BlockSpec((ROW_TILE, d), lambda i, j: (i, 0)),
                  pl.BlockSpec((None, None, 1, 3 * d), lambda i, j: (layer, _mod_row(i, n_ctx_tiles), 0, 1)),
                  pl.BlockSpec((None, d, tn), lambda i, j: (layer, 0, j))],
        out_specs=pl.BlockSpec((tn // LANES, ROW_TILE, LANES), lambda i, j: (j, i, 0)),
        out_shape=jax.ShapeDtypeStruct((n // LANES, t, LANES), out_dtype),
        scratch_shapes=[pltpu.VMEM((ROW_TILE, d), BF16)],
        compiler_params=_cparams(("arbitrary", "arbitrary")),
        name=name,
    )(x, mod4, w)


def _slabs(ref):
    return jnp.concatenate([ref[k] for k in range(ref.shape[0])], axis=1)


def _mixout_kernel(x_ref, mod_ref, yh_ref, ym_ref, ys_ref, g0_ref, g1_ref, g2_ref, wb_ref, wo_ref,
                   lg_ref, lb_ref, o_ref):
    d = x_ref.shape[1]
    p = _sigmoid(_slabs(g0_ref).astype(F32)) * _dot(_slabs(yh_ref), wb_ref[0])
    p += _sigmoid(_slabs(g1_ref).astype(F32)) * _dot(_slabs(ym_ref), wb_ref[1])
    p += _sigmoid(_slabs(g2_ref).astype(F32)) * _dot(_slabs(ys_ref), wb_ref[2])
    y = _dot(p.astype(BF16), wo_ref[...])
    gate = mod_ref[:, 2 * d:3 * d]
    z = ALPHA * x_ref[...] + gate * y
    o_ref[...] = _ln(z, lg_ref[...], lb_ref[...])


def _mixout_call(x, mod4, y_hg, y_ml, y_mb, ub, w_branch, w_out, ln_g, ln_b, layer, n_ctx_tiles):
    t, d = x.shape
    tm = 512
    per = ROW_TILE // tm
    ns = MIX_W // LANES
    ng = d // LANES
    g0 = (UB_SLABS - 3 * ng) // ng
    y_spec = pl.BlockSpec((ns, tm, LANES), lambda i: (0, i, 0))
    return pl.pallas_call(
        _mixout_kernel,
        grid=(t // tm,),
        in_specs=[pl.BlockSpec((tm, d), lambda i: (i, 0)),
                  pl.BlockSpec((None, None, 1, 3 * d), lambda i: (layer, _mod_row(i // per, n_ctx_tiles), 0, 1)),
                  y_spec, y_spec, y_spec,
                  pl.BlockSpec((ng, tm, LANES), lambda i: (g0, i, 0)),
                  pl.BlockSpec((ng, tm, LANES), lambda i: (g0 + 1, i, 0)),
                  pl.BlockSpec((ng, tm, LANES), lambda i: (g0 + 2, i, 0)),
                  pl.BlockSpec((None, 3, MIX_W, d), lambda i: (layer, 0, 0, 0)),
                  pl.BlockSpec((None, d, d), lambda i: (layer, 0, 0)),
                  pl.BlockSpec((None, None, 1, d), lambda i: (layer, 1, 0, 0)),
                  pl.BlockSpec((None, None, 1, d), lambda i: (layer, 1, 0, 0))],
        out_specs=pl.BlockSpec((tm, d), lambda i: (i, 0)),
        out_shape=jax.ShapeDtypeStruct((t, d), F32),
        compiler_params=_cparams(("arbitrary",)),
        name="mixout",
    )(x, mod4, y_hg, y_ml, y_mb, ub, ub, ub, w_branch, w_out, ln_g, ln_b)


N_LEVEL = int(math.log2(HALF))


def _block_ref_rows(c, w):
    n_rows = c.shape[0]
    if 2 * w == n_rows:
        return jnp.broadcast_to(c[w - 1:w, :], c.shape)
    if w >= 4:
        n = n_rows // (2 * w)
        c3 = c.reshape(n, 2 * w, LANES)
        return jnp.broadcast_to(c3[:, w - 1:w, :], c3.shape).reshape(n_rows, LANES)
    c8 = c.reshape(n_rows // 8, 8, LANES)
    sub = lax.broadcasted_iota(jnp.int32, c8.shape, 1)
    if w == 2:
        r = jnp.where(sub < 4, c8[:, 1:2, :], c8[:, 5:6, :])
    else:
        r = jnp.where(sub < 2, c8[:, 0:1, :],
                      jnp.where(sub < 4, c8[:, 2:3, :], jnp.where(sub < 6, c8[:, 4:5, :], c8[:, 6:7, :])))
    return r.reshape(n_rows, LANES)


def _level_ids():
    r = lax.broadcasted_iota(jnp.int32, (HALF, HALF), 0)
    s = lax.broadcasted_iota(jnp.int32, (HALF, HALF), 1)
    lv = 31 - lax.clz(r ^ s)
    lv = jnp.where(r == s, N_LEVEL, lv)
    return jnp.where(r >= s, lv, -1), jnp.where(r <= s, lv, -1)


def _hgrn_chunk(q, v, fpre, lb, onemlb, s_t, tril, lvl, backward, has_init):
    u = jnp.exp2(_neg_abs(fpre) * LOG2E)
    r = 1.0 / (1.0 + u)
    pos_side = fpre >= 0.0
    kk = onemlb * (jnp.where(pos_side, u, 1.0) * r)
    f = lb + onemlb * (jnp.where(pos_side, 1.0, u) * r)
    lf2 = jnp.maximum(jnp.log2(f), LOG2_TINY)
    cum = _cumsum_rows(tril, lf2)
    pos = (cum - lf2) if backward else cum
    qb, kb = q.astype(BF16), kk.astype(BF16)

    def sides(w):
        e = jnp.exp2(_neg_abs(pos - _block_ref_rows(cum, w))).astype(BF16)
        return qb * e, kb * e

    zq, zk = sides(HALF)
    if backward:
        cross = _dot_nt(zq[0:HALF], zk[HALF:CH])
    else:
        cross = _dot_nt(zq[HALF:CH], zk[0:HALF])
    diag = []
    for b in range(2):
        rows = slice(b * HALF, (b + 1) * HALF)
        a = jnp.where(lvl == N_LEVEL, _dot_nt(qb[rows], kb[rows]), 0.0)
        diag.append(a)
    w = HALF // 2
    while w >= 1:
        zq, zk = sides(w)
        lv = int(math.log2(w))
        for b in range(2):
            rows = slice(b * HALF, (b + 1) * HALF)
            diag[b] = jnp.where(lvl == lv, _dot_nt(zq[rows], zk[rows]), diag[b])
        w //= 2
    d0, d1, cr = diag[0].astype(BF16), diag[1].astype(BF16), cross.astype(BF16)
    if backward:
        o_lo = _dot(jnp.concatenate([d0, cr], axis=1), v)
        o_hi = _dot(d1, v[HALF:CH])
    else:
        o_lo = _dot(d0, v[0:HALF])
        o_hi = _dot(jnp.concatenate([cr, d1], axis=1), v)
    o = jnp.concatenate([o_lo, o_hi], axis=0)
    c_last = cum[CH - 1:CH, :]
    k_out = kk * jnp.exp2((pos) if backward else (c_last - cum))
    s_new = _dot_tn(v, k_out.astype(BF16))
    if has_init:
        q_in = q * jnp.exp2((c_last - pos) if backward else cum)
        o = o + _dot_nt(q_in.astype(BF16), s_t.astype(BF16))
        s_new = s_new + s_t * jnp.exp2(c_last)
    return o, s_new


def _hgrn_kernel(q_ref, v_ref, g_ref, f_ref, lbp_ref, ng_ref, *rest, seq_len, nsq, is_ctx):
    if is_ctx:
        y_ref, sfin_ref, oacc, st = rest
    else:
        s0_ref, y_ref, oacc, st = rest
    nchunk = seq_len // CH
    tril = _tri(CH, True)
    lvl_f, lvl_b = _level_ids()

    def head(h, carry):
        for sq in range(nsq):
            base = sq * seq_len
            if not is_ctx:
                for d in range(2):
                    st[2 * sq + d] = s0_ref[sq, d, h].T
                if nchunk > 1:
                    oacc[base:base + seq_len, :] = jnp.zeros((seq_len, LANES), F32)

            def run(ci, dirs, sq=sq, base=base):
                rows = _chunk_rows(ci, base)
                q = _silu(q_ref[h, rows, :].astype(F32))
                v = v_ref[h, rows, :]
                for d in dirs:
                    lrow = pl.ds(d * HG_HEADS + h, 1)
                    o, s_new = _hgrn_chunk(q, v, f_ref[d * HG_HEADS + h, rows, :],
                                           lbp_ref[0, lrow, :], lbp_ref[1, lrow, :],
                                           None if is_ctx else st[2 * sq + d], tril, lvl_b if d else lvl_f,
                                           d == 1, not is_ctx)
                    st[2 * sq + d] = s_new
                    if nchunk == 1 and d == 0:
                        oacc[rows, :] = o
                    else:
                        oacc[rows, :] += o

            _chunk_loop(nchunk, run)
            o = oacc[base:base + seq_len, :]
            y = o * lax.rsqrt(jnp.mean(o * o, -1, keepdims=True) + RMS_EPS) * ng_ref[pl.ds(h, 1), :]
            gate = _silu(g_ref[h, base:base + seq_len, :].astype(F32))
            y_ref[h, base:base + seq_len, :] = (y * gate).astype(y_ref.dtype)
            if is_ctx:
                for d in range(2):
                    sfin_ref[sq, d, h] = st[2 * sq + d].T
        return carry

    lax.fori_loop(0, HG_HEADS, head, 0)


def _seq_call(kern, name, in_specs, args, y_spec, y_shape, st_specs, st_shapes, states, y_prev, scratch, steps, **kw):
    is_ctx = y_prev is None
    in_specs, args = list(in_specs), list(args)
    n_used = len(args)
    aliases = {}
    any_spec = pl.BlockSpec(memory_space=pl.ANY)
    if is_ctx:
        out_specs, out_shape = [y_spec] + list(st_specs), [y_shape] + list(st_shapes)
        for k, s in enumerate(states or ()):
            in_specs.append(any_spec)
            args.append(s)
            aliases[len(args) - 1] = 1 + k
    else:
        in_specs += list(st_specs)
        args += list(states)
        n_used = len(args)
        in_specs.append(any_spec)
        args.append(y_prev)
        aliases[len(args) - 1] = 0
        out_specs, out_shape = [y_spec], [y_shape]
    n_args = len(args)

    def body(*refs):
        kern(*refs[:n_used], *refs[n_args:], is_ctx=is_ctx, **kw)

    return pl.pallas_call(
        body, grid=(steps,), in_specs=in_specs, out_specs=out_specs, out_shape=out_shape,
        input_output_aliases=aliases, scratch_shapes=scratch, compiler_params=_cparams(("arbitrary",)),
        name=name + ("_ctx" if is_ctx else "_lat"),
    )(*args)


def _hgrn_call(ub, uf, lbp, ng, states, y_prev, *, seq_len, row0, nseq, nsq, layer, depth):
    t = ub.shape[1]
    rows = nsq * seq_len
    bo = row0 // rows
    nh = HG_HEADS
    in_specs = [pl.BlockSpec((nh, rows, LANES), lambda i: (0, bo + i, 0)),
                pl.BlockSpec((nh, rows, LANES), lambda i: (1, bo + i, 0)),
                pl.BlockSpec((nh, rows, LANES), lambda i: (2, bo + i, 0)),
                pl.BlockSpec((2 * nh, rows, LANES), lambda i: (0, bo + i, 0)),
                pl.BlockSpec((2, 2 * nh, LANES), lambda i: (0, 0, 0)),
                pl.BlockSpec((nh, LANES), lambda i: (0, 0))]
    st_spec, st_shape = _state_io((2, nh, HG_DK, LANES), nseq, nsq, layer, depth)
    return _seq_call(
        _hgrn_kernel, "hgrn", in_specs, [ub, ub, ub, uf, lbp, ng],
        pl.BlockSpec((nh, rows, LANES), lambda i: (0, bo + i, 0)), jax.ShapeDtypeStruct((nh, t, LANES), BF16),
        [st_spec], [st_shape], states, y_prev,
        [pltpu.VMEM((rows, LANES), F32), pltpu.VMEM((2 * nsq, LANES, HG_DK), F32)],
        nseq // nsq, seq_len=seq_len, nsq=nsq)


def _mlstm_prep(gt_ref, bias_ref, rows, ms_s, triu):
    gates = gt_ref[rows, :] + bias_ref[...]
    g_t = gates.T
    i_rows = g_t[0:8, :]
    lf_rows = _log_sigmoid(g_t[8:16, :])
    c_rows = _cumsum_lanes(lf_rows, triu)
    is_b = lax.broadcasted_iota(jnp.int32, (8, CH), 0) >= ML_HEADS
    is_b1 = is_b[:, 0:1]
    m_in = ms_s[:, 0:1]
    c_last = c_rows[:, CH - 1:CH]
    pos = jnp.where(is_b, c_rows - lf_rows, c_rows)
    u = jnp.where(is_b, pos + i_rows, i_rows - pos)
    nu = jnp.where(is_b1, c_last + m_in, m_in)
    mu = jnp.maximum(jnp.where(is_b, _cummax_lanes(u, True), _cummax_lanes(u, False)), nu)
    m_t = jnp.where(is_b, mu - pos, pos + mu)
    mu_end = jnp.where(is_b1, mu[:, 0:1], mu[:, CH - 1:CH])
    m_new = jnp.where(is_b1, mu[:, 0:1], m_t[:, CH - 1:CH])
    x_rows = jnp.concatenate([-mu, -m_t, u, jnp.zeros_like(u)], axis=0) * LOG2E
    return dict(lhs=_sel_lhs(x_rows), u_rows=x_rows[16:24, :], nu=nu * LOG2E, mu_end=mu_end * LOG2E, m_new=m_new)


def _mlstm_main(p, q_ref, k_ref, v_ref, sel_ref, rows, dirs, cn_s, ms_s, oacc, masks, has_init, first_write):
    lhs = p["lhs"]
    scale = ML_DK ** -0.5
    ones_blk = jnp.ones((CH, LANES), BF16)
    heads = range(ML_HEADS)
    hds = [(d, h, d * ML_HEADS + h) for h in heads for d in dirs]
    q = [q_ref[h, rows, :] for h in heads]
    kf = [k_ref[h, rows, :].astype(F32) * scale for h in heads]
    vaug = [jnp.concatenate([v_ref[h, rows, :], ones_blk], axis=1) for h in heads]
    qk = [_dot_nt(q[h], kf[h].astype(BF16)) for h in heads]
    b_mu = {hd: _dot(lhs, sel_ref[hd]) for _, _, hd in hds}
    b_m = {hd: _dot(lhs, sel_ref[8 + hd]) for _, _, hd in hds}
    k_t = [kf[h].T for h in heads]
    s = {}
    for d, h, hd in hds:
        xe = jnp.concatenate([b_mu[hd], b_mu[hd]], axis=1) + p["u_rows"][hd:hd + 1, :]
        s[hd] = (jnp.where(masks[d], jnp.exp2(xe), 0.0) * qk[h]).astype(BF16)
    num = {hd: _dot(s[hd], vaug[h]) for _, h, hd in hds}
    w_end = jnp.exp2(p["u_rows"] - p["mu_end"])
    upd = {hd: _dot((k_t[h] * w_end[hd:hd + 1, :]).astype(BF16), vaug[h]) for _, h, hd in hds}
    if has_init:
        for d, h, hd in hds:
            cn = cn_s[d, h]
            w_int = jnp.exp2(p["nu"][hd:hd + 1, :] + b_mu[hd])
            num[hd] = num[hd] + jnp.concatenate([w_int, w_int], axis=1) * _dot(q[h], cn.astype(BF16))
            upd[hd] = upd[hd] + jnp.exp2(p["nu"][hd:hd + 1, :] - p["mu_end"][hd:hd + 1, :]) * cn
    for d, h, hd in hds:
        cn_s[d, h] = upd[hd]
    for h in heads:
        hsum = None
        for d in dirs:
            hd = d * ML_HEADS + h
            den = jnp.maximum(jnp.abs(num[hd][:, LANES:]), jnp.exp2(b_m[hd]))
            hout = num[hd][:, :LANES] / den
            hsum = hout if hsum is None else hsum + hout
        if first_write:
            oacc[h, rows, :] = hsum
        else:
            oacc[h, rows, :] += hsum
    m_new = jnp.broadcast_to(p["m_new"], (8, LANES))
    if len(dirs) == 2:
        ms_s[...] = m_new
    else:
        row = lax.broadcasted_iota(jnp.int32, (8, LANES), 0)
        mine = (row >= ML_HEADS) if dirs[0] == 1 else (row < ML_HEADS)
        ms_s[...] = jnp.where(mine, m_new, ms_s[...])


def _mlstm_kernel(q_ref, k_ref, v_ref, og_ref, gt_ref, bias_ref, ng_ref, sel_ref, *rest, seq_len, nsq, is_ctx):
    if is_ctx:
        y_ref, cfin_ref, nfin_ref, mfin_ref, oacc, cn_s, ms_s = rest
    else:
        c0_ref, n0_ref, m0_ref, y_ref, oacc, cn_s, ms_s = rest
    nchunk = seq_len // CH
    triu = _tri(CH, False)
    rr = lax.broadcasted_iota(jnp.int32, (CH, CH), 0)
    ss = lax.broadcasted_iota(jnp.int32, (CH, CH), 1)
    masks = (ss <= rr, ss >= rr)
    def main(p, rows, dirs, sq):
        _mlstm_main(p, q_ref, k_ref, v_ref, sel_ref, rows, dirs, cn_s.at[sq], ms_s.at[sq], oacc, masks,
                    not is_ctx, nchunk == 1)

    for sq in range(nsq):
        base = sq * seq_len
        if is_ctx:
            ms_s[sq] = jnp.zeros((2 * ML_HEADS, LANES), F32)
        else:
            ms_s[sq] = m0_ref[sq]
            for d in range(2):
                for h in range(ML_HEADS):
                    hd = d * ML_HEADS + h
                    nb = jnp.broadcast_to(n0_ref[sq, hd:hd + 1, :], (ML_DK, LANES)).T
                    cn_s[sq, d, h] = jnp.concatenate([c0_ref[sq, d, h], nb], axis=1)
            if nchunk > 1:
                oacc[:, base:base + seq_len, :] = jnp.zeros((ML_HEADS, seq_len, LANES), F32)
    if nchunk == 1:
        preps = [_mlstm_prep(gt_ref, bias_ref, _chunk_rows(0, sq * seq_len), ms_s.at[sq], triu) for sq in range(nsq)]
        for sq in range(nsq):
            main(preps[sq], _chunk_rows(0, sq * seq_len), (0, 1), sq)
    else:
        for sq in range(nsq):
            def body(ci, carry, sq=sq):
                rows_f = _chunk_rows(ci, sq * seq_len)
                rows_b = _chunk_rows(nchunk - 1 - ci, sq * seq_len)
                p_f = _mlstm_prep(gt_ref, bias_ref, rows_f, ms_s.at[sq], triu)
                p_b = _mlstm_prep(gt_ref, bias_ref, rows_b, ms_s.at[sq], triu)
                main(p_f, rows_f, (0,), sq)
                main(p_b, rows_b, (1,), sq)
                return carry
            lax.fori_loop(0, nchunk, body, 0)
    for sq in range(nsq):
        base = sq * seq_len
        for h in range(ML_HEADS):
            o = oacc[h, base:base + seq_len, :]
            y = o * lax.rsqrt(jnp.mean(o * o, -1, keepdims=True) + RMS_EPS) * ng_ref[h:h + 1, :]
            gate = _sigmoid(og_ref[h, base:base + seq_len, :].astype(F32))
            y_ref[h, base:base + seq_len, :] = (y * gate).astype(y_ref.dtype)
        if is_ctx:
            for d in range(2):
                for h in range(ML_HEADS):
                    cn = cn_s[sq, d, h]
                    cfin_ref[sq, d, h] = cn[:, :LANES]
                    nfin_ref[sq, pl.ds(d * ML_HEADS + h, 1), :] = cn[:, LANES:].T[0:1, :]
            mfin_ref[sq] = ms_s[sq]


def _mlstm_call(ub, uf, bias_row, ng, sel, states, y_prev, *, seq_len, row0, nseq, nsq, layer, depth):
    t = ub.shape[1]
    rows = nsq * seq_len
    bo = row0 // rows
    nh = ML_HEADS
    slab = lambda k: pl.BlockSpec((nh, rows, LANES), lambda i: (k, bo + i, 0))
    in_specs = [slab(3), slab(4), slab(5), slab(6),
                pl.BlockSpec((None, rows, LANES), lambda i: (GATE_SLAB, bo + i, 0)),
                pl.BlockSpec((1, LANES), lambda i: (0, 0)),
                pl.BlockSpec((nh, LANES), lambda i: (0, 0)),
                pl.BlockSpec((N_SEL, LANES, LANES), lambda i: (0, 0, 0))]
    c_spec, c_shape = _state_io((2, nh, ML_DK, LANES), nseq, nsq, layer, depth)
    v_spec, v_shape = _state_io((2 * nh, LANES), nseq, nsq, layer, depth)
    return _seq_call(
        _mlstm_kernel, "mlstm", in_specs, [ub, ub, ub, ub, uf, bias_row, ng, sel],
        pl.BlockSpec((nh, rows, LANES), lambda i: (0, bo + i, 0)), jax.ShapeDtypeStruct((nh, t, LANES), BF16),
        [c_spec, v_spec, v_spec], [c_shape, v_shape, v_shape], states, y_prev,
        [pltpu.VMEM((nh, rows, LANES), F32), pltpu.VMEM((nsq, 2, nh, ML_DK, 2 * LANES), F32),
         pltpu.VMEM((nsq, 2 * nh, LANES), F32)],
        nseq // nsq, seq_len=seq_len, nsq=nsq)


N_PAIR = MB_HEADS // 2
PAIRS_PER_GROUP = N_PAIR // MB_GROUPS


def _ssd_prep(gt_ref, bias_ref, nega_ref, rows, triu):
    gates = gt_ref[rows, :] + bias_ref[...]
    g_t = gates.T
    dt_rows = _softplus(g_t[16:32, :])
    la_rows = dt_rows * nega_ref[...]
    c_rows = _cumsum_lanes(la_rows, triu)
    ldt = jnp.log(dt_rows)
    is_b = lax.broadcasted_iota(jnp.int32, (16, CH), 0) >= MB_HEADS
    pos = jnp.where(is_b, c_rows - la_rows, c_rows)
    a_col = jnp.where(is_b, -pos, pos) * LOG2E
    r_row = jnp.where(is_b, pos + ldt, ldt - pos) * LOG2E
    c_last = c_rows[:, CH - 1:CH] * LOG2E
    lhs = _sel_lhs(jnp.concatenate([a_col, jnp.zeros_like(a_col)], axis=0))
    return dict(lhs=lhs, r_row=r_row, c_last=c_last)


def _ssd_main(p, xs_s, bcs_s, sel_ref, rows, dirs, hst, yacc, masks, has_init, first_write):
    lhs, r_row, c_last = p["lhs"], p["r_row"], p["c_last"]
    lane = lax.broadcasted_iota(jnp.int32, (CH, LANES), 1)
    lo = lane < MB_HEADDIM
    lo_state = lax.broadcasted_iota(jnp.int32, (LANES, LANES), 1) < MB_HEADDIM
    lo_row = lo_state[0:1, :]
    bblk = bcs_s[rows, 0:LANES]
    cblk = bcs_s[rows, LANES:2 * LANES]
    c_g, b_g, gm = [], [], []
    for grp in range(MB_GROUPS):
        in_grp = (lane >= grp * MB_DSTATE) & (lane < (grp + 1) * MB_DSTATE)
        c_g.append(jnp.where(in_grp, cblk, 0.0))
        b_g.append(jnp.where(in_grp, bblk, 0.0))
        gm.append(_dot_nt(c_g[grp].astype(BF16), b_g[grp].astype(BF16)))
    xh = []
    for j in range(N_PAIR):
        xpair = xs_s[rows, j * LANES:(j + 1) * LANES]
        xh.append((jnp.where(lo, xpair, 0.0).astype(BF16), jnp.where(lo, 0.0, xpair).astype(BF16)))
    items = [(d, j, half, d * MB_HEADS + 2 * j + half) for j in range(N_PAIR) for d in dirs for half in range(2)]
    b_a = {hd: _dot(lhs, sel_ref[hd]) for _, _, _, hd in items}
    b_t = [b_g[grp].T for grp in range(MB_GROUPS)]
    is_f = lax.broadcasted_iota(jnp.int32, (2 * MB_HEADS, 1), 0) < MB_HEADS
    w_end = jnp.exp2(r_row + jnp.where(is_f, c_last, 0.0))
    m, b_out, c_in = {}, {}, {}
    for d, j, half, hd in items:
        grp = j // PAIRS_PER_GROUP
        cl = c_last[hd:hd + 1, :]
        xe = jnp.concatenate([b_a[hd], b_a[hd]], axis=1) + r_row[hd:hd + 1, :]
        m[hd] = (jnp.where(masks[d], jnp.exp2(xe), 0.0) * gm[grp]).astype(BF16)
        b_out[hd] = (b_t[grp] * w_end[hd:hd + 1, :]).astype(BF16)
        if has_init:
            c_in[hd] = (c_g[grp] * jnp.exp2(b_a[hd] + cl if d else b_a[hd])).astype(BF16)
    yy = {hd: _dot(m[hd], xh[j][half]) for _, j, half, hd in items}
    uu = {hd: _dot(b_out[hd], xh[j][half]) for _, j, half, hd in items}
    for j in range(N_PAIR):
        ysum = None
        for d in dirs:
            hd0, hd1 = d * MB_HEADS + 2 * j, d * MB_HEADS + 2 * j + 1
            y = yy[hd0] + yy[hd1]
            upd = uu[hd0] + uu[hd1]
            if has_init:
                ht = hst[d, j]
                y = y + _dot(c_in[hd0], jnp.where(lo_state, ht, 0.0).astype(BF16))
                y = y + _dot(c_in[hd1], jnp.where(lo_state, 0.0, ht).astype(BF16))
                decay = jnp.where(lo_row, jnp.exp2(c_last[hd0:hd0 + 1, :]), jnp.exp2(c_last[hd1:hd1 + 1, :]))
                upd = upd + ht * decay
            hst[d, j] = upd
            ysum = y if ysum is None else ysum + y
        if first_write:
            yacc[rows, j * LANES:(j + 1) * LANES] = ysum
        else:
            yacc[rows, j * LANES:(j + 1) * LANES] += ysum


def _ssd_kernel(x_ref, bc_ref, z_ref, gt_ref, cw_ref, bias_ref, nega_ref, dskip_ref, ng_ref, sel_ref, *rest,
                seq_len, nsq, is_ctx):
    if is_ctx:
        y_ref, hfin_ref, xs_s, bcs_s, yacc, hst = rest
    else:
        h0_ref, y_ref, xs_s, bcs_s, yacc, hst = rest
    nchunk = seq_len // CH
    triu = _tri(CH, False)
    rr = lax.broadcasted_iota(jnp.int32, (CH, CH), 0)
    ss = lax.broadcasted_iota(jnp.int32, (CH, CH), 1)
    masks = (ss <= rr, ss >= rr)

    def conv(v, lo_col):
        row = lax.broadcasted_iota(jnp.int32, v.shape, 0)
        prev = jnp.where(row == 0, 0.0, pltpu.roll(v, 1, 0))
        nxt = jnp.where(row == seq_len - 1, 0.0, pltpu.roll(v, seq_len - 1, 0))
        cs = slice(lo_col, lo_col + v.shape[1])
        return _silu(cw_ref[0:1, cs] * prev + cw_ref[1:2, cs] * v + cw_ref[2:3, cs] * nxt + cw_ref[3:4, cs])

    for sq in range(nsq):
        base = sq * seq_len
        seq_rows = slice(base, base + seq_len)
        for k in range(MIX_W // LANES):
            xs_s[seq_rows, k * LANES:(k + 1) * LANES] = conv(x_ref[k, seq_rows, :], k * LANES)
        for k in range(2):
            bcs_s[seq_rows, k * LANES:(k + 1) * LANES] = conv(bc_ref[k, seq_rows, :], MIX_W + k * LANES)
        if not is_ctx:
            hst[sq] = h0_ref[sq]
            if nchunk > 1:
                yacc[seq_rows, :] = jnp.zeros((seq_len, MIX_W), F32)

    def prep(rows):
        return _ssd_prep(gt_ref, bias_ref, nega_ref, rows, triu)

    def main(p, rows, dirs, sq):
        _ssd_main(p, xs_s, bcs_s, sel_ref, rows, dirs, hst.at[sq], yacc, masks, not is_ctx, nchunk == 1)

    if nchunk == 1:
        preps = [prep(_chunk_rows(0, sq * seq_len)) for sq in range(nsq)]
        for sq in range(nsq):
            main(preps[sq], _chunk_rows(0, sq * seq_len), (0, 1), sq)
    else:
        for sq in range(nsq):
            def body(ci, carry, sq=sq):
                rows_f = _chunk_rows(ci, sq * seq_len)
                rows_b = _chunk_rows(nchunk - 1 - ci, sq * seq_len)
                p_f, p_b = prep(rows_f), prep(rows_b)
                main(p_f, rows_f, (0,), sq)
                main(p_b, rows_b, (1,), sq)
                return carry
            lax.fori_loop(0, nchunk, body, 0)

    for sq in range(nsq):
        base = sq * seq_len
        seq_rows = slice(base, base + seq_len)
        z = jnp.concatenate([z_ref[k, seq_rows, :] for k in range(MIX_W // LANES)], axis=1).astype(F32)
        y = (yacc[seq_rows, :] + dskip_ref[...] * xs_s[seq_rows, :]) * _silu(z)
        y = (y * lax.rsqrt(jnp.mean(y * y, -1, keepdims=True) + RMS_EPS) * ng_ref[...]).astype(y_ref.dtype)
        for k in range(MIX_W // LANES):
            y_ref[k, seq_rows, :] = y[:, k * LANES:(k + 1) * LANES]
        if is_ctx:
            hfin_ref[sq] = hst[sq]


def _ssd_call(ub, uf, cw, bias_row, nega_rows, dskip, ng, sel, states, y_prev, *, seq_len, row0, nseq, nsq, layer,
              depth):
    t = ub.shape[1]
    rows = nsq * seq_len
    bo = row0 // rows
    ns = MIX_W // LANES
    row_spec = lambda w: pl.BlockSpec((1, w), lambda i: (0, 0))
    in_specs = [pl.BlockSpec((ns, rows, LANES), lambda i: (2, bo + i, 0)),
                pl.BlockSpec((2, rows, LANES), lambda i: (6, bo + i, 0)),
                pl.BlockSpec((ns, rows, LANES), lambda i: (7, bo + i, 0)),
                pl.BlockSpec((None, rows, LANES), lambda i: (GATE_SLAB, bo + i, 0)),
                pl.BlockSpec((8, MB_XBC), lambda i: (0, 0)),
                row_spec(LANES),
                pl.BlockSpec((2 * MB_HEADS, CH), lambda i: (0, 0)),
                row_spec(MIX_W), row_spec(MIX_W),
                pl.BlockSpec((N_SEL, LANES, LANES), lambda i: (0, 0, 0))]
    st_spec, st_shape = _state_io((2, N_PAIR, LANES, LANES), nseq, nsq, layer, depth)
    return _seq_call(
        _ssd_kernel, "ssd", in_specs, [uf, uf, ub, uf, cw, bias_row, nega_rows, dskip, ng, sel],
        pl.BlockSpec((ns, rows, LANES), lambda i: (0, bo + i, 0)), jax.ShapeDtypeStruct((ns, t, LANES), BF16),
        [st_spec], [st_shape], states, y_prev,
        [pltpu.VMEM((rows, MIX_W), F32), pltpu.VMEM((rows, 2 * LANES), F32), pltpu.VMEM((rows, MIX_W), F32),
         pltpu.VMEM((nsq, 2, N_PAIR, LANES, LANES), F32)],
        nseq // nsq, seq_len=seq_len, nsq=nsq)


def _ssd_state_to_pairs(s):
    bsz = s.shape[0]
    st = jnp.swapaxes(s, -1, -2).reshape(bsz, 2, N_PAIR, 2, MB_DSTATE, MB_HEADDIM)
    st = jnp.moveaxis(st, 3, 4).reshape(bsz, 2, N_PAIR, MB_DSTATE, 2 * MB_HEADDIM)
    zero = jnp.zeros_like(st)
    grp = (jnp.arange(N_PAIR) // PAIRS_PER_GROUP).reshape(1, 1, N_PAIR, 1, 1)
    return jnp.concatenate([jnp.where(grp == 0, st, zero), jnp.where(grp == 1, st, zero)], axis=3)


def _ssd_pairs_to_state(hp):
    bsz = hp.shape[0]
    halves = hp.reshape(bsz, 2, N_PAIR, MB_GROUPS, MB_DSTATE, 2 * MB_HEADDIM)
    grp = (jnp.arange(N_PAIR) // PAIRS_PER_GROUP).reshape(1, 1, N_PAIR, 1, 1)
    st = jnp.where(grp == 0, halves[:, :, :, 0], halves[:, :, :, 1])
    st = st.reshape(bsz, 2, N_PAIR, MB_DSTATE, 2, MB_HEADDIM)
    st = jnp.moveaxis(st, 4, 3).reshape(bsz, 2, MB_HEADS, MB_DSTATE, MB_HEADDIM)
    return jnp.swapaxes(st, -1, -2)


def _grid_pos_embed(n_tok, d_model):
    rows = n_tok // GRID_W
    r, cidx = jnp.meshgrid(jnp.arange(rows, dtype=F32), jnp.arange(GRID_W, dtype=F32), indexing='ij')
    quarter = d_model // 4
    freq = jnp.exp(-math.log(10000.0) * jnp.arange(quarter, dtype=F32) / quarter)
    ar = r.reshape(-1, 1) * freq
    ac = cidx.reshape(-1, 1) * freq
    return jnp.concatenate([jnp.sin(ar), jnp.cos(ar), jnp.sin(ac), jnp.cos(ac)], axis=-1)


def _gate_row(pieces):
    v = jnp.concatenate([p.reshape(-1).astype(F32) for p in pieces])
    return jnp.concatenate([v, v, v, jnp.zeros((LANES - 3 * N_SEL,), F32)]).reshape(1, LANES)


def kernel(x_prompt, x_sample, state_hgrn, state_mlstm_C, state_mlstm_n, state_mlstm_m, state_ssd, c, c_ctx,
           w_mod, b_mod, ln_g, ln_b, ffn_w_gu, ffn_w_down, w_in, hg_lb, hg_norm_g, ml_gate_b, ml_norm_g,
           mb_conv_w, mb_conv_b, mb_dt_bias, mb_a_log, mb_d, mb_norm_g, w_branch, w_out):
    bsz, seq, d = x_prompt.shape
    dbsz, dseq, _ = x_sample.shape
    depth = w_mod.shape[0]
    t_ctx = bsz * seq
    n_ctx_tiles = t_ctx // ROW_TILE
    assert dseq == ROW_TILE and t_ctx % ROW_TILE == 0 and seq == CH and dseq % CH == 0

    xs0 = x_sample + _grid_pos_embed(dseq, d).astype(x_sample.dtype)[None]
    x = jnp.concatenate([x_prompt.reshape(t_ctx, d), xs0.reshape(dbsz * dseq, d)], axis=0)

    cv = jnp.concatenate([c_ctx[None], c, jnp.zeros((8 - 1 - dbsz, d), F32)], axis=0)
    mod4 = _mod_call(cv, w_mod, b_mod).reshape(depth, 8, 1, N_MOD * d)

    w_gu_b = ffn_w_gu.astype(BF16)
    w_down_b = ffn_w_down.astype(BF16)
    w_branch_b = w_branch.astype(BF16)
    w_out_b = w_out.astype(BF16)
    w_a = jnp.concatenate([w_in[:, :, 0:1536], w_in[:, :, 2560:4608], w_in[:, :, 4624:5136],
                           w_in[:, :, 5920:8992]], axis=2).astype(BF16)
    w_gate = jnp.concatenate([w_in[:, :, 4608:4624], w_in[:, :, 5904:5920]], axis=2)
    w_b = jnp.concatenate([w_in[:, :, 1536:2560], w_in[:, :, 5136:5904], w_gate, w_gate, w_gate,
                           jnp.zeros((depth, d, LANES - 3 * N_SEL), F32)], axis=2).astype(BF16)
    ln_g4 = ln_g.reshape(depth, 3, 1, d)
    ln_b4 = ln_b.reshape(depth, 3, 1, d)

    lbs = jnp.cumsum(jax.nn.softmax(hg_lb.astype(F32), axis=0), axis=0)
    lbs = (lbs - lbs[0]).reshape(depth, 2 * HG_HEADS, HG_DK)
    lbp = jnp.stack([lbs, 1.0 - lbs], axis=1)
    cw = jnp.concatenate([mb_conv_w, mb_conv_b[:, None, :], jnp.zeros((depth, 4, MB_XBC), F32)], axis=1)
    sel = _selectors()

    lat_h0 = _ssd_state_to_pairs(state_ssd.reshape((dbsz * depth,) + state_ssd.shape[2:])).reshape(
        (dbsz, depth, 2, N_PAIR, LANES, LANES))
    lat_n0 = state_mlstm_n.reshape(dbsz, depth, 2 * ML_HEADS, ML_DK)
    lat_m0 = jnp.broadcast_to(state_mlstm_m.reshape(dbsz, depth, 2 * ML_HEADS, 1), (dbsz, depth, 2 * ML_HEADS, LANES))

    st_hg = st_ml = st_ss = None
    ctx = dict(seq_len=seq, row0=0, nseq=bsz, nsq=CTX_SEQS_PER_STEP, depth=depth)
    lat = dict(seq_len=dseq, row0=t_ctx, nseq=dbsz, nsq=1, depth=depth)
    for l in range(depth):
        x = _ffn_call(x, mod4, w_gu_b, w_down_b, ln_g4, ln_b4, l, 0, n_ctx_tiles)
        ub = _inproj_call(x, mod4, w_a, l, INPROJ_TILE_A, BF16, n_ctx_tiles, "inproj_a")
        uf = _inproj_call(x, mod4, w_b, l, INPROJ_TILE_B, F32, n_ctx_tiles, "inproj_b")

        ng_h = hg_norm_g[l].reshape(HG_HEADS, LANES)
        y_hg, *st_hg = _hgrn_call(ub, uf, lbp[l], ng_h, st_hg, None, layer=l, **ctx)
        y_hg, = _hgrn_call(ub, uf, lbp[l], ng_h, [state_hgrn], y_hg, layer=l, **lat)

        gate_bias = _gate_row([ml_gate_b[l, 0], ml_gate_b[l, 1], mb_dt_bias[l]])
        ng_m = ml_norm_g[l].reshape(ML_HEADS, LANES)
        y_ml, *st_ml = _mlstm_call(ub, uf, gate_bias, ng_m, sel, st_ml, None, layer=l, **ctx)
        y_ml, = _mlstm_call(ub, uf, gate_bias, ng_m, sel, [state_mlstm_C, lat_n0, lat_m0], y_ml, layer=l, **lat)

        nega = jnp.broadcast_to(-jnp.exp(mb_a_log[l].astype(F32)).reshape(2 * MB_HEADS, 1), (2 * MB_HEADS, CH))
        dskip = jnp.repeat(mb_d[l], MB_HEADDIM).reshape(1, MIX_W)
        ng_s = mb_norm_g[l].reshape(1, MIX_W)
        y_mb, *st_ss = _ssd_call(ub, uf, cw[l], gate_bias, nega, dskip, ng_s, sel, st_ss, None, layer=l, **ctx)
        y_mb, = _ssd_call(ub, uf, cw[l], gate_bias, nega, dskip, ng_s, sel, [lat_h0], y_mb, layer=l, **lat)

        x = _mixout_call(x, mod4, y_hg, y_ml, y_mb, ub, w_branch_b, w_out_b, ln_g4, ln_b4, l, n_ctx_tiles)
        x = _ffn_call(x, mod4, w_gu_b, w_down_b, ln_g4, ln_b4, l, 1, n_ctx_tiles)

    y_prompt = x[:t_ctx].reshape(bsz, seq, d)
    y_sample = x[t_ctx:].reshape(dbsz, dseq, d)
    c_fin, n_fin, m_fin = st_ml
    h_fin = _ssd_pairs_to_state(st_ss[0].reshape((bsz * depth,) + st_ss[0].shape[2:]))
    return (y_prompt, y_sample, st_hg[0], c_fin, n_fin.reshape(bsz, depth, 2, ML_HEADS, ML_DK),
            m_fin[:, :, :, 0].reshape(bsz, depth, 2, ML_HEADS),
            h_fin.reshape((bsz, depth) + h_fin.shape[1:]))
```

```python
import functools
import math

import jax
import jax.numpy as jnp
from jax import lax
from jax.experimental import pallas as pl
from jax.experimental.pallas import tpu as pltpu

F32 = jnp.float32
BF16 = jnp.bfloat16

D_MODEL = 1024
DEPTH = 4
GRID_W = 64
MIX_W = 512
HG_HEADS = 4
HG_DK = 128
ML_HEADS = 4
ML_DK = 128
MB_HEADS = 8
MB_HEADDIM = 64
MB_GROUPS = 2
MB_DSTATE = 64
MB_XBC = MIX_W + 2 * MB_GROUPS * MB_DSTATE
D_FF = 2816
N_MOD = 9
ALPHA = (2 * DEPTH) ** 0.25
LN_EPS = 1e-5
RMS_EPS = 1e-6
LOG2E = 1.4426950408889634

LANES = 128
CH = 256
HALF = CH // 2
ROW_TILE = 1024
FFN_ROWS = 2048
MOD_LAT_ROW = 2
FF_TILE = 256
INPROJ_TILE_A = 1792
INPROJ_TILE_B = 1920
CTX_SEQS_PER_STEP = 4
VMEM_LIMIT = 56 * 1024 * 1024

UB_SLABS = 56
UF_SLABS = 15
GATE_SLAB = 14
N_SEL = 32
LOG2_TINY = -150.0


def _dot(a, b):
    return jnp.dot(a, b, preferred_element_type=F32)


def _dot_nt(a, b):
    return lax.dot_general(a, b, (((1,), (1,)), ((), ())), preferred_element_type=F32)


def _dot_tn(a, b):
    return lax.dot_general(a, b, (((0,), (0,)), ((), ())), preferred_element_type=F32)


def _sigmoid(x):
    return 1.0 / (1.0 + jnp.exp(-x))


def _silu(x):
    return x * _sigmoid(x)


def _log_sigmoid(x):
    return jnp.minimum(x, 0.0) - jnp.log(1.0 + jnp.exp(-jnp.abs(x)))


def _softplus(x):
    return jnp.maximum(x, 0.0) + jnp.log(1.0 + jnp.exp(-jnp.abs(x)))


def _neg_abs(x):
    return -jnp.abs(x)


def _ln(z, g, b):
    mu = jnp.mean(z, -1, keepdims=True)
    d = z - mu
    var = jnp.mean(d * d, -1, keepdims=True)
    return d * lax.rsqrt(var + LN_EPS) * g + b


def _split3(x):
    h = x.astype(BF16)
    r = x - h.astype(F32)
    m = r.astype(BF16)
    l = (r - m.astype(F32)).astype(BF16)
    return h, m, l


def _cumsum_rows(tril, x):
    h, m, l = _split3(x)
    return _dot(tril, h) + _dot(tril, m) + _dot(tril, l)


def _cumsum_lanes(x, triu):
    h, m, l = _split3(x)
    return _dot(h, triu) + _dot(m, triu) + _dot(l, triu)


def _cummax_lanes(x, reverse):
    n = x.shape[1]
    lane = lax.broadcasted_iota(jnp.int32, x.shape, 1)
    k = 1
    while k < n:
        if reverse:
            sh = jnp.where(lane < n - k, pltpu.roll(x, n - k, 1), -jnp.inf)
        else:
            sh = jnp.where(lane >= k, pltpu.roll(x, k, 1), -jnp.inf)
        x = jnp.maximum(x, sh)
        k *= 2
    return x


def _tri(c, lower):
    r = lax.broadcasted_iota(jnp.int32, (c, c), 0)
    s = lax.broadcasted_iota(jnp.int32, (c, c), 1)
    return jnp.where((s <= r) if lower else (r <= s), 1.0, 0.0).astype(BF16)


def _sel_lhs(x_rows):
    h = x_rows.astype(BF16).astype(F32)
    r = x_rows - h
    m = r.astype(BF16).astype(F32)
    rows = jnp.concatenate([h, m, r - m, jnp.ones_like(x_rows)], axis=0)
    return rows.T.astype(BF16)


def _selectors():
    r = jnp.arange(LANES)
    ch = jnp.arange(N_SEL)
    hit = ((r[None, :] % N_SEL) == ch[:, None]) & (r[None, :] < 3 * N_SEL)
    return jnp.broadcast_to(hit[:, :, None], (N_SEL, LANES, LANES)).astype(BF16)


def _cparams(sem):
    return pltpu.CompilerParams(dimension_semantics=sem, vmem_limit_bytes=VMEM_LIMIT)


def _chunk_rows(ci, base=0):
    r0 = ci * CH + base
    if not isinstance(r0, int):
        r0 = pl.multiple_of(r0, CH)
    return pl.ds(r0, CH)


def _state_io(shape_tail, nseq, nsq, layer, depth):
    nz = (0,) * len(shape_tail)
    spec = pl.BlockSpec((nsq, None) + shape_tail, lambda i: (i, layer) + nz)
    shape = jax.ShapeDtypeStruct((nseq, depth) + shape_tail, F32)
    return spec, shape


def _chunk_loop(nchunk, run):
    if nchunk == 1:
        run(0, (0, 1))
    else:
        def body(ci, carry):
            run(ci, (0,))
            run(nchunk - 1 - ci, (1,))
            return carry
        lax.fori_loop(0, nchunk, body, 0)


def _mod_kernel(c_ref, w_ref, b_ref, o_ref):
    a = _silu(c_ref[...]).astype(BF16)
    o_ref[...] = _dot(a, w_ref[...].astype(BF16)) + b_ref[...]


def _mod_call(cv, w_mod, b_mod):
    depth, d, e = w_mod.shape
    tn = 1024
    return pl.pallas_call(
        _mod_kernel,
        grid=(depth, e // tn),
        in_specs=[pl.BlockSpec((8, d), lambda l, j: (0, 0)),
                  pl.BlockSpec((None, d, tn), lambda l, j: (l, 0, j)),
                  pl.BlockSpec((None, 1, tn), lambda l, j: (l, 0, j))],
        out_specs=pl.BlockSpec((None, 8, tn), lambda l, j: (l, 0, j)),
        out_shape=jax.ShapeDtypeStruct((depth, 8, e), F32),
        compiler_params=_cparams(("arbitrary", "arbitrary")),
        name="mod",
    )(cv, w_mod, b_mod.reshape(depth, 1, e))


def _mod_row(i, n_ctx_tiles):
    return jnp.maximum(i - n_ctx_tiles + MOD_LAT_ROW, 0)


def _ffn_kernel(x_ref, mod_ref, wa_ref, wb_ref, wd_ref, g_ref, b_ref, o_ref, xm_s, *, nf):
    j = pl.program_id(1)
    d = x_ref.shape[1]
    groups = [(r, slice(r * ROW_TILE, (r + 1) * ROW_TILE)) for r in range(FFN_ROWS // ROW_TILE)]

    @pl.when(j == 0)
    def _():
        for r, rows in groups:
            sh = mod_ref[r:r + 1, 0:d]
            sc = mod_ref[r:r + 1, d:2 * d]
            xm_s[rows, :] = (x_ref[rows, :] * (1.0 + sc) + sh).astype(BF16)
        o_ref[...] = jnp.zeros_like(o_ref)

    xm = xm_s[...]
    a = _dot(xm, wa_ref[...])
    b = _dot(xm, wb_ref[...])
    o_ref[...] += _dot((_silu(a) * b).astype(BF16), wd_ref[...])

    @pl.when(j == nf - 1)
    def _():
        for r, rows in groups:
            gate = mod_ref[r:r + 1, 2 * d:3 * d]
            z = ALPHA * x_ref[rows, :] + 0.5 * gate * o_ref[rows, :]
            o_ref[rows, :] = _ln(z, g_ref[...], b_ref[...])


def _ffn_call(x, mod4, w_gu, w_down, ln_g, ln_b, layer, which, n_ctx_tiles):
    t, d = x.shape
    nf = D_FF // FF_TILE
    per = FFN_ROWS // ROW_TILE
    assert nf >= 3 and nf * FF_TILE == D_FF and n_ctx_tiles % per == 0 and MOD_LAT_ROW % per == 0
    sub = 0 if which == 0 else 2
    mod_pairs = mod4.reshape(mod4.shape[0], mod4.shape[1] // per, per, mod4.shape[3])
    first_lat = n_ctx_tiles // per
    mod_blk = lambda i, j: (layer, jnp.maximum(i - first_lat + MOD_LAT_ROW // per, 0), 0, sub)
    return pl.pallas_call(
        functools.partial(_ffn_kernel, nf=nf),
        grid=(t // FFN_ROWS, nf),
        in_specs=[pl.BlockSpec((FFN_ROWS, d), lambda i, j: (i, 0)),
                  pl.BlockSpec((None, None, per, 3 * d), mod_blk),
                  pl.BlockSpec((None, None, d, FF_TILE), lambda i, j: (layer, which, 0, j)),
                  pl.BlockSpec((None, None, d, FF_TILE), lambda i, j: (layer, which, 0, j + nf)),
                  pl.BlockSpec((None, None, FF_TILE, d), lambda i, j: (layer, which, j, 0)),
                  pl.BlockSpec((None, None, 1, d), lambda i, j: (layer, sub, 0, 0)),
                  pl.BlockSpec((None, None, 1, d), lambda i, j: (layer, sub, 0, 0))],
        out_specs=pl.BlockSpec((FFN_ROWS, d), lambda i, j: (i, 0)),
        out_shape=jax.ShapeDtypeStruct((t, d), F32),
        scratch_shapes=[pltpu.VMEM((FFN_ROWS, d), BF16)],
        compiler_params=_cparams(("arbitrary", "arbitrary")),
        name=f"ffn{which}",
    )(x, mod_pairs, w_gu, w_gu, w_down, ln_g, ln_b)


def _inproj_kernel(x_ref, mod_ref, w_ref, o_ref, xm_s):
    d = x_ref.shape[1]

    @pl.when(pl.program_id(1) == 0)
    def _():
        sh = mod_ref[:, 0:d]
        sc = mod_ref[:, d:2 * d]
        xm_s[...] = (x_ref[...] * (1.0 + sc) + sh).astype(BF16)

    res = _dot(xm_s[...], w_ref[...]).astype(o_ref.dtype)
    for k in range(o_ref.shape[0]):
        o_ref[k] = res[:, k * LANES:(k + 1) * LANES]


def _inproj_call(x, mod4, w, layer, tn, out_dtype, n_ctx_tiles, name):
    t, d = x.shape
    n = w.shape[2]
    return pl.pallas_call(
        _inproj_kernel,
        grid=(t // ROW_TILE, n // tn),
        in_specs=[pl.BlockSpec((ROW_TILE, d), lambda i, j: (i, 0)),
                  pl.BlockSpec((None, None, 1, 3 * d), lambda i, j: (layer, _mod_row(i, n_ctx_tiles), 0, 1)),
                  pl.BlockSpec((None, d, tn), lambda i, j: (layer, 0, j))],
        out_specs=pl.BlockSpec((tn // LANES, ROW_TILE, LANES), lambda i, j: (j, i, 0)),
        out_shape=jax.ShapeDtypeStruct((n // LANES, t, LANES), out_dtype),
        scratch_shapes=[pltpu.VMEM((ROW_TILE, d), BF16)],
        compiler_params=_cparams(("arbitrary", "arbitrary")),
        name=name,
    )(x, mod4, w)


def _slabs(ref):
    return jnp.concatenate([ref[k] for k in range(ref.shape[0])], axis=1)


def _mixout_kernel(x_ref, mod_ref, yh_ref, ym_ref, ys_ref, g0_ref, g1_ref, g2_ref, wb_ref, wo_ref,
                   lg_ref, lb_ref, o_ref):
    d = x_ref.shape[1]
    p = _sigmoid(_slabs(g0_ref).astype(F32)) * _dot(_slabs(yh_ref), wb_ref[0])
    p += _sigmoid(_slabs(g1_ref).astype(F32)) * _dot(_slabs(ym_ref), wb_ref[1])
    p += _sigmoid(_slabs(g2_ref).astype(F32)) * _dot(_slabs(ys_ref), wb_ref[2])
    y = _dot(p.astype(BF16), wo_ref[...])
    gate = mod_ref[:, 2 * d:3 * d]
    z = ALPHA * x_ref[...] + gate * y
    o_ref[...] = _ln(z, lg_ref[...], lb_ref[...])


def _mixout_call(x, mod4, y_hg, y_ml, y_mb, ub, w_branch, w_out, ln_g, ln_b, layer, n_ctx_tiles):
    t, d = x.shape
    tm = 512
    per = ROW_TILE // tm
    ns = MIX_W // LANES
    ng = d // LANES
    g0 = (UB_SLABS - 3 * ng) // ng
    y_spec = pl.BlockSpec((ns, tm, LANES), lambda i: (0, i, 0))
    return pl.pallas_call(
        _mixout_kernel,
        grid=(t // tm,),
        in_specs=[pl.BlockSpec((tm, d), lambda i: (i, 0)),
                  pl.BlockSpec((None, None, 1, 3 * d), lambda i: (layer, _mod_row(i // per, n_ctx_tiles), 0, 1)),
                  y_spec, y_spec, y_spec,
                  pl.BlockSpec((ng, tm, LANES), lambda i: (g0, i, 0)),
                  pl.BlockSpec((ng, tm, LANES), lambda i: (g0 + 1, i, 0)),
                  pl.BlockSpec((ng, tm, LANES), lambda i: (g0 + 2, i, 0)),
                  pl.BlockSpec((None, 3, MIX_W, d), lambda i: (layer, 0, 0, 0)),
                  pl.BlockSpec((None, d, d), lambda i: (layer, 0, 0)),
                  pl.BlockSpec((None, None, 1, d), lambda i: (layer, 1, 0, 0)),
                  pl.BlockSpec((None, None, 1, d), lambda i: (layer, 1, 0, 0))],
        out_specs=pl.BlockSpec((tm, d), lambda i: (i, 0)),
        out_shape=jax.ShapeDtypeStruct((t, d), F32),
        compiler_params=_cparams(("arbitrary",)),
        name="mixout",
    )(x, mod4, y_hg, y_ml, y_mb, ub, ub, ub, w_branch, w_out, ln_g, ln_b)


N_LEVEL = int(math.log2(HALF))


def _block_ref_rows(c, w):
    n_rows = c.shape[0]
    if 2 * w == n_rows:
        return jnp.broadcast_to(c[w - 1:w, :], c.shape)
    if w >= 4:
        n = n_rows // (2 * w)
        c3 = c.reshape(n, 2 * w, LANES)
        return jnp.broadcast_to(c3[:, w - 1:w, :], c3.shape).reshape(n_rows, LANES)
    c8 = c.reshape(n_rows // 8, 8, LANES)
    sub = lax.broadcasted_iota(jnp.int32, c8.shape, 1)
    if w == 2:
        r = jnp.where(sub < 4, c8[:, 1:2, :], c8[:, 5:6, :])
    else:
        r = jnp.where(sub < 2, c8[:, 0:1, :],
                      jnp.where(sub < 4, c8[:, 2:3, :], jnp.where(sub < 6, c8[:, 4:5, :], c8[:, 6:7, :])))
    return r.reshape(n_rows, LANES)


def _level_ids():
    r = lax.broadcasted_iota(jnp.int32, (HALF, HALF), 0)
    s = lax.broadcasted_iota(jnp.int32, (HALF, HALF), 1)
    lv = 31 - lax.clz(r ^ s)
    lv = jnp.where(r == s, N_LEVEL, lv)
    return jnp.where(r >= s, lv, -1), jnp.where(r <= s, lv, -1)


def _hgrn_chunk(q, v, fpre, lb, onemlb, s_t, tril, lvl, backward, has_init):
    u = jnp.exp2(_neg_abs(fpre) * LOG2E)
    r = 1.0 / (1.0 + u)
    pos_side = fpre >= 0.0
    kk = onemlb * (jnp.where(pos_side, u, 1.0) * r)
    f = lb + onemlb * (jnp.where(pos_side, 1.0, u) * r)
    lf2 = jnp.maximum(jnp.log2(f), LOG2_TINY)
    cum = _cumsum_rows(tril, lf2)
    pos = (cum - lf2) if backward else cum
    qb, kb = q.astype(BF16), kk.astype(BF16)

    def sides(w):
        e = jnp.exp2(_neg_abs(pos - _block_ref_rows(cum, w))).astype(BF16)
        return qb * e, kb * e

    zq, zk = sides(HALF)
    if backward:
        cross = _dot_nt(zq[0:HALF], zk[HALF:CH])
    else:
        cross = _dot_nt(zq[HALF:CH], zk[0:HALF])
    diag = []
    for b in range(2):
        rows = slice(b * HALF, (b + 1) * HALF)
        a = jnp.where(lvl == N_LEVEL, _dot_nt(qb[rows], kb[rows]), 0.0)
        diag.append(a)
    w = HALF // 2
    while w >= 1:
        zq, zk = sides(w)
        lv = int(math.log2(w))
        for b in range(2):
            rows = slice(b * HALF, (b + 1) * HALF)
            diag[b] = jnp.where(lvl == lv, _dot_nt(zq[rows], zk[rows]), diag[b])
        w //= 2
    d0, d1, cr = diag[0].astype(BF16), diag[1].astype(BF16), cross.astype(BF16)
    if backward:
        o_lo = _dot(jnp.concatenate([d0, cr], axis=1), v)
        o_hi = _dot(d1, v[HALF:CH])
    else:
        o_lo = _dot(d0, v[0:HALF])
        o_hi = _dot(jnp.concatenate([cr, d1], axis=1), v)
    o = jnp.concatenate([o_lo, o_hi], axis=0)
    c_last = cum[CH - 1:CH, :]
    k_out = kk * jnp.exp2((pos) if backward else (c_last - cum))
    s_new = _dot_tn(v, k_out.astype(BF16))
    if has_init:
        q_in = q * jnp.exp2((c_last - pos) if backward else cum)
        o = o + _dot_nt(q_in.astype(BF16), s_t.astype(BF16))
        s_new = s_new + s_t * jnp.exp2(c_last)
    return o, s_new


def _hgrn_kernel(q_ref, v_ref, g_ref, f_ref, lbp_ref, ng_ref, *rest, seq_len, nsq, is_ctx):
    if is_ctx:
        y_ref, sfin_ref, oacc, st = rest
    else:
        s0_ref, y_ref, oacc, st = rest
    nchunk = seq_len // CH
    tril = _tri(CH, True)
    lvl_f, lvl_b = _level_ids()

    def head(h, carry):
        for sq in range(nsq):
            base = sq * seq_len
            if not is_ctx:
                for d in range(2):
                    st[2 * sq + d] = s0_ref[sq, d, h].T
                if nchunk > 1:
                    oacc[base:base + seq_len, :] = jnp.zeros((seq_len, LANES), F32)

            def run(ci, dirs, sq=sq, base=base):
                rows = _chunk_rows(ci, base)
                q = _silu(q_ref[h, rows, :].astype(F32))
                v = v_ref[h, rows, :]
                for d in dirs:
                    lrow = pl.ds(d * HG_HEADS + h, 1)
                    o, s_new = _hgrn_chunk(q, v, f_ref[d * HG_HEADS + h, rows, :],
                                           lbp_ref[0, lrow, :], lbp_ref[1, lrow, :],
                                           None if is_ctx else st[2 * sq + d], tril, lvl_b if d else lvl_f,
                                           d == 1, not is_ctx)
                    st[2 * sq + d] = s_new
                    if nchunk == 1 and d == 0:
                        oacc[rows, :] = o
                    else:
                        oacc[rows, :] += o

            _chunk_loop(nchunk, run)
            o = oacc[base:base + seq_len, :]
            y = o * lax.rsqrt(jnp.mean(o * o, -1, keepdims=True) + RMS_EPS) * ng_ref[pl.ds(h, 1), :]
            gate = _silu(g_ref[h, base:base + seq_len, :].astype(F32))
            y_ref[h, base:base + seq_len, :] = (y * gate).astype(y_ref.dtype)
            if is_ctx:
                for d in range(2):
                    sfin_ref[sq, d, h] = st[2 * sq + d].T
        return carry

    lax.fori_loop(0, HG_HEADS, head, 0)


def _seq_call(kern, name, in_specs, args, y_spec, y_shape, st_specs, st_shapes, states, y_prev, scratch, steps,
              is_ctx, **kw):
    in_specs, args = list(in_specs), list(args)
    n_used = len(args)
    aliases = {}
    any_spec = pl.BlockSpec(memory_space=pl.ANY)
    if is_ctx:
        out_specs, out_shape = [y_spec] + list(st_specs), [y_shape] + list(st_shapes)
        for k, s in enumerate([y_prev] + list(states)):
            in_specs.append(any_spec)
            args.append(s)
            aliases[len(args) - 1] = k
    else:
        in_specs += list(st_specs)
        args += list(states)
        n_used = len(args)
        in_specs.append(any_spec)
        args.append(y_prev)
        aliases[len(args) - 1] = 0
        out_specs, out_shape = [y_spec], [y_shape]
    n_args = len(args)

    def body(*refs):
        kern(*refs[:n_used], *refs[n_args:], is_ctx=is_ctx, **kw)

    return pl.pallas_call(
        body, grid=(steps,), in_specs=in_specs, out_specs=out_specs, out_shape=out_shape,
        input_output_aliases=aliases, scratch_shapes=scratch, compiler_params=_cparams(("arbitrary",)),
        name=name + ("_ctx" if is_ctx else "_lat"),
    )(*args)


def _hgrn_call(ub, uf, lbp, ng, states, y_prev, *, is_ctx, seq_len, row0, nseq, nsq, layer, depth):
    t = ub.shape[1]
    rows = nsq * seq_len
    bo = row0 // rows
    nh = HG_HEADS
    in_specs = [pl.BlockSpec((nh, rows, LANES), lambda i: (0, bo + i, 0)),
                pl.BlockSpec((nh, rows, LANES), lambda i: (1, bo + i, 0)),
                pl.BlockSpec((nh, rows, LANES), lambda i: (2, bo + i, 0)),
                pl.BlockSpec((2 * nh, rows, LANES), lambda i: (0, bo + i, 0)),
                pl.BlockSpec((2, 2 * nh, LANES), lambda i: (0, 0, 0)),
                pl.BlockSpec((nh, LANES), lambda i: (0, 0))]
    st_spec, st_shape = _state_io((2, nh, HG_DK, LANES), nseq, nsq, layer, depth)
    return _seq_call(
        _hgrn_kernel, "hgrn", in_specs, [ub, ub, ub, uf, lbp, ng],
        pl.BlockSpec((nh, rows, LANES), lambda i: (0, bo + i, 0)), jax.ShapeDtypeStruct((nh, t, LANES), BF16),
        [st_spec], [st_shape], states, y_prev,
        [pltpu.VMEM((rows, LANES), F32), pltpu.VMEM((2 * nsq, LANES, HG_DK), F32)],
        nseq // nsq, is_ctx, seq_len=seq_len, nsq=nsq)


def _mlstm_prep(gt_ref, bias_ref, rows, ms_s, triu):
    gates = gt_ref[rows, :] + bias_ref[...]
    g_t = gates.T
    i_rows = g_t[0:8, :]
    lf_rows = _log_sigmoid(g_t[8:16, :])
    c_rows = _cumsum_lanes(lf_rows, triu)
    is_b = lax.broadcasted_iota(jnp.int32, (8, CH), 0) >= ML_HEADS
    is_b1 = is_b[:, 0:1]
    m_in = ms_s[:, 0:1]
    c_last = c_rows[:, CH - 1:CH]
    pos = jnp.where(is_b, c_rows - lf_rows, c_rows)
    u = jnp.where(is_b, pos + i_rows, i_rows - pos)
    nu = jnp.where(is_b1, c_last + m_in, m_in)
    mu = jnp.maximum(jnp.where(is_b, _cummax_lanes(u, True), _cummax_lanes(u, False)), nu)
    m_t = jnp.where(is_b, mu - pos, pos + mu)
    mu_end = jnp.where(is_b1, mu[:, 0:1], mu[:, CH - 1:CH])
    m_new = jnp.where(is_b1, mu[:, 0:1], m_t[:, CH - 1:CH])
    x_rows = jnp.concatenate([-mu, -m_t, u, jnp.zeros_like(u)], axis=0) * LOG2E
    return dict(lhs=_sel_lhs(x_rows), u_rows=x_rows[16:24, :], nu=nu * LOG2E, mu_end=mu_end * LOG2E, m_new=m_new)


def _mlstm_main(p, q_ref, k_ref, v_ref, sel_ref, rows, dirs, cn_s, ms_s, oacc, masks, has_init, first_write):
    lhs = p["lhs"]
    scale = ML_DK ** -0.5
    ones_blk = jnp.ones((CH, LANES), BF16)
    heads = range(ML_HEADS)
    hds = [(d, h, d * ML_HEADS + h) for h in heads for d in dirs]
    q = [q_ref[h, rows, :] for h in heads]
    kf = [k_ref[h, rows, :].astype(F32) * scale for h in heads]
    vaug = [jnp.concatenate([v_ref[h, rows, :], ones_blk], axis=1) for h in heads]
    qk = [_dot_nt(q[h], kf[h].astype(BF16)) for h in heads]
    b_mu = {hd: _dot(lhs, sel_ref[hd]) for _, _, hd in hds}
    b_m = {hd: _dot(lhs, sel_ref[8 + hd]) for _, _, hd in hds}
    k_t = [kf[h].T for h in heads]
    s = {}
    for d, h, hd in hds:
        xe = jnp.concatenate([b_mu[hd], b_mu[hd]], axis=1) + p["u_rows"][hd:hd + 1, :]
        s[hd] = (jnp.where(masks[d], jnp.exp2(xe), 0.0) * qk[h]).astype(BF16)
    num = {hd: _dot(s[hd], vaug[h]) for _, h, hd in hds}
    w_end = jnp.exp2(p["u_rows"] - p["mu_end"])
    upd = {hd: _dot((k_t[h] * w_end[hd:hd + 1, :]).astype(BF16), vaug[h]) for _, h, hd in hds}
    if has_init:
        for d, h, hd in hds:
            cn = cn_s[d, h]
            w_int = jnp.exp2(p["nu"][hd:hd + 1, :] + b_mu[hd])
            num[hd] = num[hd] + jnp.concatenate([w_int, w_int], axis=1) * _dot(q[h], cn.astype(BF16))
            upd[hd] = upd[hd] + jnp.exp2(p["nu"][hd:hd + 1, :] - p["mu_end"][hd:hd + 1, :]) * cn
    for d, h, hd in hds:
        cn_s[d, h] = upd[hd]
    for h in heads:
        hsum = None
        for d in dirs:
            hd = d * ML_HEADS + h
            den = jnp.maximum(jnp.abs(num[hd][:, LANES:]), jnp.exp2(b_m[hd]))
            hout = num[hd][:, :LANES] / den
            hsum = hout if hsum is None else hsum + hout
        if first_write:
            oacc[h, rows, :] = hsum
        else:
            oacc[h, rows, :] += hsum
    m_new = jnp.broadcast_to(p["m_new"], (8, LANES))
    if len(dirs) == 2:
        ms_s[...] = m_new
    else:
        row = lax.broadcasted_iota(jnp.int32, (8, LANES), 0)
        mine = (row >= ML_HEADS) if dirs[0] == 1 else (row < ML_HEADS)
        ms_s[...] = jnp.where(mine, m_new, ms_s[...])


def _mlstm_kernel(q_ref, k_ref, v_ref, og_ref, gt_ref, bias_ref, ng_ref, sel_ref, *rest, seq_len, nsq, is_ctx):
    if is_ctx:
        y_ref, cfin_ref, nfin_ref, mfin_ref, oacc, cn_s, ms_s = rest
    else:
        c0_ref, n0_ref, m0_ref, y_ref, oacc, cn_s, ms_s = rest
    nchunk = seq_len // CH
    triu = _tri(CH, False)
    rr = lax.broadcasted_iota(jnp.int32, (CH, CH), 0)
    ss = lax.broadcasted_iota(jnp.int32, (CH, CH), 1)
    masks = (ss <= rr, ss >= rr)
    def main(p, rows, dirs, sq):
        _mlstm_main(p, q_ref, k_ref, v_ref, sel_ref, rows, dirs, cn_s.at[sq], ms_s.at[sq], oacc, masks,
                    not is_ctx, nchunk == 1)

    for sq in range(nsq):
        base = sq * seq_len
        if is_ctx:
            ms_s[sq] = jnp.zeros((2 * ML_HEADS, LANES), F32)
        else:
            ms_s[sq] = m0_ref[sq]
            for d in range(2):
                for h in range(ML_HEADS):
                    hd = d * ML_HEADS + h
                    nb = jnp.broadcast_to(n0_ref[sq, hd:hd + 1, :], (ML_DK, LANES)).T
                    cn_s[sq, d, h] = jnp.concatenate([c0_ref[sq, d, h], nb], axis=1)
            if nchunk > 1:
                oacc[:, base:base + seq_len, :] = jnp.zeros((ML_HEADS, seq_len, LANES), F32)
    if nchunk == 1:
        preps = [_mlstm_prep(gt_ref, bias_ref, _chunk_rows(0, sq * seq_len), ms_s.at[sq], triu) for sq in range(nsq)]
        for sq in range(nsq):
            main(preps[sq], _chunk_rows(0, sq * seq_len), (0, 1), sq)
    else:
        for sq in range(nsq):
            def body(ci, carry, sq=sq):
                rows_f = _chunk_rows(ci, sq * seq_len)
                rows_b = _chunk_rows(nchunk - 1 - ci, sq * seq_len)
                p_f = _mlstm_prep(gt_ref, bias_ref, rows_f, ms_s.at[sq], triu)
                p_b = _mlstm_prep(gt_ref, bias_ref, rows_b, ms_s.at[sq], triu)
                main(p_f, rows_f, (0,), sq)
                main(p_b, rows_b, (1,), sq)
                return carry
            lax.fori_loop(0, nchunk, body, 0)
    for sq in range(nsq):
        base = sq * seq_len
        for h in range(ML_HEADS):
            o = oacc[h, base:base + seq_len, :]
            y = o * lax.rsqrt(jnp.mean(o * o, -1, keepdims=True) + RMS_EPS) * ng_ref[h:h + 1, :]
            gate = _sigmoid(og_ref[h, base:base + seq_len, :].astype(F32))
            y_ref[h, base:base + seq_len, :] = (y * gate).astype(y_ref.dtype)
        if is_ctx:
            for d in range(2):
                for h in range(ML_HEADS):
                    cn = cn_s[sq, d, h]
                    cfin_ref[sq, d, h] = cn[:, :LANES]
                    nfin_ref[sq, pl.ds(d * ML_HEADS + h, 1), :] = cn[:, LANES:].T[0:1, :]
            mfin_ref[sq] = ms_s[sq]


def _mlstm_call(ub, uf, bias_row, ng, sel, states, y_prev, *, is_ctx, seq_len, row0, nseq, nsq, layer, depth):
    t = ub.shape[1]
    rows = nsq * seq_len
    bo = row0 // rows
    nh = ML_HEADS
    slab = lambda k: pl.BlockSpec((nh, rows, LANES), lambda i: (k, bo + i, 0))
    in_specs = [slab(3), slab(4), slab(5), slab(6),
                pl.BlockSpec((None, rows, LANES), lambda i: (GATE_SLAB, bo + i, 0)),
                pl.BlockSpec((1, LANES), lambda i: (0, 0)),
                pl.BlockSpec((nh, LANES), lambda i: (0, 0)),
                pl.BlockSpec((N_SEL, LANES, LANES), lambda i: (0, 0, 0))]
    c_spec, c_shape = _state_io((2, nh, ML_DK, LANES), nseq, nsq, layer, depth)
    v_spec, v_shape = _state_io((2 * nh, LANES), nseq, nsq, layer, depth)
    return _seq_call(
        _mlstm_kernel, "mlstm", in_specs, [ub, ub, ub, ub, uf, bias_row, ng, sel],
        pl.BlockSpec((nh, rows, LANES), lambda i: (0, bo + i, 0)), jax.ShapeDtypeStruct((nh, t, LANES), BF16),
        [c_spec, v_spec, v_spec], [c_shape, v_shape, v_shape], states, y_prev,
        [pltpu.VMEM((nh, rows, LANES), F32), pltpu.VMEM((nsq, 2, nh, ML_DK, 2 * LANES), F32),
         pltpu.VMEM((nsq, 2 * nh, LANES), F32)],
        nseq // nsq, is_ctx, seq_len=seq_len, nsq=nsq)


N_PAIR = MB_HEADS // 2
PAIRS_PER_GROUP = N_PAIR // MB_GROUPS


def _ssd_prep(gt_ref, bias_ref, nega_ref, rows, triu):
    gates = gt_ref[rows, :] + bias_ref[...]
    g_t = gates.T
    dt_rows = _softplus(g_t[16:32, :])
    la_rows = dt_rows * nega_ref[...]
    c_rows = _cumsum_lanes(la_rows, triu)
    ldt = jnp.log(dt_rows)
    is_b = lax.broadcasted_iota(jnp.int32, (16, CH), 0) >= MB_HEADS
    pos = jnp.where(is_b, c_rows - la_rows, c_rows)
    a_col = jnp.where(is_b, -pos, pos) * LOG2E
    r_row = jnp.where(is_b, pos + ldt, ldt - pos) * LOG2E
    c_last = c_rows[:, CH - 1:CH] * LOG2E
    lhs = _sel_lhs(jnp.concatenate([a_col, jnp.zeros_like(a_col)], axis=0))
    return dict(lhs=lhs, r_row=r_row, c_last=c_last)


def _ssd_main(p, xs_s, bcs_s, sel_ref, rows, dirs, hst, yacc, masks, has_init, first_write):
    lhs, r_row, c_last = p["lhs"], p["r_row"], p["c_last"]
    lane = lax.broadcasted_iota(jnp.int32, (CH, LANES), 1)
    lo = lane < MB_HEADDIM
    lo_state = lax.broadcasted_iota(jnp.int32, (LANES, LANES), 1) < MB_HEADDIM
    lo_row = lo_state[0:1, :]
    bblk = bcs_s[rows, 0:LANES]
    cblk = bcs_s[rows, LANES:2 * LANES]
    c_g, b_g, gm = [], [], []
    for grp in range(MB_GROUPS):
        in_grp = (lane >= grp * MB_DSTATE) & (lane < (grp + 1) * MB_DSTATE)
        c_g.append(jnp.where(in_grp, cblk, 0.0))
        b_g.append(jnp.where(in_grp, bblk, 0.0))
        gm.append(_dot_nt(c_g[grp].astype(BF16), b_g[grp].astype(BF16)))
    xh = []
    for j in range(N_PAIR):
        xpair = xs_s[rows, j * LANES:(j + 1) * LANES]
        xh.append((jnp.where(lo, xpair, 0.0).astype(BF16), jnp.where(lo, 0.0, xpair).astype(BF16)))
    items = [(d, j, half, d * MB_HEADS + 2 * j + half) for j in range(N_PAIR) for d in dirs for half in range(2)]
    b_a = {hd: _dot(lhs, sel_ref[hd]) for _, _, _, hd in items}
    b_t = [b_g[grp].T for grp in range(MB_GROUPS)]
    is_f = lax.broadcasted_iota(jnp.int32, (2 * MB_HEADS, 1), 0) < MB_HEADS
    w_end = jnp.exp2(r_row + jnp.where(is_f, c_last, 0.0))
    m, b_out, c_in = {}, {}, {}
    for d, j, half, hd in items:
        grp = j // PAIRS_PER_GROUP
        cl = c_last[hd:hd + 1, :]
        xe = jnp.concatenate([b_a[hd], b_a[hd]], axis=1) + r_row[hd:hd + 1, :]
        m[hd] = (jnp.where(masks[d], jnp.exp2(xe), 0.0) * gm[grp]).astype(BF16)
        b_out[hd] = (b_t[grp] * w_end[hd:hd + 1, :]).astype(BF16)
        if has_init:
            c_in[hd] = (c_g[grp] * jnp.exp2(b_a[hd] + cl if d else b_a[hd])).astype(BF16)
    yy = {hd: _dot(m[hd], xh[j][half]) for _, j, half, hd in items}
    uu = {hd: _dot(b_out[hd], xh[j][half]) for _, j, half, hd in items}
    for j in range(N_PAIR):
        ysum = None
        for d in dirs:
            hd0, hd1 = d * MB_HEADS + 2 * j, d * MB_HEADS + 2 * j + 1
            y = yy[hd0] + yy[hd1]
            upd = uu[hd0] + uu[hd1]
            if has_init:
                ht = hst[d, j]
                y = y + _dot(c_in[hd0], jnp.where(lo_state, ht, 0.0).astype(BF16))
                y = y + _dot(c_in[hd1], jnp.where(lo_state, 0.0, ht).astype(BF16))
                decay = jnp.where(lo_row, jnp.exp2(c_last[hd0:hd0 + 1, :]), jnp.exp2(c_last[hd1:hd1 + 1, :]))
                upd = upd + ht * decay
            hst[d, j] = upd
            ysum = y if ysum is None else ysum + y
        if first_write:
            yacc[rows, j * LANES:(j + 1) * LANES] = ysum
        else:
            yacc[rows, j * LANES:(j + 1) * LANES] += ysum


def _ssd_kernel(x_ref, bc_ref, z_ref, gt_ref, cw_ref, bias_ref, nega_ref, dskip_ref, ng_ref, sel_ref, *rest,
                seq_len, nsq, is_ctx):
    if is_ctx:
        y_ref, hfin_ref, xs_s, bcs_s, yacc, hst = rest
    else:
        h0_ref, y_ref, xs_s, bcs_s, yacc, hst = rest
    nchunk = seq_len // CH
    triu = _tri(CH, False)
    rr = lax.broadcasted_iota(jnp.int32, (CH, CH), 0)
    ss = lax.broadcasted_iota(jnp.int32, (CH, CH), 1)
    masks = (ss <= rr, ss >= rr)

    def conv(v, lo_col):
        row = lax.broadcasted_iota(jnp.int32, v.shape, 0)
        prev = jnp.where(row == 0, 0.0, pltpu.roll(v, 1, 0))
        nxt = jnp.where(row == seq_len - 1, 0.0, pltpu.roll(v, seq_len - 1, 0))
        cs = slice(lo_col, lo_col + v.shape[1])
        return _silu(cw_ref[0:1, cs] * prev + cw_ref[1:2, cs] * v + cw_ref[2:3, cs] * nxt + cw_ref[3:4, cs])

    for sq in range(nsq):
        base = sq * seq_len
        seq_rows = slice(base, base + seq_len)
        for k in range(MIX_W // LANES):
            xs_s[seq_rows, k * LANES:(k + 1) * LANES] = conv(x_ref[k, seq_rows, :], k * LANES)
        for k in range(2):
            bcs_s[seq_rows, k * LANES:(k + 1) * LANES] = conv(bc_ref[k, seq_rows, :], MIX_W + k * LANES)
        if not is_ctx:
            hst[sq] = h0_ref[sq]
            if nchunk > 1:
                yacc[seq_rows, :] = jnp.zeros((seq_len, MIX_W), F32)

    def prep(rows):
        return _ssd_prep(gt_ref, bias_ref, nega_ref, rows, triu)

    def main(p, rows, dirs, sq):
        _ssd_main(p, xs_s, bcs_s, sel_ref, rows, dirs, hst.at[sq], yacc, masks, not is_ctx, nchunk == 1)

    if nchunk == 1:
        preps = [prep(_chunk_rows(0, sq * seq_len)) for sq in range(nsq)]
        for sq in range(nsq):
            main(preps[sq], _chunk_rows(0, sq * seq_len), (0, 1), sq)
    else:
        for sq in range(nsq):
            def body(ci, carry, sq=sq):
                rows_f = _chunk_rows(ci, sq * seq_len)
                rows_b = _chunk_rows(nchunk - 1 - ci, sq * seq_len)
                p_f, p_b = prep(rows_f), prep(rows_b)
                main(p_f, rows_f, (0,), sq)
                main(p_b, rows_b, (1,), sq)
                return carry
            lax.fori_loop(0, nchunk, body, 0)

    for sq in range(nsq):
        base = sq * seq_len
        seq_rows = slice(base, base + seq_len)
        z = jnp.concatenate([z_ref[k, seq_rows, :] for k in range(MIX_W // LANES)], axis=1).astype(F32)
        y = (yacc[seq_rows, :] + dskip_ref[...] * xs_s[seq_rows, :]) * _silu(z)
        y = (y * lax.rsqrt(jnp.mean(y * y, -1, keepdims=True) + RMS_EPS) * ng_ref[...]).astype(y_ref.dtype)
        for k in range(MIX_W // LANES):
            y_ref[k, seq_rows, :] = y[:, k * LANES:(k + 1) * LANES]
        if is_ctx:
            hfin_ref[sq] = hst[sq]


def _ssd_call(ub, uf, cw, bias_row, nega_rows, dskip, ng, sel, states, y_prev, *, is_ctx, seq_len, row0, nseq, nsq,
              layer, depth):
    t = ub.shape[1]
    rows = nsq * seq_len
    bo = row0 // rows
    ns = MIX_W // LANES
    row_spec = lambda w: pl.BlockSpec((1, w), lambda i: (0, 0))
    in_specs = [pl.BlockSpec((ns, rows, LANES), lambda i: (2, bo + i, 0)),
                pl.BlockSpec((2, rows, LANES), lambda i: (6, bo + i, 0)),
                pl.BlockSpec((ns, rows, LANES), lambda i: (7, bo + i, 0)),
                pl.BlockSpec((None, rows, LANES), lambda i: (GATE_SLAB, bo + i, 0)),
                pl.BlockSpec((8, MB_XBC), lambda i: (0, 0)),
                row_spec(LANES),
                pl.BlockSpec((2 * MB_HEADS, CH), lambda i: (0, 0)),
                row_spec(MIX_W), row_spec(MIX_W),
                pl.BlockSpec((N_SEL, LANES, LANES), lambda i: (0, 0, 0))]
    st_spec, st_shape = _state_io((2, N_PAIR, LANES, LANES), nseq, nsq, layer, depth)
    return _seq_call(
        _ssd_kernel, "ssd", in_specs, [uf, uf, ub, uf, cw, bias_row, nega_rows, dskip, ng, sel],
        pl.BlockSpec((ns, rows, LANES), lambda i: (0, bo + i, 0)), jax.ShapeDtypeStruct((ns, t, LANES), BF16),
        [st_spec], [st_shape], states, y_prev,
        [pltpu.VMEM((rows, MIX_W), F32), pltpu.VMEM((rows, 2 * LANES), F32), pltpu.VMEM((rows, MIX_W), F32),
         pltpu.VMEM((nsq, 2, N_PAIR, LANES, LANES), F32)],
        nseq // nsq, is_ctx, seq_len=seq_len, nsq=nsq)


def _ssd_state_to_pairs(s):
    bsz = s.shape[0]
    st = jnp.swapaxes(s, -1, -2).reshape(bsz, 2, N_PAIR, 2, MB_DSTATE, MB_HEADDIM)
    st = jnp.moveaxis(st, 3, 4).reshape(bsz, 2, N_PAIR, MB_DSTATE, 2 * MB_HEADDIM)
    zero = jnp.zeros_like(st)
    grp = (jnp.arange(N_PAIR) // PAIRS_PER_GROUP).reshape(1, 1, N_PAIR, 1, 1)
    return jnp.concatenate([jnp.where(grp == 0, st, zero), jnp.where(grp == 1, st, zero)], axis=3)


def _ssd_pairs_to_state(hp):
    bsz = hp.shape[0]
    halves = hp.reshape(bsz, 2, N_PAIR, MB_GROUPS, MB_DSTATE, 2 * MB_HEADDIM)
    grp = (jnp.arange(N_PAIR) // PAIRS_PER_GROUP).reshape(1, 1, N_PAIR, 1, 1)
    st = jnp.where(grp == 0, halves[:, :, :, 0], halves[:, :, :, 1])
    st = st.reshape(bsz, 2, N_PAIR, MB_DSTATE, 2, MB_HEADDIM)
    st = jnp.moveaxis(st, 4, 3).reshape(bsz, 2, MB_HEADS, MB_DSTATE, MB_HEADDIM)
    return jnp.swapaxes(st, -1, -2)


def _grid_pos_embed(n_tok, d_model):
    rows = n_tok // GRID_W
    r, cidx = jnp.meshgrid(jnp.arange(rows, dtype=F32), jnp.arange(GRID_W, dtype=F32), indexing='ij')
    quarter = d_model // 4
    freq = jnp.exp(-math.log(10000.0) * jnp.arange(quarter, dtype=F32) / quarter)
    ar = r.reshape(-1, 1) * freq
    ac = cidx.reshape(-1, 1) * freq
    return jnp.concatenate([jnp.sin(ar), jnp.cos(ar), jnp.sin(ac), jnp.cos(ac)], axis=-1)


def _gate_row(pieces):
    v = jnp.concatenate([p.reshape(-1).astype(F32) for p in pieces])
    return jnp.concatenate([v, v, v, jnp.zeros((LANES - 3 * N_SEL,), F32)]).reshape(1, LANES)


def kernel(x_prompt, x_sample, state_hgrn, state_mlstm_C, state_mlstm_n, state_mlstm_m, state_ssd, c, c_ctx,
           w_mod, b_mod, ln_g, ln_b, ffn_w_gu, ffn_w_down, w_in, hg_lb, hg_norm_g, ml_gate_b, ml_norm_g,
           mb_conv_w, mb_conv_b, mb_dt_bias, mb_a_log, mb_d, mb_norm_g, w_branch, w_out):
    bsz, seq, d = x_prompt.shape
    dbsz, dseq, _ = x_sample.shape
    depth = w_mod.shape[0]
    t_ctx = bsz * seq
    n_ctx_tiles = t_ctx // ROW_TILE
    assert dseq == ROW_TILE and t_ctx % ROW_TILE == 0 and seq == CH and dseq % CH == 0

    xs0 = x_sample + _grid_pos_embed(dseq, d).astype(x_sample.dtype)[None]
    x = jnp.concatenate([x_prompt.reshape(t_ctx, d), xs0.reshape(dbsz * dseq, d)], axis=0)

    cv = jnp.concatenate([c_ctx[None]] * MOD_LAT_ROW + [c, jnp.zeros((8 - MOD_LAT_ROW - dbsz, d), F32)], axis=0)
    mod4 = _mod_call(cv, w_mod, b_mod).reshape(depth, 8, 1, N_MOD * d)

    w_gu_b = ffn_w_gu.astype(BF16)
    w_down_b = ffn_w_down.astype(BF16)
    w_branch_b = w_branch.astype(BF16)
    w_out_b = w_out.astype(BF16)
    w_in_b = w_in.astype(BF16)
    w_a = jnp.concatenate([w_in_b[:, :, 0:1536], w_in_b[:, :, 2560:4608], w_in_b[:, :, 4624:5136],
                           w_in_b[:, :, 5920:8992]], axis=2)
    w_gate = jnp.concatenate([w_in_b[:, :, 4608:4624], w_in_b[:, :, 5904:5920]], axis=2)
    w_b = jnp.concatenate([w_in_b[:, :, 1536:2560], w_in_b[:, :, 5136:5904], w_gate, w_gate, w_gate,
                           jnp.zeros((depth, d, LANES - 3 * N_SEL), BF16)], axis=2)
    ln_g4 = ln_g.reshape(depth, 3, 1, d)
    ln_b4 = ln_b.reshape(depth, 3, 1, d)

    lbs = jnp.cumsum(jax.nn.softmax(hg_lb.astype(F32), axis=0), axis=0)
    lbs = (lbs - lbs[0]).reshape(depth, 2 * HG_HEADS, HG_DK)
    lbp = jnp.stack([lbs, 1.0 - lbs], axis=1)
    cw = jnp.concatenate([mb_conv_w, mb_conv_b[:, None, :], jnp.zeros((depth, 4, MB_XBC), F32)], axis=1)
    sel = _selectors()

    lat_h0 = _ssd_state_to_pairs(state_ssd.reshape((dbsz * depth,) + state_ssd.shape[2:])).reshape(
        (dbsz, depth, 2, N_PAIR, LANES, LANES))
    lat_n0 = state_mlstm_n.reshape(dbsz, depth, 2 * ML_HEADS, ML_DK)
    lat_m0 = jnp.broadcast_to(state_mlstm_m.reshape(dbsz, depth, 2 * ML_HEADS, 1), (dbsz, depth, 2 * ML_HEADS, LANES))

    zeros = lambda *s: jnp.zeros((bsz, depth) + s, F32)
    st_hg = [zeros(2, HG_HEADS, HG_DK, LANES)]
    st_ml = [zeros(2, ML_HEADS, ML_DK, LANES), zeros(2 * ML_HEADS, LANES), zeros(2 * ML_HEADS, LANES)]
    st_ss = [zeros(2, N_PAIR, LANES, LANES)]
    y_fill = lambda: jnp.zeros((MIX_W // LANES, x.shape[0], LANES), BF16)
    ctx = dict(is_ctx=True, seq_len=seq, row0=0, nseq=bsz, nsq=CTX_SEQS_PER_STEP, depth=depth)
    lat = dict(is_ctx=False, seq_len=dseq, row0=t_ctx, nseq=dbsz, nsq=1, depth=depth)
    for l in range(depth):
        x = _ffn_call(x, mod4, w_gu_b, w_down_b, ln_g4, ln_b4, l, 0, n_ctx_tiles)
        ub = _inproj_call(x, mod4, w_a, l, INPROJ_TILE_A, BF16, n_ctx_tiles, "inproj_a")
        uf = _inproj_call(x, mod4, w_b, l, INPROJ_TILE_B, F32, n_ctx_tiles, "inproj_b")

        ng_h = hg_norm_g[l].reshape(HG_HEADS, LANES)
        y_hg, *st_hg = _hgrn_call(ub, uf, lbp[l], ng_h, st_hg, y_fill(), layer=l, **ctx)
        y_hg, = _hgrn_call(ub, uf, lbp[l], ng_h, [state_hgrn], y_hg, layer=l, **lat)

        gate_bias = _gate_row([ml_gate_b[l, 0], ml_gate_b[l, 1], mb_dt_bias[l]])
        ng_m = ml_norm_g[l].reshape(ML_HEADS, LANES)
        y_ml, *st_ml = _mlstm_call(ub, uf, gate_bias, ng_m, sel, st_ml, y_fill(), layer=l, **ctx)
        y_ml, = _mlstm_call(ub, uf, gate_bias, ng_m, sel, [state_mlstm_C, lat_n0, lat_m0], y_ml, layer=l, **lat)

        nega = jnp.broadcast_to(-jnp.exp(mb_a_log[l].astype(F32)).reshape(2 * MB_HEADS, 1), (2 * MB_HEADS, CH))
        dskip = jnp.repeat(mb_d[l], MB_HEADDIM).reshape(1, MIX_W)
        ng_s = mb_norm_g[l].reshape(1, MIX_W)
        y_mb, *st_ss = _ssd_call(ub, uf, cw[l], gate_bias, nega, dskip, ng_s, sel, st_ss, y_fill(), layer=l, **ctx)
        y_mb, = _ssd_call(ub, uf, cw[l], gate_bias, nega, dskip, ng_s, sel, [lat_h0], y_mb, layer=l, **lat)

        x = _mixout_call(x, mod4, y_hg, y_ml, y_mb, ub, w_branch_b, w_out_b, ln_g4, ln_b4, l, n_ctx_tiles)
        x = _ffn_call(x, mod4, w_gu_b, w_down_b, ln_g4, ln_b4, l, 1, n_ctx_tiles)

    y_prompt = x[:t_ctx].reshape(bsz, seq, d)
    y_sample = x[t_ctx:].reshape(dbsz, dseq, d)
    c_fin, n_fin, m_fin = st_ml
    h_fin = _ssd_pairs_to_state(st_ss[0].reshape((bsz * depth,) + st_ss[0].shape[2:]))
    return (y_prompt, y_sample, st_hg[0], c_fin, n_fin.reshape(bsz, depth, 2, ML_HEADS, ML_DK),
            m_fin[:, :, :, 0].reshape(bsz, depth, 2, ML_HEADS),
            h_fin.reshape((bsz, depth) + h_fin.shape[1:]))
```

```python
import functools
import math

import jax
import jax.numpy as jnp
from jax import lax
from jax.experimental import pallas as pl
from jax.experimental.pallas import tpu as pltpu

F32 = jnp.float32
BF16 = jnp.bfloat16

D_MODEL = 1024
DEPTH = 4
GRID_W = 64
MIX_W = 512
HG_HEADS = 4
HG_DK = 128
ML_HEADS = 4
ML_DK = 128
MB_HEADS = 8
MB_HEADDIM = 64
MB_GROUPS = 2
MB_DSTATE = 64
MB_XBC = MIX_W + 2 * MB_GROUPS * MB_DSTATE
D_FF = 2816
N_MOD = 9
ALPHA = (2 * DEPTH) ** 0.25
LN_EPS = 1e-5
RMS_EPS = 1e-6
LOG2E = 1.4426950408889634

LANES = 128
CH = 256
HALF = CH // 2
ROW_TILE = 1024
FFN_ROWS = 2048
MOD_LAT_ROW = 2
FF_TILE = 256
INPROJ_TILE_A = 1792
INPROJ_TILE_B = 1920
CTX_SEQS_PER_STEP = 4
VMEM_LIMIT = 56 * 1024 * 1024

UB_SLABS = 56
UF_SLABS = 15
GATE_SLAB = 14
N_SEL = 32
LOG2_TINY = -150.0


def _dot(a, b):
    return jnp.dot(a, b, preferred_element_type=F32)


def _dot_nt(a, b):
    return lax.dot_general(a, b, (((1,), (1,)), ((), ())), preferred_element_type=F32)


def _dot_tn(a, b):
    return lax.dot_general(a, b, (((0,), (0,)), ((), ())), preferred_element_type=F32)


def _sigmoid(x):
    return 1.0 / (1.0 + jnp.exp(-x))


def _silu(x):
    return x * _sigmoid(x)


def _log_sigmoid(x):
    return jnp.minimum(x, 0.0) - jnp.log(1.0 + jnp.exp(-jnp.abs(x)))


def _softplus(x):
    return jnp.maximum(x, 0.0) + jnp.log(1.0 + jnp.exp(-jnp.abs(x)))


def _neg_abs(x):
    return -jnp.abs(x)


def _ln(z, g, b):
    mu = jnp.mean(z, -1, keepdims=True)
    d = z - mu
    var = jnp.mean(d * d, -1, keepdims=True)
    return d * lax.rsqrt(var + LN_EPS) * g + b


def _split3(x):
    h = x.astype(BF16)
    r = x - h.astype(F32)
    m = r.astype(BF16)
    l = (r - m.astype(F32)).astype(BF16)
    return h, m, l


def _cumsum_rows(tril, x):
    h, m, l = _split3(x)
    return _dot(tril, h) + _dot(tril, m) + _dot(tril, l)


def _cumsum_lanes(x, triu):
    h, m, l = _split3(x)
    return _dot(h, triu) + _dot(m, triu) + _dot(l, triu)


def _cummax_lanes(x, reverse):
    n = x.shape[1]
    lane = lax.broadcasted_iota(jnp.int32, x.shape, 1)
    k = 1
    while k < n:
        if reverse:
            sh = jnp.where(lane < n - k, pltpu.roll(x, n - k, 1), -jnp.inf)
        else:
            sh = jnp.where(lane >= k, pltpu.roll(x, k, 1), -jnp.inf)
        x = jnp.maximum(x, sh)
        k *= 2
    return x


def _tri(c, lower):
    r = lax.broadcasted_iota(jnp.int32, (c, c), 0)
    s = lax.broadcasted_iota(jnp.int32, (c, c), 1)
    return jnp.where((s <= r) if lower else (r <= s), 1.0, 0.0).astype(BF16)


def _sel_lhs(x_rows):
    h = x_rows.astype(BF16).astype(F32)
    r = x_rows - h
    m = r.astype(BF16).astype(F32)
    rows = jnp.concatenate([h, m, r - m, jnp.ones_like(x_rows)], axis=0)
    return rows.T.astype(BF16)


def _selectors():
    r = jnp.arange(LANES)
    ch = jnp.arange(N_SEL)
    hit = ((r[None, :] % N_SEL) == ch[:, None]) & (r[None, :] < 3 * N_SEL)
    return jnp.broadcast_to(hit[:, :, None], (N_SEL, LANES, LANES)).astype(BF16)


def _cparams(sem):
    return pltpu.CompilerParams(dimension_semantics=sem, vmem_limit_bytes=VMEM_LIMIT)


def _chunk_rows(ci, base=0):
    r0 = ci * CH + base
    if not isinstance(r0, int):
        r0 = pl.multiple_of(r0, CH)
    return pl.ds(r0, CH)


def _state_io(shape_tail, nseq, nsq, layer, depth):
    nz = (0,) * len(shape_tail)
    spec = pl.BlockSpec((nsq, None) + shape_tail, lambda i: (i, layer) + nz)
    shape = jax.ShapeDtypeStruct((nseq, depth) + shape_tail, F32)
    return spec, shape


def _chunk_loop(nchunk, run):
    if nchunk == 1:
        run(0, (0, 1))
    else:
        def body(ci, carry):
            run(ci, (0,))
            run(nchunk - 1 - ci, (1,))
            return carry
        lax.fori_loop(0, nchunk, body, 0)


def _mod_kernel(c_ref, w_ref, b_ref, o_ref):
    a = _silu(c_ref[...]).astype(BF16)
    o_ref[...] = _dot(a, w_ref[...].astype(BF16)) + b_ref[...]


def _mod_call(cv, w_mod, b_mod):
    depth, d, e = w_mod.shape
    tn = 1024
    return pl.pallas_call(
        _mod_kernel,
        grid=(depth, e // tn),
        in_specs=[pl.BlockSpec((8, d), lambda l, j: (0, 0)),
                  pl.BlockSpec((None, d, tn), lambda l, j: (l, 0, j)),
                  pl.BlockSpec((None, 1, tn), lambda l, j: (l, 0, j))],
        out_specs=pl.BlockSpec((None, 8, tn), lambda l, j: (l, 0, j)),
        out_shape=jax.ShapeDtypeStruct((depth, 8, e), F32),
        compiler_params=_cparams(("arbitrary", "arbitrary")),
        name="mod",
    )(cv, w_mod, b_mod.reshape(depth, 1, e))


def _mod_row(i, n_ctx_tiles):
    return jnp.maximum(i - n_ctx_tiles + MOD_LAT_ROW, 0)


def _ffn_kernel(x_ref, mod_ref, wa_ref, wb_ref, wd_ref, g_ref, b_ref, o_ref, xm_s, *, nf):
    j = pl.program_id(1)
    d = x_ref.shape[1]
    groups = [(r, slice(r * ROW_TILE, (r + 1) * ROW_TILE)) for r in range(FFN_ROWS // ROW_TILE)]

    @pl.when(j == 0)
    def _():
        for r, rows in groups:
            sh = mod_ref[r:r + 1, 0:d]
            sc = mod_ref[r:r + 1, d:2 * d]
            xm_s[rows, :] = (x_ref[rows, :] * (1.0 + sc) + sh).astype(BF16)
        o_ref[...] = jnp.zeros_like(o_ref)

    xm = xm_s[...]
    a = _dot(xm, wa_ref[...].astype(BF16))
    b = _dot(xm, wb_ref[...].astype(BF16))
    o_ref[...] += _dot((_silu(a) * b).astype(BF16), wd_ref[...].astype(BF16))

    @pl.when(j == nf - 1)
    def _():
        for r, rows in groups:
            gate = mod_ref[r:r + 1, 2 * d:3 * d]
            z = ALPHA * x_ref[rows, :] + 0.5 * gate * o_ref[rows, :]
            o_ref[rows, :] = _ln(z, g_ref[...], b_ref[...])


def _ffn_call(x, mod4, w_gu, w_down, ln_g, ln_b, layer, which, n_ctx_tiles):
    t, d = x.shape
    nf = D_FF // FF_TILE
    per = FFN_ROWS // ROW_TILE
    assert nf >= 3 and nf * FF_TILE == D_FF and n_ctx_tiles % per == 0 and MOD_LAT_ROW % per == 0
    sub = 0 if which == 0 else 2
    mod_pairs = mod4.reshape(mod4.shape[0], mod4.shape[1] // per, per, mod4.shape[3])
    first_lat = n_ctx_tiles // per
    mod_blk = lambda i, j: (layer, jnp.maximum(i - first_lat + MOD_LAT_ROW // per, 0), 0, sub)
    return pl.pallas_call(
        functools.partial(_ffn_kernel, nf=nf),
        grid=(t // FFN_ROWS, nf),
        in_specs=[pl.BlockSpec((FFN_ROWS, d), lambda i, j: (i, 0)),
                  pl.BlockSpec((None, None, per, 3 * d), mod_blk),
                  pl.BlockSpec((None, None, d, FF_TILE), lambda i, j: (layer, which, 0, j)),
                  pl.BlockSpec((None, None, d, FF_TILE), lambda i, j: (layer, which, 0, j + nf)),
                  pl.BlockSpec((None, None, FF_TILE, d), lambda i, j: (layer, which, j, 0)),
                  pl.BlockSpec((None, None, 1, d), lambda i, j: (layer, sub, 0, 0)),
                  pl.BlockSpec((None, None, 1, d), lambda i, j: (layer, sub, 0, 0))],
        out_specs=pl.BlockSpec((FFN_ROWS, d), lambda i, j: (i, 0)),
        out_shape=jax.ShapeDtypeStruct((t, d), F32),
        scratch_shapes=[pltpu.VMEM((FFN_ROWS, d), BF16)],
        compiler_params=_cparams(("arbitrary", "arbitrary")),
        name=f"ffn{which}",
    )(x, mod_pairs, w_gu, w_gu, w_down, ln_g, ln_b)


def _inproj_kernel(x_ref, mod_ref, w_ref, o_ref, xm_s):
    d = x_ref.shape[1]

    @pl.when(pl.program_id(1) == 0)
    def _():
        sh = mod_ref[:, 0:d]
        sc = mod_ref[:, d:2 * d]
        xm_s[...] = (x_ref[...] * (1.0 + sc) + sh).astype(BF16)

    res = _dot(xm_s[...], w_ref[...]).astype(o_ref.dtype)
    for k in range(o_ref.shape[0]):
        o_ref[k] = res[:, k * LANES:(k + 1) * LANES]


def _inproj_call(x, mod4, w, layer, tn, out_dtype, n_ctx_tiles, name):
    t, d = x.shape
    n = w.shape[2]
    return pl.pallas_call(
        _inproj_kernel,
        grid=(t // ROW_TILE, n // tn),
        in_specs=[pl.BlockSpec((ROW_TILE, d), lambda i, j: (i, 0)),
                  pl.BlockSpec((None, None, 1, 3 * d), lambda i, j: (layer, _mod_row(i, n_ctx_tiles), 0, 1)),
                  pl.BlockSpec((None, d, tn), lambda i, j: (layer, 0, j))],
        out_specs=pl.BlockSpec((tn // LANES, ROW_TILE, LANES), lambda i, j: (j, i, 0)),
        out_shape=jax.ShapeDtypeStruct((n // LANES, t, LANES), out_dtype),
        scratch_shapes=[pltpu.VMEM((ROW_TILE, d), BF16)],
        compiler_params=_cparams(("arbitrary", "arbitrary")),
        name=name,
    )(x, mod4, w)


def _slabs(ref):
    return jnp.concatenate([ref[k] for k in range(ref.shape[0])], axis=1)


def _mixout_kernel(x_ref, mod_ref, yhc_ref, ymc_ref, ysc_ref, yhl_ref, yml_ref, ysl_ref, g0_ref, g1_ref, g2_ref,
                   wb_ref, wo_ref, lg_ref, lb_ref, o_ref, *, n_ctx_steps):
    d = x_ref.shape[1]
    is_lat = pl.program_id(0) >= n_ctx_steps

    def branch(c_ref, l_ref):
        return jnp.where(is_lat, _slabs(l_ref), _slabs(c_ref))

    p = _sigmoid(_slabs(g0_ref).astype(F32)) * _dot(branch(yhc_ref, yhl_ref), wb_ref[0])
    p += _sigmoid(_slabs(g1_ref).astype(F32)) * _dot(branch(ymc_ref, yml_ref), wb_ref[1])
    p += _sigmoid(_slabs(g2_ref).astype(F32)) * _dot(branch(ysc_ref, ysl_ref), wb_ref[2])
    y = _dot(p.astype(BF16), wo_ref[...])
    gate = mod_ref[:, 2 * d:3 * d]
    z = ALPHA * x_ref[...] + gate * y
    o_ref[...] = _ln(z, lg_ref[...], lb_ref[...])


def _mixout_call(x, mod4, y_ctx, y_lat, ub, w_branch, w_out, ln_g, ln_b, layer, n_ctx_tiles):
    t, d = x.shape
    tm = 512
    per = ROW_TILE // tm
    ns = MIX_W // LANES
    ng = d // LANES
    g0 = (UB_SLABS - 3 * ng) // ng
    n_ctx_steps = n_ctx_tiles * per
    n_lat_steps = t // tm - n_ctx_steps
    yc_spec = pl.BlockSpec((ns, tm, LANES), lambda i: (0, jnp.minimum(i, n_ctx_steps - 1), 0))
    yl_spec = pl.BlockSpec((ns, tm, LANES), lambda i: (0, jnp.clip(i - n_ctx_steps, 0, n_lat_steps - 1), 0))
    return pl.pallas_call(
        functools.partial(_mixout_kernel, n_ctx_steps=n_ctx_steps),
        grid=(t // tm,),
        in_specs=[pl.BlockSpec((tm, d), lambda i: (i, 0)),
                  pl.BlockSpec((None, None, 1, 3 * d), lambda i: (layer, _mod_row(i // per, n_ctx_tiles), 0, 1)),
                  yc_spec, yc_spec, yc_spec, yl_spec, yl_spec, yl_spec,
                  pl.BlockSpec((ng, tm, LANES), lambda i: (g0, i, 0)),
                  pl.BlockSpec((ng, tm, LANES), lambda i: (g0 + 1, i, 0)),
                  pl.BlockSpec((ng, tm, LANES), lambda i: (g0 + 2, i, 0)),
                  pl.BlockSpec((None, 3, MIX_W, d), lambda i: (layer, 0, 0, 0)),
                  pl.BlockSpec((None, d, d), lambda i: (layer, 0, 0)),
                  pl.BlockSpec((None, None, 1, d), lambda i: (layer, 1, 0, 0)),
                  pl.BlockSpec((None, None, 1, d), lambda i: (layer, 1, 0, 0))],
        out_specs=pl.BlockSpec((tm, d), lambda i: (i, 0)),
        out_shape=jax.ShapeDtypeStruct((t, d), F32),
        compiler_params=_cparams(("arbitrary",)),
        name="mixout",
    )(x, mod4, *y_ctx, *y_lat, ub, ub, ub, w_branch, w_out, ln_g, ln_b)


N_LEVEL = int(math.log2(HALF))


def _block_ref_rows(c, w):
    n_rows = c.shape[0]
    if 2 * w == n_rows:
        return jnp.broadcast_to(c[w - 1:w, :], c.shape)
    if w >= 4:
        n = n_rows // (2 * w)
        c3 = c.reshape(n, 2 * w, LANES)
        return jnp.broadcast_to(c3[:, w - 1:w, :], c3.shape).reshape(n_rows, LANES)
    c8 = c.reshape(n_rows // 8, 8, LANES)
    sub = lax.broadcasted_iota(jnp.int32, c8.shape, 1)
    if w == 2:
        r = jnp.where(sub < 4, c8[:, 1:2, :], c8[:, 5:6, :])
    else:
        r = jnp.where(sub < 2, c8[:, 0:1, :],
                      jnp.where(sub < 4, c8[:, 2:3, :], jnp.where(sub < 6, c8[:, 4:5, :], c8[:, 6:7, :])))
    return r.reshape(n_rows, LANES)


def _level_ids():
    r = lax.broadcasted_iota(jnp.int32, (HALF, HALF), 0)
    s = lax.broadcasted_iota(jnp.int32, (HALF, HALF), 1)
    lv = 31 - lax.clz(r ^ s)
    lv = jnp.where(r == s, N_LEVEL, lv)
    return jnp.where(r >= s, lv, -1), jnp.where(r <= s, lv, -1)


def _hgrn_chunk(q, v, fpre, lb, onemlb, s_t, tril, lvl, sgn_ref, backward, has_init):
    u = jnp.exp2(_neg_abs(fpre) * LOG2E)
    r = 1.0 / (1.0 + u)
    pos_side = fpre >= 0.0
    kk = onemlb * (jnp.where(pos_side, u, 1.0) * r)
    f = lb + onemlb * (jnp.where(pos_side, 1.0, u) * r)
    lf2 = jnp.maximum(jnp.log2(f), LOG2_TINY)
    cum = _cumsum_rows(tril, lf2)
    pos = (cum - lf2) if backward else cum
    qb, kb = q.astype(BF16), kk.astype(BF16)

    def sides(w):
        e = jnp.exp2((pos - _block_ref_rows(cum, w)) * sgn_ref[N_LEVEL - int(math.log2(w))]).astype(BF16)
        return qb * e, kb * e

    zq, zk = sides(HALF)
    if backward:
        cross = _dot_nt(zq[0:HALF], zk[HALF:CH])
    else:
        cross = _dot_nt(zq[HALF:CH], zk[0:HALF])
    halves = (slice(0, HALF), slice(HALF, CH))
    prods = [_dot_nt(qb[rows], kb[rows]) for rows in halves]
    zq, zk = sides(HALF // 2)
    diag = [jnp.where(lvl == N_LEVEL, prods[b], 0.0) for b in range(2)]
    w = HALF // 2
    while w >= 1:
        prods = [_dot_nt(zq[rows], zk[rows]) for rows in halves]
        if w > 1:
            zq, zk = sides(w // 2)
        lv = int(math.log2(w))
        diag = [jnp.where(lvl == lv, prods[b], diag[b]) for b in range(2)]
        w //= 2
    d0, d1, cr = diag[0].astype(BF16), diag[1].astype(BF16), cross.astype(BF16)
    if backward:
        o_lo = _dot(jnp.concatenate([d0, cr], axis=1), v)
        o_hi = _dot(d1, v[HALF:CH])
    else:
        o_lo = _dot(d0, v[0:HALF])
        o_hi = _dot(jnp.concatenate([cr, d1], axis=1), v)
    o = jnp.concatenate([o_lo, o_hi], axis=0)
    c_last = cum[CH - 1:CH, :]
    k_out = kk * jnp.exp2((pos) if backward else (c_last - cum))
    s_new = _dot_tn(v, k_out.astype(BF16))
    if has_init:
        q_in = q * jnp.exp2((c_last - pos) if backward else cum)
        o = o + _dot_nt(q_in.astype(BF16), s_t.astype(BF16))
        s_new = s_new + s_t * jnp.exp2(c_last)
    return o, s_new


def _hgrn_kernel(q_ref, v_ref, g_ref, f_ref, lbp_ref, ng_ref, sgn_ref, *rest, seq_len, nsq, is_ctx):
    if is_ctx:
        y_ref, sfin_ref, oacc, st = rest
    else:
        s0_ref, y_ref, oacc, st = rest
    nchunk = seq_len // CH
    tril = _tri(CH, True)
    lvl_f, lvl_b = _level_ids()

    def head(h, carry):
        for sq in range(nsq):
            base = sq * seq_len
            if not is_ctx:
                for d in range(2):
                    st[2 * sq + d] = s0_ref[sq, d, h].T
                if nchunk > 1:
                    oacc[base:base + seq_len, :] = jnp.zeros((seq_len, LANES), F32)

            def run(ci, dirs, sq=sq, base=base):
                rows = _chunk_rows(ci, base)
                q = _silu(q_ref[h, rows, :].astype(F32))
                v = v_ref[h, rows, :]
                for d in dirs:
                    lrow = pl.ds(d * HG_HEADS + h, 1)
                    o, s_new = _hgrn_chunk(q, v, f_ref[d * HG_HEADS + h, rows, :],
                                           lbp_ref[0, lrow, :], lbp_ref[1, lrow, :],
                                           None if is_ctx else st[2 * sq + d], tril, lvl_b if d else lvl_f,
                                           sgn_ref, d == 1, not is_ctx)
                    st[2 * sq + d] = s_new
                    if nchunk == 1 and d == 0:
                        oacc[rows, :] = o
                    else:
                        oacc[rows, :] += o

            _chunk_loop(nchunk, run)
            o = oacc[base:base + seq_len, :]
            y = o * lax.rsqrt(jnp.mean(o * o, -1, keepdims=True) + RMS_EPS) * ng_ref[pl.ds(h, 1), :]
            gate = _silu(g_ref[h, base:base + seq_len, :].astype(F32))
            y_ref[h, base:base + seq_len, :] = (y * gate).astype(y_ref.dtype)
            if is_ctx:
                for d in range(2):
                    sfin_ref[sq, d, h] = st[2 * sq + d].T
        return carry

    lax.fori_loop(0, HG_HEADS, head, 0)


def _seq_call(kern, name, in_specs, args, heads, st_specs, st_shapes, states, scratch, is_ctx, *, seq_len, nseq, nsq):
    in_specs, args = list(in_specs), list(args)
    n_used = len(args)
    aliases = {}
    y_spec = pl.BlockSpec((heads, nsq * seq_len, LANES), lambda i: (0, i, 0))
    y_shape = jax.ShapeDtypeStruct((heads, nseq * seq_len, LANES), BF16)
    if is_ctx:
        out_specs, out_shape = [y_spec] + list(st_specs), [y_shape] + list(st_shapes)
        for k, s in enumerate(states):
            in_specs.append(pl.BlockSpec(memory_space=pl.ANY))
            args.append(s)
            aliases[len(args) - 1] = 1 + k
    else:
        in_specs += list(st_specs)
        args += list(states)
        n_used = len(args)
        out_specs, out_shape = [y_spec], [y_shape]
    n_args = len(args)
    steps = nseq // nsq
    kw = dict(seq_len=seq_len, nsq=nsq)

    def body(*refs):
        kern(*refs[:n_used], *refs[n_args:], is_ctx=is_ctx, **kw)

    return pl.pallas_call(
        body, grid=(steps,), in_specs=in_specs, out_specs=out_specs, out_shape=out_shape,
        input_output_aliases=aliases, scratch_shapes=scratch, compiler_params=_cparams(("arbitrary",)),
        name=name + ("_ctx" if is_ctx else "_lat"),
    )(*args)


def _hgrn_call(ub, uf, lbp, ng, sgn, states, *, is_ctx, seq_len, row0, nseq, nsq, layer, depth):
    rows = nsq * seq_len
    bo = row0 // rows
    nh = HG_HEADS
    in_specs = [pl.BlockSpec((nh, rows, LANES), lambda i: (0, bo + i, 0)),
                pl.BlockSpec((nh, rows, LANES), lambda i: (1, bo + i, 0)),
                pl.BlockSpec((nh, rows, LANES), lambda i: (2, bo + i, 0)),
                pl.BlockSpec((2 * nh, rows, LANES), lambda i: (0, bo + i, 0)),
                pl.BlockSpec((2, 2 * nh, LANES), lambda i: (0, 0, 0)),
                pl.BlockSpec((nh, LANES), lambda i: (0, 0)),
                pl.BlockSpec((N_LEVEL + 1, CH, LANES), lambda i: (0, 0, 0))]
    st_spec, st_shape = _state_io((2, nh, HG_DK, LANES), nseq, nsq, layer, depth)
    return _seq_call(
        _hgrn_kernel, "hgrn", in_specs, [ub, ub, ub, uf, lbp, ng, sgn], nh, [st_spec], [st_shape], states,
        [pltpu.VMEM((rows, LANES), F32), pltpu.VMEM((2 * nsq, LANES, HG_DK), F32)],
        is_ctx, seq_len=seq_len, nseq=nseq, nsq=nsq)


def _mlstm_prep(gt_ref, bias_ref, rows, ms_s, triu):
    gates = gt_ref[rows, :] + bias_ref[...]
    g_t = gates.T
    i_rows = g_t[0:8, :]
    lf_rows = _log_sigmoid(g_t[8:16, :])
    c_rows = _cumsum_lanes(lf_rows, triu)
    is_b = lax.broadcasted_iota(jnp.int32, (8, CH), 0) >= ML_HEADS
    is_b1 = is_b[:, 0:1]
    m_in = ms_s[:, 0:1]
    c_last = c_rows[:, CH - 1:CH]
    pos = jnp.where(is_b, c_rows - lf_rows, c_rows)
    u = jnp.where(is_b, pos + i_rows, i_rows - pos)
    nu = jnp.where(is_b1, c_last + m_in, m_in)
    mu = jnp.maximum(jnp.where(is_b, _cummax_lanes(u, True), _cummax_lanes(u, False)), nu)
    m_t = jnp.where(is_b, mu - pos, pos + mu)
    mu_end = jnp.where(is_b1, mu[:, 0:1], mu[:, CH - 1:CH])
    m_new = jnp.where(is_b1, mu[:, 0:1], m_t[:, CH - 1:CH])
    x_rows = jnp.concatenate([-mu, -m_t, u, jnp.zeros_like(u)], axis=0) * LOG2E
    return dict(lhs=_sel_lhs(x_rows), u_rows=x_rows[16:24, :], nu=nu * LOG2E, mu_end=mu_end * LOG2E, m_new=m_new)


def _mlstm_main(p, q_ref, k_ref, v_ref, sel_ref, rows, dirs, cn_s, ms_s, oacc, masks, has_init, first_write):
    lhs = p["lhs"]
    scale = ML_DK ** -0.5
    ones_blk = jnp.ones((CH, LANES), BF16)
    heads = range(ML_HEADS)
    hds = [(d, h, d * ML_HEADS + h) for h in heads for d in dirs]
    q = [q_ref[h, rows, :] for h in heads]
    kf = [k_ref[h, rows, :].astype(F32) * scale for h in heads]
    vaug = [jnp.concatenate([v_ref[h, rows, :], ones_blk], axis=1) for h in heads]
    qk = [_dot_nt(q[h], kf[h].astype(BF16)) for h in heads]
    b_mu = {hd: _dot(lhs, sel_ref[hd]) for _, _, hd in hds}
    b_m = {hd: _dot(lhs, sel_ref[8 + hd]) for _, _, hd in hds}
    k_t = [kf[h].T for h in heads]
    s = {}
    for d, h, hd in hds:
        xe = jnp.concatenate([b_mu[hd], b_mu[hd]], axis=1) + p["u_rows"][hd:hd + 1, :]
        s[hd] = (jnp.where(masks[d], jnp.exp2(xe), 0.0) * qk[h]).astype(BF16)
    num = {hd: _dot(s[hd], vaug[h]) for _, h, hd in hds}
    w_end = jnp.exp2(p["u_rows"] - p["mu_end"])
    upd = {hd: _dot((k_t[h] * w_end[hd:hd + 1, :]).astype(BF16), vaug[h]) for _, h, hd in hds}
    if has_init:
        for d, h, hd in hds:
            cn = cn_s[d, h]
            w_int = jnp.exp2(p["nu"][hd:hd + 1, :] + b_mu[hd])
            num[hd] = num[hd] + jnp.concatenate([w_int, w_int], axis=1) * _dot(q[h], cn.astype(BF16))
            upd[hd] = upd[hd] + jnp.exp2(p["nu"][hd:hd + 1, :] - p["mu_end"][hd:hd + 1, :]) * cn
    for d, h, hd in hds:
        cn_s[d, h] = upd[hd]
    for h in heads:
        hsum = None
        for d in dirs:
            hd = d * ML_HEADS + h
            den = jnp.maximum(jnp.abs(num[hd][:, LANES:]), jnp.exp2(b_m[hd]))
            hout = num[hd][:, :LANES] / den
            hsum = hout if hsum is None else hsum + hout
        if first_write:
            oacc[h, rows, :] = hsum
        else:
            oacc[h, rows, :] += hsum
    m_new = jnp.broadcast_to(p["m_new"], (8, LANES))
    if len(dirs) == 2:
        ms_s[...] = m_new
    else:
        row = lax.broadcasted_iota(jnp.int32, (8, LANES), 0)
        mine = (row >= ML_HEADS) if dirs[0] == 1 else (row < ML_HEADS)
        ms_s[...] = jnp.where(mine, m_new, ms_s[...])


def _mlstm_kernel(q_ref, k_ref, v_ref, og_ref, gt_ref, bias_ref, ng_ref, sel_ref, *rest, seq_len, nsq, is_ctx):
    if is_ctx:
        y_ref, cfin_ref, nfin_ref, mfin_ref, oacc, cn_s, ms_s = rest
    else:
        c0_ref, n0_ref, m0_ref, y_ref, oacc, cn_s, ms_s = rest
    nchunk = seq_len // CH
    triu = _tri(CH, False)
    rr = lax.broadcasted_iota(jnp.int32, (CH, CH), 0)
    ss = lax.broadcasted_iota(jnp.int32, (CH, CH), 1)
    masks = (ss <= rr, ss >= rr)
    def main(p, rows, dirs, sq):
        _mlstm_main(p, q_ref, k_ref, v_ref, sel_ref, rows, dirs, cn_s.at[sq], ms_s.at[sq], oacc, masks,
                    not is_ctx, nchunk == 1)

    for sq in range(nsq):
        base = sq * seq_len
        if is_ctx:
            ms_s[sq] = jnp.zeros((2 * ML_HEADS, LANES), F32)
        else:
            ms_s[sq] = m0_ref[sq]
            for d in range(2):
                for h in range(ML_HEADS):
                    hd = d * ML_HEADS + h
                    nb = jnp.broadcast_to(n0_ref[sq, hd:hd + 1, :], (ML_DK, LANES)).T
                    cn_s[sq, d, h] = jnp.concatenate([c0_ref[sq, d, h], nb], axis=1)
            if nchunk > 1:
                oacc[:, base:base + seq_len, :] = jnp.zeros((ML_HEADS, seq_len, LANES), F32)
    if nchunk == 1:
        preps = [_mlstm_prep(gt_ref, bias_ref, _chunk_rows(0, sq * seq_len), ms_s.at[sq], triu) for sq in range(nsq)]
        for sq in range(nsq):
            main(preps[sq], _chunk_rows(0, sq * seq_len), (0, 1), sq)
    else:
        for sq in range(nsq):
            def body(ci, carry, sq=sq):
                rows_f = _chunk_rows(ci, sq * seq_len)
                rows_b = _chunk_rows(nchunk - 1 - ci, sq * seq_len)
                p_f = _mlstm_prep(gt_ref, bias_ref, rows_f, ms_s.at[sq], triu)
                p_b = _mlstm_prep(gt_ref, bias_ref, rows_b, ms_s.at[sq], triu)
                main(p_f, rows_f, (0,), sq)
                main(p_b, rows_b, (1,), sq)
                return carry
            lax.fori_loop(0, nchunk, body, 0)
    for sq in range(nsq):
        base = sq * seq_len
        for h in range(ML_HEADS):
            o = oacc[h, base:base + seq_len, :]
            y = o * lax.rsqrt(jnp.mean(o * o, -1, keepdims=True) + RMS_EPS) * ng_ref[h:h + 1, :]
            gate = _sigmoid(og_ref[h, base:base + seq_len, :].astype(F32))
            y_ref[h, base:base + seq_len, :] = (y * gate).astype(y_ref.dtype)
        if is_ctx:
            for d in range(2):
                for h in range(ML_HEADS):
                    cn = cn_s[sq, d, h]
                    cfin_ref[sq, d, h] = cn[:, :LANES]
                    nfin_ref[sq, pl.ds(d * ML_HEADS + h, 1), :] = cn[:, LANES:].T[0:1, :]
            mfin_ref[sq] = ms_s[sq]


def _mlstm_call(ub, uf, bias_row, ng, sel, states, *, is_ctx, seq_len, row0, nseq, nsq, layer, depth):
    rows = nsq * seq_len
    bo = row0 // rows
    nh = ML_HEADS
    slab = lambda k: pl.BlockSpec((nh, rows, LANES), lambda i: (k, bo + i, 0))
    in_specs = [slab(3), slab(4), slab(5), slab(6),
                pl.BlockSpec((None, rows, LANES), lambda i: (GATE_SLAB, bo + i, 0)),
                pl.BlockSpec((1, LANES), lambda i: (0, 0)),
                pl.BlockSpec((nh, LANES), lambda i: (0, 0)),
                pl.BlockSpec((N_SEL, LANES, LANES), lambda i: (0, 0, 0))]
    c_spec, c_shape = _state_io((2, nh, ML_DK, LANES), nseq, nsq, layer, depth)
    v_spec, v_shape = _state_io((2 * nh, LANES), nseq, nsq, layer, depth)
    return _seq_call(
        _mlstm_kernel, "mlstm", in_specs, [ub, ub, ub, ub, uf, bias_row, ng, sel], nh,
        [c_spec, v_spec, v_spec], [c_shape, v_shape, v_shape], states,
        [pltpu.VMEM((nh, rows, LANES), F32), pltpu.VMEM((nsq, 2, nh, ML_DK, 2 * LANES), F32),
         pltpu.VMEM((nsq, 2 * nh, LANES), F32)],
        is_ctx, seq_len=seq_len, nseq=nseq, nsq=nsq)


N_PAIR = MB_HEADS // 2
PAIRS_PER_GROUP = N_PAIR // MB_GROUPS


def _ssd_prep(gt_ref, bias_ref, nega_ref, rows, triu):
    gates = gt_ref[rows, :] + bias_ref[...]
    g_t = gates.T
    dt_rows = _softplus(g_t[16:32, :])
    la_rows = dt_rows * nega_ref[...]
    c_rows = _cumsum_lanes(la_rows, triu)
    ldt = jnp.log(dt_rows)
    is_b = lax.broadcasted_iota(jnp.int32, (16, CH), 0) >= MB_HEADS
    pos = jnp.where(is_b, c_rows - la_rows, c_rows)
    a_col = jnp.where(is_b, -pos, pos) * LOG2E
    r_row = jnp.where(is_b, pos + ldt, ldt - pos) * LOG2E
    c_last = c_rows[:, CH - 1:CH] * LOG2E
    lhs = _sel_lhs(jnp.concatenate([a_col, jnp.zeros_like(a_col)], axis=0))
    return dict(lhs=lhs, r_row=r_row, c_last=c_last)


def _ssd_main(p, xs_s, bcs_s, sel_ref, rows, dirs, hst, yacc, masks, has_init, first_write):
    lhs, r_row, c_last = p["lhs"], p["r_row"], p["c_last"]
    lane = lax.broadcasted_iota(jnp.int32, (CH, LANES), 1)
    lo = lane < MB_HEADDIM
    lo_state = lax.broadcasted_iota(jnp.int32, (LANES, LANES), 1) < MB_HEADDIM
    lo_row = lo_state[0:1, :]
    bblk = bcs_s[rows, 0:LANES]
    cblk = bcs_s[rows, LANES:2 * LANES]
    c_g, b_g, gm = [], [], []
    for grp in range(MB_GROUPS):
        in_grp = (lane >= grp * MB_DSTATE) & (lane < (grp + 1) * MB_DSTATE)
        c_g.append(jnp.where(in_grp, cblk, 0.0))
        b_g.append(jnp.where(in_grp, bblk, 0.0))
        gm.append(_dot_nt(c_g[grp].astype(BF16), b_g[grp].astype(BF16)))
    xh = []
    for j in range(N_PAIR):
        xpair = xs_s[rows, j * LANES:(j + 1) * LANES]
        xh.append((jnp.where(lo, xpair, 0.0).astype(BF16), jnp.where(lo, 0.0, xpair).astype(BF16)))
    items = [(d, j, half, d * MB_HEADS + 2 * j + half) for j in range(N_PAIR) for d in dirs for half in range(2)]
    b_a = {hd: _dot(lhs, sel_ref[hd]) for _, _, _, hd in items}
    b_t = [b_g[grp].T for grp in range(MB_GROUPS)]
    is_f = lax.broadcasted_iota(jnp.int32, (2 * MB_HEADS, 1), 0) < MB_HEADS
    w_end = jnp.exp2(r_row + jnp.where(is_f, c_last, 0.0))
    m, b_out, c_in = {}, {}, {}
    for d, j, half, hd in items:
        grp = j // PAIRS_PER_GROUP
        cl = c_last[hd:hd + 1, :]
        xe = jnp.concatenate([b_a[hd], b_a[hd]], axis=1) + r_row[hd:hd + 1, :]
        m[hd] = (jnp.where(masks[d], jnp.exp2(xe), 0.0) * gm[grp]).astype(BF16)
        b_out[hd] = (b_t[grp] * w_end[hd:hd + 1, :]).astype(BF16)
        if has_init:
            c_in[hd] = (c_g[grp] * jnp.exp2(b_a[hd] + cl if d else b_a[hd])).astype(BF16)
    yy = {hd: _dot(m[hd], xh[j][half]) for _, j, half, hd in items}
    uu = {hd: _dot(b_out[hd], xh[j][half]) for _, j, half, hd in items}
    for j in range(N_PAIR):
        ysum = None
        for d in dirs:
            hd0, hd1 = d * MB_HEADS + 2 * j, d * MB_HEADS + 2 * j + 1
            y = yy[hd0] + yy[hd1]
            upd = uu[hd0] + uu[hd1]
            if has_init:
                ht = hst[d, j]
                y = y + _dot(c_in[hd0], jnp.where(lo_state, ht, 0.0).astype(BF16))
                y = y + _dot(c_in[hd1], jnp.where(lo_state, 0.0, ht).astype(BF16))
                decay = jnp.where(lo_row, jnp.exp2(c_last[hd0:hd0 + 1, :]), jnp.exp2(c_last[hd1:hd1 + 1, :]))
                upd = upd + ht * decay
            hst[d, j] = upd
            ysum = y if ysum is None else ysum + y
        if first_write:
            yacc[rows, j * LANES:(j + 1) * LANES] = ysum
        else:
            yacc[rows, j * LANES:(j + 1) * LANES] += ysum


def _ssd_kernel(x_ref, bc_ref, z_ref, gt_ref, cw_ref, bias_ref, nega_ref, dskip_ref, ng_ref, sel_ref, *rest,
                seq_len, nsq, is_ctx):
    if is_ctx:
        y_ref, hfin_ref, xs_s, bcs_s, yacc, hst = rest
    else:
        h0_ref, y_ref, xs_s, bcs_s, yacc, hst = rest
    nchunk = seq_len // CH
    triu = _tri(CH, False)
    rr = lax.broadcasted_iota(jnp.int32, (CH, CH), 0)
    ss = lax.broadcasted_iota(jnp.int32, (CH, CH), 1)
    masks = (ss <= rr, ss >= rr)

    def conv(v, lo_col):
        row = lax.broadcasted_iota(jnp.int32, v.shape, 0)
        prev = jnp.where(row == 0, 0.0, pltpu.roll(v, 1, 0))
        nxt = jnp.where(row == seq_len - 1, 0.0, pltpu.roll(v, seq_len - 1, 0))
        cs = slice(lo_col, lo_col + v.shape[1])
        return _silu(cw_ref[0:1, cs] * prev + cw_ref[1:2, cs] * v + cw_ref[2:3, cs] * nxt + cw_ref[3:4, cs])

    for sq in range(nsq):
        base = sq * seq_len
        seq_rows = slice(base, base + seq_len)
        for k in range(MIX_W // LANES):
            xs_s[seq_rows, k * LANES:(k + 1) * LANES] = conv(x_ref[k, seq_rows, :], k * LANES)
        for k in range(2):
            bcs_s[seq_rows, k * LANES:(k + 1) * LANES] = conv(bc_ref[k, seq_rows, :], MIX_W + k * LANES)
        if not is_ctx:
            hst[sq] = h0_ref[sq]
            if nchunk > 1:
                yacc[seq_rows, :] = jnp.zeros((seq_len, MIX_W), F32)

    def prep(rows):
        return _ssd_prep(gt_ref, bias_ref, nega_ref, rows, triu)

    def main(p, rows, dirs, sq):
        _ssd_main(p, xs_s, bcs_s, sel_ref, rows, dirs, hst.at[sq], yacc, masks, not is_ctx, nchunk == 1)

    if nchunk == 1:
        preps = [prep(_chunk_rows(0, sq * seq_len)) for sq in range(nsq)]
        for sq in range(nsq):
            main(preps[sq], _chunk_rows(0, sq * seq_len), (0, 1), sq)
    else:
        for sq in range(nsq):
            def body(ci, carry, sq=sq):
                rows_f = _chunk_rows(ci, sq * seq_len)
                rows_b = _chunk_rows(nchunk - 1 - ci, sq * seq_len)
                p_f, p_b = prep(rows_f), prep(rows_b)
                main(p_f, rows_f, (0,), sq)
                main(p_b, rows_b, (1,), sq)
                return carry
            lax.fori_loop(0, nchunk, body, 0)

    for sq in range(nsq):
        base = sq * seq_len
        seq_rows = slice(base, base + seq_len)
        z = jnp.concatenate([z_ref[k, seq_rows, :] for k in range(MIX_W // LANES)], axis=1).astype(F32)
        y = (yacc[seq_rows, :] + dskip_ref[...] * xs_s[seq_rows, :]) * _silu(z)
        y = (y * lax.rsqrt(jnp.mean(y * y, -1, keepdims=True) + RMS_EPS) * ng_ref[...]).astype(y_ref.dtype)
        for k in range(MIX_W // LANES):
            y_ref[k, seq_rows, :] = y[:, k * LANES:(k + 1) * LANES]
        if is_ctx:
            hfin_ref[sq] = hst[sq]


def _ssd_call(ub, uf, cw, bias_row, nega_rows, dskip, ng, sel, states, *, is_ctx, seq_len, row0, nseq, nsq, layer,
              depth):
    rows = nsq * seq_len
    bo = row0 // rows
    ns = MIX_W // LANES
    row_spec = lambda w: pl.BlockSpec((1, w), lambda i: (0, 0))
    in_specs = [pl.BlockSpec((ns, rows, LANES), lambda i: (2, bo + i, 0)),
                pl.BlockSpec((2, rows, LANES), lambda i: (6, bo + i, 0)),
                pl.BlockSpec((ns, rows, LANES), lambda i: (7, bo + i, 0)),
                pl.BlockSpec((None, rows, LANES), lambda i: (GATE_SLAB, bo + i, 0)),
                pl.BlockSpec((8, MB_XBC), lambda i: (0, 0)),
                row_spec(LANES),
                pl.BlockSpec((2 * MB_HEADS, CH), lambda i: (0, 0)),
                row_spec(MIX_W), row_spec(MIX_W),
                pl.BlockSpec((N_SEL, LANES, LANES), lambda i: (0, 0, 0))]
    st_spec, st_shape = _state_io((2, N_PAIR, LANES, LANES), nseq, nsq, layer, depth)
    return _seq_call(
        _ssd_kernel, "ssd", in_specs, [uf, uf, ub, uf, cw, bias_row, nega_rows, dskip, ng, sel], ns,
        [st_spec], [st_shape], states,
        [pltpu.VMEM((rows, MIX_W), F32), pltpu.VMEM((rows, 2 * LANES), F32), pltpu.VMEM((rows, MIX_W), F32),
         pltpu.VMEM((nsq, 2, N_PAIR, LANES, LANES), F32)],
        is_ctx, seq_len=seq_len, nseq=nseq, nsq=nsq)


def _ssd_state_to_pairs(s):
    bsz = s.shape[0]
    st = jnp.swapaxes(s, -1, -2).reshape(bsz, 2, N_PAIR, 2, MB_DSTATE, MB_HEADDIM)
    st = jnp.moveaxis(st, 3, 4).reshape(bsz, 2, N_PAIR, MB_DSTATE, 2 * MB_HEADDIM)
    zero = jnp.zeros_like(st)
    grp = (jnp.arange(N_PAIR) // PAIRS_PER_GROUP).reshape(1, 1, N_PAIR, 1, 1)
    return jnp.concatenate([jnp.where(grp == 0, st, zero), jnp.where(grp == 1, st, zero)], axis=3)


def _ssd_pairs_to_state(hp):
    bsz = hp.shape[0]
    halves = hp.reshape(bsz, 2, N_PAIR, MB_GROUPS, MB_DSTATE, 2 * MB_HEADDIM)
    grp = (jnp.arange(N_PAIR) // PAIRS_PER_GROUP).reshape(1, 1, N_PAIR, 1, 1)
    st = jnp.where(grp == 0, halves[:, :, :, 0], halves[:, :, :, 1])
    st = st.reshape(bsz, 2, N_PAIR, MB_DSTATE, 2, MB_HEADDIM)
    st = jnp.moveaxis(st, 4, 3).reshape(bsz, 2, MB_HEADS, MB_DSTATE, MB_HEADDIM)
    return jnp.swapaxes(st, -1, -2)


def _grid_pos_embed(n_tok, d_model):
    rows = n_tok // GRID_W
    r, cidx = jnp.meshgrid(jnp.arange(rows, dtype=F32), jnp.arange(GRID_W, dtype=F32), indexing='ij')
    quarter = d_model // 4
    freq = jnp.exp(-math.log(10000.0) * jnp.arange(quarter, dtype=F32) / quarter)
    ar = r.reshape(-1, 1) * freq
    ac = cidx.reshape(-1, 1) * freq
    return jnp.concatenate([jnp.sin(ar), jnp.cos(ar), jnp.sin(ac), jnp.cos(ac)], axis=-1)


def _gate_row(pieces):
    v = jnp.concatenate([p.reshape(-1).astype(F32) for p in pieces])
    return jnp.concatenate([v, v, v, jnp.zeros((LANES - 3 * N_SEL,), F32)]).reshape(1, LANES)


def kernel(x_prompt, x_sample, state_hgrn, state_mlstm_C, state_mlstm_n, state_mlstm_m, state_ssd, c, c_ctx,
           w_mod, b_mod, ln_g, ln_b, ffn_w_gu, ffn_w_down, w_in, hg_lb, hg_norm_g, ml_gate_b, ml_norm_g,
           mb_conv_w, mb_conv_b, mb_dt_bias, mb_a_log, mb_d, mb_norm_g, w_branch, w_out):
    bsz, seq, d = x_prompt.shape
    dbsz, dseq, _ = x_sample.shape
    depth = w_mod.shape[0]
    t_ctx = bsz * seq
    n_ctx_tiles = t_ctx // ROW_TILE
    assert dseq == ROW_TILE and t_ctx % ROW_TILE == 0 and seq == CH and dseq % CH == 0

    xs0 = x_sample + _grid_pos_embed(dseq, d).astype(x_sample.dtype)[None]
    x = jnp.concatenate([x_prompt.reshape(t_ctx, d), xs0.reshape(dbsz * dseq, d)], axis=0)

    cv = jnp.concatenate([c_ctx[None]] * MOD_LAT_ROW + [c, jnp.zeros((8 - MOD_LAT_ROW - dbsz, d), F32)], axis=0)
    mod4 = _mod_call(cv, w_mod, b_mod).reshape(depth, 8, 1, N_MOD * d)

    w_branch_b = w_branch.astype(BF16)
    w_out_b = w_out.astype(BF16)

    def regroup(width, pieces):
        out = jnp.zeros((depth, d, width), BF16)
        at = 0
        for lo, hi in pieces:
            out = lax.dynamic_update_slice(out, w_in[:, :, lo:hi].astype(BF16), (0, 0, at))
            at += hi - lo
        return out, at

    w_a, _ = regroup(UB_SLABS * LANES, [(0, 1536), (2560, 4608), (4624, 5136), (5920, 8992)])
    gate_cols = [(4608, 4624), (5904, 5920)]
    w_b, _ = regroup(UF_SLABS * LANES, [(1536, 2560), (5136, 5904)] + gate_cols * 3)
    ln_g4 = ln_g.reshape(depth, 3, 1, d)
    ln_b4 = ln_b.reshape(depth, 3, 1, d)

    lbs = jnp.cumsum(jax.nn.softmax(hg_lb.astype(F32), axis=0), axis=0)
    lbs = (lbs - lbs[0]).reshape(depth, 2 * HG_HEADS, HG_DK)
    lbp = jnp.stack([lbs, 1.0 - lbs], axis=1)
    cw = jnp.concatenate([mb_conv_w, mb_conv_b[:, None, :], jnp.zeros((depth, 4, MB_XBC), F32)], axis=1)
    sel = _selectors()

    lat_h0 = _ssd_state_to_pairs(state_ssd.reshape((dbsz * depth,) + state_ssd.shape[2:])).reshape(
        (dbsz, depth, 2, N_PAIR, LANES, LANES))
    lat_n0 = state_mlstm_n.reshape(dbsz, depth, 2 * ML_HEADS, ML_DK)
    lat_m0 = jnp.broadcast_to(state_mlstm_m.reshape(dbsz, depth, 2 * ML_HEADS, 1), (dbsz, depth, 2 * ML_HEADS, LANES))

    zeros = lambda *s: jnp.zeros((bsz, depth) + s, F32)
    st_hg = [zeros(2, HG_HEADS, HG_DK, LANES)]
    st_ml = [zeros(2, ML_HEADS, ML_DK, LANES), zeros(2 * ML_HEADS, LANES), zeros(2 * ML_HEADS, LANES)]
    st_ss = [zeros(2, N_PAIR, LANES, LANES)]
    t_idx = jnp.arange(CH).reshape(1, CH, 1)
    widths = (HALF >> jnp.arange(N_LEVEL + 1)).reshape(N_LEVEL + 1, 1, 1)
    sgn = jnp.broadcast_to(jnp.where((t_idx & widths) != 0, 1.0, -1.0), (N_LEVEL + 1, CH, LANES)).astype(F32)
    ctx = dict(is_ctx=True, seq_len=seq, row0=0, nseq=bsz, nsq=CTX_SEQS_PER_STEP, depth=depth)
    lat = dict(is_ctx=False, seq_len=dseq, row0=t_ctx, nseq=dbsz, nsq=1, depth=depth)
    for l in range(depth):
        x = _ffn_call(x, mod4, ffn_w_gu, ffn_w_down, ln_g4, ln_b4, l, 0, n_ctx_tiles)
        ub = _inproj_call(x, mod4, w_a, l, INPROJ_TILE_A, BF16, n_ctx_tiles, "inproj_a")
        uf = _inproj_call(x, mod4, w_b, l, INPROJ_TILE_B, F32, n_ctx_tiles, "inproj_b")

        ng_h = hg_norm_g[l].reshape(HG_HEADS, LANES)
        y_hg, *st_hg = _hgrn_call(ub, uf, lbp[l], ng_h, sgn, st_hg, layer=l, **ctx)
        y_hg_lat, = _hgrn_call(ub, uf, lbp[l], ng_h, sgn, [state_hgrn], layer=l, **lat)

        gate_bias = _gate_row([ml_gate_b[l, 0], ml_gate_b[l, 1], mb_dt_bias[l]])
        ng_m = ml_norm_g[l].reshape(ML_HEADS, LANES)
        y_ml, *st_ml = _mlstm_call(ub, uf, gate_bias, ng_m, sel, st_ml, layer=l, **ctx)
        y_ml_lat, = _mlstm_call(ub, uf, gate_bias, ng_m, sel, [state_mlstm_C, lat_n0, lat_m0], layer=l, **lat)

        nega = jnp.broadcast_to(-jnp.exp(mb_a_log[l].astype(F32)).reshape(2 * MB_HEADS, 1), (2 * MB_HEADS, CH))
        dskip = jnp.repeat(mb_d[l], MB_HEADDIM).reshape(1, MIX_W)
        ng_s = mb_norm_g[l].reshape(1, MIX_W)
        y_mb, *st_ss = _ssd_call(ub, uf, cw[l], gate_bias, nega, dskip, ng_s, sel, st_ss, layer=l, **ctx)
        y_mb_lat, = _ssd_call(ub, uf, cw[l], gate_bias, nega, dskip, ng_s, sel, [lat_h0], layer=l, **lat)

        x = _mixout_call(x, mod4, (y_hg, y_ml, y_mb), (y_hg_lat, y_ml_lat, y_mb_lat), ub, w_branch_b, w_out_b,
                         ln_g4, ln_b4, l, n_ctx_tiles)
        x = _ffn_call(x, mod4, ffn_w_gu, ffn_w_down, ln_g4, ln_b4, l, 1, n_ctx_tiles)

    y_prompt = x[:t_ctx].reshape(bsz, seq, d)
    y_sample = x[t_ctx:].reshape(dbsz, dseq, d)
    c_fin, n_fin, m_fin = st_ml
    h_fin = _ssd_pairs_to_state(st_ss[0].reshape((bsz * depth,) + st_ss[0].shape[2:]))
    return (y_prompt, y_sample, st_hg[0], c_fin, n_fin.reshape(bsz, depth, 2, ML_HEADS, ML_DK),
            m_fin[:, :, :, 0].reshape(bsz, depth, 2, ML_HEADS),
            h_fin.reshape((bsz, depth) + h_fin.shape[1:]))
```

```python
import functools
import math

import jax
import jax.numpy as jnp
from jax import lax
from jax.experimental import pallas as pl
from jax.experimental.pallas import tpu as pltpu

F32 = jnp.float32
BF16 = jnp.bfloat16

D_MODEL = 1024
DEPTH = 4
GRID_W = 64
MIX_W = 512
HG_HEADS = 4
HG_DK = 128
ML_HEADS = 4
ML_DK = 128
MB_HEADS = 8
MB_HEADDIM = 64
MB_GROUPS = 2
MB_DSTATE = 64
MB_XBC = MIX_W + 2 * MB_GROUPS * MB_DSTATE
D_FF = 2816
N_MOD = 9
ALPHA = (2 * DEPTH) ** 0.25
LN_EPS = 1e-5
RMS_EPS = 1e-6
LOG2E = 1.4426950408889634

LANES = 128
CH = 256
HALF = CH // 2
ROW_TILE = 1024
FFN_ROWS = 2048
MOD_LAT_ROW = 2
FF_TILE = 256
INPROJ_TILE_A = 1792
INPROJ_TILE_B = 1920
CTX_SEQS_PER_STEP = 4
VMEM_LIMIT = 56 * 1024 * 1024

UB_SLABS = 56
UF_SLABS = 15
GATE_SLAB = 14
N_SEL = 32
LOG2_TINY = -150.0


def _dot(a, b):
    return jnp.dot(a, b, preferred_element_type=F32)


def _dot_nt(a, b):
    return lax.dot_general(a, b, (((1,), (1,)), ((), ())), preferred_element_type=F32)


def _dot_tn(a, b):
    return lax.dot_general(a, b, (((0,), (0,)), ((), ())), preferred_element_type=F32)


def _sigmoid(x):
    return 1.0 / (1.0 + jnp.exp(-x))


def _silu(x):
    return x * _sigmoid(x)


def _log_sigmoid(x):
    return jnp.minimum(x, 0.0) - jnp.log(1.0 + jnp.exp(-jnp.abs(x)))


def _softplus(x):
    return jnp.maximum(x, 0.0) + jnp.log(1.0 + jnp.exp(-jnp.abs(x)))


def _neg_abs(x):
    return -jnp.abs(x)


def _ln(z, g, b):
    mu = jnp.mean(z, -1, keepdims=True)
    d = z - mu
    var = jnp.mean(d * d, -1, keepdims=True)
    return d * lax.rsqrt(var + LN_EPS) * g + b


def _split3(x):
    h = x.astype(BF16)
    r = x - h.astype(F32)
    m = r.astype(BF16)
    l = (r - m.astype(F32)).astype(BF16)
    return h, m, l


def _cumsum_rows(tril, x):
    h, m, l = _split3(x)
    return _dot(tril, h) + _dot(tril, m) + _dot(tril, l)


def _cumsum_lanes(x, triu):
    h, m, l = _split3(x)
    return _dot(h, triu) + _dot(m, triu) + _dot(l, triu)


def _cummax_lanes(x, reverse):
    n = x.shape[1]
    lane = lax.broadcasted_iota(jnp.int32, x.shape, 1)
    k = 1
    while k < n:
        if reverse:
            sh = jnp.where(lane < n - k, pltpu.roll(x, n - k, 1), -jnp.inf)
        else:
            sh = jnp.where(lane >= k, pltpu.roll(x, k, 1), -jnp.inf)
        x = jnp.maximum(x, sh)
        k *= 2
    return x


def _tri(c, lower):
    r = lax.broadcasted_iota(jnp.int32, (c, c), 0)
    s = lax.broadcasted_iota(jnp.int32, (c, c), 1)
    return jnp.where((s <= r) if lower else (r <= s), 1.0, 0.0).astype(BF16)


def _sel_lhs(x_rows):
    h = x_rows.astype(BF16).astype(F32)
    r = x_rows - h
    m = r.astype(BF16).astype(F32)
    rows = jnp.concatenate([h, m, r - m, jnp.ones_like(x_rows)], axis=0)
    return rows.T.astype(BF16)


def _selectors():
    r = jnp.arange(LANES)
    ch = jnp.arange(N_SEL)
    hit = ((r[None, :] % N_SEL) == ch[:, None]) & (r[None, :] < 3 * N_SEL)
    return jnp.broadcast_to(hit[:, :, None], (N_SEL, LANES, LANES)).astype(BF16)


def _cparams(sem):
    return pltpu.CompilerParams(dimension_semantics=sem, vmem_limit_bytes=VMEM_LIMIT)


def _chunk_rows(ci, base=0):
    r0 = ci * CH + base
    if not isinstance(r0, int):
        r0 = pl.multiple_of(r0, CH)
    return pl.ds(r0, CH)


def _state_io(shape_tail, nseq, nsq, layer, depth):
    nz = (0,) * len(shape_tail)
    spec = pl.BlockSpec((nsq, None) + shape_tail, lambda i: (i, layer) + nz)
    shape = jax.ShapeDtypeStruct((nseq, depth) + shape_tail, F32)
    return spec, shape


def _chunk_loop(nchunk, run):
    if nchunk == 1:
        run(0, (0, 1))
    else:
        def body(ci, carry):
            run(ci, (0,))
            run(nchunk - 1 - ci, (1,))
            return carry
        lax.fori_loop(0, nchunk, body, 0)


def _mod_kernel(c_ref, w_ref, b_ref, o_ref):
    a = _silu(c_ref[...]).astype(BF16)
    o_ref[...] = _dot(a, w_ref[...].astype(BF16)) + b_ref[...]


def _mod_call(cv, w_mod, b_mod):
    depth, d, e = w_mod.shape
    tn = 1024
    return pl.pallas_call(
        _mod_kernel,
        grid=(depth, e // tn),
        in_specs=[pl.BlockSpec((8, d), lambda l, j: (0, 0)),
                  pl.BlockSpec((None, d, tn), lambda l, j: (l, 0, j)),
                  pl.BlockSpec((None, 1, tn), lambda l, j: (l, 0, j))],
        out_specs=pl.BlockSpec((None, 8, tn), lambda l, j: (l, 0, j)),
        out_shape=jax.ShapeDtypeStruct((depth, 8, e), F32),
        compiler_params=_cparams(("arbitrary", "arbitrary")),
        name="mod",
    )(cv, w_mod, b_mod.reshape(depth, 1, e))


def _mod_row(i, n_ctx_tiles):
    return jnp.maximum(i - n_ctx_tiles + MOD_LAT_ROW, 0)


def _ffn_kernel(x_ref, mod_ref, wa_ref, wb_ref, wd_ref, g_ref, b_ref, o_ref, xm_s, *, nf):
    j = pl.program_id(1)
    d = x_ref.shape[1]
    groups = [(r, slice(r * ROW_TILE, (r + 1) * ROW_TILE)) for r in range(FFN_ROWS // ROW_TILE)]

    @pl.when(j == 0)
    def _():
        for r, rows in groups:
            sh = mod_ref[r:r + 1, 0:d]
            sc = mod_ref[r:r + 1, d:2 * d]
            xm_s[rows, :] = (x_ref[rows, :] * (1.0 + sc) + sh).astype(BF16)
        o_ref[...] = jnp.zeros_like(o_ref)

    xm = xm_s[...]
    a = _dot(xm, wa_ref[...].astype(BF16))
    b = _dot(xm, wb_ref[...].astype(BF16))
    o_ref[...] += _dot((_silu(a) * b).astype(BF16), wd_ref[...].astype(BF16))

    @pl.when(j == nf - 1)
    def _():
        for r, rows in groups:
            gate = mod_ref[r:r + 1, 2 * d:3 * d]
            z = ALPHA * x_ref[rows, :] + 0.5 * gate * o_ref[rows, :]
            o_ref[rows, :] = _ln(z, g_ref[...], b_ref[...])


def _ffn_call(x, mod4, w_gu, w_down, ln_g, ln_b, layer, which, n_ctx_tiles):
    t, d = x.shape
    nf = D_FF // FF_TILE
    per = FFN_ROWS // ROW_TILE
    assert nf >= 3 and nf * FF_TILE == D_FF and n_ctx_tiles % per == 0 and MOD_LAT_ROW % per == 0
    sub = 0 if which == 0 else 2
    mod_pairs = mod4.reshape(mod4.shape[0], mod4.shape[1] // per, per, mod4.shape[3])
    first_lat = n_ctx_tiles // per
    mod_blk = lambda i, j: (layer, jnp.maximum(i - first_lat + MOD_LAT_ROW // per, 0), 0, sub)
    return pl.pallas_call(
        functools.partial(_ffn_kernel, nf=nf),
        grid=(t // FFN_ROWS, nf),
        in_specs=[pl.BlockSpec((FFN_ROWS, d), lambda i, j: (i, 0)),
                  pl.BlockSpec((None, None, per, 3 * d), mod_blk),
                  pl.BlockSpec((None, None, d, FF_TILE), lambda i, j: (layer, which, 0, j)),
                  pl.BlockSpec((None, None, d, FF_TILE), lambda i, j: (layer, which, 0, j + nf)),
                  pl.BlockSpec((None, None, FF_TILE, d), lambda i, j: (layer, which, j, 0)),
                  pl.BlockSpec((None, None, 1, d), lambda i, j: (layer, sub, 0, 0)),
                  pl.BlockSpec((None, None, 1, d), lambda i, j: (layer, sub, 0, 0))],
        out_specs=pl.BlockSpec((FFN_ROWS, d), lambda i, j: (i, 0)),
        out_shape=jax.ShapeDtypeStruct((t, d), F32),
        scratch_shapes=[pltpu.VMEM((FFN_ROWS, d), BF16)],
        compiler_params=_cparams(("arbitrary", "arbitrary")),
        name=f"ffn{which}",
    )(x, mod_pairs, w_gu, w_gu, w_down, ln_g, ln_b)


def _inproj_kernel(x_ref, mod_ref, w_ref, o_ref, xm_s):
    d = x_ref.shape[1]

    @pl.when(pl.program_id(1) == 0)
    def _():
        sh = mod_ref[:, 0:d]
        sc = mod_ref[:, d:2 * d]
        xm_s[...] = (x_ref[...] * (1.0 + sc) + sh).astype(BF16)

    res = _dot(xm_s[...], w_ref[...]).astype(o_ref.dtype)
    for k in range(o_ref.shape[0]):
        o_ref[k] = res[:, k * LANES:(k + 1) * LANES]


def _inproj_call(x, mod4, w, layer, tn, out_dtype, n_ctx_tiles, name):
    t, d = x.shape
    n = w.shape[2]
    return pl.pallas_call(
        _inproj_kernel,
        grid=(t // ROW_TILE, n // tn),
        in_specs=[pl.BlockSpec((ROW_TILE, d), lambda i, j: (i, 0)),
                  pl.BlockSpec((None, None, 1, 3 * d), lambda i, j: (layer, _mod_row(i, n_ctx_tiles), 0, 1)),
                  pl.BlockSpec((None, d, tn), lambda i, j: (layer, 0, j))],
        out_specs=pl.BlockSpec((tn // LANES, ROW_TILE, LANES), lambda i, j: (j, i, 0)),
        out_shape=jax.ShapeDtypeStruct((n // LANES, t, LANES), out_dtype),
        scratch_shapes=[pltpu.VMEM((ROW_TILE, d), BF16)],
        compiler_params=_cparams(("arbitrary", "arbitrary")),
        name=name,
    )(x, mod4, w)


def _slabs(ref):
    return jnp.concatenate([ref[k] for k in range(ref.shape[0])], axis=1)


def _mixout_kernel(x_ref, mod_ref, yhc_ref, ymc_ref, ysc_ref, yhl_ref, yml_ref, ysl_ref, g0_ref, g1_ref, g2_ref,
                   wb_ref, wo_ref, lg_ref, lb_ref, o_ref, *, n_ctx_steps):
    d = x_ref.shape[1]
    is_lat = pl.program_id(0) >= n_ctx_steps

    def branch(c_ref, l_ref):
        return jnp.where(is_lat, _slabs(l_ref), _slabs(c_ref))

    p = _sigmoid(_slabs(g0_ref).astype(F32)) * _dot(branch(yhc_ref, yhl_ref), wb_ref[0])
    p += _sigmoid(_slabs(g1_ref).astype(F32)) * _dot(branch(ymc_ref, yml_ref), wb_ref[1])
    p += _sigmoid(_slabs(g2_ref).astype(F32)) * _dot(branch(ysc_ref, ysl_ref), wb_ref[2])
    y = _dot(p.astype(BF16), wo_ref[...])
    gate = mod_ref[:, 2 * d:3 * d]
    z = ALPHA * x_ref[...] + gate * y
    o_ref[...] = _ln(z, lg_ref[...], lb_ref[...])


def _mixout_call(x, mod4, y_ctx, y_lat, ub, w_branch, w_out, ln_g, ln_b, layer, n_ctx_tiles):
    t, d = x.shape
    tm = 512
    per = ROW_TILE // tm
    ns = MIX_W // LANES
    ng = d // LANES
    g0 = (UB_SLABS - 3 * ng) // ng
    n_ctx_steps = n_ctx_tiles * per
    n_lat_steps = t // tm - n_ctx_steps
    yc_spec = pl.BlockSpec((ns, tm, LANES), lambda i: (0, jnp.minimum(i, n_ctx_steps - 1), 0))
    yl_spec = pl.BlockSpec((ns, tm, LANES), lambda i: (0, jnp.clip(i - n_ctx_steps, 0, n_lat_steps - 1), 0))
    return pl.pallas_call(
        functools.partial(_mixout_kernel, n_ctx_steps=n_ctx_steps),
        grid=(t // tm,),
        in_specs=[pl.BlockSpec((tm, d), lambda i: (i, 0)),
                  pl.BlockSpec((None, None, 1, 3 * d), lambda i: (layer, _mod_row(i // per, n_ctx_tiles), 0, 1)),
                  yc_spec, yc_spec, yc_spec, yl_spec, yl_spec, yl_spec,
                  pl.BlockSpec((ng, tm, LANES), lambda i: (g0, i, 0)),
                  pl.BlockSpec((ng, tm, LANES), lambda i: (g0 + 1, i, 0)),
                  pl.BlockSpec((ng, tm, LANES), lambda i: (g0 + 2, i, 0)),
                  pl.BlockSpec((None, 3, MIX_W, d), lambda i: (layer, 0, 0, 0)),
                  pl.BlockSpec((None, d, d), lambda i: (layer, 0, 0)),
                  pl.BlockSpec((None, None, 1, d), lambda i: (layer, 1, 0, 0)),
                  pl.BlockSpec((None, None, 1, d), lambda i: (layer, 1, 0, 0))],
        out_specs=pl.BlockSpec((tm, d), lambda i: (i, 0)),
        out_shape=jax.ShapeDtypeStruct((t, d), F32),
        compiler_params=_cparams(("arbitrary",)),
        name="mixout",
    )(x, mod4, *y_ctx, *y_lat, ub, ub, ub, w_branch, w_out, ln_g, ln_b)


N_LEVEL = int(math.log2(HALF))


def _block_ref_rows(c, w):
    n_rows = c.shape[0]
    if 2 * w == n_rows:
        return jnp.broadcast_to(c[w - 1:w, :], c.shape)
    if w >= 4:
        n = n_rows // (2 * w)
        c3 = c.reshape(n, 2 * w, LANES)
        return jnp.broadcast_to(c3[:, w - 1:w, :], c3.shape).reshape(n_rows, LANES)
    c8 = c.reshape(n_rows // 8, 8, LANES)
    sub = lax.broadcasted_iota(jnp.int32, c8.shape, 1)
    if w == 2:
        r = jnp.where(sub < 4, c8[:, 1:2, :], c8[:, 5:6, :])
    else:
        r = jnp.where(sub < 2, c8[:, 0:1, :],
                      jnp.where(sub < 4, c8[:, 2:3, :], jnp.where(sub < 6, c8[:, 4:5, :], c8[:, 6:7, :])))
    return r.reshape(n_rows, LANES)


def _level_ids():
    r = lax.broadcasted_iota(jnp.int32, (HALF, HALF), 0)
    s = lax.broadcasted_iota(jnp.int32, (HALF, HALF), 1)
    lv = 31 - lax.clz(r ^ s)
    lv = jnp.where(r == s, N_LEVEL, lv)
    return jnp.where(r >= s, lv, -1), jnp.where(r <= s, lv, -1)


def _hgrn_gates(q, fpre, lb, onemlb, tril, backward):
    u = jnp.exp2(_neg_abs(fpre) * LOG2E)
    r = 1.0 / (1.0 + u)
    pos_side = fpre >= 0.0
    kk = onemlb * (jnp.where(pos_side, u, 1.0) * r)
    f = lb + onemlb * (jnp.where(pos_side, 1.0, u) * r)
    lf2 = jnp.maximum(jnp.log2(f), LOG2_TINY)
    cum = _cumsum_rows(tril, lf2)
    pos = (cum - lf2) if backward else cum
    return dict(q=q, kk=kk, qb=q.astype(BF16), kb=kk.astype(BF16), cum=cum, pos=pos, f=f)


def _hgrn_levels(p, lvl, sgn_ref, backward):
    qb, kb, cum, pos = p["qb"], p["kb"], p["cum"], p["pos"]

    def sides(w):
        if w == 1:
            odd = (lax.broadcasted_iota(jnp.int32, (CH, LANES), 0) & 1) == 1
            e = (jnp.where(odd, 1.0, p["f"]) if backward else jnp.where(odd, p["f"], 1.0)).astype(BF16)
            return qb * e, kb * e
        e = jnp.exp2((pos - _block_ref_rows(cum, w)) * sgn_ref[N_LEVEL - int(math.log2(w))]).astype(BF16)
        return qb * e, kb * e

    zq, zk = sides(HALF)
    if backward:
        cross = _dot_nt(zq[0:HALF], zk[HALF:CH])
    else:
        cross = _dot_nt(zq[HALF:CH], zk[0:HALF])
    halves = (slice(0, HALF), slice(HALF, CH))
    prods = [_dot_nt(qb[rows], kb[rows]) for rows in halves]
    zq, zk = sides(HALF // 2)
    diag = [jnp.where(lvl == N_LEVEL, prods[b], 0.0) for b in range(2)]
    w = HALF // 2
    while w >= 1:
        prods = [_dot_nt(zq[rows], zk[rows]) for rows in halves]
        if w > 1:
            zq, zk = sides(w // 2)
        lv = int(math.log2(w))
        diag = [jnp.where(lvl == lv, prods[b], diag[b]) for b in range(2)]
        w //= 2
    return diag[0].astype(BF16), diag[1].astype(BF16), cross.astype(BF16)


def _hgrn_finish(p, att, v, s_t, backward, has_init):
    q, kk, cum, pos = p["q"], p["kk"], p["cum"], p["pos"]
    d0, d1, cr = att
    if backward:
        o_lo = _dot(jnp.concatenate([d0, cr], axis=1), v)
        o_hi = _dot(d1, v[HALF:CH])
    else:
        o_lo = _dot(d0, v[0:HALF])
        o_hi = _dot(jnp.concatenate([cr, d1], axis=1), v)
    o = jnp.concatenate([o_lo, o_hi], axis=0)
    c_last = cum[CH - 1:CH, :]
    k_out = kk * jnp.exp2((pos) if backward else (c_last - cum))
    s_new = _dot_tn(v, k_out.astype(BF16))
    if has_init:
        q_in = q * jnp.exp2((c_last - pos) if backward else cum)
        o = o + _dot_nt(q_in.astype(BF16), s_t.astype(BF16))
        s_new = s_new + s_t * jnp.exp2(c_last)
    return o, s_new


def _hgrn_kernel(q_ref, v_ref, g_ref, f_ref, lbp_ref, ng_ref, sgn_ref, *rest, seq_len, nsq, is_ctx):
    if is_ctx:
        y_ref, sfin_ref, oacc, st = rest
    else:
        s0_ref, y_ref, oacc, st = rest
    nchunk = seq_len // CH
    tril = _tri(CH, True)
    lvl_f, lvl_b = _level_ids()

    def head(h, carry):
        for sq in range(nsq):
            base = sq * seq_len
            if not is_ctx:
                for d in range(2):
                    st[2 * sq + d] = s0_ref[sq, d, h].T
                if nchunk > 1:
                    oacc[base:base + seq_len, :] = jnp.zeros((seq_len, LANES), F32)

        def gates(rows, d, q=None):
            lrow = pl.ds(d * HG_HEADS + h, 1)
            q = _silu(q_ref[h, rows, :].astype(F32)) if q is None else q
            return _hgrn_gates(q, f_ref[d * HG_HEADS + h, rows, :], lbp_ref[0, lrow, :], lbp_ref[1, lrow, :],
                               tril, d == 1)

        def levels(p, d):
            return _hgrn_levels(p, lvl_b if d else lvl_f, sgn_ref, d == 1)

        def finish(p, att, rows, d, sq, first):
            o, s_new = _hgrn_finish(p, att, v_ref[h, rows, :], None if is_ctx else st[2 * sq + d], d == 1,
                                    not is_ctx)
            st[2 * sq + d] = s_new
            if first:
                oacc[rows, :] = o
            else:
                oacc[rows, :] += o

        if nchunk == 1:
            work = [(sq, d, _chunk_rows(0, sq * seq_len)) for sq in range(nsq) for d in range(2)]
            qs = [_silu(q_ref[h, _chunk_rows(0, sq * seq_len), :].astype(F32)) for sq in range(nsq)]
            ps = [gates(rows, d, qs[sq]) for sq, d, rows in work]
            atts = [levels(p, d) for (_, d, _), p in zip(work, ps)]
            for (sq, d, rows), p, att in zip(work, ps, atts):
                finish(p, att, rows, d, sq, d == 0)
        else:
            for sq in range(nsq):
                def body(ci, carry, sq=sq):
                    rows_f = _chunk_rows(ci, sq * seq_len)
                    rows_b = _chunk_rows(nchunk - 1 - ci, sq * seq_len)
                    p_f, p_b = gates(rows_f, 0), gates(rows_b, 1)
                    a_f, a_b = levels(p_f, 0), levels(p_b, 1)
                    finish(p_f, a_f, rows_f, 0, sq, False)
                    finish(p_b, a_b, rows_b, 1, sq, False)
                    return carry
                lax.fori_loop(0, nchunk, body, 0)

        for sq in range(nsq):
            base = sq * seq_len
            o = oacc[base:base + seq_len, :]
            y = o * lax.rsqrt(jnp.mean(o * o, -1, keepdims=True) + RMS_EPS) * ng_ref[pl.ds(h, 1), :]
            gate = _silu(g_ref[h, base:base + seq_len, :].astype(F32))
            y_ref[h, base:base + seq_len, :] = (y * gate).astype(y_ref.dtype)
            if is_ctx:
                for d in range(2):
                    sfin_ref[sq, d, h] = st[2 * sq + d].T
        return carry

    lax.fori_loop(0, HG_HEADS, head, 0)


def _seq_call(kern, name, in_specs, args, heads, st_specs, st_shapes, states, scratch, is_ctx, *, seq_len, nseq, nsq):
    in_specs, args = list(in_specs), list(args)
    n_used = len(args)
    aliases = {}
    y_spec = pl.BlockSpec((heads, nsq * seq_len, LANES), lambda i: (0, i, 0))
    y_shape = jax.ShapeDtypeStruct((heads, nseq * seq_len, LANES), BF16)
    if is_ctx:
        out_specs, out_shape = [y_spec] + list(st_specs), [y_shape] + list(st_shapes)
        for k, s in enumerate(states):
            in_specs.append(pl.BlockSpec(memory_space=pl.ANY))
            args.append(s)
            aliases[len(args) - 1] = 1 + k
    else:
        in_specs += list(st_specs)
        args += list(states)
        n_used = len(args)
        out_specs, out_shape = [y_spec], [y_shape]
    n_args = len(args)
    steps = nseq // nsq
    kw = dict(seq_len=seq_len, nsq=nsq)

    def body(*refs):
        kern(*refs[:n_used], *refs[n_args:], is_ctx=is_ctx, **kw)

    return pl.pallas_call(
        body, grid=(steps,), in_specs=in_specs, out_specs=out_specs, out_shape=out_shape,
        input_output_aliases=aliases, scratch_shapes=scratch, compiler_params=_cparams(("arbitrary",)),
        name=name + ("_ctx" if is_ctx else "_lat"),
    )(*args)


def _hgrn_call(ub, uf, lbp, ng, sgn, states, *, is_ctx, seq_len, row0, nseq, nsq, layer, depth):
    rows = nsq * seq_len
    bo = row0 // rows
    nh = HG_HEADS
    in_specs = [pl.BlockSpec((nh, rows, LANES), lambda i: (0, bo + i, 0)),
                pl.BlockSpec((nh, rows, LANES), lambda i: (1, bo + i, 0)),
                pl.BlockSpec((nh, rows, LANES), lambda i: (2, bo + i, 0)),
                pl.BlockSpec((2 * nh, rows, LANES), lambda i: (0, bo + i, 0)),
                pl.BlockSpec((2, 2 * nh, LANES), lambda i: (0, 0, 0)),
                pl.BlockSpec((nh, LANES), lambda i: (0, 0)),
                pl.BlockSpec((N_LEVEL + 1, CH, LANES), lambda i: (0, 0, 0))]
    st_spec, st_shape = _state_io((2, nh, HG_DK, LANES), nseq, nsq, layer, depth)
    return _seq_call(
        _hgrn_kernel, "hgrn", in_specs, [ub, ub, ub, uf, lbp, ng, sgn], nh, [st_spec], [st_shape], states,
        [pltpu.VMEM((rows, LANES), F32), pltpu.VMEM((2 * nsq, LANES, HG_DK), F32)],
        is_ctx, seq_len=seq_len, nseq=nseq, nsq=nsq)


def _mlstm_prep(gt_ref, bias_ref, rows, ms_s, triu):
    gates = gt_ref[rows, :] + bias_ref[...]
    g_t = gates.T
    i_rows = g_t[0:8, :]
    lf_rows = _log_sigmoid(g_t[8:16, :])
    c_rows = _cumsum_lanes(lf_rows, triu)
    is_b = lax.broadcasted_iota(jnp.int32, (8, CH), 0) >= ML_HEADS
    is_b1 = is_b[:, 0:1]
    m_in = ms_s[:, 0:1]
    c_last = c_rows[:, CH - 1:CH]
    pos = jnp.where(is_b, c_rows - lf_rows, c_rows)
    u = jnp.where(is_b, pos + i_rows, i_rows - pos)
    nu = jnp.where(is_b1, c_last + m_in, m_in)
    mu = jnp.maximum(jnp.where(is_b, _cummax_lanes(u, True), _cummax_lanes(u, False)), nu)
    m_t = jnp.where(is_b, mu - pos, pos + mu)
    mu_end = jnp.where(is_b1, mu[:, 0:1], mu[:, CH - 1:CH])
    m_new = jnp.where(is_b1, mu[:, 0:1], m_t[:, CH - 1:CH])
    x_rows = jnp.concatenate([-mu, -m_t, u, jnp.zeros_like(u)], axis=0) * LOG2E
    return dict(lhs=_sel_lhs(x_rows), u_rows=x_rows[16:24, :], nu=nu * LOG2E, mu_end=mu_end * LOG2E, m_new=m_new)


def _mlstm_main(p, q_ref, k_ref, v_ref, sel_ref, rows, dirs, cn_s, ms_s, oacc, masks, has_init, first_write):
    lhs = p["lhs"]
    scale = ML_DK ** -0.5
    ones_blk = jnp.ones((CH, LANES), BF16)
    heads = range(ML_HEADS)
    hds = [(d, h, d * ML_HEADS + h) for h in heads for d in dirs]
    q = [q_ref[h, rows, :] for h in heads]
    kf = [k_ref[h, rows, :].astype(F32) * scale for h in heads]
    vaug = [jnp.concatenate([v_ref[h, rows, :], ones_blk], axis=1) for h in heads]
    qk = [_dot_nt(q[h], kf[h].astype(BF16)) for h in heads]
    b_mu = {hd: _dot(lhs, sel_ref[hd]) for _, _, hd in hds}
    b_m = {hd: _dot(lhs, sel_ref[8 + hd]) for _, _, hd in hds}
    k_t = [kf[h].T for h in heads]
    s = {}
    for d, h, hd in hds:
        xe = jnp.concatenate([b_mu[hd], b_mu[hd]], axis=1) + p["u_rows"][hd:hd + 1, :]
        s[hd] = (jnp.where(masks[d], jnp.exp2(xe), 0.0) * qk[h]).astype(BF16)
    num = {hd: _dot(s[hd], vaug[h]) for _, h, hd in hds}
    w_end = jnp.exp2(p["u_rows"] - p["mu_end"])
    upd = {hd: _dot((k_t[h] * w_end[hd:hd + 1, :]).astype(BF16), vaug[h]) for _, h, hd in hds}
    if has_init:
        for d, h, hd in hds:
            cn = cn_s[d, h]
            w_int = jnp.exp2(p["nu"][hd:hd + 1, :] + b_mu[hd])
            num[hd] = num[hd] + jnp.concatenate([w_int, w_int], axis=1) * _dot(q[h], cn.astype(BF16))
            upd[hd] = upd[hd] + jnp.exp2(p["nu"][hd:hd + 1, :] - p["mu_end"][hd:hd + 1, :]) * cn
    for d, h, hd in hds:
        cn_s[d, h] = upd[hd]
    for h in heads:
        hsum = None
        for d in dirs:
            hd = d * ML_HEADS + h
            den = jnp.maximum(jnp.abs(num[hd][:, LANES:]), jnp.exp2(b_m[hd]))
            hout = num[hd][:, :LANES] / den
            hsum = hout if hsum is None else hsum + hout
        if first_write:
            oacc[h, rows, :] = hsum
        else:
            oacc[h, rows, :] += hsum
    m_new = jnp.broadcast_to(p["m_new"], (8, LANES))
    if len(dirs) == 2:
        ms_s[...] = m_new
    else:
        row = lax.broadcasted_iota(jnp.int32, (8, LANES), 0)
        mine = (row >= ML_HEADS) if dirs[0] == 1 else (row < ML_HEADS)
        ms_s[...] = jnp.where(mine, m_new, ms_s[...])


def _mlstm_kernel(q_ref, k_ref, v_ref, og_ref, gt_ref, bias_ref, ng_ref, sel_ref, *rest, seq_len, nsq, is_ctx):
    if is_ctx:
        y_ref, cfin_ref, nfin_ref, mfin_ref, oacc, cn_s, ms_s = rest
    else:
        c0_ref, n0_ref, m0_ref, y_ref, oacc, cn_s, ms_s = rest
    nchunk = seq_len // CH
    triu = _tri(CH, False)
    rr = lax.broadcasted_iota(jnp.int32, (CH, CH), 0)
    ss = lax.broadcasted_iota(jnp.int32, (CH, CH), 1)
    masks = (ss <= rr, ss >= rr)
    def main(p, rows, dirs, sq):
        _mlstm_main(p, q_ref, k_ref, v_ref, sel_ref, rows, dirs, cn_s.at[sq], ms_s.at[sq], oacc, masks,
                    not is_ctx, nchunk == 1)

    for sq in range(nsq):
        base = sq * seq_len
        if is_ctx:
            ms_s[sq] = jnp.zeros((2 * ML_HEADS, LANES), F32)
        else:
            ms_s[sq] = m0_ref[sq]
            for d in range(2):
                for h in range(ML_HEADS):
                    hd = d * ML_HEADS + h
                    nb = jnp.broadcast_to(n0_ref[sq, hd:hd + 1, :], (ML_DK, LANES)).T
                    cn_s[sq, d, h] = jnp.concatenate([c0_ref[sq, d, h], nb], axis=1)
            if nchunk > 1:
                oacc[:, base:base + seq_len, :] = jnp.zeros((ML_HEADS, seq_len, LANES), F32)
    if nchunk == 1:
        preps = [_mlstm_prep(gt_ref, bias_ref, _chunk_rows(0, sq * seq_len), ms_s.at[sq], triu) for sq in range(nsq)]
        for sq in range(nsq):
            main(preps[sq], _chunk_rows(0, sq * seq_len), (0, 1), sq)
    else:
        for sq in range(nsq):
            def body(ci, carry, sq=sq):
                rows_f = _chunk_rows(ci, sq * seq_len)
                rows_b = _chunk_rows(nchunk - 1 - ci, sq * seq_len)
                p_f = _mlstm_prep(gt_ref, bias_ref, rows_f, ms_s.at[sq], triu)
                p_b = _mlstm_prep(gt_ref, bias_ref, rows_b, ms_s.at[sq], triu)
                main(p_f, rows_f, (0,), sq)
                main(p_b, rows_b, (1,), sq)
                return carry
            lax.fori_loop(0, nchunk, body, 0)
    for sq in range(nsq):
        base = sq * seq_len
        for h in range(ML_HEADS):
            o = oacc[h, base:base + seq_len, :]
            y = o * lax.rsqrt(jnp.mean(o * o, -1, keepdims=True) + RMS_EPS) * ng_ref[h:h + 1, :]
            gate = _sigmoid(og_ref[h, base:base + seq_len, :].astype(F32))
            y_ref[h, base:base + seq_len, :] = (y * gate).astype(y_ref.dtype)
        if is_ctx:
            for d in range(2):
                for h in range(ML_HEADS):
                    cn = cn_s[sq, d, h]
                    cfin_ref[sq, d, h] = cn[:, :LANES]
                    nfin_ref[sq, pl.ds(d * ML_HEADS + h, 1), :] = cn[:, LANES:].T[0:1, :]
            mfin_ref[sq] = ms_s[sq]


def _mlstm_call(ub, uf, bias_row, ng, sel, states, *, is_ctx, seq_len, row0, nseq, nsq, layer, depth):
    rows = nsq * seq_len
    bo = row0 // rows
    nh = ML_HEADS
    slab = lambda k: pl.BlockSpec((nh, rows, LANES), lambda i: (k, bo + i, 0))
    in_specs = [slab(3), slab(4), slab(5), slab(6),
                pl.BlockSpec((None, rows, LANES), lambda i: (GATE_SLAB, bo + i, 0)),
                pl.BlockSpec((1, LANES), lambda i: (0, 0)),
                pl.BlockSpec((nh, LANES), lambda i: (0, 0)),
                pl.BlockSpec((N_SEL, LANES, LANES), lambda i: (0, 0, 0))]
    c_spec, c_shape = _state_io((2, nh, ML_DK, LANES), nseq, nsq, layer, depth)
    v_spec, v_shape = _state_io((2 * nh, LANES), nseq, nsq, layer, depth)
    return _seq_call(
        _mlstm_kernel, "mlstm", in_specs, [ub, ub, ub, ub, uf, bias_row, ng, sel], nh,
        [c_spec, v_spec, v_spec], [c_shape, v_shape, v_shape], states,
        [pltpu.VMEM((nh, rows, LANES), F32), pltpu.VMEM((nsq, 2, nh, ML_DK, 2 * LANES), F32),
         pltpu.VMEM((nsq, 2 * nh, LANES), F32)],
        is_ctx, seq_len=seq_len, nseq=nseq, nsq=nsq)


N_PAIR = MB_HEADS // 2
PAIRS_PER_GROUP = N_PAIR // MB_GROUPS


def _ssd_prep(gt_ref, bias_ref, nega_ref, rows, triu):
    gates = gt_ref[rows, :] + bias_ref[...]
    g_t = gates.T
    dt_rows = _softplus(g_t[16:32, :])
    la_rows = dt_rows * nega_ref[...]
    c_rows = _cumsum_lanes(la_rows, triu)
    ldt = jnp.log(dt_rows)
    is_b = lax.broadcasted_iota(jnp.int32, (16, CH), 0) >= MB_HEADS
    pos = jnp.where(is_b, c_rows - la_rows, c_rows)
    a_col = jnp.where(is_b, -pos, pos) * LOG2E
    r_row = jnp.where(is_b, pos + ldt, ldt - pos) * LOG2E
    c_last = c_rows[:, CH - 1:CH] * LOG2E
    lhs = _sel_lhs(jnp.concatenate([a_col, jnp.zeros_like(a_col)], axis=0))
    return dict(lhs=lhs, r_row=r_row, c_last=c_last)


def _ssd_main(p, xs_s, bcs_s, sel_ref, rows, dirs, hst, yacc, masks, has_init, first_write):
    lhs, r_row, c_last = p["lhs"], p["r_row"], p["c_last"]
    lane = lax.broadcasted_iota(jnp.int32, (CH, LANES), 1)
    lo = lane < MB_HEADDIM
    lo_state = lax.broadcasted_iota(jnp.int32, (LANES, LANES), 1) < MB_HEADDIM
    lo_row = lo_state[0:1, :]
    bblk = bcs_s[rows, 0:LANES]
    cblk = bcs_s[rows, LANES:2 * LANES]
    c_g, b_g, gm = [], [], []
    for grp in range(MB_GROUPS):
        in_grp = (lane >= grp * MB_DSTATE) & (lane < (grp + 1) * MB_DSTATE)
        c_g.append(jnp.where(in_grp, cblk, 0.0))
        b_g.append(jnp.where(in_grp, bblk, 0.0))
        gm.append(_dot_nt(c_g[grp].astype(BF16), b_g[grp].astype(BF16)))
    xh = []
    for j in range(N_PAIR):
        xpair = xs_s[rows, j * LANES:(j + 1) * LANES]
        xh.append((jnp.where(lo, xpair, 0.0).astype(BF16), jnp.where(lo, 0.0, xpair).astype(BF16)))
    items = [(d, j, half, d * MB_HEADS + 2 * j + half) for j in range(N_PAIR) for d in dirs for half in range(2)]
    b_a = {hd: _dot(lhs, sel_ref[hd]) for _, _, _, hd in items}
    b_t = [b_g[grp].T for grp in range(MB_GROUPS)]
    is_f = lax.broadcasted_iota(jnp.int32, (2 * MB_HEADS, 1), 0) < MB_HEADS
    w_end = jnp.exp2(r_row + jnp.where(is_f, c_last, 0.0))
    m, b_out, c_in = {}, {}, {}
    for d, j, half, hd in items:
        grp = j // PAIRS_PER_GROUP
        cl = c_last[hd:hd + 1, :]
        xe = jnp.concatenate([b_a[hd], b_a[hd]], axis=1) + r_row[hd:hd + 1, :]
        m[hd] = (jnp.where(masks[d], jnp.exp2(xe), 0.0) * gm[grp]).astype(BF16)
        b_out[hd] = (b_t[grp] * w_end[hd:hd + 1, :]).astype(BF16)
        if has_init:
            c_in[hd] = (c_g[grp] * jnp.exp2(b_a[hd] + cl if d else b_a[hd])).astype(BF16)
    yy = {hd: _dot(m[hd], xh[j][half]) for _, j, half, hd in items}
    uu = {hd: _dot(b_out[hd], xh[j][half]) for _, j, half, hd in items}
    for j in range(N_PAIR):
        ysum = None
        for d in dirs:
            hd0, hd1 = d * MB_HEADS + 2 * j, d * MB_HEADS + 2 * j + 1
            y = yy[hd0] + yy[hd1]
            upd = uu[hd0] + uu[hd1]
            if has_init:
                ht = hst[d, j]
                y = y + _dot(c_in[hd0], jnp.where(lo_state, ht, 0.0).astype(BF16))
                y = y + _dot(c_in[hd1], jnp.where(lo_state, 0.0, ht).astype(BF16))
                decay = jnp.where(lo_row, jnp.exp2(c_last[hd0:hd0 + 1, :]), jnp.exp2(c_last[hd1:hd1 + 1, :]))
                upd = upd + ht * decay
            hst[d, j] = upd
            ysum = y if ysum is None else ysum + y
        if first_write:
            yacc[rows, j * LANES:(j + 1) * LANES] = ysum
        else:
            yacc[rows, j * LANES:(j + 1) * LANES] += ysum


def _ssd_kernel(x_ref, bc_ref, z_ref, gt_ref, cw_ref, bias_ref, nega_ref, dskip_ref, ng_ref, sel_ref, *rest,
                seq_len, nsq, is_ctx):
    if is_ctx:
        y_ref, hfin_ref, xs_s, bcs_s, yacc, hst = rest
    else:
        h0_ref, y_ref, xs_s, bcs_s, yacc, hst = rest
    nchunk = seq_len // CH
    triu = _tri(CH, False)
    rr = lax.broadcasted_iota(jnp.int32, (CH, CH), 0)
    ss = lax.broadcasted_iota(jnp.int32, (CH, CH), 1)
    masks = (ss <= rr, ss >= rr)

    def conv(v, lo_col):
        row = lax.broadcasted_iota(jnp.int32, v.shape, 0)
        prev = jnp.where(row == 0, 0.0, pltpu.roll(v, 1, 0))
        nxt = jnp.where(row == seq_len - 1, 0.0, pltpu.roll(v, seq_len - 1, 0))
        cs = slice(lo_col, lo_col + v.shape[1])
        return _silu(cw_ref[0:1, cs] * prev + cw_ref[1:2, cs] * v + cw_ref[2:3, cs] * nxt + cw_ref[3:4, cs])

    for sq in range(nsq):
        base = sq * seq_len
        seq_rows = slice(base, base + seq_len)
        for k in range(MIX_W // LANES):
            xs_s[seq_rows, k * LANES:(k + 1) * LANES] = conv(x_ref[k, seq_rows, :], k * LANES)
        for k in range(2):
            bcs_s[seq_rows, k * LANES:(k + 1) * LANES] = conv(bc_ref[k, seq_rows, :], MIX_W + k * LANES)
        if not is_ctx:
            hst[sq] = h0_ref[sq]
            if nchunk > 1:
                yacc[seq_rows, :] = jnp.zeros((seq_len, MIX_W), F32)

    def prep(rows):
        return _ssd_prep(gt_ref, bias_ref, nega_ref, rows, triu)

    def main(p, rows, dirs, sq):
        _ssd_main(p, xs_s, bcs_s, sel_ref, rows, dirs, hst.at[sq], yacc, masks, not is_ctx, nchunk == 1)

    if nchunk == 1:
        preps = [prep(_chunk_rows(0, sq * seq_len)) for sq in range(nsq)]
        for sq in range(nsq):
            main(preps[sq], _chunk_rows(0, sq * seq_len), (0, 1), sq)
    else:
        for sq in range(nsq):
            def body(ci, carry, sq=sq):
                rows_f = _chunk_rows(ci, sq * seq_len)
                rows_b = _chunk_rows(nchunk - 1 - ci, sq * seq_len)
                p_f, p_b = prep(rows_f), prep(rows_b)
                main(p_f, rows_f, (0,), sq)
                main(p_b, rows_b, (1,), sq)
                return carry
            lax.fori_loop(0, nchunk, body, 0)

    for sq in range(nsq):
        base = sq * seq_len
        seq_rows = slice(base, base + seq_len)
        z = jnp.concatenate([z_ref[k, seq_rows, :] for k in range(MIX_W // LANES)], axis=1).astype(F32)
        y = (yacc[seq_rows, :] + dskip_ref[...] * xs_s[seq_rows, :]) * _silu(z)
        y = (y * lax.rsqrt(jnp.mean(y * y, -1, keepdims=True) + RMS_EPS) * ng_ref[...]).astype(y_ref.dtype)
        for k in range(MIX_W // LANES):
            y_ref[k, seq_rows, :] = y[:, k * LANES:(k + 1) * LANES]
        if is_ctx:
            hfin_ref[sq] = hst[sq]


def _ssd_call(ub, uf, cw, bias_row, nega_rows, dskip, ng, sel, states, *, is_ctx, seq_len, row0, nseq, nsq, layer,
              depth):
    rows = nsq * seq_len
    bo = row0 // rows
    ns = MIX_W // LANES
    row_spec = lambda w: pl.BlockSpec((1, w), lambda i: (0, 0))
    in_specs = [pl.BlockSpec((ns, rows, LANES), lambda i: (2, bo + i, 0)),
                pl.BlockSpec((2, rows, LANES), lambda i: (6, bo + i, 0)),
                pl.BlockSpec((ns, rows, LANES), lambda i: (7, bo + i, 0)),
                pl.BlockSpec((None, rows, LANES), lambda i: (GATE_SLAB, bo + i, 0)),
                pl.BlockSpec((8, MB_XBC), lambda i: (0, 0)),
                row_spec(LANES),
                pl.BlockSpec((2 * MB_HEADS, CH), lambda i: (0, 0)),
                row_spec(MIX_W), row_spec(MIX_W),
                pl.BlockSpec((N_SEL, LANES, LANES), lambda i: (0, 0, 0))]
    st_spec, st_shape = _state_io((2, N_PAIR, LANES, LANES), nseq, nsq, layer, depth)
    return _seq_call(
        _ssd_kernel, "ssd", in_specs, [uf, uf, ub, uf, cw, bias_row, nega_rows, dskip, ng, sel], ns,
        [st_spec], [st_shape], states,
        [pltpu.VMEM((rows, MIX_W), F32), pltpu.VMEM((rows, 2 * LANES), F32), pltpu.VMEM((rows, MIX_W), F32),
         pltpu.VMEM((nsq, 2, N_PAIR, LANES, LANES), F32)],
        is_ctx, seq_len=seq_len, nseq=nseq, nsq=nsq)


def _ssd_state_to_pairs(s):
    bsz = s.shape[0]
    st = jnp.swapaxes(s, -1, -2).reshape(bsz, 2, N_PAIR, 2, MB_DSTATE, MB_HEADDIM)
    st = jnp.moveaxis(st, 3, 4).reshape(bsz, 2, N_PAIR, MB_DSTATE, 2 * MB_HEADDIM)
    zero = jnp.zeros_like(st)
    grp = (jnp.arange(N_PAIR) // PAIRS_PER_GROUP).reshape(1, 1, N_PAIR, 1, 1)
    return jnp.concatenate([jnp.where(grp == 0, st, zero), jnp.where(grp == 1, st, zero)], axis=3)


def _ssd_pairs_to_state(hp):
    bsz = hp.shape[0]
    halves = hp.reshape(bsz, 2, N_PAIR, MB_GROUPS, MB_DSTATE, 2 * MB_HEADDIM)
    grp = (jnp.arange(N_PAIR) // PAIRS_PER_GROUP).reshape(1, 1, N_PAIR, 1, 1)
    st = jnp.where(grp == 0, halves[:, :, :, 0], halves[:, :, :, 1])
    st = st.reshape(bsz, 2, N_PAIR, MB_DSTATE, 2, MB_HEADDIM)
    st = jnp.moveaxis(st, 4, 3).reshape(bsz, 2, MB_HEADS, MB_DSTATE, MB_HEADDIM)
    return jnp.swapaxes(st, -1, -2)


def _grid_pos_embed(n_tok, d_model):
    rows = n_tok // GRID_W
    r, cidx = jnp.meshgrid(jnp.arange(rows, dtype=F32), jnp.arange(GRID_W, dtype=F32), indexing='ij')
    quarter = d_model // 4
    freq = jnp.exp(-math.log(10000.0) * jnp.arange(quarter, dtype=F32) / quarter)
    ar = r.reshape(-1, 1) * freq
    ac = cidx.reshape(-1, 1) * freq
    return jnp.concatenate([jnp.sin(ar), jnp.cos(ar), jnp.sin(ac), jnp.cos(ac)], axis=-1)


def _gate_row(pieces):
    v = jnp.concatenate([p.reshape(-1).astype(F32) for p in pieces])
    return jnp.concatenate([v, v, v, jnp.zeros((LANES - 3 * N_SEL,), F32)]).reshape(1, LANES)


def kernel(x_prompt, x_sample, state_hgrn, state_mlstm_C, state_mlstm_n, state_mlstm_m, state_ssd, c, c_ctx,
           w_mod, b_mod, ln_g, ln_b, ffn_w_gu, ffn_w_down, w_in, hg_lb, hg_norm_g, ml_gate_b, ml_norm_g,
           mb_conv_w, mb_conv_b, mb_dt_bias, mb_a_log, mb_d, mb_norm_g, w_branch, w_out):
    bsz, seq, d = x_prompt.shape
    dbsz, dseq, _ = x_sample.shape
    depth = w_mod.shape[0]
    t_ctx = bsz * seq
    n_ctx_tiles = t_ctx // ROW_TILE
    assert dseq == ROW_TILE and t_ctx % ROW_TILE == 0 and seq == CH and dseq % CH == 0

    xs0 = x_sample + _grid_pos_embed(dseq, d).astype(x_sample.dtype)[None]
    x = jnp.concatenate([x_prompt.reshape(t_ctx, d), xs0.reshape(dbsz * dseq, d)], axis=0)

    cv = jnp.concatenate([c_ctx[None]] * MOD_LAT_ROW + [c, jnp.zeros((8 - MOD_LAT_ROW - dbsz, d), F32)], axis=0)
    mod4 = _mod_call(cv, w_mod, b_mod).reshape(depth, 8, 1, N_MOD * d)

    w_branch_b = w_branch.astype(BF16)
    w_out_b = w_out.astype(BF16)

    def regroup(width, pieces):
        out = jnp.zeros((depth, d, width), BF16)
        at = 0
        for lo, hi in pieces:
            out = lax.dynamic_update_slice(out, w_in[:, :, lo:hi].astype(BF16), (0, 0, at))
            at += hi - lo
        return out, at

    w_a, _ = regroup(UB_SLABS * LANES, [(0, 1536), (2560, 4608), (4624, 5136), (5920, 8992)])
    gate_cols = [(4608, 4624), (5904, 5920)]
    w_b, _ = regroup(UF_SLABS * LANES, [(1536, 2560), (5136, 5904)] + gate_cols * 3)
    ln_g4 = ln_g.reshape(depth, 3, 1, d)
    ln_b4 = ln_b.reshape(depth, 3, 1, d)

    lbs = jnp.cumsum(jax.nn.softmax(hg_lb.astype(F32), axis=0), axis=0)
    lbs = (lbs - lbs[0]).reshape(depth, 2 * HG_HEADS, HG_DK)
    lbp = jnp.stack([lbs, 1.0 - lbs], axis=1)
    cw = jnp.concatenate([mb_conv_w, mb_conv_b[:, None, :], jnp.zeros((depth, 4, MB_XBC), F32)], axis=1)
    sel = _selectors()

    lat_h0 = _ssd_state_to_pairs(state_ssd.reshape((dbsz * depth,) + state_ssd.shape[2:])).reshape(
        (dbsz, depth, 2, N_PAIR, LANES, LANES))
    lat_n0 = state_mlstm_n.reshape(dbsz, depth, 2 * ML_HEADS, ML_DK)
    lat_m0 = jnp.broadcast_to(state_mlstm_m.reshape(dbsz, depth, 2 * ML_HEADS, 1), (dbsz, depth, 2 * ML_HEADS, LANES))

    zeros = lambda *s: jnp.zeros((bsz, depth) + s, F32)
    st_hg = [zeros(2, HG_HEADS, HG_DK, LANES)]
    st_ml = [zeros(2, ML_HEADS, ML_DK, LANES), zeros(2 * ML_HEADS, LANES), zeros(2 * ML_HEADS, LANES)]
    st_ss = [zeros(2, N_PAIR, LANES, LANES)]
    t_idx = jnp.arange(CH).reshape(1, CH, 1)
    widths = (HALF >> jnp.arange(N_LEVEL + 1)).reshape(N_LEVEL + 1, 1, 1)
    sgn = jnp.broadcast_to(jnp.where((t_idx & widths) != 0, 1.0, -1.0), (N_LEVEL + 1, CH, LANES)).astype(F32)
    ctx = dict(is_ctx=True, seq_len=seq, row0=0, nseq=bsz, nsq=CTX_SEQS_PER_STEP, depth=depth)
    lat = dict(is_ctx=False, seq_len=dseq, row0=t_ctx, nseq=dbsz, nsq=1, depth=depth)
    for l in range(depth):
        x = _ffn_call(x, mod4, ffn_w_gu, ffn_w_down, ln_g4, ln_b4, l, 0, n_ctx_tiles)
        ub = _inproj_call(x, mod4, w_a, l, INPROJ_TILE_A, BF16, n_ctx_tiles, "inproj_a")
        uf = _inproj_call(x, mod4, w_b, l, INPROJ_TILE_B, F32, n_ctx_tiles, "inproj_b")

        ng_h = hg_norm_g[l].reshape(HG_HEADS, LANES)
        y_hg, *st_hg = _hgrn_call(ub, uf, lbp[l], ng_h, sgn, st_hg, layer=l, **ctx)
        y_hg_lat, = _hgrn_call(ub, uf, lbp[l], ng_h, sgn, [state_hgrn], layer=l, **lat)

        gate_bias = _gate_row([ml_gate_b[l, 0], ml_gate_b[l, 1], mb_dt_bias[l]])
        ng_m = ml_norm_g[l].reshape(ML_HEADS, LANES)
        y_ml, *st_ml = _mlstm_call(ub, uf, gate_bias, ng_m, sel, st_ml, layer=l, **ctx)
        y_ml_lat, = _mlstm_call(ub, uf, gate_bias, ng_m, sel, [state_mlstm_C, lat_n0, lat_m0], layer=l, **lat)

        nega = jnp.broadcast_to(-jnp.exp(mb_a_log[l].astype(F32)).reshape(2 * MB_HEADS, 1), (2 * MB_HEADS, CH))
        dskip = jnp.repeat(mb_d[l], MB_HEADDIM).reshape(1, MIX_W)
        ng_s = mb_norm_g[l].reshape(1, MIX_W)
        y_mb, *st_ss = _ssd_call(ub, uf, cw[l], gate_bias, nega, dskip, ng_s, sel, st_ss, layer=l, **ctx)
        y_mb_lat, = _ssd_call(ub, uf, cw[l], gate_bias, nega, dskip, ng_s, sel, [lat_h0], layer=l, **lat)

        x = _mixout_call(x, mod4, (y_hg, y_ml, y_mb), (y_hg_lat, y_ml_lat, y_mb_lat), ub, w_branch_b, w_out_b,
                         ln_g4, ln_b4, l, n_ctx_tiles)
        x = _ffn_call(x, mod4, ffn_w_gu, ffn_w_down, ln_g4, ln_b4, l, 1, n_ctx_tiles)

    y_prompt = x[:t_ctx].reshape(bsz, seq, d)
    y_sample = x[t_ctx:].reshape(dbsz, dseq, d)
    c_fin, n_fin, m_fin = st_ml
    h_fin = _ssd_pairs_to_state(st_ss[0].reshape((bsz * depth,) + st_ss[0].shape[2:]))
    return (y_prompt, y_sample, st_hg[0], c_fin, n_fin.reshape(bsz, depth, 2, ML_HEADS, ML_DK),
            m_fin[:, :, :, 0].reshape(bsz, depth, 2, ML_HEADS),
            h_fin.reshape((bsz, depth) + h_fin.shape[1:]))
```

```python
import functools
import math

import jax
import jax.numpy as jnp
from jax import lax
from jax.experimental import pallas as pl
from jax.experimental.pallas import tpu as pltpu

F32 = jnp.float32
BF16 = jnp.bfloat16

D_MODEL = 1024
DEPTH = 4
GRID_W = 64
MIX_W = 512
HG_HEADS = 4
HG_DK = 128
ML_HEADS = 4
ML_DK = 128
MB_HEADS = 8
MB_HEADDIM = 64
MB_GROUPS = 2
MB_DSTATE = 64
MB_XBC = MIX_W + 2 * MB_GROUPS * MB_DSTATE
D_FF = 2816
N_MOD = 9
ALPHA = (2 * DEPTH) ** 0.25
LN_EPS = 1e-5
RMS_EPS = 1e-6
LOG2E = 1.4426950408889634

LANES = 128
CH = 256
HALF = CH // 2
ROW_TILE = 1024
FFN_ROWS = 2048
MOD_LAT_ROW = 2
FF_TILE = 256
INPROJ_TILE_A = 1792
INPROJ_TILE_B = 1920
LAT_HEADS_PER_ITER = 2
CTX_SEQS_PER_STEP = 4
VMEM_LIMIT = 56 * 1024 * 1024

UB_SLABS = 56
UF_SLABS = 15
GATE_SLAB = 14
N_SEL = 32
LOG2_TINY = -150.0


def _dot(a, b):
    return jnp.dot(a, b, preferred_element_type=F32)


def _dot_nt(a, b):
    return lax.dot_general(a, b, (((1,), (1,)), ((), ())), preferred_element_type=F32)


def _dot_tn(a, b):
    return lax.dot_general(a, b, (((0,), (0,)), ((), ())), preferred_element_type=F32)


def _sigmoid(x):
    return 1.0 / (1.0 + jnp.exp(-x))


def _silu(x):
    return x * _sigmoid(x)


def _log_sigmoid(x):
    return jnp.minimum(x, 0.0) - jnp.log(1.0 + jnp.exp(-jnp.abs(x)))


def _softplus(x):
    return jnp.maximum(x, 0.0) + jnp.log(1.0 + jnp.exp(-jnp.abs(x)))


def _neg_abs(x):
    return -jnp.abs(x)


def _ln(z, g, b):
    mu = jnp.mean(z, -1, keepdims=True)
    d = z - mu
    var = jnp.mean(d * d, -1, keepdims=True)
    return d * lax.rsqrt(var + LN_EPS) * g + b


def _split3(x):
    h = x.astype(BF16)
    r = x - h.astype(F32)
    m = r.astype(BF16)
    l = (r - m.astype(F32)).astype(BF16)
    return h, m, l


def _cumsum_rows(tril, x):
    h, m, l = _split3(x)
    return _dot(tril, h) + _dot(tril, m) + _dot(tril, l)


def _cumsum_lanes(x, triu):
    h, m, l = _split3(x)
    return _dot(h, triu) + _dot(m, triu) + _dot(l, triu)


def _cummax_lanes(x, reverse):
    n = x.shape[1]
    lane = lax.broadcasted_iota(jnp.int32, x.shape, 1)
    k = 1
    while k < n:
        if reverse:
            sh = jnp.where(lane < n - k, pltpu.roll(x, n - k, 1), -jnp.inf)
        else:
            sh = jnp.where(lane >= k, pltpu.roll(x, k, 1), -jnp.inf)
        x = jnp.maximum(x, sh)
        k *= 2
    return x


def _tri(c, lower):
    r = lax.broadcasted_iota(jnp.int32, (c, c), 0)
    s = lax.broadcasted_iota(jnp.int32, (c, c), 1)
    return jnp.where((s <= r) if lower else (r <= s), 1.0, 0.0).astype(BF16)


def _sel_lhs(x_rows):
    h = x_rows.astype(BF16).astype(F32)
    r = x_rows - h
    m = r.astype(BF16).astype(F32)
    rows = jnp.concatenate([h, m, r - m, jnp.ones_like(x_rows)], axis=0)
    return rows.T.astype(BF16)


def _selectors():
    r = jnp.arange(LANES)
    ch = jnp.arange(N_SEL)
    hit = ((r[None, :] % N_SEL) == ch[:, None]) & (r[None, :] < 3 * N_SEL)
    return jnp.broadcast_to(hit[:, :, None], (N_SEL, LANES, LANES)).astype(BF16)


def _cparams(sem):
    return pltpu.CompilerParams(dimension_semantics=sem, vmem_limit_bytes=VMEM_LIMIT)


def _chunk_rows(ci, base=0):
    r0 = ci * CH + base
    if not isinstance(r0, int):
        r0 = pl.multiple_of(r0, CH)
    return pl.ds(r0, CH)


def _state_io(shape_tail, nseq, nsq, layer, depth):
    nz = (0,) * len(shape_tail)
    spec = pl.BlockSpec((nsq, None) + shape_tail, lambda i: (i, layer) + nz)
    shape = jax.ShapeDtypeStruct((nseq, depth) + shape_tail, F32)
    return spec, shape


def _chunk_loop(nchunk, run):
    if nchunk == 1:
        run(0, (0, 1))
    else:
        def body(ci, carry):
            run(ci, (0,))
            run(nchunk - 1 - ci, (1,))
            return carry
        lax.fori_loop(0, nchunk, body, 0)


def _mod_kernel(c_ref, w_ref, b_ref, o_ref):
    a = _silu(c_ref[...]).astype(BF16)
    o_ref[...] = _dot(a, w_ref[...].astype(BF16)) + b_ref[...]


def _mod_call(cv, w_mod, b_mod):
    depth, d, e = w_mod.shape
    tn = 1024
    return pl.pallas_call(
        _mod_kernel,
        grid=(depth, e // tn),
        in_specs=[pl.BlockSpec((8, d), lambda l, j: (0, 0)),
                  pl.BlockSpec((None, d, tn), lambda l, j: (l, 0, j)),
                  pl.BlockSpec((None, 1, tn), lambda l, j: (l, 0, j))],
        out_specs=pl.BlockSpec((None, 8, tn), lambda l, j: (l, 0, j)),
        out_shape=jax.ShapeDtypeStruct((depth, 8, e), F32),
        compiler_params=_cparams(("arbitrary", "arbitrary")),
        name="mod",
    )(cv, w_mod, b_mod.reshape(depth, 1, e))


def _mod_row(i, n_ctx_tiles):
    return jnp.maximum(i - n_ctx_tiles + MOD_LAT_ROW, 0)


def _ffn_kernel(x_ref, mod_ref, wa_ref, wb_ref, wd_ref, g_ref, b_ref, o_ref, xm_s, *, nf):
    j = pl.program_id(1)
    d = x_ref.shape[1]
    groups = [(r, slice(r * ROW_TILE, (r + 1) * ROW_TILE)) for r in range(FFN_ROWS // ROW_TILE)]

    def ffn(xm):
        wa, wb, wd = wa_ref[...].astype(BF16), wb_ref[...].astype(BF16), wd_ref[...].astype(BF16)
        return _dot((_silu(_dot(xm, wa)) * _dot(xm, wb)).astype(BF16), wd)

    @pl.when(j == 0)
    def _():
        for r, rows in groups:
            sh = mod_ref[r:r + 1, 0:d]
            sc = mod_ref[r:r + 1, d:2 * d]
            xm = (x_ref[rows, :] * (1.0 + sc) + sh).astype(BF16)
            xm_s[rows, :] = xm
            o_ref[rows, :] = ffn(xm)

    @pl.when(jnp.logical_and(j > 0, j < nf - 1))
    def _():
        o_ref[...] += ffn(xm_s[...])

    @pl.when(j == nf - 1)
    def _():
        for r, rows in groups:
            gate = mod_ref[r:r + 1, 2 * d:3 * d]
            z = ALPHA * x_ref[rows, :] + 0.5 * gate * (o_ref[rows, :] + ffn(xm_s[rows, :]))
            o_ref[rows, :] = _ln(z, g_ref[...], b_ref[...])


def _ffn_call(x, mod4, w_gu, w_down, ln_g, ln_b, layer, which, n_ctx_tiles):
    t, d = x.shape
    nf = D_FF // FF_TILE
    per = FFN_ROWS // ROW_TILE
    assert nf >= 3 and nf * FF_TILE == D_FF and n_ctx_tiles % per == 0 and MOD_LAT_ROW % per == 0
    sub = 0 if which == 0 else 2
    mod_pairs = mod4.reshape(mod4.shape[0], mod4.shape[1] // per, per, mod4.shape[3])
    first_lat = n_ctx_tiles // per
    mod_blk = lambda i, j: (layer, jnp.maximum(i - first_lat + MOD_LAT_ROW // per, 0), 0, sub)
    return pl.pallas_call(
        functools.partial(_ffn_kernel, nf=nf),
        grid=(t // FFN_ROWS, nf),
        in_specs=[pl.BlockSpec((FFN_ROWS, d), lambda i, j: (i, 0)),
                  pl.BlockSpec((None, None, per, 3 * d), mod_blk),
                  pl.BlockSpec((None, None, d, FF_TILE), lambda i, j: (layer, which, 0, j)),
                  pl.BlockSpec((None, None, d, FF_TILE), lambda i, j: (layer, which, 0, j + nf)),
                  pl.BlockSpec((None, None, FF_TILE, d), lambda i, j: (layer, which, j, 0)),
                  pl.BlockSpec((None, None, 1, d), lambda i, j: (layer, sub, 0, 0)),
                  pl.BlockSpec((None, None, 1, d), lambda i, j: (layer, sub, 0, 0))],
        out_specs=pl.BlockSpec((FFN_ROWS, d), lambda i, j: (i, 0)),
        out_shape=jax.ShapeDtypeStruct((t, d), F32),
        scratch_shapes=[pltpu.VMEM((FFN_ROWS, d), BF16)],
        compiler_params=_cparams(("arbitrary", "arbitrary")),
        name=f"ffn{which}",
    )(x, mod_pairs, w_gu, w_gu, w_down, ln_g, ln_b)


def _inproj_kernel(x_ref, mod_ref, w_ref, o_ref, xm_s):
    d = x_ref.shape[1]

    @pl.when(pl.program_id(1) == 0)
    def _():
        sh = mod_ref[:, 0:d]
        sc = mod_ref[:, d:2 * d]
        xm_s[...] = (x_ref[...] * (1.0 + sc) + sh).astype(BF16)

    res = _dot(xm_s[...], w_ref[...]).astype(o_ref.dtype)
    for k in range(o_ref.shape[0]):
        o_ref[k] = res[:, k * LANES:(k + 1) * LANES]


def _inproj_call(x, mod4, w, layer, tn, out_dtype, n_ctx_tiles, name):
    t, d = x.shape
    n = w.shape[2]
    return pl.pallas_call(
        _inproj_kernel,
        grid=(t // ROW_TILE, n // tn),
        in_specs=[pl.BlockSpec((ROW_TILE, d), lambda i, j: (i, 0)),
                  pl.BlockSpec((None, None, 1, 3 * d), lambda i, j: (layer, _mod_row(i, n_ctx_tiles), 0, 1)),
                  pl.BlockSpec((None, d, tn), lambda i, j: (layer, 0, j))],
        out_specs=pl.BlockSpec((tn // LANES, ROW_TILE, LANES), lambda i, j: (j, i, 0)),
        out_shape=jax.ShapeDtypeStruct((n // LANES, t, LANES), out_dtype),
        scratch_shapes=[pltpu.VMEM((ROW_TILE, d), BF16)],
        compiler_params=_cparams(("arbitrary", "arbitrary")),
        name=name,
    )(x, mod4, w)


def _slabs(ref):
    return jnp.concatenate([ref[k] for k in range(ref.shape[0])], axis=1)


def _mixout_kernel(x_ref, mod_ref, yhc_ref, ymc_ref, ysc_ref, yhl_ref, yml_ref, ysl_ref, g0_ref, g1_ref, g2_ref,
                   wb_ref, wo_ref, lg_ref, lb_ref, o_ref, *, n_ctx_steps):
    d = x_ref.shape[1]
    is_lat = pl.program_id(0) >= n_ctx_steps

    def branch(c_ref, l_ref):
        return jnp.where(is_lat, _slabs(l_ref), _slabs(c_ref))

    p = _sigmoid(_slabs(g0_ref).astype(F32)) * _dot(branch(yhc_ref, yhl_ref), wb_ref[0])
    p += _sigmoid(_slabs(g1_ref).astype(F32)) * _dot(branch(ymc_ref, yml_ref), wb_ref[1])
    p += _sigmoid(_slabs(g2_ref).astype(F32)) * _dot(branch(ysc_ref, ysl_ref), wb_ref[2])
    y = _dot(p.astype(BF16), wo_ref[...])
    gate = mod_ref[:, 2 * d:3 * d]
    z = ALPHA * x_ref[...] + gate * y
    o_ref[...] = _ln(z, lg_ref[...], lb_ref[...])


def _mixout_call(x, mod4, y_ctx, y_lat, ub, w_branch, w_out, ln_g, ln_b, layer, n_ctx_tiles):
    t, d = x.shape
    tm = 512
    per = ROW_TILE // tm
    ns = MIX_W // LANES
    ng = d // LANES
    g0 = (UB_SLABS - 3 * ng) // ng
    n_ctx_steps = n_ctx_tiles * per
    n_lat_steps = t // tm - n_ctx_steps
    yc_spec = pl.BlockSpec((ns, tm, LANES), lambda i: (0, jnp.minimum(i, n_ctx_steps - 1), 0))
    yl_spec = pl.BlockSpec((ns, tm, LANES), lambda i: (0, jnp.clip(i - n_ctx_steps, 0, n_lat_steps - 1), 0))
    return pl.pallas_call(
        functools.partial(_mixout_kernel, n_ctx_steps=n_ctx_steps),
        grid=(t // tm,),
        in_specs=[pl.BlockSpec((tm, d), lambda i: (i, 0)),
                  pl.BlockSpec((None, None, 1, 3 * d), lambda i: (layer, _mod_row(i // per, n_ctx_tiles), 0, 1)),
                  yc_spec, yc_spec, yc_spec, yl_spec, yl_spec, yl_spec,
                  pl.BlockSpec((ng, tm, LANES), lambda i: (g0, i, 0)),
                  pl.BlockSpec((ng, tm, LANES), lambda i: (g0 + 1, i, 0)),
                  pl.BlockSpec((ng, tm, LANES), lambda i: (g0 + 2, i, 0)),
                  pl.BlockSpec((None, 3, MIX_W, d), lambda i: (layer, 0, 0, 0)),
                  pl.BlockSpec((None, d, d), lambda i: (layer, 0, 0)),
                  pl.BlockSpec((None, None, 1, d), lambda i: (layer, 1, 0, 0)),
                  pl.BlockSpec((None, None, 1, d), lambda i: (layer, 1, 0, 0))],
        out_specs=pl.BlockSpec((tm, d), lambda i: (i, 0)),
        out_shape=jax.ShapeDtypeStruct((t, d), F32),
        compiler_params=_cparams(("arbitrary",)),
        name="mixout",
    )(x, mod4, *y_ctx, *y_lat, ub, ub, ub, w_branch, w_out, ln_g, ln_b)


N_LEVEL = int(math.log2(HALF))


def _block_ref_rows(c, w):
    n_rows = c.shape[0]
    if 2 * w == n_rows:
        return jnp.broadcast_to(c[w - 1:w, :], c.shape)
    if w >= 4:
        n = n_rows // (2 * w)
        c3 = c.reshape(n, 2 * w, LANES)
        return jnp.broadcast_to(c3[:, w - 1:w, :], c3.shape).reshape(n_rows, LANES)
    c8 = c.reshape(n_rows // 8, 8, LANES)
    sub = lax.broadcasted_iota(jnp.int32, c8.shape, 1)
    if w == 2:
        r = jnp.where(sub < 4, c8[:, 1:2, :], c8[:, 5:6, :])
    else:
        r = jnp.where(sub < 2, c8[:, 0:1, :],
                      jnp.where(sub < 4, c8[:, 2:3, :], jnp.where(sub < 6, c8[:, 4:5, :], c8[:, 6:7, :])))
    return r.reshape(n_rows, LANES)


def _level_ids():
    r = lax.broadcasted_iota(jnp.int32, (HALF, HALF), 0)
    s = lax.broadcasted_iota(jnp.int32, (HALF, HALF), 1)
    lv = 31 - lax.clz(r ^ s)
    lv = jnp.where(r == s, N_LEVEL, lv)
    return jnp.where(r >= s, lv, -1), jnp.where(r <= s, lv, -1)


def _hgrn_gates(q, fpre, lb, onemlb, tril, backward):
    u = jnp.exp2(_neg_abs(fpre) * LOG2E)
    r = 1.0 / (1.0 + u)
    pos_side = fpre >= 0.0
    kk = onemlb * (jnp.where(pos_side, u, 1.0) * r)
    f = lb + onemlb * (jnp.where(pos_side, 1.0, u) * r)
    lf2 = jnp.maximum(jnp.log2(f), LOG2_TINY)
    cum = _cumsum_rows(tril, lf2)
    pos = (cum - lf2) if backward else cum
    return dict(q=q, kk=kk, qb=q.astype(BF16), kb=kk.astype(BF16), cum=cum, pos=pos, f=f)


def _hgrn_levels(p, lvl, sgn_ref, backward):
    qb, kb, cum, pos = p["qb"], p["kb"], p["cum"], p["pos"]

    def sides(w):
        if w == 1:
            odd = (lax.broadcasted_iota(jnp.int32, (CH, LANES), 0) & 1) == 1
            e = (jnp.where(odd, 1.0, p["f"]) if backward else jnp.where(odd, p["f"], 1.0)).astype(BF16)
            return qb * e, kb * e
        e = jnp.exp2((pos - _block_ref_rows(cum, w)) * sgn_ref[N_LEVEL - int(math.log2(w))]).astype(BF16)
        return qb * e, kb * e

    zq, zk = sides(HALF)
    if backward:
        cross = _dot_nt(zq[0:HALF], zk[HALF:CH])
    else:
        cross = _dot_nt(zq[HALF:CH], zk[0:HALF])
    halves = (slice(0, HALF), slice(HALF, CH))
    prods = [_dot_nt(qb[rows], kb[rows]) for rows in halves]
    zq, zk = sides(HALF // 2)
    diag = [jnp.where(lvl == N_LEVEL, prods[b], 0.0) for b in range(2)]
    w = HALF // 2
    while w >= 1:
        prods = [_dot_nt(zq[rows], zk[rows]) for rows in halves]
        if w > 1:
            zq, zk = sides(w // 2)
        lv = int(math.log2(w))
        diag = [jnp.where(lvl == lv, prods[b], diag[b]) for b in range(2)]
        w //= 2
    return diag[0].astype(BF16), diag[1].astype(BF16), cross.astype(BF16)


def _hgrn_finish(p, att, v, s_t, backward, has_init):
    q, kk, cum, pos = p["q"], p["kk"], p["cum"], p["pos"]
    d0, d1, cr = att
    if backward:
        o_lo = _dot(jnp.concatenate([d0, cr], axis=1), v)
        o_hi = _dot(d1, v[HALF:CH])
    else:
        o_lo = _dot(d0, v[0:HALF])
        o_hi = _dot(jnp.concatenate([cr, d1], axis=1), v)
    o = jnp.concatenate([o_lo, o_hi], axis=0)
    c_last = cum[CH - 1:CH, :]
    k_out = kk * jnp.exp2((pos) if backward else (c_last - cum))
    s_new = _dot_tn(v, k_out.astype(BF16))
    if has_init:
        q_in = q * jnp.exp2((c_last - pos) if backward else cum)
        o = o + _dot_nt(q_in.astype(BF16), s_t.astype(BF16))
        s_new = s_new + s_t * jnp.exp2(c_last)
    return o, s_new


def _hgrn_kernel(q_ref, v_ref, g_ref, f_ref, lbp_ref, ng_ref, sgn_ref, *rest, seq_len, nsq, is_ctx):
    if is_ctx:
        y_ref, sfin_ref, oacc, st = rest
    else:
        s0_ref, y_ref, oacc, st = rest
    nchunk = seq_len // CH
    hpi = oacc.shape[0]
    tril = _tri(CH, True)
    lvl_f, lvl_b = _level_ids()
    units = [(hh, sq) for hh in range(hpi) for sq in range(nsq)]

    def head_group(hg, carry):
        head = lambda hh: hg * hpi + hh
        slot = lambda hh, sq, d: (hh * nsq + sq) * 2 + d
        for hh, sq in units:
            base = sq * seq_len
            if not is_ctx:
                for d in range(2):
                    st[slot(hh, sq, d)] = s0_ref[sq, d, head(hh)].T
                if nchunk > 1:
                    oacc[hh, base:base + seq_len, :] = jnp.zeros((seq_len, LANES), F32)

        def gates(hh, rows, d, q=None):
            h = head(hh)
            lrow = pl.ds(d * HG_HEADS + h, 1)
            q = _silu(q_ref[h, rows, :].astype(F32)) if q is None else q
            return _hgrn_gates(q, f_ref[d * HG_HEADS + h, rows, :], lbp_ref[0, lrow, :], lbp_ref[1, lrow, :],
                               tril, d == 1)

        def levels(p, d):
            return _hgrn_levels(p, lvl_b if d else lvl_f, sgn_ref, d == 1)

        def finish(p, att, hh, rows, d, sq, first):
            o, s_new = _hgrn_finish(p, att, v_ref[head(hh), rows, :], None if is_ctx else st[slot(hh, sq, d)],
                                    d == 1, not is_ctx)
            st[slot(hh, sq, d)] = s_new
            if first:
                oacc[hh, rows, :] = o
            else:
                oacc[hh, rows, :] += o

        if nchunk == 1:
            work = [(hh, sq, d, _chunk_rows(0, sq * seq_len)) for hh, sq in units for d in range(2)]
            qs = {(hh, sq): _silu(q_ref[head(hh), _chunk_rows(0, sq * seq_len), :].astype(F32)) for hh, sq in units}
            ps = [gates(hh, rows, d, qs[hh, sq]) for hh, sq, d, rows in work]
            atts = [levels(p, d) for (_, _, d, _), p in zip(work, ps)]
            for (hh, sq, d, rows), p, att in zip(work, ps, atts):
                finish(p, att, hh, rows, d, sq, d == 0)
        else:
            def body(ci, carry):
                work = [(hh, sq, d, _chunk_rows(ci if d == 0 else nchunk - 1 - ci, sq * seq_len))
                        for hh, sq in units for d in range(2)]
                ps = [gates(hh, rows, d) for hh, _, d, rows in work]
                atts = [levels(p, d) for (_, _, d, _), p in zip(work, ps)]
                for (hh, sq, d, rows), p, att in zip(work, ps, atts):
                    finish(p, att, hh, rows, d, sq, False)
                return carry
            lax.fori_loop(0, nchunk, body, 0)

        for hh, sq in units:
            h = head(hh)
            base = sq * seq_len
            o = oacc[hh, base:base + seq_len, :]
            y = o * lax.rsqrt(jnp.mean(o * o, -1, keepdims=True) + RMS_EPS) * ng_ref[pl.ds(h, 1), :]
            gate = _silu(g_ref[h, base:base + seq_len, :].astype(F32))
            y_ref[h, base:base + seq_len, :] = (y * gate).astype(y_ref.dtype)
            if is_ctx:
                for d in range(2):
                    sfin_ref[sq, d, h] = st[slot(hh, sq, d)].T
        return carry

    lax.fori_loop(0, HG_HEADS // hpi, head_group, 0)


def _seq_call(kern, name, in_specs, args, heads, st_specs, st_shapes, states, scratch, is_ctx, *, seq_len, nseq, nsq):
    in_specs, args = list(in_specs), list(args)
    n_used = len(args)
    aliases = {}
    y_spec = pl.BlockSpec((heads, nsq * seq_len, LANES), lambda i: (0, i, 0))
    y_shape = jax.ShapeDtypeStruct((heads, nseq * seq_len, LANES), BF16)
    if is_ctx:
        out_specs, out_shape = [y_spec] + list(st_specs), [y_shape] + list(st_shapes)
        for k, s in enumerate(states):
            in_specs.append(pl.BlockSpec(memory_space=pl.ANY))
            args.append(s)
            aliases[len(args) - 1] = 1 + k
    else:
        in_specs += list(st_specs)
        args += list(states)
        n_used = len(args)
        out_specs, out_shape = [y_spec], [y_shape]
    n_args = len(args)
    steps = nseq // nsq
    kw = dict(seq_len=seq_len, nsq=nsq)

    def body(*refs):
        kern(*refs[:n_used], *refs[n_args:], is_ctx=is_ctx, **kw)

    return pl.pallas_call(
        body, grid=(steps,), in_specs=in_specs, out_specs=out_specs, out_shape=out_shape,
        input_output_aliases=aliases, scratch_shapes=scratch, compiler_params=_cparams(("arbitrary",)),
        name=name + ("_ctx" if is_ctx else "_lat"),
    )(*args)


def _hgrn_call(ub, uf, lbp, ng, sgn, states, *, is_ctx, seq_len, row0, nseq, nsq, layer, depth):
    hpi = 1 if is_ctx else LAT_HEADS_PER_ITER
    rows = nsq * seq_len
    bo = row0 // rows
    nh = HG_HEADS
    in_specs = [pl.BlockSpec((nh, rows, LANES), lambda i: (0, bo + i, 0)),
                pl.BlockSpec((nh, rows, LANES), lambda i: (1, bo + i, 0)),
                pl.BlockSpec((nh, rows, LANES), lambda i: (2, bo + i, 0)),
                pl.BlockSpec((2 * nh, rows, LANES), lambda i: (0, bo + i, 0)),
                pl.BlockSpec((2, 2 * nh, LANES), lambda i: (0, 0, 0)),
                pl.BlockSpec((nh, LANES), lambda i: (0, 0)),
                pl.BlockSpec((N_LEVEL + 1, CH, LANES), lambda i: (0, 0, 0))]
    st_spec, st_shape = _state_io((2, nh, HG_DK, LANES), nseq, nsq, layer, depth)
    return _seq_call(
        _hgrn_kernel, "hgrn", in_specs, [ub, ub, ub, uf, lbp, ng, sgn], nh, [st_spec], [st_shape], states,
        [pltpu.VMEM((hpi, rows, LANES), F32), pltpu.VMEM((2 * nsq * hpi, LANES, HG_DK), F32)],
        is_ctx, seq_len=seq_len, nseq=nseq, nsq=nsq)


def _mlstm_prep(gt_ref, bias_ref, rows, ms_s, triu):
    gates = gt_ref[rows, :] + bias_ref[...]
    g_t = gates.T
    i_rows = g_t[0:8, :]
    lf_rows = _log_sigmoid(g_t[8:16, :])
    c_rows = _cumsum_lanes(lf_rows, triu)
    is_b = lax.broadcasted_iota(jnp.int32, (8, CH), 0) >= ML_HEADS
    is_b1 = is_b[:, 0:1]
    m_in = ms_s[:, 0:1]
    c_last = c_rows[:, CH - 1:CH]
    pos = jnp.where(is_b, c_rows - lf_rows, c_rows)
    u = jnp.where(is_b, pos + i_rows, i_rows - pos)
    nu = jnp.where(is_b1, c_last + m_in, m_in)
    mu = jnp.maximum(jnp.where(is_b, _cummax_lanes(u, True), _cummax_lanes(u, False)), nu)
    m_t = jnp.where(is_b, mu - pos, pos + mu)
    mu_end = jnp.where(is_b1, mu[:, 0:1], mu[:, CH - 1:CH])
    m_new = jnp.where(is_b1, mu[:, 0:1], m_t[:, CH - 1:CH])
    x_rows = jnp.concatenate([-mu, -m_t, u, jnp.zeros_like(u)], axis=0) * LOG2E
    return dict(lhs=_sel_lhs(x_rows), u_rows=x_rows[16:24, :], nu=nu * LOG2E, mu_end=mu_end * LOG2E, m_new=m_new)


def _mlstm_main(p, q_ref, k_ref, v_ref, sel_ref, rows, dirs, cn_s, ms_s, oacc, masks, has_init, first_write):
    lhs = p["lhs"]
    scale = ML_DK ** -0.5
    ones_blk = jnp.ones((CH, LANES), BF16)
    heads = range(ML_HEADS)
    hds = [(d, h, d * ML_HEADS + h) for h in heads for d in dirs]
    q = [q_ref[h, rows, :] for h in heads]
    kf = [k_ref[h, rows, :].astype(F32) * scale for h in heads]
    vaug = [jnp.concatenate([v_ref[h, rows, :], ones_blk], axis=1) for h in heads]
    qk = [_dot_nt(q[h], kf[h].astype(BF16)) for h in heads]
    b_mu = {hd: _dot(lhs, sel_ref[hd]) for _, _, hd in hds}
    b_m = {hd: _dot(lhs, sel_ref[8 + hd]) for _, _, hd in hds}
    k_t = [kf[h].T for h in heads]
    s = {}
    for d, h, hd in hds:
        xe = jnp.concatenate([b_mu[hd], b_mu[hd]], axis=1) + p["u_rows"][hd:hd + 1, :]
        s[hd] = (jnp.where(masks[d], jnp.exp2(xe), 0.0) * qk[h]).astype(BF16)
    num = {hd: _dot(s[hd], vaug[h]) for _, h, hd in hds}
    w_end = jnp.exp2(p["u_rows"] - p["mu_end"])
    upd = {hd: _dot((k_t[h] * w_end[hd:hd + 1, :]).astype(BF16), vaug[h]) for _, h, hd in hds}
    if has_init:
        for d, h, hd in hds:
            cn = cn_s[d, h]
            w_int = jnp.exp2(p["nu"][hd:hd + 1, :] + b_mu[hd])
            num[hd] = num[hd] + jnp.concatenate([w_int, w_int], axis=1) * _dot(q[h], cn.astype(BF16))
            upd[hd] = upd[hd] + jnp.exp2(p["nu"][hd:hd + 1, :] - p["mu_end"][hd:hd + 1, :]) * cn
    for d, h, hd in hds:
        cn_s[d, h] = upd[hd]
    for h in heads:
        hsum = None
        for d in dirs:
            hd = d * ML_HEADS + h
            den = jnp.maximum(jnp.abs(num[hd][:, LANES:]), jnp.exp2(b_m[hd]))
            hout = num[hd][:, :LANES] / den
            hsum = hout if hsum is None else hsum + hout
        if first_write:
            oacc[h, rows, :] = hsum
        else:
            oacc[h, rows, :] += hsum
    m_new = jnp.broadcast_to(p["m_new"], (8, LANES))
    if len(dirs) == 2:
        ms_s[...] = m_new
    else:
        row = lax.broadcasted_iota(jnp.int32, (8, LANES), 0)
        mine = (row >= ML_HEADS) if dirs[0] == 1 else (row < ML_HEADS)
        ms_s[...] = jnp.where(mine, m_new, ms_s[...])


def _mlstm_kernel(q_ref, k_ref, v_ref, og_ref, gt_ref, bias_ref, ng_ref, sel_ref, *rest, seq_len, nsq, is_ctx):
    if is_ctx:
        y_ref, cfin_ref, nfin_ref, mfin_ref, oacc, cn_s, ms_s = rest
    else:
        c0_ref, n0_ref, m0_ref, y_ref, oacc, cn_s, ms_s = rest
    nchunk = seq_len // CH
    triu = _tri(CH, False)
    rr = lax.broadcasted_iota(jnp.int32, (CH, CH), 0)
    ss = lax.broadcasted_iota(jnp.int32, (CH, CH), 1)
    masks = (ss <= rr, ss >= rr)
    def main(p, rows, dirs, sq):
        _mlstm_main(p, q_ref, k_ref, v_ref, sel_ref, rows, dirs, cn_s.at[sq], ms_s.at[sq], oacc, masks,
                    not is_ctx, nchunk == 1)

    for sq in range(nsq):
        base = sq * seq_len
        if is_ctx:
            ms_s[sq] = jnp.zeros((2 * ML_HEADS, LANES), F32)
        else:
            ms_s[sq] = m0_ref[sq]
            for d in range(2):
                for h in range(ML_HEADS):
                    hd = d * ML_HEADS + h
                    nb = jnp.broadcast_to(n0_ref[sq, hd:hd + 1, :], (ML_DK, LANES)).T
                    cn_s[sq, d, h] = jnp.concatenate([c0_ref[sq, d, h], nb], axis=1)
            if nchunk > 1:
                oacc[:, base:base + seq_len, :] = jnp.zeros((ML_HEADS, seq_len, LANES), F32)
    if nchunk == 1:
        preps = [_mlstm_prep(gt_ref, bias_ref, _chunk_rows(0, sq * seq_len), ms_s.at[sq], triu) for sq in range(nsq)]
        for sq in range(nsq):
            main(preps[sq], _chunk_rows(0, sq * seq_len), (0, 1), sq)
    else:
        for sq in range(nsq):
            def body(ci, carry, sq=sq):
                rows_f = _chunk_rows(ci, sq * seq_len)
                rows_b = _chunk_rows(nchunk - 1 - ci, sq * seq_len)
                p_f = _mlstm_prep(gt_ref, bias_ref, rows_f, ms_s.at[sq], triu)
                p_b = _mlstm_prep(gt_ref, bias_ref, rows_b, ms_s.at[sq], triu)
                main(p_f, rows_f, (0,), sq)
                main(p_b, rows_b, (1,), sq)
                return carry
            lax.fori_loop(0, nchunk, body, 0)
    for sq in range(nsq):
        base = sq * seq_len
        for h in range(ML_HEADS):
            o = oacc[h, base:base + seq_len, :]
            y = o * lax.rsqrt(jnp.mean(o * o, -1, keepdims=True) + RMS_EPS) * ng_ref[h:h + 1, :]
            gate = _sigmoid(og_ref[h, base:base + seq_len, :].astype(F32))
            y_ref[h, base:base + seq_len, :] = (y * gate).astype(y_ref.dtype)
        if is_ctx:
            for d in range(2):
                for h in range(ML_HEADS):
                    cn = cn_s[sq, d, h]
                    cfin_ref[sq, d, h] = cn[:, :LANES]
                    nfin_ref[sq, pl.ds(d * ML_HEADS + h, 1), :] = cn[:, LANES:].T[0:1, :]
            mfin_ref[sq] = ms_s[sq]


def _mlstm_call(ub, uf, bias_row, ng, sel, states, *, is_ctx, seq_len, row0, nseq, nsq, layer, depth):
    rows = nsq * seq_len
    bo = row0 // rows
    nh = ML_HEADS
    slab = lambda k: pl.BlockSpec((nh, rows, LANES), lambda i: (k, bo + i, 0))
    in_specs = [slab(3), slab(4), slab(5), slab(6),
                pl.BlockSpec((None, rows, LANES), lambda i: (GATE_SLAB, bo + i, 0)),
                pl.BlockSpec((1, LANES), lambda i: (0, 0)),
                pl.BlockSpec((nh, LANES), lambda i: (0, 0)),
                pl.BlockSpec((N_SEL, LANES, LANES), lambda i: (0, 0, 0))]
    c_spec, c_shape = _state_io((2, nh, ML_DK, LANES), nseq, nsq, layer, depth)
    v_spec, v_shape = _state_io((2 * nh, LANES), nseq, nsq, layer, depth)
    return _seq_call(
        _mlstm_kernel, "mlstm", in_specs, [ub, ub, ub, ub, uf, bias_row, ng, sel], nh,
        [c_spec, v_spec, v_spec], [c_shape, v_shape, v_shape], states,
        [pltpu.VMEM((nh, rows, LANES), F32), pltpu.VMEM((nsq, 2, nh, ML_DK, 2 * LANES), F32),
         pltpu.VMEM((nsq, 2 * nh, LANES), F32)],
        is_ctx, seq_len=seq_len, nseq=nseq, nsq=nsq)


N_PAIR = MB_HEADS // 2
PAIRS_PER_GROUP = N_PAIR // MB_GROUPS


def _ssd_prep(gt_ref, bias_ref, nega_ref, rows, triu):
    gates = gt_ref[rows, :] + bias_ref[...]
    g_t = gates.T
    dt_rows = _softplus(g_t[16:32, :])
    la_rows = dt_rows * nega_ref[...]
    c_rows = _cumsum_lanes(la_rows, triu)
    ldt = jnp.log(dt_rows)
    is_b = lax.broadcasted_iota(jnp.int32, (16, CH), 0) >= MB_HEADS
    pos = jnp.where(is_b, c_rows - la_rows, c_rows)
    a_col = jnp.where(is_b, -pos, pos) * LOG2E
    r_row = jnp.where(is_b, pos + ldt, ldt - pos) * LOG2E
    c_last = c_rows[:, CH - 1:CH] * LOG2E
    lhs = _sel_lhs(jnp.concatenate([a_col, jnp.zeros_like(a_col)], axis=0))
    return dict(lhs=lhs, r_row=r_row, c_last=c_last)


def _ssd_main(p, xs_s, bcs_s, sel_ref, rows, dirs, hst, yacc, masks, has_init, first_write):
    lhs, r_row, c_last = p["lhs"], p["r_row"], p["c_last"]
    lane = lax.broadcasted_iota(jnp.int32, (CH, LANES), 1)
    lo = lane < MB_HEADDIM
    lo_state = lax.broadcasted_iota(jnp.int32, (LANES, LANES), 1) < MB_HEADDIM
    lo_row = lo_state[0:1, :]
    bblk = bcs_s[rows, 0:LANES]
    cblk = bcs_s[rows, LANES:2 * LANES]
    c_g, b_g, gm = [], [], []
    for grp in range(MB_GROUPS):
        in_grp = (lane >= grp * MB_DSTATE) & (lane < (grp + 1) * MB_DSTATE)
        c_g.append(jnp.where(in_grp, cblk, 0.0))
        b_g.append(jnp.where(in_grp, bblk, 0.0))
        gm.append(_dot_nt(c_g[grp].astype(BF16), b_g[grp].astype(BF16)))
    xh = []
    for j in range(N_PAIR):
        xpair = xs_s[rows, j * LANES:(j + 1) * LANES]
        xh.append((jnp.where(lo, xpair, 0.0).astype(BF16), jnp.where(lo, 0.0, xpair).astype(BF16)))
    items = [(d, j, half, d * MB_HEADS + 2 * j + half) for j in range(N_PAIR) for d in dirs for half in range(2)]
    b_a = {hd: _dot(lhs, sel_ref[hd]) for _, _, _, hd in items}
    b_t = [b_g[grp].T for grp in range(MB_GROUPS)]
    is_f = lax.broadcasted_iota(jnp.int32, (2 * MB_HEADS, 1), 0) < MB_HEADS
    w_end = jnp.exp2(r_row + jnp.where(is_f, c_last, 0.0))
    m, b_out, c_in = {}, {}, {}
    for d, j, half, hd in items:
        grp = j // PAIRS_PER_GROUP
        cl = c_last[hd:hd + 1, :]
        xe = jnp.concatenate([b_a[hd], b_a[hd]], axis=1) + r_row[hd:hd + 1, :]
        m[hd] = (jnp.where(masks[d], jnp.exp2(xe), 0.0) * gm[grp]).astype(BF16)
        b_out[hd] = (b_t[grp] * w_end[hd:hd + 1, :]).astype(BF16)
        if has_init:
            c_in[hd] = (c_g[grp] * jnp.exp2(b_a[hd] + cl if d else b_a[hd])).astype(BF16)
    yy = {hd: _dot(m[hd], xh[j][half]) for _, j, half, hd in items}
    uu = {hd: _dot(b_out[hd], xh[j][half]) for _, j, half, hd in items}
    for j in range(N_PAIR):
        ysum = None
        for d in dirs:
            hd0, hd1 = d * MB_HEADS + 2 * j, d * MB_HEADS + 2 * j + 1
            y = yy[hd0] + yy[hd1]
            upd = uu[hd0] + uu[hd1]
            if has_init:
                ht = hst[d, j]
                y = y + _dot(c_in[hd0], jnp.where(lo_state, ht, 0.0).astype(BF16))
                y = y + _dot(c_in[hd1], jnp.where(lo_state, 0.0, ht).astype(BF16))
                decay = jnp.where(lo_row, jnp.exp2(c_last[hd0:hd0 + 1, :]), jnp.exp2(c_last[hd1:hd1 + 1, :]))
                upd = upd + ht * decay
            hst[d, j] = upd
            ysum = y if ysum is None else ysum + y
        if first_write:
            yacc[rows, j * LANES:(j + 1) * LANES] = ysum
        else:
            yacc[rows, j * LANES:(j + 1) * LANES] += ysum


def _ssd_kernel(x_ref, bc_ref, z_ref, gt_ref, cw_ref, bias_ref, nega_ref, dskip_ref, ng_ref, sel_ref, *rest,
                seq_len, nsq, is_ctx):
    if is_ctx:
        y_ref, hfin_ref, xs_s, bcs_s, yacc, hst = rest
    else:
        h0_ref, y_ref, xs_s, bcs_s, yacc, hst = rest
    nchunk = seq_len // CH
    triu = _tri(CH, False)
    rr = lax.broadcasted_iota(jnp.int32, (CH, CH), 0)
    ss = lax.broadcasted_iota(jnp.int32, (CH, CH), 1)
    masks = (ss <= rr, ss >= rr)

    def conv(v, lo_col):
        row = lax.broadcasted_iota(jnp.int32, v.shape, 0)
        prev = jnp.where(row == 0, 0.0, pltpu.roll(v, 1, 0))
        nxt = jnp.where(row == seq_len - 1, 0.0, pltpu.roll(v, seq_len - 1, 0))
        cs = slice(lo_col, lo_col + v.shape[1])
        return _silu(cw_ref[0:1, cs] * prev + cw_ref[1:2, cs] * v + cw_ref[2:3, cs] * nxt + cw_ref[3:4, cs])

    for sq in range(nsq):
        base = sq * seq_len
        seq_rows = slice(base, base + seq_len)
        for k in range(MIX_W // LANES):
            xs_s[seq_rows, k * LANES:(k + 1) * LANES] = conv(x_ref[k, seq_rows, :], k * LANES)
        for k in range(2):
            bcs_s[seq_rows, k * LANES:(k + 1) * LANES] = conv(bc_ref[k, seq_rows, :], MIX_W + k * LANES)
        if not is_ctx:
            hst[sq] = h0_ref[sq]
            if nchunk > 1:
                yacc[seq_rows, :] = jnp.zeros((seq_len, MIX_W), F32)

    def prep(rows):
        return _ssd_prep(gt_ref, bias_ref, nega_ref, rows, triu)

    def main(p, rows, dirs, sq):
        _ssd_main(p, xs_s, bcs_s, sel_ref, rows, dirs, hst.at[sq], yacc, masks, not is_ctx, nchunk == 1)

    if nchunk == 1:
        preps = [prep(_chunk_rows(0, sq * seq_len)) for sq in range(nsq)]
        for sq in range(nsq):
            main(preps[sq], _chunk_rows(0, sq * seq_len), (0, 1), sq)
    else:
        for sq in range(nsq):
            def body(ci, carry, sq=sq):
                rows_f = _chunk_rows(ci, sq * seq_len)
                rows_b = _chunk_rows(nchunk - 1 - ci, sq * seq_len)
                p_f, p_b = prep(rows_f), prep(rows_b)
                main(p_f, rows_f, (0,), sq)
                main(p_b, rows_b, (1,), sq)
                return carry
            lax.fori_loop(0, nchunk, body, 0)

    for sq in range(nsq):
        base = sq * seq_len
        seq_rows = slice(base, base + seq_len)
        z = jnp.concatenate([z_ref[k, seq_rows, :] for k in range(MIX_W // LANES)], axis=1).astype(F32)
        y = (yacc[seq_rows, :] + dskip_ref[...] * xs_s[seq_rows, :]) * _silu(z)
        y = (y * lax.rsqrt(jnp.mean(y * y, -1, keepdims=True) + RMS_EPS) * ng_ref[...]).astype(y_ref.dtype)
        for k in range(MIX_W // LANES):
            y_ref[k, seq_rows, :] = y[:, k * LANES:(k + 1) * LANES]
        if is_ctx:
            hfin_ref[sq] = hst[sq]


def _ssd_call(ub, uf, cw, bias_row, nega_rows, dskip, ng, sel, states, *, is_ctx, seq_len, row0, nseq, nsq, layer,
              depth):
    rows = nsq * seq_len
    bo = row0 // rows
    ns = MIX_W // LANES
    row_spec = lambda w: pl.BlockSpec((1, w), lambda i: (0, 0))
    in_specs = [pl.BlockSpec((ns, rows, LANES), lambda i: (2, bo + i, 0)),
                pl.BlockSpec((2, rows, LANES), lambda i: (6, bo + i, 0)),
                pl.BlockSpec((ns, rows, LANES), lambda i: (7, bo + i, 0)),
                pl.BlockSpec((None, rows, LANES), lambda i: (GATE_SLAB, bo + i, 0)),
                pl.BlockSpec((8, MB_XBC), lambda i: (0, 0)),
                row_spec(LANES),
                pl.BlockSpec((2 * MB_HEADS, CH), lambda i: (0, 0)),
                row_spec(MIX_W), row_spec(MIX_W),
                pl.BlockSpec((N_SEL, LANES, LANES), lambda i: (0, 0, 0))]
    st_spec, st_shape = _state_io((2, N_PAIR, LANES, LANES), nseq, nsq, layer, depth)
    return _seq_call(
        _ssd_kernel, "ssd", in_specs, [uf, uf, ub, uf, cw, bias_row, nega_rows, dskip, ng, sel], ns,
        [st_spec], [st_shape], states,
        [pltpu.VMEM((rows, MIX_W), F32), pltpu.VMEM((rows, 2 * LANES), F32), pltpu.VMEM((rows, MIX_W), F32),
         pltpu.VMEM((nsq, 2, N_PAIR, LANES, LANES), F32)],
        is_ctx, seq_len=seq_len, nseq=nseq, nsq=nsq)


def _ssd_state_to_pairs(s):
    bsz = s.shape[0]
    st = jnp.swapaxes(s, -1, -2).reshape(bsz, 2, N_PAIR, 2, MB_DSTATE, MB_HEADDIM)
    st = jnp.moveaxis(st, 3, 4).reshape(bsz, 2, N_PAIR, MB_DSTATE, 2 * MB_HEADDIM)
    zero = jnp.zeros_like(st)
    grp = (jnp.arange(N_PAIR) // PAIRS_PER_GROUP).reshape(1, 1, N_PAIR, 1, 1)
    return jnp.concatenate([jnp.where(grp == 0, st, zero), jnp.where(grp == 1, st, zero)], axis=3)


def _ssd_pairs_to_state(hp):
    bsz = hp.shape[0]
    halves = hp.reshape(bsz, 2, N_PAIR, MB_GROUPS, MB_DSTATE, 2 * MB_HEADDIM)
    grp = (jnp.arange(N_PAIR) // PAIRS_PER_GROUP).reshape(1, 1, N_PAIR, 1, 1)
    st = jnp.where(grp == 0, halves[:, :, :, 0], halves[:, :, :, 1])
    st = st.reshape(bsz, 2, N_PAIR, MB_DSTATE, 2, MB_HEADDIM)
    st = jnp.moveaxis(st, 4, 3).reshape(bsz, 2, MB_HEADS, MB_DSTATE, MB_HEADDIM)
    return jnp.swapaxes(st, -1, -2)


def _grid_pos_embed(n_tok, d_model):
    rows = n_tok // GRID_W
    r, cidx = jnp.meshgrid(jnp.arange(rows, dtype=F32), jnp.arange(GRID_W, dtype=F32), indexing='ij')
    quarter = d_model // 4
    freq = jnp.exp(-math.log(10000.0) * jnp.arange(quarter, dtype=F32) / quarter)
    ar = r.reshape(-1, 1) * freq
    ac = cidx.reshape(-1, 1) * freq
    return jnp.concatenate([jnp.sin(ar), jnp.cos(ar), jnp.sin(ac), jnp.cos(ac)], axis=-1)


def _gate_row(pieces):
    v = jnp.concatenate([p.reshape(-1).astype(F32) for p in pieces])
    return jnp.concatenate([v, v, v, jnp.zeros((LANES - 3 * N_SEL,), F32)]).reshape(1, LANES)


def kernel(x_prompt, x_sample, state_hgrn, state_mlstm_C, state_mlstm_n, state_mlstm_m, state_ssd, c, c_ctx,
           w_mod, b_mod, ln_g, ln_b, ffn_w_gu, ffn_w_down, w_in, hg_lb, hg_norm_g, ml_gate_b, ml_norm_g,
           mb_conv_w, mb_conv_b, mb_dt_bias, mb_a_log, mb_d, mb_norm_g, w_branch, w_out):
    bsz, seq, d = x_prompt.shape
    dbsz, dseq, _ = x_sample.shape
    depth = w_mod.shape[0]
    t_ctx = bsz * seq
    n_ctx_tiles = t_ctx // ROW_TILE
    assert dseq == ROW_TILE and t_ctx % ROW_TILE == 0 and seq == CH and dseq % CH == 0

    xs0 = x_sample + _grid_pos_embed(dseq, d).astype(x_sample.dtype)[None]
    x = jnp.concatenate([x_prompt.reshape(t_ctx, d), xs0.reshape(dbsz * dseq, d)], axis=0)

    cv = jnp.concatenate([c_ctx[None]] * MOD_LAT_ROW + [c, jnp.zeros((8 - MOD_LAT_ROW - dbsz, d), F32)], axis=0)
    mod4 = _mod_call(cv, w_mod, b_mod).reshape(depth, 8, 1, N_MOD * d)

    w_branch_b = w_branch.astype(BF16)
    w_out_b = w_out.astype(BF16)

    def regroup(width, pieces):
        out = jnp.zeros((depth, d, width), BF16)
        at = 0
        for lo, hi in pieces:
            out = lax.dynamic_update_slice(out, w_in[:, :, lo:hi].astype(BF16), (0, 0, at))
            at += hi - lo
        return out, at

    w_a, _ = regroup(UB_SLABS * LANES, [(0, 1536), (2560, 4608), (4624, 5136), (5920, 8992)])
    gate_cols = [(4608, 4624), (5904, 5920)]
    w_b, _ = regroup(UF_SLABS * LANES, [(1536, 2560), (5136, 5904)] + gate_cols * 3)
    ln_g4 = ln_g.reshape(depth, 3, 1, d)
    ln_b4 = ln_b.reshape(depth, 3, 1, d)

    lbs = jnp.cumsum(jax.nn.softmax(hg_lb.astype(F32), axis=0), axis=0)
    lbs = (lbs - lbs[0]).reshape(depth, 2 * HG_HEADS, HG_DK)
    lbp = jnp.stack([lbs, 1.0 - lbs], axis=1)
    cw = jnp.concatenate([mb_conv_w, mb_conv_b[:, None, :], jnp.zeros((depth, 4, MB_XBC), F32)], axis=1)
    sel = _selectors()

    lat_h0 = _ssd_state_to_pairs(state_ssd.reshape((dbsz * depth,) + state_ssd.shape[2:])).reshape(
        (dbsz, depth, 2, N_PAIR, LANES, LANES))
    lat_n0 = state_mlstm_n.reshape(dbsz, depth, 2 * ML_HEADS, ML_DK)
    lat_m0 = jnp.broadcast_to(state_mlstm_m.reshape(dbsz, depth, 2 * ML_HEADS, 1), (dbsz, depth, 2 * ML_HEADS, LANES))

    zeros = lambda *s: jnp.zeros((bsz, depth) + s, F32)
    st_hg = [zeros(2, HG_HEADS, HG_DK, LANES)]
    st_ml = [zeros(2, ML_HEADS, ML_DK, LANES), zeros(2 * ML_HEADS, LANES), zeros(2 * ML_HEADS, LANES)]
    st_ss = [zeros(2, N_PAIR, LANES, LANES)]
    t_idx = jnp.arange(CH).reshape(1, CH, 1)
    widths = (HALF >> jnp.arange(N_LEVEL + 1)).reshape(N_LEVEL + 1, 1, 1)
    sgn = jnp.broadcast_to(jnp.where((t_idx & widths) != 0, 1.0, -1.0), (N_LEVEL + 1, CH, LANES)).astype(F32)
    ctx = dict(is_ctx=True, seq_len=seq, row0=0, nseq=bsz, nsq=CTX_SEQS_PER_STEP, depth=depth)
    lat = dict(is_ctx=False, seq_len=dseq, row0=t_ctx, nseq=dbsz, nsq=1, depth=depth)
    for l in range(depth):
        x = _ffn_call(x, mod4, ffn_w_gu, ffn_w_down, ln_g4, ln_b4, l, 0, n_ctx_tiles)
        ub = _inproj_call(x, mod4, w_a, l, INPROJ_TILE_A, BF16, n_ctx_tiles, "inproj_a")
        uf = _inproj_call(x, mod4, w_b, l, INPROJ_TILE_B, F32, n_ctx_tiles, "inproj_b")

        ng_h = hg_norm_g[l].reshape(HG_HEADS, LANES)
        y_hg, *st_hg = _hgrn_call(ub, uf, lbp[l], ng_h, sgn, st_hg, layer=l, **ctx)
        y_hg_lat, = _hgrn_call(ub, uf, lbp[l], ng_h, sgn, [state_hgrn], layer=l, **lat)

        gate_bias = _gate_row([ml_gate_b[l, 0], ml_gate_b[l, 1], mb_dt_bias[l]])
        ng_m = ml_norm_g[l].reshape(ML_HEADS, LANES)
        y_ml, *st_ml = _mlstm_call(ub, uf, gate_bias, ng_m, sel, st_ml, layer=l, **ctx)
        y_ml_lat, = _mlstm_call(ub, uf, gate_bias, ng_m, sel, [state_mlstm_C, lat_n0, lat_m0], layer=l, **lat)

        nega = jnp.broadcast_to(-jnp.exp(mb_a_log[l].astype(F32)).reshape(2 * MB_HEADS, 1), (2 * MB_HEADS, CH))
        dskip = jnp.repeat(mb_d[l], MB_HEADDIM).reshape(1, MIX_W)
        ng_s = mb_norm_g[l].reshape(1, MIX_W)
        y_mb, *st_ss = _ssd_call(ub, uf, cw[l], gate_bias, nega, dskip, ng_s, sel, st_ss, layer=l, **ctx)
        y_mb_lat, = _ssd_call(ub, uf, cw[l], gate_bias, nega, dskip, ng_s, sel, [lat_h0], layer=l, **lat)

        x = _mixout_call(x, mod4, (y_hg, y_ml, y_mb), (y_hg_lat, y_ml_lat, y_mb_lat), ub, w_branch_b, w_out_b,
                         ln_g4, ln_b4, l, n_ctx_tiles)
        x = _ffn_call(x, mod4, ffn_w_gu, ffn_w_down, ln_g4, ln_b4, l, 1, n_ctx_tiles)

    y_prompt = x[:t_ctx].reshape(bsz, seq, d)
    y_sample = x[t_ctx:].reshape(dbsz, dseq, d)
    c_fin, n_fin, m_fin = st_ml
    h_fin = _ssd_pairs_to_state(st_ss[0].reshape((bsz * depth,) + st_ss[0].shape[2:]))
    return (y_prompt, y_sample, st_hg[0], c_fin, n_fin.reshape(bsz, depth, 2, ML_HEADS, ML_DK),
            m_fin[:, :, :, 0].reshape(bsz, depth, 2, ML_HEADS),
            h_fin.reshape((bsz, depth) + h_fin.shape[1:]))
```

```python
import functools
import math

import jax
import jax.numpy as jnp
from jax import lax
from jax.experimental import pallas as pl
from jax.experimental.pallas import tpu as pltpu

F32 = jnp.float32
BF16 = jnp.bfloat16

D_MODEL = 1024
DEPTH = 4
GRID_W = 64
MIX_W = 512
HG_HEADS = 4
HG_DK = 128
ML_HEADS = 4
ML_DK = 128
MB_HEADS = 8
MB_HEADDIM = 64
MB_GROUPS = 2
MB_DSTATE = 64
MB_XBC = MIX_W + 2 * MB_GROUPS * MB_DSTATE
D_FF = 2816
N_MOD = 9
ALPHA = (2 * DEPTH) ** 0.25
LN_EPS = 1e-5
RMS_EPS = 1e-6
LOG2E = 1.4426950408889634

LANES = 128
CH = 256
HALF = CH // 2
ROW_TILE = 1024
FFN_ROWS = 2048
MOD_LAT_ROW = 2
FF_TILE = 256
INPROJ_TILE_A = 1792
INPROJ_TILE_B = 1920
LAT_HEADS_PER_ITER = 2
CTX_SEQS_PER_STEP_SCALAR = 4
LAT_SEQS_PER_STEP_SCALAR = 2
CTX_SEQS_PER_STEP = 4
VMEM_LIMIT = 56 * 1024 * 1024

UB_SLABS = 56
UF_SLABS = 15
GATE_SLAB = 14
N_SEL = 32
LOG2_TINY = -150.0


def _dot(a, b):
    return jnp.dot(a, b, preferred_element_type=F32)


def _dot_nt(a, b):
    return lax.dot_general(a, b, (((1,), (1,)), ((), ())), preferred_element_type=F32)


def _dot_tn(a, b):
    return lax.dot_general(a, b, (((0,), (0,)), ((), ())), preferred_element_type=F32)


def _sigmoid(x):
    return 1.0 / (1.0 + jnp.exp(-x))


def _silu(x):
    return x * _sigmoid(x)


def _log_sigmoid(x):
    return jnp.minimum(x, 0.0) - jnp.log(1.0 + jnp.exp(-jnp.abs(x)))


def _softplus(x):
    return jnp.maximum(x, 0.0) + jnp.log(1.0 + jnp.exp(-jnp.abs(x)))


def _neg_abs(x):
    return -jnp.abs(x)


def _ln(z, g, b):
    mu = jnp.mean(z, -1, keepdims=True)
    d = z - mu
    var = jnp.mean(d * d, -1, keepdims=True)
    return d * lax.rsqrt(var + LN_EPS) * g + b


def _split3(x):
    h = x.astype(BF16)
    r = x - h.astype(F32)
    m = r.astype(BF16)
    l = (r - m.astype(F32)).astype(BF16)
    return h, m, l


def _cumsum_rows(tril, x):
    h, m, l = _split3(x)
    return _dot(tril, h) + _dot(tril, m) + _dot(tril, l)


def _cumsum_lanes(x, triu):
    h, m, l = _split3(x)
    return _dot(h, triu) + _dot(m, triu) + _dot(l, triu)


def _cummax_lanes(x, reverse):
    n = x.shape[1]
    lane = lax.broadcasted_iota(jnp.int32, x.shape, 1)
    k = 1
    while k < n:
        if reverse:
            sh = jnp.where(lane < n - k, pltpu.roll(x, n - k, 1), -jnp.inf)
        else:
            sh = jnp.where(lane >= k, pltpu.roll(x, k, 1), -jnp.inf)
        x = jnp.maximum(x, sh)
        k *= 2
    return x


def _tri(c, lower):
    r = lax.broadcasted_iota(jnp.int32, (c, c), 0)
    s = lax.broadcasted_iota(jnp.int32, (c, c), 1)
    return jnp.where((s <= r) if lower else (r <= s), 1.0, 0.0).astype(BF16)


def _sel_lhs(x_rows):
    h = x_rows.astype(BF16).astype(F32)
    r = x_rows - h
    m = r.astype(BF16).astype(F32)
    rows = jnp.concatenate([h, m, r - m, jnp.ones_like(x_rows)], axis=0)
    return rows.T.astype(BF16)


def _selectors():
    r = jnp.arange(LANES)
    ch = jnp.arange(N_SEL)
    hit = ((r[None, :] % N_SEL) == ch[:, None]) & (r[None, :] < 3 * N_SEL)
    return jnp.broadcast_to(hit[:, :, None], (N_SEL, LANES, LANES)).astype(BF16)


def _cparams(sem):
    return pltpu.CompilerParams(dimension_semantics=sem, vmem_limit_bytes=VMEM_LIMIT)


def _chunk_rows(ci, base=0):
    r0 = ci * CH + base
    if not isinstance(r0, int):
        r0 = pl.multiple_of(r0, CH)
    return pl.ds(r0, CH)


def _state_io(shape_tail, nseq, nsq, layer, depth):
    nz = (0,) * len(shape_tail)
    spec = pl.BlockSpec((nsq, None) + shape_tail, lambda i: (i, layer) + nz)
    shape = jax.ShapeDtypeStruct((nseq, depth) + shape_tail, F32)
    return spec, shape


def _chunk_loop(nchunk, run):
    if nchunk == 1:
        run(0, (0, 1))
    else:
        def body(ci, carry):
            run(ci, (0,))
            run(nchunk - 1 - ci, (1,))
            return carry
        lax.fori_loop(0, nchunk, body, 0)


def _mod_kernel(c_ref, w_ref, b_ref, o_ref):
    a = _silu(c_ref[...]).astype(BF16)
    o_ref[...] = _dot(a, w_ref[...].astype(BF16)) + b_ref[...]


def _mod_call(cv, w_mod, b_mod):
    depth, d, e = w_mod.shape
    tn = 1024
    return pl.pallas_call(
        _mod_kernel,
        grid=(depth, e // tn),
        in_specs=[pl.BlockSpec((8, d), lambda l, j: (0, 0)),
                  pl.BlockSpec((None, d, tn), lambda l, j: (l, 0, j)),
                  pl.BlockSpec((None, 1, tn), lambda l, j: (l, 0, j))],
        out_specs=pl.BlockSpec((None, 8, tn), lambda l, j: (l, 0, j)),
        out_shape=jax.ShapeDtypeStruct((depth, 8, e), F32),
        compiler_params=_cparams(("arbitrary", "arbitrary")),
        name="mod",
    )(cv, w_mod, b_mod.reshape(depth, 1, e))


def _mod_row(i, n_ctx_tiles):
    return jnp.maximum(i - n_ctx_tiles + MOD_LAT_ROW, 0)


def _ffn_kernel(x_ref, mod_ref, wa_ref, wb_ref, wd_ref, g_ref, b_ref, o_ref, xm_s, *, nf):
    j = pl.program_id(1)
    d = x_ref.shape[1]
    groups = [(r, slice(r * ROW_TILE, (r + 1) * ROW_TILE)) for r in range(FFN_ROWS // ROW_TILE)]

    def ffn(xm):
        wa, wb, wd = wa_ref[...].astype(BF16), wb_ref[...].astype(BF16), wd_ref[...].astype(BF16)
        return _dot((_silu(_dot(xm, wa)) * _dot(xm, wb)).astype(BF16), wd)

    @pl.when(j == 0)
    def _():
        for r, rows in groups:
            sh = mod_ref[r:r + 1, 0:d]
            sc = mod_ref[r:r + 1, d:2 * d]
            xm = (x_ref[rows, :] * (1.0 + sc) + sh).astype(BF16)
            xm_s[rows, :] = xm
            o_ref[rows, :] = ffn(xm)

    @pl.when(jnp.logical_and(j > 0, j < nf - 1))
    def _():
        o_ref[...] += ffn(xm_s[...])

    @pl.when(j == nf - 1)
    def _():
        for r, rows in groups:
            gate = mod_ref[r:r + 1, 2 * d:3 * d]
            z = ALPHA * x_ref[rows, :] + 0.5 * gate * (o_ref[rows, :] + ffn(xm_s[rows, :]))
            o_ref[rows, :] = _ln(z, g_ref[...], b_ref[...])


def _ffn_call(x, mod4, w_gu, w_down, ln_g, ln_b, layer, which, n_ctx_tiles):
    t, d = x.shape
    nf = D_FF // FF_TILE
    per = FFN_ROWS // ROW_TILE
    assert nf >= 3 and nf * FF_TILE == D_FF and n_ctx_tiles % per == 0 and MOD_LAT_ROW % per == 0
    sub = 0 if which == 0 else 2
    mod_pairs = mod4.reshape(mod4.shape[0], mod4.shape[1] // per, per, mod4.shape[3])
    first_lat = n_ctx_tiles // per
    mod_blk = lambda i, j: (layer, jnp.maximum(i - first_lat + MOD_LAT_ROW // per, 0), 0, sub)
    return pl.pallas_call(
        functools.partial(_ffn_kernel, nf=nf),
        grid=(t // FFN_ROWS, nf),
        in_specs=[pl.BlockSpec((FFN_ROWS, d), lambda i, j: (i, 0)),
                  pl.BlockSpec((None, None, per, 3 * d), mod_blk),
                  pl.BlockSpec((None, None, d, FF_TILE), lambda i, j: (layer, which, 0, j)),
                  pl.BlockSpec((None, None, d, FF_TILE), lambda i, j: (layer, which, 0, j + nf)),
                  pl.BlockSpec((None, None, FF_TILE, d), lambda i, j: (layer, which, j, 0)),
                  pl.BlockSpec((None, None, 1, d), lambda i, j: (layer, sub, 0, 0)),
                  pl.BlockSpec((None, None, 1, d), lambda i, j: (layer, sub, 0, 0))],
        out_specs=pl.BlockSpec((FFN_ROWS, d), lambda i, j: (i, 0)),
        out_shape=jax.ShapeDtypeStruct((t, d), F32),
        scratch_shapes=[pltpu.VMEM((FFN_ROWS, d), BF16)],
        compiler_params=_cparams(("arbitrary", "arbitrary")),
        name=f"ffn{which}",
    )(x, mod_pairs, w_gu, w_gu, w_down, ln_g, ln_b)


def _inproj_kernel(x_ref, mod_ref, w_ref, o_ref, xm_s):
    d = x_ref.shape[1]

    @pl.when(pl.program_id(1) == 0)
    def _():
        sh = mod_ref[:, 0:d]
        sc = mod_ref[:, d:2 * d]
        xm_s[...] = (x_ref[...] * (1.0 + sc) + sh).astype(BF16)

    res = _dot(xm_s[...], w_ref[...]).astype(o_ref.dtype)
    for k in range(o_ref.shape[0]):
        o_ref[k] = res[:, k * LANES:(k + 1) * LANES]


def _inproj_call(x, mod4, w, layer, tn, out_dtype, n_ctx_tiles, name):
    t, d = x.shape
    n = w.shape[2]
    return pl.pallas_call(
        _inproj_kernel,
        grid=(t // ROW_TILE, n // tn),
        in_specs=[pl.BlockSpec((ROW_TILE, d), lambda i, j: (i, 0)),
                  pl.BlockSpec((None, None, 1, 3 * d), lambda i, j: (layer, _mod_row(i, n_ctx_tiles), 0, 1)),
                  pl.BlockSpec((None, d, tn), lambda i, j: (layer, 0, j))],
        out_specs=pl.BlockSpec((tn // LANES, ROW_TILE, LANES), lambda i, j: (j, i, 0)),
        out_shape=jax.ShapeDtypeStruct((n // LANES, t, LANES), out_dtype),
        scratch_shapes=[pltpu.VMEM((ROW_TILE, d), BF16)],
        compiler_params=_cparams(("arbitrary", "arbitrary")),
        name=name,
    )(x, mod4, w)


def _slabs(ref):
    return jnp.concatenate([ref[k] for k in range(ref.shape[0])], axis=1)


def _mixout_kernel(x_ref, mod_ref, yhc_ref, ymc_ref, ysc_ref, yhl_ref, yml_ref, ysl_ref, g0_ref, g1_ref, g2_ref,
                   wb_ref, wo_ref, lg_ref, lb_ref, o_ref, *, n_ctx_steps):
    d = x_ref.shape[1]
    is_lat = pl.program_id(0) >= n_ctx_steps

    def branch(c_ref, l_ref):
        return jnp.where(is_lat, _slabs(l_ref), _slabs(c_ref))

    p = _sigmoid(_slabs(g0_ref).astype(F32)) * _dot(branch(yhc_ref, yhl_ref), wb_ref[0])
    p += _sigmoid(_slabs(g1_ref).astype(F32)) * _dot(branch(ymc_ref, yml_ref), wb_ref[1])
    p += _sigmoid(_slabs(g2_ref).astype(F32)) * _dot(branch(ysc_ref, ysl_ref), wb_ref[2])
    y = _dot(p.astype(BF16), wo_ref[...])
    gate = mod_ref[:, 2 * d:3 * d]
    z = ALPHA * x_ref[...] + gate * y
    o_ref[...] = _ln(z, lg_ref[...], lb_ref[...])


def _mixout_call(x, mod4, y_ctx, y_lat, ub, w_branch, w_out, ln_g, ln_b, layer, n_ctx_tiles):
    t, d = x.shape
    tm = 512
    per = ROW_TILE // tm
    ns = MIX_W // LANES
    ng = d // LANES
    g0 = (UB_SLABS - 3 * ng) // ng
    n_ctx_steps = n_ctx_tiles * per
    n_lat_steps = t // tm - n_ctx_steps
    yc_spec = pl.BlockSpec((ns, tm, LANES), lambda i: (0, jnp.minimum(i, n_ctx_steps - 1), 0))
    yl_spec = pl.BlockSpec((ns, tm, LANES), lambda i: (0, jnp.clip(i - n_ctx_steps, 0, n_lat_steps - 1), 0))
    return pl.pallas_call(
        functools.partial(_mixout_kernel, n_ctx_steps=n_ctx_steps),
        grid=(t // tm,),
        in_specs=[pl.BlockSpec((tm, d), lambda i: (i, 0)),
                  pl.BlockSpec((None, None, 1, 3 * d), lambda i: (layer, _mod_row(i // per, n_ctx_tiles), 0, 1)),
                  yc_spec, yc_spec, yc_spec, yl_spec, yl_spec, yl_spec,
                  pl.BlockSpec((ng, tm, LANES), lambda i: (g0, i, 0)),
                  pl.BlockSpec((ng, tm, LANES), lambda i: (g0 + 1, i, 0)),
                  pl.BlockSpec((ng, tm, LANES), lambda i: (g0 + 2, i, 0)),
                  pl.BlockSpec((None, 3, MIX_W, d), lambda i: (layer, 0, 0, 0)),
                  pl.BlockSpec((None, d, d), lambda i: (layer, 0, 0)),
                  pl.BlockSpec((None, None, 1, d), lambda i: (layer, 1, 0, 0)),
                  pl.BlockSpec((None, None, 1, d), lambda i: (layer, 1, 0, 0))],
        out_specs=pl.BlockSpec((tm, d), lambda i: (i, 0)),
        out_shape=jax.ShapeDtypeStruct((t, d), F32),
        compiler_params=_cparams(("arbitrary",)),
        name="mixout",
    )(x, mod4, *y_ctx, *y_lat, ub, ub, ub, w_branch, w_out, ln_g, ln_b)


N_LEVEL = int(math.log2(HALF))


def _block_ref_rows(c, w):
    n_rows = c.shape[0]
    if 2 * w == n_rows:
        return jnp.broadcast_to(c[w - 1:w, :], c.shape)
    if w >= 4:
        n = n_rows // (2 * w)
        c3 = c.reshape(n, 2 * w, LANES)
        return jnp.broadcast_to(c3[:, w - 1:w, :], c3.shape).reshape(n_rows, LANES)
    c8 = c.reshape(n_rows // 8, 8, LANES)
    sub = lax.broadcasted_iota(jnp.int32, c8.shape, 1)
    if w == 2:
        r = jnp.where(sub < 4, c8[:, 1:2, :], c8[:, 5:6, :])
    else:
        r = jnp.where(sub < 2, c8[:, 0:1, :],
                      jnp.where(sub < 4, c8[:, 2:3, :], jnp.where(sub < 6, c8[:, 4:5, :], c8[:, 6:7, :])))
    return r.reshape(n_rows, LANES)


def _level_ids():
    r = lax.broadcasted_iota(jnp.int32, (HALF, HALF), 0)
    s = lax.broadcasted_iota(jnp.int32, (HALF, HALF), 1)
    lv = 31 - lax.clz(r ^ s)
    lv = jnp.where(r == s, N_LEVEL, lv)
    return jnp.where(r >= s, lv, -1), jnp.where(r <= s, lv, -1)


def _hgrn_gates(q, fpre, lb, onemlb, tril, backward):
    u = jnp.exp2(_neg_abs(fpre) * LOG2E)
    r = 1.0 / (1.0 + u)
    pos_side = fpre >= 0.0
    kk = onemlb * (jnp.where(pos_side, u, 1.0) * r)
    f = lb + onemlb * (jnp.where(pos_side, 1.0, u) * r)
    lf2 = jnp.maximum(jnp.log2(f), LOG2_TINY)
    cum = _cumsum_rows(tril, lf2)
    pos = (cum - lf2) if backward else cum
    return dict(q=q, kk=kk, qb=q.astype(BF16), kb=kk.astype(BF16), cum=cum, pos=pos, f=f)


def _hgrn_levels(p, lvl, sgn_ref, backward):
    qb, kb, cum, pos = p["qb"], p["kb"], p["cum"], p["pos"]

    def sides(w):
        if w == 1:
            odd = (lax.broadcasted_iota(jnp.int32, (CH, LANES), 0) & 1) == 1
            e = (jnp.where(odd, 1.0, p["f"]) if backward else jnp.where(odd, p["f"], 1.0)).astype(BF16)
            return qb * e, kb * e
        e = jnp.exp2((pos - _block_ref_rows(cum, w)) * sgn_ref[N_LEVEL - int(math.log2(w))]).astype(BF16)
        return qb * e, kb * e

    zq, zk = sides(HALF)
    if backward:
        cross = _dot_nt(zq[0:HALF], zk[HALF:CH])
    else:
        cross = _dot_nt(zq[HALF:CH], zk[0:HALF])
    halves = (slice(0, HALF), slice(HALF, CH))
    prods = [_dot_nt(qb[rows], kb[rows]) for rows in halves]
    zq, zk = sides(HALF // 2)
    diag = [jnp.where(lvl == N_LEVEL, prods[b], 0.0) for b in range(2)]
    w = HALF // 2
    while w >= 1:
        prods = [_dot_nt(zq[rows], zk[rows]) for rows in halves]
        if w > 1:
            zq, zk = sides(w // 2)
        lv = int(math.log2(w))
        diag = [jnp.where(lvl == lv, prods[b], diag[b]) for b in range(2)]
        w //= 2
    return diag[0].astype(BF16), diag[1].astype(BF16), cross.astype(BF16)


def _hgrn_finish(p, att, v, s_t, backward, has_init):
    q, kk, cum, pos = p["q"], p["kk"], p["cum"], p["pos"]
    d0, d1, cr = att
    if backward:
        o_lo = _dot(jnp.concatenate([d0, cr], axis=1), v)
        o_hi = _dot(d1, v[HALF:CH])
    else:
        o_lo = _dot(d0, v[0:HALF])
        o_hi = _dot(jnp.concatenate([cr, d1], axis=1), v)
    o = jnp.concatenate([o_lo, o_hi], axis=0)
    c_last = cum[CH - 1:CH, :]
    k_out = kk * jnp.exp2((pos) if backward else (c_last - cum))
    s_new = _dot_tn(v, k_out.astype(BF16))
    if has_init:
        q_in = q * jnp.exp2((c_last - pos) if backward else cum)
        o = o + _dot_nt(q_in.astype(BF16), s_t.astype(BF16))
        s_new = s_new + s_t * jnp.exp2(c_last)
    return o, s_new


def _hgrn_kernel(q_ref, v_ref, g_ref, f_ref, lbp_ref, ng_ref, sgn_ref, *rest, seq_len, nsq, is_ctx):
    if is_ctx:
        y_ref, sfin_ref, oacc, st = rest
    else:
        s0_ref, y_ref, oacc, st = rest
    nchunk = seq_len // CH
    hpi = oacc.shape[0]
    tril = _tri(CH, True)
    lvl_f, lvl_b = _level_ids()
    units = [(hh, sq) for hh in range(hpi) for sq in range(nsq)]

    def head_group(hg, carry):
        head = lambda hh: hg * hpi + hh
        slot = lambda hh, sq, d: (hh * nsq + sq) * 2 + d
        for hh, sq in units:
            base = sq * seq_len
            if not is_ctx:
                for d in range(2):
                    st[slot(hh, sq, d)] = s0_ref[sq, d, head(hh)].T
                if nchunk > 1:
                    oacc[hh, base:base + seq_len, :] = jnp.zeros((seq_len, LANES), F32)

        def gates(hh, rows, d, q=None):
            h = head(hh)
            lrow = pl.ds(d * HG_HEADS + h, 1)
            q = _silu(q_ref[h, rows, :].astype(F32)) if q is None else q
            return _hgrn_gates(q, f_ref[d * HG_HEADS + h, rows, :], lbp_ref[0, lrow, :], lbp_ref[1, lrow, :],
                               tril, d == 1)

        def levels(p, d):
            return _hgrn_levels(p, lvl_b if d else lvl_f, sgn_ref, d == 1)

        def finish(p, att, hh, rows, d, sq, first):
            o, s_new = _hgrn_finish(p, att, v_ref[head(hh), rows, :], None if is_ctx else st[slot(hh, sq, d)],
                                    d == 1, not is_ctx)
            st[slot(hh, sq, d)] = s_new
            if first:
                oacc[hh, rows, :] = o
            else:
                oacc[hh, rows, :] += o

        if nchunk == 1:
            work = [(hh, sq, d, _chunk_rows(0, sq * seq_len)) for hh, sq in units for d in range(2)]
            qs = {(hh, sq): _silu(q_ref[head(hh), _chunk_rows(0, sq * seq_len), :].astype(F32)) for hh, sq in units}
            ps = [gates(hh, rows, d, qs[hh, sq]) for hh, sq, d, rows in work]
            atts = [levels(p, d) for (_, _, d, _), p in zip(work, ps)]
            for (hh, sq, d, rows), p, att in zip(work, ps, atts):
                finish(p, att, hh, rows, d, sq, d == 0)
        else:
            def body(ci, carry):
                work = [(hh, sq, d, _chunk_rows(ci if d == 0 else nchunk - 1 - ci, sq * seq_len))
                        for hh, sq in units for d in range(2)]
                ps = [gates(hh, rows, d) for hh, _, d, rows in work]
                atts = [levels(p, d) for (_, _, d, _), p in zip(work, ps)]
                for (hh, sq, d, rows), p, att in zip(work, ps, atts):
                    finish(p, att, hh, rows, d, sq, False)
                return carry
            lax.fori_loop(0, nchunk, body, 0)

        for hh, sq in units:
            h = head(hh)
            base = sq * seq_len
            o = oacc[hh, base:base + seq_len, :]
            y = o * lax.rsqrt(jnp.mean(o * o, -1, keepdims=True) + RMS_EPS) * ng_ref[pl.ds(h, 1), :]
            gate = _silu(g_ref[h, base:base + seq_len, :].astype(F32))
            y_ref[h, base:base + seq_len, :] = (y * gate).astype(y_ref.dtype)
            if is_ctx:
                for d in range(2):
                    sfin_ref[sq, d, h] = st[slot(hh, sq, d)].T
        return carry

    lax.fori_loop(0, HG_HEADS // hpi, head_group, 0)


def _seq_call(kern, name, in_specs, args, heads, st_specs, st_shapes, states, scratch, is_ctx, *, seq_len, nseq, nsq):
    in_specs, args = list(in_specs), list(args)
    n_used = len(args)
    aliases = {}
    y_spec = pl.BlockSpec((heads, nsq * seq_len, LANES), lambda i: (0, i, 0))
    y_shape = jax.ShapeDtypeStruct((heads, nseq * seq_len, LANES), BF16)
    if is_ctx:
        out_specs, out_shape = [y_spec] + list(st_specs), [y_shape] + list(st_shapes)
        for k, s in enumerate(states):
            in_specs.append(pl.BlockSpec(memory_space=pl.ANY))
            args.append(s)
            aliases[len(args) - 1] = 1 + k
    else:
        in_specs += list(st_specs)
        args += list(states)
        n_used = len(args)
        out_specs, out_shape = [y_spec], [y_shape]
    n_args = len(args)
    steps = nseq // nsq
    kw = dict(seq_len=seq_len, nsq=nsq)

    def body(*refs):
        kern(*refs[:n_used], *refs[n_args:], is_ctx=is_ctx, **kw)

    return pl.pallas_call(
        body, grid=(steps,), in_specs=in_specs, out_specs=out_specs, out_shape=out_shape,
        input_output_aliases=aliases, scratch_shapes=scratch, compiler_params=_cparams(("arbitrary",)),
        name=name + ("_ctx" if is_ctx else "_lat"),
    )(*args)


def _hgrn_call(ub, uf, lbp, ng, sgn, states, *, is_ctx, seq_len, row0, nseq, nsq, layer, depth):
    hpi = 1 if is_ctx else LAT_HEADS_PER_ITER
    rows = nsq * seq_len
    bo = row0 // rows
    nh = HG_HEADS
    in_specs = [pl.BlockSpec((nh, rows, LANES), lambda i: (0, bo + i, 0)),
                pl.BlockSpec((nh, rows, LANES), lambda i: (1, bo + i, 0)),
                pl.BlockSpec((nh, rows, LANES), lambda i: (2, bo + i, 0)),
                pl.BlockSpec((2 * nh, rows, LANES), lambda i: (0, bo + i, 0)),
                pl.BlockSpec((2, 2 * nh, LANES), lambda i: (0, 0, 0)),
                pl.BlockSpec((nh, LANES), lambda i: (0, 0)),
                pl.BlockSpec((N_LEVEL + 1, CH, LANES), lambda i: (0, 0, 0))]
    st_spec, st_shape = _state_io((2, nh, HG_DK, LANES), nseq, nsq, layer, depth)
    return _seq_call(
        _hgrn_kernel, "hgrn", in_specs, [ub, ub, ub, uf, lbp, ng, sgn], nh, [st_spec], [st_shape], states,
        [pltpu.VMEM((hpi, rows, LANES), F32), pltpu.VMEM((2 * nsq * hpi, LANES, HG_DK), F32)],
        is_ctx, seq_len=seq_len, nseq=nseq, nsq=nsq)


def _mlstm_prep(gt_ref, bias_ref, rows, ms_s, triu):
    gates = gt_ref[rows, :] + bias_ref[...]
    g_t = gates.T
    i_rows = g_t[0:8, :]
    lf_rows = _log_sigmoid(g_t[8:16, :])
    c_rows = _cumsum_lanes(lf_rows, triu)
    is_b = lax.broadcasted_iota(jnp.int32, (8, CH), 0) >= ML_HEADS
    is_b1 = is_b[:, 0:1]
    m_in = ms_s[:, 0:1]
    c_last = c_rows[:, CH - 1:CH]
    pos = jnp.where(is_b, c_rows - lf_rows, c_rows)
    u = jnp.where(is_b, pos + i_rows, i_rows - pos)
    nu = jnp.where(is_b1, c_last + m_in, m_in)
    mu = jnp.maximum(jnp.where(is_b, _cummax_lanes(u, True), _cummax_lanes(u, False)), nu)
    m_t = jnp.where(is_b, mu - pos, pos + mu)
    mu_end = jnp.where(is_b1, mu[:, 0:1], mu[:, CH - 1:CH])
    m_new = jnp.where(is_b1, mu[:, 0:1], m_t[:, CH - 1:CH])
    x_rows = jnp.concatenate([-mu, -m_t, u, jnp.zeros_like(u)], axis=0) * LOG2E
    return dict(lhs=_sel_lhs(x_rows), u_rows=x_rows[16:24, :], nu=nu * LOG2E, mu_end=mu_end * LOG2E, m_new=m_new)


def _mlstm_main(p, q_ref, k_ref, v_ref, sel_ref, rows, dirs, cn_s, ms_s, oacc, masks, has_init, first_write):
    lhs = p["lhs"]
    scale = ML_DK ** -0.5
    ones_blk = jnp.ones((CH, LANES), BF16)
    heads = range(ML_HEADS)
    hds = [(d, h, d * ML_HEADS + h) for h in heads for d in dirs]
    q = [q_ref[h, rows, :] for h in heads]
    kf = [k_ref[h, rows, :].astype(F32) * scale for h in heads]
    vaug = [jnp.concatenate([v_ref[h, rows, :], ones_blk], axis=1) for h in heads]
    qk = [_dot_nt(q[h], kf[h].astype(BF16)) for h in heads]
    b_mu = {hd: _dot(lhs, sel_ref[hd]) for _, _, hd in hds}
    b_m = {hd: _dot(lhs, sel_ref[8 + hd]) for _, _, hd in hds}
    k_t = [kf[h].T for h in heads]
    s = {}
    for d, h, hd in hds:
        xe = jnp.concatenate([b_mu[hd], b_mu[hd]], axis=1) + p["u_rows"][hd:hd + 1, :]
        s[hd] = (jnp.where(masks[d], jnp.exp2(xe), 0.0) * qk[h]).astype(BF16)
    num = {hd: _dot(s[hd], vaug[h]) for _, h, hd in hds}
    w_end = jnp.exp2(p["u_rows"] - p["mu_end"])
    upd = {hd: _dot((k_t[h] * w_end[hd:hd + 1, :]).astype(BF16), vaug[h]) for _, h, hd in hds}
    if has_init:
        for d, h, hd in hds:
            cn = cn_s[d, h]
            w_int = jnp.exp2(p["nu"][hd:hd + 1, :] + b_mu[hd])
            num[hd] = num[hd] + jnp.concatenate([w_int, w_int], axis=1) * _dot(q[h], cn.astype(BF16))
            upd[hd] = upd[hd] + jnp.exp2(p["nu"][hd:hd + 1, :] - p["mu_end"][hd:hd + 1, :]) * cn
    for d, h, hd in hds:
        cn_s[d, h] = upd[hd]
    for h in heads:
        hsum = None
        for d in dirs:
            hd = d * ML_HEADS + h
            den = jnp.maximum(jnp.abs(num[hd][:, LANES:]), jnp.exp2(b_m[hd]))
            hout = num[hd][:, :LANES] / den
            hsum = hout if hsum is None else hsum + hout
        if first_write:
            oacc[h, rows, :] = hsum
        else:
            oacc[h, rows, :] += hsum
    m_new = jnp.broadcast_to(p["m_new"], (8, LANES))
    if len(dirs) == 2:
        ms_s[...] = m_new
    else:
        row = lax.broadcasted_iota(jnp.int32, (8, LANES), 0)
        mine = (row >= ML_HEADS) if dirs[0] == 1 else (row < ML_HEADS)
        ms_s[...] = jnp.where(mine, m_new, ms_s[...])


def _mlstm_kernel(q_ref, k_ref, v_ref, og_ref, gt_ref, bias_ref, ng_ref, sel_ref, *rest, seq_len, nsq, is_ctx):
    if is_ctx:
        y_ref, cfin_ref, nfin_ref, mfin_ref, oacc, cn_s, ms_s = rest
    else:
        c0_ref, n0_ref, m0_ref, y_ref, oacc, cn_s, ms_s = rest
    nchunk = seq_len // CH
    triu = _tri(CH, False)
    rr = lax.broadcasted_iota(jnp.int32, (CH, CH), 0)
    ss = lax.broadcasted_iota(jnp.int32, (CH, CH), 1)
    masks = (ss <= rr, ss >= rr)
    def main(p, rows, dirs, sq):
        _mlstm_main(p, q_ref, k_ref, v_ref, sel_ref, rows, dirs, cn_s.at[sq], ms_s.at[sq], oacc, masks,
                    not is_ctx, nchunk == 1)

    for sq in range(nsq):
        base = sq * seq_len
        if is_ctx:
            ms_s[sq] = jnp.zeros((2 * ML_HEADS, LANES), F32)
        else:
            ms_s[sq] = m0_ref[sq]
            for d in range(2):
                for h in range(ML_HEADS):
                    hd = d * ML_HEADS + h
                    nb = jnp.broadcast_to(n0_ref[sq, hd:hd + 1, :], (ML_DK, LANES)).T
                    cn_s[sq, d, h] = jnp.concatenate([c0_ref[sq, d, h], nb], axis=1)
            if nchunk > 1:
                oacc[:, base:base + seq_len, :] = jnp.zeros((ML_HEADS, seq_len, LANES), F32)
    if nchunk == 1:
        preps = [_mlstm_prep(gt_ref, bias_ref, _chunk_rows(0, sq * seq_len), ms_s.at[sq], triu) for sq in range(nsq)]
        for sq in range(nsq):
            main(preps[sq], _chunk_rows(0, sq * seq_len), (0, 1), sq)
    else:
        def body(ci, carry):
            work = [(sq, d, _chunk_rows(ci if d == 0 else nchunk - 1 - ci, sq * seq_len))
                    for sq in range(nsq) for d in range(2)]
            ps = [_mlstm_prep(gt_ref, bias_ref, rows, ms_s.at[sq], triu) for sq, _, rows in work]
            for (sq, d, rows), p in zip(work, ps):
                main(p, rows, (d,), sq)
            return carry
        lax.fori_loop(0, nchunk, body, 0)
    for sq in range(nsq):
        base = sq * seq_len
        for h in range(ML_HEADS):
            o = oacc[h, base:base + seq_len, :]
            y = o * lax.rsqrt(jnp.mean(o * o, -1, keepdims=True) + RMS_EPS) * ng_ref[h:h + 1, :]
            gate = _sigmoid(og_ref[h, base:base + seq_len, :].astype(F32))
            y_ref[h, base:base + seq_len, :] = (y * gate).astype(y_ref.dtype)
        if is_ctx:
            for d in range(2):
                for h in range(ML_HEADS):
                    cn = cn_s[sq, d, h]
                    cfin_ref[sq, d, h] = cn[:, :LANES]
                    nfin_ref[sq, pl.ds(d * ML_HEADS + h, 1), :] = cn[:, LANES:].T[0:1, :]
            mfin_ref[sq] = ms_s[sq]


def _mlstm_call(ub, uf, bias_row, ng, sel, states, *, is_ctx, seq_len, row0, nseq, nsq, layer, depth):
    rows = nsq * seq_len
    bo = row0 // rows
    nh = ML_HEADS
    slab = lambda k: pl.BlockSpec((nh, rows, LANES), lambda i: (k, bo + i, 0))
    in_specs = [slab(3), slab(4), slab(5), slab(6),
                pl.BlockSpec((None, rows, LANES), lambda i: (GATE_SLAB, bo + i, 0)),
                pl.BlockSpec((1, LANES), lambda i: (0, 0)),
                pl.BlockSpec((nh, LANES), lambda i: (0, 0)),
                pl.BlockSpec((N_SEL, LANES, LANES), lambda i: (0, 0, 0))]
    c_spec, c_shape = _state_io((2, nh, ML_DK, LANES), nseq, nsq, layer, depth)
    v_spec, v_shape = _state_io((2 * nh, LANES), nseq, nsq, layer, depth)
    return _seq_call(
        _mlstm_kernel, "mlstm", in_specs, [ub, ub, ub, ub, uf, bias_row, ng, sel], nh,
        [c_spec, v_spec, v_spec], [c_shape, v_shape, v_shape], states,
        [pltpu.VMEM((nh, rows, LANES), F32), pltpu.VMEM((nsq, 2, nh, ML_DK, 2 * LANES), F32),
         pltpu.VMEM((nsq, 2 * nh, LANES), F32)],
        is_ctx, seq_len=seq_len, nseq=nseq, nsq=nsq)


N_PAIR = MB_HEADS // 2
PAIRS_PER_GROUP = N_PAIR // MB_GROUPS


def _ssd_prep(gt_ref, bias_ref, nega_ref, rows, triu):
    gates = gt_ref[rows, :] + bias_ref[...]
    g_t = gates.T
    dt_rows = _softplus(g_t[16:32, :])
    la_rows = dt_rows * nega_ref[...]
    c_rows = _cumsum_lanes(la_rows, triu)
    ldt = jnp.log(dt_rows)
    is_b = lax.broadcasted_iota(jnp.int32, (16, CH), 0) >= MB_HEADS
    pos = jnp.where(is_b, c_rows - la_rows, c_rows)
    a_col = jnp.where(is_b, -pos, pos) * LOG2E
    r_row = jnp.where(is_b, pos + ldt, ldt - pos) * LOG2E
    c_last = c_rows[:, CH - 1:CH] * LOG2E
    lhs = _sel_lhs(jnp.concatenate([a_col, jnp.zeros_like(a_col)], axis=0))
    return dict(lhs=lhs, r_row=r_row, c_last=c_last)


def _ssd_main(p, xs_s, bcs_s, sel_ref, rows, dirs, hst, yacc, masks, has_init, first_write):
    lhs, r_row, c_last = p["lhs"], p["r_row"], p["c_last"]
    lane = lax.broadcasted_iota(jnp.int32, (CH, LANES), 1)
    lo = lane < MB_HEADDIM
    lo_state = lax.broadcasted_iota(jnp.int32, (LANES, LANES), 1) < MB_HEADDIM
    lo_row = lo_state[0:1, :]
    bblk = bcs_s[rows, 0:LANES]
    cblk = bcs_s[rows, LANES:2 * LANES]
    c_g, b_g, gm = [], [], []
    for grp in range(MB_GROUPS):
        in_grp = (lane >= grp * MB_DSTATE) & (lane < (grp + 1) * MB_DSTATE)
        c_g.append(jnp.where(in_grp, cblk, 0.0))
        b_g.append(jnp.where(in_grp, bblk, 0.0))
        gm.append(_dot_nt(c_g[grp].astype(BF16), b_g[grp].astype(BF16)))
    xh = []
    for j in range(N_PAIR):
        xpair = xs_s[rows, j * LANES:(j + 1) * LANES]
        xh.append((jnp.where(lo, xpair, 0.0).astype(BF16), jnp.where(lo, 0.0, xpair).astype(BF16)))
    items = [(d, j, half, d * MB_HEADS + 2 * j + half) for j in range(N_PAIR) for d in dirs for half in range(2)]
    b_a = {hd: _dot(lhs, sel_ref[hd]) for _, _, _, hd in items}
    b_t = [b_g[grp].T for grp in range(MB_GROUPS)]
    is_f = lax.broadcasted_iota(jnp.int32, (2 * MB_HEADS, 1), 0) < MB_HEADS
    w_end = jnp.exp2(r_row + jnp.where(is_f, c_last, 0.0))
    m, b_out, c_in = {}, {}, {}
    for d, j, half, hd in items:
        grp = j // PAIRS_PER_GROUP
        cl = c_last[hd:hd + 1, :]
        xe = jnp.concatenate([b_a[hd], b_a[hd]], axis=1) + r_row[hd:hd + 1, :]
        m[hd] = (jnp.where(masks[d], jnp.exp2(xe), 0.0) * gm[grp]).astype(BF16)
        b_out[hd] = (b_t[grp] * w_end[hd:hd + 1, :]).astype(BF16)
        if has_init:
            c_in[hd] = (c_g[grp] * jnp.exp2(b_a[hd] + cl if d else b_a[hd])).astype(BF16)
    yy = {hd: _dot(m[hd], xh[j][half]) for _, j, half, hd in items}
    uu = {hd: _dot(b_out[hd], xh[j][half]) for _, j, half, hd in items}
    for j in range(N_PAIR):
        ysum = None
        for d in dirs:
            hd0, hd1 = d * MB_HEADS + 2 * j, d * MB_HEADS + 2 * j + 1
            y = yy[hd0] + yy[hd1]
            upd = uu[hd0] + uu[hd1]
            if has_init:
                ht = hst[d, j]
                y = y + _dot(c_in[hd0], jnp.where(lo_state, ht, 0.0).astype(BF16))
                y = y + _dot(c_in[hd1], jnp.where(lo_state, 0.0, ht).astype(BF16))
                decay = jnp.where(lo_row, jnp.exp2(c_last[hd0:hd0 + 1, :]), jnp.exp2(c_last[hd1:hd1 + 1, :]))
                upd = upd + ht * decay
            hst[d, j] = upd
            ysum = y if ysum is None else ysum + y
        if first_write:
            yacc[rows, j * LANES:(j + 1) * LANES] = ysum
        else:
            yacc[rows, j * LANES:(j + 1) * LANES] += ysum


def _ssd_kernel(x_ref, bc_ref, z_ref, gt_ref, cw_ref, bias_ref, nega_ref, dskip_ref, ng_ref, sel_ref, *rest,
                seq_len, nsq, is_ctx):
    if is_ctx:
        y_ref, hfin_ref, xs_s, bcs_s, yacc, hst = rest
    else:
        h0_ref, y_ref, xs_s, bcs_s, yacc, hst = rest
    nchunk = seq_len // CH
    triu = _tri(CH, False)
    rr = lax.broadcasted_iota(jnp.int32, (CH, CH), 0)
    ss = lax.broadcasted_iota(jnp.int32, (CH, CH), 1)
    masks = (ss <= rr, ss >= rr)

    def conv(v, lo_col):
        row = lax.broadcasted_iota(jnp.int32, v.shape, 0)
        prev = jnp.where(row == 0, 0.0, pltpu.roll(v, 1, 0))
        nxt = jnp.where(row == seq_len - 1, 0.0, pltpu.roll(v, seq_len - 1, 0))
        cs = slice(lo_col, lo_col + v.shape[1])
        return _silu(cw_ref[0:1, cs] * prev + cw_ref[1:2, cs] * v + cw_ref[2:3, cs] * nxt + cw_ref[3:4, cs])

    for sq in range(nsq):
        base = sq * seq_len
        seq_rows = slice(base, base + seq_len)
        for k in range(MIX_W // LANES):
            xs_s[seq_rows, k * LANES:(k + 1) * LANES] = conv(x_ref[k, seq_rows, :], k * LANES)
        for k in range(2):
            bcs_s[seq_rows, k * LANES:(k + 1) * LANES] = conv(bc_ref[k, seq_rows, :], MIX_W + k * LANES)
        if not is_ctx:
            hst[sq] = h0_ref[sq]
            if nchunk > 1:
                yacc[seq_rows, :] = jnp.zeros((seq_len, MIX_W), F32)

    def prep(rows):
        return _ssd_prep(gt_ref, bias_ref, nega_ref, rows, triu)

    def main(p, rows, dirs, sq):
        _ssd_main(p, xs_s, bcs_s, sel_ref, rows, dirs, hst.at[sq], yacc, masks, not is_ctx, nchunk == 1)

    if nchunk == 1:
        preps = [prep(_chunk_rows(0, sq * seq_len)) for sq in range(nsq)]
        for sq in range(nsq):
            main(preps[sq], _chunk_rows(0, sq * seq_len), (0, 1), sq)
    else:
        def body(ci, carry):
            work = [(sq, d, _chunk_rows(ci if d == 0 else nchunk - 1 - ci, sq * seq_len))
                    for sq in range(nsq) for d in range(2)]
            ps = [prep(rows) for _, _, rows in work]
            for (sq, d, rows), p in zip(work, ps):
                main(p, rows, (d,), sq)
            return carry
        lax.fori_loop(0, nchunk, body, 0)

    for sq in range(nsq):
        base = sq * seq_len
        seq_rows = slice(base, base + seq_len)
        z = jnp.concatenate([z_ref[k, seq_rows, :] for k in range(MIX_W // LANES)], axis=1).astype(F32)
        y = (yacc[seq_rows, :] + dskip_ref[...] * xs_s[seq_rows, :]) * _silu(z)
        y = (y * lax.rsqrt(jnp.mean(y * y, -1, keepdims=True) + RMS_EPS) * ng_ref[...]).astype(y_ref.dtype)
        for k in range(MIX_W // LANES):
            y_ref[k, seq_rows, :] = y[:, k * LANES:(k + 1) * LANES]
        if is_ctx:
            hfin_ref[sq] = hst[sq]


def _ssd_call(ub, uf, cw, bias_row, nega_rows, dskip, ng, sel, states, *, is_ctx, seq_len, row0, nseq, nsq, layer,
              depth):
    rows = nsq * seq_len
    bo = row0 // rows
    ns = MIX_W // LANES
    row_spec = lambda w: pl.BlockSpec((1, w), lambda i: (0, 0))
    in_specs = [pl.BlockSpec((ns, rows, LANES), lambda i: (2, bo + i, 0)),
                pl.BlockSpec((2, rows, LANES), lambda i: (6, bo + i, 0)),
                pl.BlockSpec((ns, rows, LANES), lambda i: (7, bo + i, 0)),
                pl.BlockSpec((None, rows, LANES), lambda i: (GATE_SLAB, bo + i, 0)),
                pl.BlockSpec((8, MB_XBC), lambda i: (0, 0)),
                row_spec(LANES),
                pl.BlockSpec((2 * MB_HEADS, CH), lambda i: (0, 0)),
                row_spec(MIX_W), row_spec(MIX_W),
                pl.BlockSpec((N_SEL, LANES, LANES), lambda i: (0, 0, 0))]
    st_spec, st_shape = _state_io((2, N_PAIR, LANES, LANES), nseq, nsq, layer, depth)
    return _seq_call(
        _ssd_kernel, "ssd", in_specs, [uf, uf, ub, uf, cw, bias_row, nega_rows, dskip, ng, sel], ns,
        [st_spec], [st_shape], states,
        [pltpu.VMEM((rows, MIX_W), F32), pltpu.VMEM((rows, 2 * LANES), F32), pltpu.VMEM((rows, MIX_W), F32),
         pltpu.VMEM((nsq, 2, N_PAIR, LANES, LANES), F32)],
        is_ctx, seq_len=seq_len, nseq=nseq, nsq=nsq)


def _ssd_state_to_pairs(s):
    bsz = s.shape[0]
    st = jnp.swapaxes(s, -1, -2).reshape(bsz, 2, N_PAIR, 2, MB_DSTATE, MB_HEADDIM)
    st = jnp.moveaxis(st, 3, 4).reshape(bsz, 2, N_PAIR, MB_DSTATE, 2 * MB_HEADDIM)
    zero = jnp.zeros_like(st)
    grp = (jnp.arange(N_PAIR) // PAIRS_PER_GROUP).reshape(1, 1, N_PAIR, 1, 1)
    return jnp.concatenate([jnp.where(grp == 0, st, zero), jnp.where(grp == 1, st, zero)], axis=3)


def _ssd_pairs_to_state(hp):
    bsz = hp.shape[0]
    halves = hp.reshape(bsz, 2, N_PAIR, MB_GROUPS, MB_DSTATE, 2 * MB_HEADDIM)
    grp = (jnp.arange(N_PAIR) // PAIRS_PER_GROUP).reshape(1, 1, N_PAIR, 1, 1)
    st = jnp.where(grp == 0, halves[:, :, :, 0], halves[:, :, :, 1])
    st = st.reshape(bsz, 2, N_PAIR, MB_DSTATE, 2, MB_HEADDIM)
    st = jnp.moveaxis(st, 4, 3).reshape(bsz, 2, MB_HEADS, MB_DSTATE, MB_HEADDIM)
    return jnp.swapaxes(st, -1, -2)


def _grid_pos_embed(n_tok, d_model):
    rows = n_tok // GRID_W
    r, cidx = jnp.meshgrid(jnp.arange(rows, dtype=F32), jnp.arange(GRID_W, dtype=F32), indexing='ij')
    quarter = d_model // 4
    freq = jnp.exp(-math.log(10000.0) * jnp.arange(quarter, dtype=F32) / quarter)
    ar = r.reshape(-1, 1) * freq
    ac = cidx.reshape(-1, 1) * freq
    return jnp.concatenate([jnp.sin(ar), jnp.cos(ar), jnp.sin(ac), jnp.cos(ac)], axis=-1)


def _gate_row(pieces):
    v = jnp.concatenate([p.reshape(-1).astype(F32) for p in pieces])
    return jnp.concatenate([v, v, v, jnp.zeros((LANES - 3 * N_SEL,), F32)]).reshape(1, LANES)


def kernel(x_prompt, x_sample, state_hgrn, state_mlstm_C, state_mlstm_n, state_mlstm_m, state_ssd, c, c_ctx,
           w_mod, b_mod, ln_g, ln_b, ffn_w_gu, ffn_w_down, w_in, hg_lb, hg_norm_g, ml_gate_b, ml_norm_g,
           mb_conv_w, mb_conv_b, mb_dt_bias, mb_a_log, mb_d, mb_norm_g, w_branch, w_out):
    bsz, seq, d = x_prompt.shape
    dbsz, dseq, _ = x_sample.shape
    depth = w_mod.shape[0]
    t_ctx = bsz * seq
    n_ctx_tiles = t_ctx // ROW_TILE
    assert dseq == ROW_TILE and t_ctx % ROW_TILE == 0 and seq == CH and dseq % CH == 0

    xs0 = x_sample + _grid_pos_embed(dseq, d).astype(x_sample.dtype)[None]
    x = jnp.concatenate([x_prompt.reshape(t_ctx, d), xs0.reshape(dbsz * dseq, d)], axis=0)

    cv = jnp.concatenate([c_ctx[None]] * MOD_LAT_ROW + [c, jnp.zeros((8 - MOD_LAT_ROW - dbsz, d), F32)], axis=0)
    mod4 = _mod_call(cv, w_mod, b_mod).reshape(depth, 8, 1, N_MOD * d)

    w_branch_b = w_branch.astype(BF16)
    w_out_b = w_out.astype(BF16)

    def regroup(width, pieces):
        out = jnp.zeros((depth, d, width), BF16)
        at = 0
        for lo, hi in pieces:
            out = lax.dynamic_update_slice(out, w_in[:, :, lo:hi].astype(BF16), (0, 0, at))
            at += hi - lo
        return out, at

    w_a, _ = regroup(UB_SLABS * LANES, [(0, 1536), (2560, 4608), (4624, 5136), (5920, 8992)])
    gate_cols = [(4608, 4624), (5904, 5920)]
    w_b, _ = regroup(UF_SLABS * LANES, [(1536, 2560), (5136, 5904)] + gate_cols * 3)
    ln_g4 = ln_g.reshape(depth, 3, 1, d)
    ln_b4 = ln_b.reshape(depth, 3, 1, d)

    lbs = jnp.cumsum(jax.nn.softmax(hg_lb.astype(F32), axis=0), axis=0)
    lbs = (lbs - lbs[0]).reshape(depth, 2 * HG_HEADS, HG_DK)
    lbp = jnp.stack([lbs, 1.0 - lbs], axis=1)
    cw = jnp.concatenate([mb_conv_w, mb_conv_b[:, None, :], jnp.zeros((depth, 4, MB_XBC), F32)], axis=1)
    sel = _selectors()

    lat_h0 = _ssd_state_to_pairs(state_ssd.reshape((dbsz * depth,) + state_ssd.shape[2:])).reshape(
        (dbsz, depth, 2, N_PAIR, LANES, LANES))
    lat_n0 = state_mlstm_n.reshape(dbsz, depth, 2 * ML_HEADS, ML_DK)
    lat_m0 = jnp.broadcast_to(state_mlstm_m.reshape(dbsz, depth, 2 * ML_HEADS, 1), (dbsz, depth, 2 * ML_HEADS, LANES))

    zeros = lambda *s: jnp.zeros((bsz, depth) + s, F32)
    st_hg = [zeros(2, HG_HEADS, HG_DK, LANES)]
    st_ml = [zeros(2, ML_HEADS, ML_DK, LANES), zeros(2 * ML_HEADS, LANES), zeros(2 * ML_HEADS, LANES)]
    st_ss = [zeros(2, N_PAIR, LANES, LANES)]
    t_idx = jnp.arange(CH).reshape(1, CH, 1)
    widths = (HALF >> jnp.arange(N_LEVEL + 1)).reshape(N_LEVEL + 1, 1, 1)
    sgn = jnp.broadcast_to(jnp.where((t_idx & widths) != 0, 1.0, -1.0), (N_LEVEL + 1, CH, LANES)).astype(F32)
    ctx = dict(is_ctx=True, seq_len=seq, row0=0, nseq=bsz, depth=depth)
    lat = dict(is_ctx=False, seq_len=dseq, row0=t_ctx, nseq=dbsz, depth=depth)
    for l in range(depth):
        x = _ffn_call(x, mod4, ffn_w_gu, ffn_w_down, ln_g4, ln_b4, l, 0, n_ctx_tiles)
        ub = _inproj_call(x, mod4, w_a, l, INPROJ_TILE_A, BF16, n_ctx_tiles, "inproj_a")
        uf = _inproj_call(x, mod4, w_b, l, INPROJ_TILE_B, F32, n_ctx_tiles, "inproj_b")

        ng_h = hg_norm_g[l].reshape(HG_HEADS, LANES)
        y_hg, *st_hg = _hgrn_call(ub, uf, lbp[l], ng_h, sgn, st_hg, layer=l, nsq=CTX_SEQS_PER_STEP, **ctx)
        y_hg_lat, = _hgrn_call(ub, uf, lbp[l], ng_h, sgn, [state_hgrn], layer=l, nsq=1, **lat)

        gate_bias = _gate_row([ml_gate_b[l, 0], ml_gate_b[l, 1], mb_dt_bias[l]])
        ng_m = ml_norm_g[l].reshape(ML_HEADS, LANES)
        y_ml, *st_ml = _mlstm_call(ub, uf, gate_bias, ng_m, sel, st_ml, layer=l, nsq=CTX_SEQS_PER_STEP_SCALAR, **ctx)
        y_ml_lat, = _mlstm_call(ub, uf, gate_bias, ng_m, sel, [state_mlstm_C, lat_n0, lat_m0], layer=l, nsq=LAT_SEQS_PER_STEP_SCALAR, **lat)

        nega = jnp.broadcast_to(-jnp.exp(mb_a_log[l].astype(F32)).reshape(2 * MB_HEADS, 1), (2 * MB_HEADS, CH))
        dskip = jnp.repeat(mb_d[l], MB_HEADDIM).reshape(1, MIX_W)
        ng_s = mb_norm_g[l].reshape(1, MIX_W)
        y_mb, *st_ss = _ssd_call(ub, uf, cw[l], gate_bias, nega, dskip, ng_s, sel, st_ss, layer=l, nsq=CTX_SEQS_PER_STEP_SCALAR, **ctx)
        y_mb_lat, = _ssd_call(ub, uf, cw[l], gate_bias, nega, dskip, ng_s, sel, [lat_h0], layer=l, nsq=LAT_SEQS_PER_STEP_SCALAR, **lat)

        x = _mixout_call(x, mod4, (y_hg, y_ml, y_mb), (y_hg_lat, y_ml_lat, y_mb_lat), ub, w_branch_b, w_out_b,
                         ln_g4, ln_b4, l, n_ctx_tiles)
        x = _ffn_call(x, mod4, ffn_w_gu, ffn_w_down, ln_g4, ln_b4, l, 1, n_ctx_tiles)

    y_prompt = x[:t_ctx].reshape(bsz, seq, d)
    y_sample = x[t_ctx:].reshape(dbsz, dseq, d)
    c_fin, n_fin, m_fin = st_ml
    h_fin = _ssd_pairs_to_state(st_ss[0].reshape((bsz * depth,) + st_ss[0].shape[2:]))
    return (y_prompt, y_sample, st_hg[0], c_fin, n_fin.reshape(bsz, depth, 2, ML_HEADS, ML_DK),
            m_fin[:, :, :, 0].reshape(bsz, depth, 2, ML_HEADS),
            h_fin.reshape((bsz, depth) + h_fin.shape[1:]))
```

```python
import functools
import math

import jax
import jax.numpy as jnp
from jax import lax
from jax.experimental import pallas as pl
from jax.experimental.pallas import tpu as pltpu

F32 = jnp.float32
BF16 = jnp.bfloat16

D_MODEL = 1024
DEPTH = 4
GRID_W = 64
MIX_W = 512
HG_HEADS = 4
HG_DK = 128
ML_HEADS = 4
ML_DK = 128
MB_HEADS = 8
MB_HEADDIM = 64
MB_GROUPS = 2
MB_DSTATE = 64
MB_XBC = MIX_W + 2 * MB_GROUPS * MB_DSTATE
D_FF = 2816
N_MOD = 9
ALPHA = (2 * DEPTH) ** 0.25
LN_EPS = 1e-5
RMS_EPS = 1e-6
LOG2E = 1.4426950408889634

LANES = 128
CH = 256
HALF = CH // 2
ROW_TILE = 1024
FFN_ROWS = 2048
MOD_LAT_ROW = 2
FF_TILE = 256
INPROJ_TILE_A = 1792
INPROJ_TILE_B = 1920
LAT_HEADS_PER_ITER = 2
CTX_SEQS_PER_STEP_SCALAR = 4
LAT_SEQS_PER_STEP_SCALAR = 2
CTX_SEQS_PER_STEP = 4
VMEM_LIMIT = 56 * 1024 * 1024

UB_SLABS = 56
UF_SLABS = 15
GATE_SLAB = 14
N_SEL = 32
LOG2_TINY = -150.0


def _dot(a, b):
    return jnp.dot(a, b, preferred_element_type=F32)


def _dot_nt(a, b):
    return lax.dot_general(a, b, (((1,), (1,)), ((), ())), preferred_element_type=F32)


def _dot_tn(a, b):
    return lax.dot_general(a, b, (((0,), (0,)), ((), ())), preferred_element_type=F32)


def _sigmoid(x):
    return 1.0 / (1.0 + jnp.exp(-x))


def _silu(x):
    return x * _sigmoid(x)


def _log_sigmoid(x):
    return jnp.minimum(x, 0.0) - jnp.log(1.0 + jnp.exp(-jnp.abs(x)))


def _softplus(x):
    return jnp.maximum(x, 0.0) + jnp.log(1.0 + jnp.exp(-jnp.abs(x)))


def _neg_abs(x):
    return -jnp.abs(x)


def _ln(z, g, b):
    mu = jnp.mean(z, -1, keepdims=True)
    d = z - mu
    var = jnp.mean(d * d, -1, keepdims=True)
    return d * lax.rsqrt(var + LN_EPS) * g + b


def _split3(x):
    h = x.astype(BF16)
    r = x - h.astype(F32)
    m = r.astype(BF16)
    l = (r - m.astype(F32)).astype(BF16)
    return h, m, l


def _cumsum_rows(tril, x):
    h, m, l = _split3(x)
    return _dot(tril, h) + _dot(tril, m) + _dot(tril, l)


def _cumsum_lanes(x, triu):
    h, m, l = _split3(x)
    return _dot(h, triu) + _dot(m, triu) + _dot(l, triu)


def _cummax_lanes(x, reverse):
    n = x.shape[1]
    lane = lax.broadcasted_iota(jnp.int32, x.shape, 1)
    k = 1
    while k < n:
        if reverse:
            sh = jnp.where(lane < n - k, pltpu.roll(x, n - k, 1), -jnp.inf)
        else:
            sh = jnp.where(lane >= k, pltpu.roll(x, k, 1), -jnp.inf)
        x = jnp.maximum(x, sh)
        k *= 2
    return x


def _tri(c, lower):
    r = lax.broadcasted_iota(jnp.int32, (c, c), 0)
    s = lax.broadcasted_iota(jnp.int32, (c, c), 1)
    return jnp.where((s <= r) if lower else (r <= s), 1.0, 0.0).astype(BF16)


def _sel_lhs(x_rows):
    h = x_rows.astype(BF16).astype(F32)
    r = x_rows - h
    m = r.astype(BF16).astype(F32)
    rows = jnp.concatenate([h, m, r - m, jnp.ones_like(x_rows)], axis=0)
    return rows.T.astype(BF16)


def _selectors():
    r = jnp.arange(LANES)
    ch = jnp.arange(N_SEL)
    hit = ((r[None, :] % N_SEL) == ch[:, None]) & (r[None, :] < 3 * N_SEL)
    return jnp.broadcast_to(hit[:, :, None], (N_SEL, LANES, LANES)).astype(BF16)


def _cparams(sem):
    return pltpu.CompilerParams(dimension_semantics=sem, vmem_limit_bytes=VMEM_LIMIT)


def _chunk_rows(ci, base=0):
    r0 = ci * CH + base
    if not isinstance(r0, int):
        r0 = pl.multiple_of(r0, CH)
    return pl.ds(r0, CH)


def _state_io(shape_tail, nseq, nsq, layer, depth):
    nz = (0,) * len(shape_tail)
    spec = pl.BlockSpec((nsq, None) + shape_tail, lambda i: (i, layer) + nz)
    shape = jax.ShapeDtypeStruct((nseq, depth) + shape_tail, F32)
    return spec, shape


def _chunk_loop(nchunk, run):
    if nchunk == 1:
        run(0, (0, 1))
    else:
        def body(ci, carry):
            run(ci, (0,))
            run(nchunk - 1 - ci, (1,))
            return carry
        lax.fori_loop(0, nchunk, body, 0)


def _mod_kernel(c_ref, w_ref, b_ref, o_ref):
    a = _silu(c_ref[...]).astype(BF16)
    o_ref[...] = _dot(a, w_ref[...].astype(BF16)) + b_ref[...]


def _mod_call(cv, w_mod, b_mod):
    depth, d, e = w_mod.shape
    tn = 1024
    return pl.pallas_call(
        _mod_kernel,
        grid=(depth, e // tn),
        in_specs=[pl.BlockSpec((8, d), lambda l, j: (0, 0)),
                  pl.BlockSpec((None, d, tn), lambda l, j: (l, 0, j)),
                  pl.BlockSpec((None, 1, tn), lambda l, j: (l, 0, j))],
        out_specs=pl.BlockSpec((None, 8, tn), lambda l, j: (l, 0, j)),
        out_shape=jax.ShapeDtypeStruct((depth, 8, e), F32),
        compiler_params=_cparams(("arbitrary", "arbitrary")),
        name="mod",
    )(cv, w_mod, b_mod.reshape(depth, 1, e))


def _mod_row(i, n_ctx_tiles):
    return jnp.maximum(i - n_ctx_tiles + MOD_LAT_ROW, 0)


def _ffn_kernel(x_ref, mod_ref, wa_ref, wb_ref, wd_ref, g_ref, b_ref, o_ref, xm_s, *, nf):
    j = pl.program_id(1)
    d = x_ref.shape[1]
    groups = [(r, slice(r * ROW_TILE, (r + 1) * ROW_TILE)) for r in range(FFN_ROWS // ROW_TILE)]

    def ffn(xm):
        wa, wb, wd = wa_ref[...].astype(BF16), wb_ref[...].astype(BF16), wd_ref[...].astype(BF16)
        return _dot((_silu(_dot(xm, wa)) * _dot(xm, wb)).astype(BF16), wd)

    @pl.when(j == 0)
    def _():
        for r, rows in groups:
            sh = mod_ref[r:r + 1, 0:d]
            sc = mod_ref[r:r + 1, d:2 * d]
            xm = (x_ref[rows, :] * (1.0 + sc) + sh).astype(BF16)
            xm_s[rows, :] = xm
            o_ref[rows, :] = ffn(xm)

    @pl.when(jnp.logical_and(j > 0, j < nf - 1))
    def _():
        o_ref[...] += ffn(xm_s[...])

    @pl.when(j == nf - 1)
    def _():
        for r, rows in groups:
            gate = mod_ref[r:r + 1, 2 * d:3 * d]
            z = ALPHA * x_ref[rows, :] + 0.5 * gate * (o_ref[rows, :] + ffn(xm_s[rows, :]))
            o_ref[rows, :] = _ln(z, g_ref[...], b_ref[...])


def _ffn_call(x, mod4, w_gu, w_down, ln_g, ln_b, layer, which, n_ctx_tiles):
    t, d = x.shape
    nf = D_FF // FF_TILE
    per = FFN_ROWS // ROW_TILE
    assert nf >= 3 and nf * FF_TILE == D_FF and n_ctx_tiles % per == 0 and MOD_LAT_ROW % per == 0
    sub = 0 if which == 0 else 2
    mod_pairs = mod4.reshape(mod4.shape[0], mod4.shape[1] // per, per, mod4.shape[3])
    first_lat = n_ctx_tiles // per
    mod_blk = lambda i, j: (layer, jnp.maximum(i - first_lat + MOD_LAT_ROW // per, 0), 0, sub)
    return pl.pallas_call(
        functools.partial(_ffn_kernel, nf=nf),
        grid=(t // FFN_ROWS, nf),
        in_specs=[pl.BlockSpec((FFN_ROWS, d), lambda i, j: (i, 0)),
                  pl.BlockSpec((None, None, per, 3 * d), mod_blk),
                  pl.BlockSpec((None, None, d, FF_TILE), lambda i, j: (layer, which, 0, j)),
                  pl.BlockSpec((None, None, d, FF_TILE), lambda i, j: (layer, which, 0, j + nf)),
                  pl.BlockSpec((None, None, FF_TILE, d), lambda i, j: (layer, which, j, 0)),
                  pl.BlockSpec((None, None, 1, d), lambda i, j: (layer, sub, 0, 0)),
                  pl.BlockSpec((None, None, 1, d), lambda i, j: (layer, sub, 0, 0))],
        out_specs=pl.BlockSpec((FFN_ROWS, d), lambda i, j: (i, 0)),
        out_shape=jax.ShapeDtypeStruct((t, d), F32),
        scratch_shapes=[pltpu.VMEM((FFN_ROWS, d), BF16)],
        compiler_params=_cparams(("arbitrary", "arbitrary")),
        name=f"ffn{which}",
    )(x, mod_pairs, w_gu, w_gu, w_down, ln_g, ln_b)


def _inproj_kernel(x_ref, mod_ref, w_ref, o_ref, xm_s):
    d = x_ref.shape[1]

    @pl.when(pl.program_id(1) == 0)
    def _():
        sh = mod_ref[:, 0:d]
        sc = mod_ref[:, d:2 * d]
        xm_s[...] = (x_ref[...] * (1.0 + sc) + sh).astype(BF16)

    res = _dot(xm_s[...], w_ref[...]).astype(o_ref.dtype)
    for k in range(o_ref.shape[0]):
        o_ref[k] = res[:, k * LANES:(k + 1) * LANES]


def _inproj_call(x, mod4, w, layer, tn, out_dtype, n_ctx_tiles, name):
    t, d = x.shape
    n = w.shape[2]
    return pl.pallas_call(
        _inproj_kernel,
        grid=(t // ROW_TILE, n // tn),
        in_specs=[pl.BlockSpec((ROW_TILE, d), lambda i, j: (i, 0)),
                  pl.BlockSpec((None, None, 1, 3 * d), lambda i, j: (layer, _mod_row(i, n_ctx_tiles), 0, 1)),
                  pl.BlockSpec((None, d, tn), lambda i, j: (layer, 0, j))],
        out_specs=pl.BlockSpec((tn // LANES, ROW_TILE, LANES), lambda i, j: (j, i, 0)),
        out_shape=jax.ShapeDtypeStruct((n // LANES, t, LANES), out_dtype),
        scratch_shapes=[pltpu.VMEM((ROW_TILE, d), BF16)],
        compiler_params=_cparams(("arbitrary", "arbitrary")),
        name=name,
    )(x, mod4, w)


def _slabs(ref):
    return jnp.concatenate([ref[k] for k in range(ref.shape[0])], axis=1)


def _mixout_kernel(x_ref, mod_ref, yhc_ref, ymc_ref, ysc_ref, yhl_ref, yml_ref, ysl_ref, g0_ref, g1_ref, g2_ref,
                   wb_ref, wo_ref, lg_ref, lb_ref, o_ref, *, n_ctx_steps):
    d = x_ref.shape[1]
    is_lat = pl.program_id(0) >= n_ctx_steps

    def branch(c_ref, l_ref):
        return jnp.where(is_lat, _slabs(l_ref), _slabs(c_ref))

    p = _sigmoid(_slabs(g0_ref).astype(F32)) * _dot(branch(yhc_ref, yhl_ref), wb_ref[0])
    p += _sigmoid(_slabs(g1_ref).astype(F32)) * _dot(branch(ymc_ref, yml_ref), wb_ref[1])
    p += _sigmoid(_slabs(g2_ref).astype(F32)) * _dot(branch(ysc_ref, ysl_ref), wb_ref[2])
    y = _dot(p.astype(BF16), wo_ref[...])
    gate = mod_ref[:, 2 * d:3 * d]
    z = ALPHA * x_ref[...] + gate * y
    o_ref[...] = _ln(z, lg_ref[...], lb_ref[...])


def _mixout_call(x, mod4, y_ctx, y_lat, ub, w_branch, w_out, ln_g, ln_b, layer, n_ctx_tiles):
    t, d = x.shape
    tm = 512
    per = ROW_TILE // tm
    ns = MIX_W // LANES
    ng = d // LANES
    g0 = (UB_SLABS - 3 * ng) // ng
    n_ctx_steps = n_ctx_tiles * per
    n_lat_steps = t // tm - n_ctx_steps
    yc_spec = pl.BlockSpec((ns, tm, LANES), lambda i: (0, jnp.minimum(i, n_ctx_steps - 1), 0))
    yl_spec = pl.BlockSpec((ns, tm, LANES), lambda i: (0, jnp.clip(i - n_ctx_steps, 0, n_lat_steps - 1), 0))
    return pl.pallas_call(
        functools.partial(_mixout_kernel, n_ctx_steps=n_ctx_steps),
        grid=(t // tm,),
        in_specs=[pl.BlockSpec((tm, d), lambda i: (i, 0)),
                  pl.BlockSpec((None, None, 1, 3 * d), lambda i: (layer, _mod_row(i // per, n_ctx_tiles), 0, 1)),
                  yc_spec, yc_spec, yc_spec, yl_spec, yl_spec, yl_spec,
                  pl.BlockSpec((ng, tm, LANES), lambda i: (g0, i, 0)),
                  pl.BlockSpec((ng, tm, LANES), lambda i: (g0 + 1, i, 0)),
                  pl.BlockSpec((ng, tm, LANES), lambda i: (g0 + 2, i, 0)),
                  pl.BlockSpec((None, 3, MIX_W, d), lambda i: (layer, 0, 0, 0)),
                  pl.BlockSpec((None, d, d), lambda i: (layer, 0, 0)),
                  pl.BlockSpec((None, None, 1, d), lambda i: (layer, 1, 0, 0)),
                  pl.BlockSpec((None, None, 1, d), lambda i: (layer, 1, 0, 0))],
        out_specs=pl.BlockSpec((tm, d), lambda i: (i, 0)),
        out_shape=jax.ShapeDtypeStruct((t, d), F32),
        compiler_params=_cparams(("arbitrary",)),
        name="mixout",
    )(x, mod4, *y_ctx, *y_lat, ub, ub, ub, w_branch, w_out, ln_g, ln_b)


N_LEVEL = int(math.log2(HALF))


def _block_ref_rows(c, w):
    n_rows = c.shape[0]
    if 2 * w == n_rows:
        return jnp.broadcast_to(c[w - 1:w, :], c.shape)
    if w >= 4:
        n = n_rows // (2 * w)
        c3 = c.reshape(n, 2 * w, LANES)
        return jnp.broadcast_to(c3[:, w - 1:w, :], c3.shape).reshape(n_rows, LANES)
    c8 = c.reshape(n_rows // 8, 8, LANES)
    sub = lax.broadcasted_iota(jnp.int32, c8.shape, 1)
    if w == 2:
        r = jnp.where(sub < 4, c8[:, 1:2, :], c8[:, 5:6, :])
    else:
        r = jnp.where(sub < 2, c8[:, 0:1, :],
                      jnp.where(sub < 4, c8[:, 2:3, :], jnp.where(sub < 6, c8[:, 4:5, :], c8[:, 6:7, :])))
    return r.reshape(n_rows, LANES)


def _level_ids():
    r = lax.broadcasted_iota(jnp.int32, (HALF, HALF), 0)
    s = lax.broadcasted_iota(jnp.int32, (HALF, HALF), 1)
    lv = 31 - lax.clz(r ^ s)
    lv = jnp.where(r == s, N_LEVEL, lv)
    return jnp.where(r >= s, lv, -1), jnp.where(r <= s, lv, -1)


def _hgrn_gates(q, fpre, lb, onemlb, tril, backward):
    u = jnp.exp2(_neg_abs(fpre) * LOG2E)
    r = 1.0 / (1.0 + u)
    pos_side = fpre >= 0.0
    kk = onemlb * (jnp.where(pos_side, u, 1.0) * r)
    f = lb + onemlb * (jnp.where(pos_side, 1.0, u) * r)
    lf2 = jnp.maximum(jnp.log2(f), LOG2_TINY)
    cum = _cumsum_rows(tril, lf2)
    pos = (cum - lf2) if backward else cum
    return dict(q=q, kk=kk, qb=q.astype(BF16), kb=kk.astype(BF16), cum=cum, pos=pos, f=f)


def _hgrn_levels(p, lvl, sgn_ref, backward):
    qb, kb, cum, pos = p["qb"], p["kb"], p["cum"], p["pos"]

    def sides(w):
        if w == 1:
            odd = (lax.broadcasted_iota(jnp.int32, (CH, LANES), 0) & 1) == 1
            e = (jnp.where(odd, 1.0, p["f"]) if backward else jnp.where(odd, p["f"], 1.0)).astype(BF16)
            return qb * e, kb * e
        e = jnp.exp2((pos - _block_ref_rows(cum, w)) * sgn_ref[N_LEVEL - int(math.log2(w))]).astype(BF16)
        return qb * e, kb * e

    zq, zk = sides(HALF)
    if backward:
        cross = _dot_nt(zq[0:HALF], zk[HALF:CH])
    else:
        cross = _dot_nt(zq[HALF:CH], zk[0:HALF])
    halves = (slice(0, HALF), slice(HALF, CH))
    prods = [_dot_nt(qb[rows], kb[rows]) for rows in halves]
    zq, zk = sides(HALF // 2)
    diag = [jnp.where(lvl == N_LEVEL, prods[b], 0.0) for b in range(2)]
    w = HALF // 2
    while w >= 1:
        prods = [_dot_nt(zq[rows], zk[rows]) for rows in halves]
        if w > 1:
            zq, zk = sides(w // 2)
        lv = int(math.log2(w))
        diag = [jnp.where(lvl == lv, prods[b], diag[b]) for b in range(2)]
        w //= 2
    return diag[0].astype(BF16), diag[1].astype(BF16), cross.astype(BF16)


def _hgrn_finish(p, att, v, s_t, backward, has_init):
    q, kk, cum, pos = p["q"], p["kk"], p["cum"], p["pos"]
    d0, d1, cr = att
    if backward:
        o_lo = _dot(jnp.concatenate([d0, cr], axis=1), v)
        o_hi = _dot(d1, v[HALF:CH])
    else:
        o_lo = _dot(d0, v[0:HALF])
        o_hi = _dot(jnp.concatenate([cr, d1], axis=1), v)
    o = jnp.concatenate([o_lo, o_hi], axis=0)
    c_last = cum[CH - 1:CH, :]
    k_out = kk * jnp.exp2((pos) if backward else (c_last - cum))
    s_new = _dot_tn(v, k_out.astype(BF16))
    if has_init:
        q_in = q * jnp.exp2((c_last - pos) if backward else cum)
        o = o + _dot_nt(q_in.astype(BF16), s_t.astype(BF16))
        s_new = s_new + s_t * jnp.exp2(c_last)
    return o, s_new


def _hgrn_kernel(q_ref, v_ref, g_ref, f_ref, lbp_ref, ng_ref, sgn_ref, *rest, seq_len, nsq, is_ctx):
    if is_ctx:
        y_ref, sfin_ref, oacc, st = rest
    else:
        s0_ref, y_ref, oacc, st = rest
    nchunk = seq_len // CH
    hpi = oacc.shape[0]
    tril = _tri(CH, True)
    lvl_f, lvl_b = _level_ids()
    units = [(hh, sq) for hh in range(hpi) for sq in range(nsq)]

    def head_group(hg, carry):
        head = lambda hh: hg * hpi + hh
        slot = lambda hh, sq, d: (hh * nsq + sq) * 2 + d
        for hh, sq in units:
            base = sq * seq_len
            if not is_ctx:
                for d in range(2):
                    st[slot(hh, sq, d)] = s0_ref[sq, d, head(hh)].T
                if nchunk > 1:
                    oacc[hh, base:base + seq_len, :] = jnp.zeros((seq_len, LANES), F32)

        def gates(hh, rows, d, q=None):
            h = head(hh)
            lrow = pl.ds(d * HG_HEADS + h, 1)
            q = _silu(q_ref[h, rows, :].astype(F32)) if q is None else q
            return _hgrn_gates(q, f_ref[d * HG_HEADS + h, rows, :], lbp_ref[0, lrow, :], lbp_ref[1, lrow, :],
                               tril, d == 1)

        def levels(p, d):
            return _hgrn_levels(p, lvl_b if d else lvl_f, sgn_ref, d == 1)

        def finish(p, att, hh, rows, d, sq, first):
            o, s_new = _hgrn_finish(p, att, v_ref[head(hh), rows, :], None if is_ctx else st[slot(hh, sq, d)],
                                    d == 1, not is_ctx)
            st[slot(hh, sq, d)] = s_new
            if first:
                oacc[hh, rows, :] = o
            else:
                oacc[hh, rows, :] += o

        if nchunk == 1:
            work = [(hh, sq, d, _chunk_rows(0, sq * seq_len)) for hh, sq in units for d in range(2)]
            qs = {(hh, sq): _silu(q_ref[head(hh), _chunk_rows(0, sq * seq_len), :].astype(F32)) for hh, sq in units}
            ps = [gates(hh, rows, d, qs[hh, sq]) for hh, sq, d, rows in work]
            atts = [levels(p, d) for (_, _, d, _), p in zip(work, ps)]
            for (hh, sq, d, rows), p, att in zip(work, ps, atts):
                finish(p, att, hh, rows, d, sq, d == 0)
        else:
            def body(ci, carry):
                work = [(hh, sq, d, _chunk_rows(ci if d == 0 else nchunk - 1 - ci, sq * seq_len))
                        for hh, sq in units for d in range(2)]
                ps = [gates(hh, rows, d) for hh, _, d, rows in work]
                atts = [levels(p, d) for (_, _, d, _), p in zip(work, ps)]
                for (hh, sq, d, rows), p, att in zip(work, ps, atts):
                    finish(p, att, hh, rows, d, sq, False)
                return carry
            lax.fori_loop(0, nchunk, body, 0)

        for hh, sq in units:
            h = head(hh)
            base = sq * seq_len
            o = oacc[hh, base:base + seq_len, :]
            y = o * lax.rsqrt(jnp.mean(o * o, -1, keepdims=True) + RMS_EPS) * ng_ref[pl.ds(h, 1), :]
            gate = _silu(g_ref[h, base:base + seq_len, :].astype(F32))
            y_ref[h, base:base + seq_len, :] = (y * gate).astype(y_ref.dtype)
            if is_ctx:
                for d in range(2):
                    sfin_ref[sq, d, h] = st[slot(hh, sq, d)].T
        return carry

    lax.fori_loop(0, HG_HEADS // hpi, head_group, 0)


def _seq_call(kern, name, in_specs, args, heads, st_specs, st_shapes, states, scratch, is_ctx, *, seq_len, nseq, nsq):
    in_specs, args = list(in_specs), list(args)
    n_used = len(args)
    aliases = {}
    y_spec = pl.BlockSpec((heads, nsq * seq_len, LANES), lambda i: (0, i, 0))
    y_shape = jax.ShapeDtypeStruct((heads, nseq * seq_len, LANES), BF16)
    if is_ctx:
        out_specs, out_shape = [y_spec] + list(st_specs), [y_shape] + list(st_shapes)
        for k, s in enumerate(states):
            in_specs.append(pl.BlockSpec(memory_space=pl.ANY))
            args.append(s)
            aliases[len(args) - 1] = 1 + k
    else:
        in_specs += list(st_specs)
        args += list(states)
        n_used = len(args)
        out_specs, out_shape = [y_spec], [y_shape]
    n_args = len(args)
    steps = nseq // nsq
    kw = dict(seq_len=seq_len, nsq=nsq)

    def body(*refs):
        kern(*refs[:n_used], *refs[n_args:], is_ctx=is_ctx, **kw)

    return pl.pallas_call(
        body, grid=(steps,), in_specs=in_specs, out_specs=out_specs, out_shape=out_shape,
        input_output_aliases=aliases, scratch_shapes=scratch, compiler_params=_cparams(("arbitrary",)),
        name=name + ("_ctx" if is_ctx else "_lat"),
    )(*args)


def _hgrn_call(ub, uf, lbp, ng, sgn, states, *, is_ctx, seq_len, row0, nseq, nsq, layer, depth):
    hpi = 1 if is_ctx else LAT_HEADS_PER_ITER
    rows = nsq * seq_len
    bo = row0 // rows
    nh = HG_HEADS
    in_specs = [pl.BlockSpec((nh, rows, LANES), lambda i: (0, bo + i, 0)),
                pl.BlockSpec((nh, rows, LANES), lambda i: (1, bo + i, 0)),
                pl.BlockSpec((nh, rows, LANES), lambda i: (2, bo + i, 0)),
                pl.BlockSpec((2 * nh, rows, LANES), lambda i: (0, bo + i, 0)),
                pl.BlockSpec((2, 2 * nh, LANES), lambda i: (0, 0, 0)),
                pl.BlockSpec((nh, LANES), lambda i: (0, 0)),
                pl.BlockSpec((N_LEVEL + 1, CH, LANES), lambda i: (0, 0, 0))]
    st_spec, st_shape = _state_io((2, nh, HG_DK, LANES), nseq, nsq, layer, depth)
    return _seq_call(
        _hgrn_kernel, "hgrn", in_specs, [ub, ub, ub, uf, lbp, ng, sgn], nh, [st_spec], [st_shape], states,
        [pltpu.VMEM((hpi, rows, LANES), F32), pltpu.VMEM((2 * nsq * hpi, LANES, HG_DK), F32)],
        is_ctx, seq_len=seq_len, nseq=nseq, nsq=nsq)


def _mlstm_prep(gt_ref, bias_ref, rows, ms_s, triu):
    gates = gt_ref[rows, :] + bias_ref[...]
    g_t = gates.T
    i_rows = g_t[0:8, :]
    lf_rows = _log_sigmoid(g_t[8:16, :])
    c_rows = _cumsum_lanes(lf_rows, triu)
    is_b = lax.broadcasted_iota(jnp.int32, (8, CH), 0) >= ML_HEADS
    is_b1 = is_b[:, 0:1]
    m_in = ms_s[:, 0:1]
    c_last = c_rows[:, CH - 1:CH]
    pos = jnp.where(is_b, c_rows - lf_rows, c_rows)
    u = jnp.where(is_b, pos + i_rows, i_rows - pos)
    nu = jnp.where(is_b1, c_last + m_in, m_in)
    mu = jnp.maximum(jnp.where(is_b, _cummax_lanes(u, True), _cummax_lanes(u, False)), nu)
    m_t = jnp.where(is_b, mu - pos, pos + mu)
    mu_end = jnp.where(is_b1, mu[:, 0:1], mu[:, CH - 1:CH])
    m_new = jnp.where(is_b1, mu[:, 0:1], m_t[:, CH - 1:CH])
    x_rows = jnp.concatenate([-mu, -m_t, u, jnp.zeros_like(u)], axis=0) * LOG2E
    return dict(lhs=_sel_lhs(x_rows), u_rows=x_rows[16:24, :], nu=nu * LOG2E, mu_end=mu_end * LOG2E, m_new=m_new)


def _mlstm_main(p, q_ref, k_ref, v_ref, sel_ref, rows, dirs, cn_s, ms_s, oacc, masks, has_init, first_write):
    lhs = p["lhs"]
    scale = ML_DK ** -0.5
    ones_blk = jnp.ones((CH, LANES), BF16)
    heads = range(ML_HEADS)
    hds = [(d, h, d * ML_HEADS + h) for h in heads for d in dirs]
    q = [q_ref[h, rows, :] for h in heads]
    kf = [k_ref[h, rows, :].astype(F32) * scale for h in heads]
    vaug = [jnp.concatenate([v_ref[h, rows, :], ones_blk], axis=1) for h in heads]
    qk = [_dot_nt(q[h], kf[h].astype(BF16)) for h in heads]
    b_mu = {hd: _dot(lhs, sel_ref[hd]) for _, _, hd in hds}
    b_m = {hd: _dot(lhs, sel_ref[8 + hd]) for _, _, hd in hds}
    k_t = [kf[h].T for h in heads]
    s = {}
    for d, h, hd in hds:
        xe = jnp.concatenate([b_mu[hd], b_mu[hd]], axis=1) + p["u_rows"][hd:hd + 1, :]
        s[hd] = (jnp.where(masks[d], jnp.exp2(xe), 0.0) * qk[h]).astype(BF16)
    num = {hd: _dot(s[hd], vaug[h]) for _, h, hd in hds}
    w_end = jnp.exp2(p["u_rows"] - p["mu_end"])
    upd = {hd: _dot((k_t[h] * w_end[hd:hd + 1, :]).astype(BF16), vaug[h]) for _, h, hd in hds}
    if has_init:
        for d, h, hd in hds:
            cn = cn_s[d, h]
            w_int = jnp.exp2(p["nu"][hd:hd + 1, :] + b_mu[hd])
            num[hd] = num[hd] + jnp.concatenate([w_int, w_int], axis=1) * _dot(q[h], cn.astype(BF16))
            upd[hd] = upd[hd] + jnp.exp2(p["nu"][hd:hd + 1, :] - p["mu_end"][hd:hd + 1, :]) * cn
    for d, h, hd in hds:
        cn_s[d, h] = upd[hd]
    for h in heads:
        hsum = None
        for d in dirs:
            hd = d * ML_HEADS + h
            den = jnp.maximum(jnp.abs(num[hd][:, LANES:]), jnp.exp2(b_m[hd]))
            hout = num[hd][:, :LANES] / den
            hsum = hout if hsum is None else hsum + hout
        if first_write:
            oacc[h, rows, :] = hsum
        else:
            oacc[h, rows, :] += hsum
    m_new = jnp.broadcast_to(p["m_new"], (8, LANES))
    if len(dirs) == 2:
        ms_s[...] = m_new
    else:
        row = lax.broadcasted_iota(jnp.int32, (8, LANES), 0)
        mine = (row >= ML_HEADS) if dirs[0] == 1 else (row < ML_HEADS)
        ms_s[...] = jnp.where(mine, m_new, ms_s[...])


def _mlstm_kernel(q_ref, k_ref, v_ref, og_ref, gt_ref, bias_ref, ng_ref, sel_ref, *rest, seq_len, nsq, is_ctx):
    if is_ctx:
        y_ref, cfin_ref, nfin_ref, mfin_ref, oacc, cn_s, ms_s = rest
    else:
        c0_ref, n0_ref, m0_ref, y_ref, oacc, cn_s, ms_s = rest
    nchunk = seq_len // CH
    triu = _tri(CH, False)
    rr = lax.broadcasted_iota(jnp.int32, (CH, CH), 0)
    ss = lax.broadcasted_iota(jnp.int32, (CH, CH), 1)
    masks = (ss <= rr, ss >= rr)
    def main(p, rows, dirs, sq):
        _mlstm_main(p, q_ref, k_ref, v_ref, sel_ref, rows, dirs, cn_s.at[sq], ms_s.at[sq], oacc, masks,
                    not is_ctx, nchunk == 1)

    for sq in range(nsq):
        base = sq * seq_len
        if is_ctx:
            ms_s[sq] = jnp.zeros((2 * ML_HEADS, LANES), F32)
        else:
            ms_s[sq] = m0_ref[sq]
            for d in range(2):
                for h in range(ML_HEADS):
                    hd = d * ML_HEADS + h
                    nb = jnp.broadcast_to(n0_ref[sq, hd:hd + 1, :], (ML_DK, LANES)).T
                    cn_s[sq, d, h] = jnp.concatenate([c0_ref[sq, d, h], nb], axis=1)
            if nchunk > 1:
                oacc[:, base:base + seq_len, :] = jnp.zeros((ML_HEADS, seq_len, LANES), F32)
    if nchunk == 1:
        preps = [_mlstm_prep(gt_ref, bias_ref, _chunk_rows(0, sq * seq_len), ms_s.at[sq], triu) for sq in range(nsq)]
        for sq in range(nsq):
            main(preps[sq], _chunk_rows(0, sq * seq_len), (0, 1), sq)
    else:
        def body(ci, carry):
            work = [(sq, d, _chunk_rows(ci if d == 0 else nchunk - 1 - ci, sq * seq_len))
                    for sq in range(nsq) for d in range(2)]
            ps = [_mlstm_prep(gt_ref, bias_ref, rows, ms_s.at[sq], triu) for sq, _, rows in work]
            for (sq, d, rows), p in zip(work, ps):
                main(p, rows, (d,), sq)
            return carry
        lax.fori_loop(0, nchunk, body, 0)
    for sq in range(nsq):
        base = sq * seq_len
        for h in range(ML_HEADS):
            o = oacc[h, base:base + seq_len, :]
            y = o * lax.rsqrt(jnp.mean(o * o, -1, keepdims=True) + RMS_EPS) * ng_ref[h:h + 1, :]
            gate = _sigmoid(og_ref[h, base:base + seq_len, :].astype(F32))
            y_ref[h, base:base + seq_len, :] = (y * gate).astype(y_ref.dtype)
        if is_ctx:
            for d in range(2):
                for h in range(ML_HEADS):
                    cn = cn_s[sq, d, h]
                    cfin_ref[sq, d, h] = cn[:, :LANES]
                    nfin_ref[sq, pl.ds(d * ML_HEADS + h, 1), :] = cn[:, LANES:].T[0:1, :]
            mfin_ref[sq] = ms_s[sq]


def _mlstm_call(ub, uf, bias_row, ng, sel, states, *, is_ctx, seq_len, row0, nseq, nsq, layer, depth):
    rows = nsq * seq_len
    bo = row0 // rows
    nh = ML_HEADS
    slab = lambda k: pl.BlockSpec((nh, rows, LANES), lambda i: (k, bo + i, 0))
    in_specs = [slab(3), slab(4), slab(5), slab(6),
                pl.BlockSpec((None, rows, LANES), lambda i: (GATE_SLAB, bo + i, 0)),
                pl.BlockSpec((1, LANES), lambda i: (0, 0)),
                pl.BlockSpec((nh, LANES), lambda i: (0, 0)),
                pl.BlockSpec((N_SEL, LANES, LANES), lambda i: (0, 0, 0))]
    c_spec, c_shape = _state_io((2, nh, ML_DK, LANES), nseq, nsq, layer, depth)
    v_spec, v_shape = _state_io((2 * nh, LANES), nseq, nsq, layer, depth)
    return _seq_call(
        _mlstm_kernel, "mlstm", in_specs, [ub, ub, ub, ub, uf, bias_row, ng, sel], nh,
        [c_spec, v_spec, v_spec], [c_shape, v_shape, v_shape], states,
        [pltpu.VMEM((nh, rows, LANES), F32), pltpu.VMEM((nsq, 2, nh, ML_DK, 2 * LANES), F32),
         pltpu.VMEM((nsq, 2 * nh, LANES), F32)],
        is_ctx, seq_len=seq_len, nseq=nseq, nsq=nsq)


N_PAIR = MB_HEADS // 2
PAIRS_PER_GROUP = N_PAIR // MB_GROUPS


def _ssd_prep(gt_ref, bias_ref, nega_ref, rows, triu):
    gates = gt_ref[rows, :] + bias_ref[...]
    g_t = gates.T
    dt_rows = _softplus(g_t[16:32, :])
    la_rows = dt_rows * nega_ref[...]
    c_rows = _cumsum_lanes(la_rows, triu)
    ldt = jnp.log(dt_rows)
    is_b = lax.broadcasted_iota(jnp.int32, (16, CH), 0) >= MB_HEADS
    pos = jnp.where(is_b, c_rows - la_rows, c_rows)
    a_col = jnp.where(is_b, -pos, pos) * LOG2E
    r_row = jnp.where(is_b, pos + ldt, ldt - pos) * LOG2E
    c_last = c_rows[:, CH - 1:CH] * LOG2E
    lhs = _sel_lhs(jnp.concatenate([a_col, jnp.zeros_like(a_col)], axis=0))
    return dict(lhs=lhs, r_row=r_row, c_last=c_last)


def _ssd_main(p, xs_s, bcs_s, sel_ref, rows, dirs, hst, yacc, masks, has_init, first_write):
    lhs, r_row, c_last = p["lhs"], p["r_row"], p["c_last"]
    lane = lax.broadcasted_iota(jnp.int32, (CH, LANES), 1)
    lo = lane < MB_HEADDIM
    lo_state = lax.broadcasted_iota(jnp.int32, (LANES, LANES), 1) < MB_HEADDIM
    lo_row = lo_state[0:1, :]
    bblk = bcs_s[rows, 0:LANES]
    cblk = bcs_s[rows, LANES:2 * LANES]
    c_g, b_g, gm = [], [], []
    for grp in range(MB_GROUPS):
        in_grp = (lane >= grp * MB_DSTATE) & (lane < (grp + 1) * MB_DSTATE)
        c_g.append(jnp.where(in_grp, cblk, 0.0))
        b_g.append(jnp.where(in_grp, bblk, 0.0))
        gm.append(_dot_nt(c_g[grp].astype(BF16), b_g[grp].astype(BF16)))
    xh = []
    for j in range(N_PAIR):
        xpair = xs_s[rows, j * LANES:(j + 1) * LANES]
        xh.append((jnp.where(lo, xpair, 0.0).astype(BF16), jnp.where(lo, 0.0, xpair).astype(BF16)))
    items = [(d, j, half, d * MB_HEADS + 2 * j + half) for j in range(N_PAIR) for d in dirs for half in range(2)]
    b_a = {hd: _dot(lhs, sel_ref[hd]) for _, _, _, hd in items}
    b_t = [b_g[grp].T for grp in range(MB_GROUPS)]
    is_f = lax.broadcasted_iota(jnp.int32, (2 * MB_HEADS, 1), 0) < MB_HEADS
    w_end = jnp.exp2(r_row + jnp.where(is_f, c_last, 0.0))
    m, b_out, c_in = {}, {}, {}
    for d, j, half, hd in items:
        grp = j // PAIRS_PER_GROUP
        cl = c_last[hd:hd + 1, :]
        xe = jnp.concatenate([b_a[hd], b_a[hd]], axis=1) + r_row[hd:hd + 1, :]
        m[hd] = (jnp.where(masks[d], jnp.exp2(xe), 0.0) * gm[grp]).astype(BF16)
        b_out[hd] = (b_t[grp] * w_end[hd:hd + 1, :]).astype(BF16)
        if has_init:
            c_in[hd] = (c_g[grp] * jnp.exp2(b_a[hd] + cl if d else b_a[hd])).astype(BF16)
    yy = {hd: _dot(m[hd], xh[j][half]) for _, j, half, hd in items}
    uu = {hd: _dot(b_out[hd], xh[j][half]) for _, j, half, hd in items}
    for j in range(N_PAIR):
        ysum = None
        for d in dirs:
            hd0, hd1 = d * MB_HEADS + 2 * j, d * MB_HEADS + 2 * j + 1
            y = yy[hd0] + yy[hd1]
            upd = uu[hd0] + uu[hd1]
            if has_init:
                ht = hst[d, j]
                y = y + _dot(c_in[hd0], jnp.where(lo_state, ht, 0.0).astype(BF16))
                y = y + _dot(c_in[hd1], jnp.where(lo_state, 0.0, ht).astype(BF16))
                decay = jnp.where(lo_row, jnp.exp2(c_last[hd0:hd0 + 1, :]), jnp.exp2(c_last[hd1:hd1 + 1, :]))
                upd = upd + ht * decay
            hst[d, j] = upd
            ysum = y if ysum is None else ysum + y
        if first_write:
            yacc[rows, j * LANES:(j + 1) * LANES] = ysum
        else:
            yacc[rows, j * LANES:(j + 1) * LANES] += ysum


def _ssd_kernel(x_ref, bc_ref, z_ref, gt_ref, cw_ref, bias_ref, nega_ref, dskip_ref, ng_ref, sel_ref, *rest,
                seq_len, nsq, is_ctx):
    if is_ctx:
        y_ref, hfin_ref, xs_s, bcs_s, yacc, hst = rest
    else:
        h0_ref, y_ref, xs_s, bcs_s, yacc, hst = rest
    nchunk = seq_len // CH
    triu = _tri(CH, False)
    rr = lax.broadcasted_iota(jnp.int32, (CH, CH), 0)
    ss = lax.broadcasted_iota(jnp.int32, (CH, CH), 1)
    masks = (ss <= rr, ss >= rr)

    def conv(v, lo_col):
        row = lax.broadcasted_iota(jnp.int32, v.shape, 0)
        prev = jnp.where(row == 0, 0.0, pltpu.roll(v, 1, 0))
        nxt = jnp.where(row == seq_len - 1, 0.0, pltpu.roll(v, seq_len - 1, 0))
        cs = slice(lo_col, lo_col + v.shape[1])
        return _silu(cw_ref[0:1, cs] * prev + cw_ref[1:2, cs] * v + cw_ref[2:3, cs] * nxt + cw_ref[3:4, cs])

    for sq in range(nsq):
        base = sq * seq_len
        seq_rows = slice(base, base + seq_len)
        for k in range(MIX_W // LANES):
            xs_s[seq_rows, k * LANES:(k + 1) * LANES] = conv(x_ref[k, seq_rows, :], k * LANES)
        for k in range(2):
            bcs_s[seq_rows, k * LANES:(k + 1) * LANES] = conv(bc_ref[k, seq_rows, :], MIX_W + k * LANES)
        if not is_ctx:
            hst[sq] = h0_ref[sq]
            if nchunk > 1:
                yacc[seq_rows, :] = jnp.zeros((seq_len, MIX_W), F32)

    def prep(rows):
        return _ssd_prep(gt_ref, bias_ref, nega_ref, rows, triu)

    def main(p, rows, dirs, sq):
        _ssd_main(p, xs_s, bcs_s, sel_ref, rows, dirs, hst.at[sq], yacc, masks, not is_ctx, nchunk == 1)

    if nchunk == 1:
        preps = [prep(_chunk_rows(0, sq * seq_len)) for sq in range(nsq)]
        for sq in range(nsq):
            main(preps[sq], _chunk_rows(0, sq * seq_len), (0, 1), sq)
    else:
        def body(ci, carry):
            work = [(sq, d, _chunk_rows(ci if d == 0 else nchunk - 1 - ci, sq * seq_len))
                    for sq in range(nsq) for d in range(2)]
            ps = [prep(rows) for _, _, rows in work]
            for (sq, d, rows), p in zip(work, ps):
                main(p, rows, (d,), sq)
            return carry
        lax.fori_loop(0, nchunk, body, 0)

    for sq in range(nsq):
        base = sq * seq_len
        seq_rows = slice(base, base + seq_len)
        z = jnp.concatenate([z_ref[k, seq_rows, :] for k in range(MIX_W // LANES)], axis=1).astype(F32)
        y = (yacc[seq_rows, :] + dskip_ref[...] * xs_s[seq_rows, :]) * _silu(z)
        y = (y * lax.rsqrt(jnp.mean(y * y, -1, keepdims=True) + RMS_EPS) * ng_ref[...]).astype(y_ref.dtype)
        for k in range(MIX_W // LANES):
            y_ref[k, seq_rows, :] = y[:, k * LANES:(k + 1) * LANES]
        if is_ctx:
            hfin_ref[sq] = hst[sq]


def _ssd_call(ub, uf, cw, bias_row, nega_rows, dskip, ng, sel, states, *, is_ctx, seq_len, row0, nseq, nsq, layer,
              depth):
    rows = nsq * seq_len
    bo = row0 // rows
    ns = MIX_W // LANES
    row_spec = lambda w: pl.BlockSpec((1, w), lambda i: (0, 0))
    in_specs = [pl.BlockSpec((ns, rows, LANES), lambda i: (2, bo + i, 0)),
                pl.BlockSpec((2, rows, LANES), lambda i: (6, bo + i, 0)),
                pl.BlockSpec((ns, rows, LANES), lambda i: (7, bo + i, 0)),
                pl.BlockSpec((None, rows, LANES), lambda i: (GATE_SLAB, bo + i, 0)),
                pl.BlockSpec((8, MB_XBC), lambda i: (0, 0)),
                row_spec(LANES),
                pl.BlockSpec((2 * MB_HEADS, CH), lambda i: (0, 0)),
                row_spec(MIX_W), row_spec(MIX_W),
                pl.BlockSpec((N_SEL, LANES, LANES), lambda i: (0, 0, 0))]
    st_spec, st_shape = _state_io((2, N_PAIR, LANES, LANES), nseq, nsq, layer, depth)
    return _seq_call(
        _ssd_kernel, "ssd", in_specs, [uf, uf, ub, uf, cw, bias_row, nega_rows, dskip, ng, sel], ns,
        [st_spec], [st_shape], states,
        [pltpu.VMEM((rows, MIX_W), F32), pltpu.VMEM((rows, 2 * LANES), F32), pltpu.VMEM((rows, MIX_W), F32),
         pltpu.VMEM((nsq, 2, N_PAIR, LANES, LANES), F32)],
        is_ctx, seq_len=seq_len, nseq=nseq, nsq=nsq)


def _ssd_state_to_pairs(s):
    bsz = s.shape[0]
    st = jnp.swapaxes(s, -1, -2).reshape(bsz, 2, N_PAIR, 2, MB_DSTATE, MB_HEADDIM)
    st = jnp.moveaxis(st, 3, 4).reshape(bsz, 2, N_PAIR, MB_DSTATE, 2 * MB_HEADDIM)
    zero = jnp.zeros_like(st)
    grp = (jnp.arange(N_PAIR) // PAIRS_PER_GROUP).reshape(1, 1, N_PAIR, 1, 1)
    return jnp.concatenate([jnp.where(grp == 0, st, zero), jnp.where(grp == 1, st, zero)], axis=3)


def _ssd_pairs_to_state(hp):
    bsz = hp.shape[0]
    halves = hp.reshape(bsz, 2, N_PAIR, MB_GROUPS, MB_DSTATE, 2 * MB_HEADDIM)
    grp = (jnp.arange(N_PAIR) // PAIRS_PER_GROUP).reshape(1, 1, N_PAIR, 1, 1)
    st = jnp.where(grp == 0, halves[:, :, :, 0], halves[:, :, :, 1])
    st = st.reshape(bsz, 2, N_PAIR, MB_DSTATE, 2, MB_HEADDIM)
    st = jnp.moveaxis(st, 4, 3).reshape(bsz, 2, MB_HEADS, MB_DSTATE, MB_HEADDIM)
    return jnp.swapaxes(st, -1, -2)


def _grid_pos_embed(n_tok, d_model):
    rows = n_tok // GRID_W
    r, cidx = jnp.meshgrid(jnp.arange(rows, dtype=F32), jnp.arange(GRID_W, dtype=F32), indexing='ij')
    quarter = d_model // 4
    freq = jnp.exp(-math.log(10000.0) * jnp.arange(quarter, dtype=F32) / quarter)
    ar = r.reshape(-1, 1) * freq
    ac = cidx.reshape(-1, 1) * freq
    return jnp.concatenate([jnp.sin(ar), jnp.cos(ar), jnp.sin(ac), jnp.cos(ac)], axis=-1)


def _gate_row(pieces):
    v = jnp.concatenate([p.reshape(-1).astype(F32) for p in pieces])
    return jnp.concatenate([v, v, v, jnp.zeros((LANES - 3 * N_SEL,), F32)]).reshape(1, LANES)


def kernel(x_prompt, x_sample, state_hgrn, state_mlstm_C, state_mlstm_n, state_mlstm_m, state_ssd, c, c_ctx,
           w_mod, b_mod, ln_g, ln_b, ffn_w_gu, ffn_w_down, w_in, hg_lb, hg_norm_g, ml_gate_b, ml_norm_g,
           mb_conv_w, mb_conv_b, mb_dt_bias, mb_a_log, mb_d, mb_norm_g, w_branch, w_out):
    bsz, seq, d = x_prompt.shape
    dbsz, dseq, _ = x_sample.shape
    depth = w_mod.shape[0]
    t_ctx = bsz * seq
    n_ctx_tiles = t_ctx // ROW_TILE
    assert dseq == ROW_TILE and t_ctx % ROW_TILE == 0 and seq == CH and dseq % CH == 0

    xs0 = x_sample + _grid_pos_embed(dseq, d).astype(x_sample.dtype)[None]
    x = jnp.concatenate([x_prompt.reshape(t_ctx, d), xs0.reshape(dbsz * dseq, d)], axis=0)

    cv = jnp.concatenate([c_ctx[None]] * MOD_LAT_ROW + [c, jnp.zeros((8 - MOD_LAT_ROW - dbsz, d), F32)], axis=0)
    mod4 = _mod_call(cv, w_mod, b_mod).reshape(depth, 8, 1, N_MOD * d)

    w_branch_b = w_branch.astype(BF16)
    w_out_b = w_out.astype(BF16)

    tail0 = 4608
    w_tail = w_in[:, :, tail0:].astype(BF16)

    def regroup(width, pieces):
        out = jnp.zeros((depth, d, width), BF16)
        at = 0
        for lo, hi in pieces:
            piece = w_tail[:, :, lo - tail0:hi - tail0] if lo >= tail0 else w_in[:, :, lo:hi].astype(BF16)
            out = lax.dynamic_update_slice(out, piece, (0, 0, at))
            at += hi - lo
        return out, at

    w_a, _ = regroup(UB_SLABS * LANES, [(0, 1536), (2560, 4608), (4624, 5136), (5920, 8992)])
    gate_cols = [(4608, 4624), (5904, 5920)]
    w_b, _ = regroup(UF_SLABS * LANES, [(1536, 2560), (5136, 5904)] + gate_cols * 3)
    ln_g4 = ln_g.reshape(depth, 3, 1, d)
    ln_b4 = ln_b.reshape(depth, 3, 1, d)

    lbs = jnp.cumsum(jax.nn.softmax(hg_lb.astype(F32), axis=0), axis=0)
    lbs = (lbs - lbs[0]).reshape(depth, 2 * HG_HEADS, HG_DK)
    lbp = jnp.stack([lbs, 1.0 - lbs], axis=1)
    cw = jnp.concatenate([mb_conv_w, mb_conv_b[:, None, :], jnp.zeros((depth, 4, MB_XBC), F32)], axis=1)
    sel = _selectors()

    lat_h0 = _ssd_state_to_pairs(state_ssd.reshape((dbsz * depth,) + state_ssd.shape[2:])).reshape(
        (dbsz, depth, 2, N_PAIR, LANES, LANES))
    lat_n0 = state_mlstm_n.reshape(dbsz, depth, 2 * ML_HEADS, ML_DK)
    lat_m0 = jnp.broadcast_to(state_mlstm_m.reshape(dbsz, depth, 2 * ML_HEADS, 1), (dbsz, depth, 2 * ML_HEADS, LANES))

    zeros = lambda *s: jnp.zeros((bsz, depth) + s, F32)
    st_hg = [zeros(2, HG_HEADS, HG_DK, LANES)]
    st_ml = [zeros(2, ML_HEADS, ML_DK, LANES), zeros(2 * ML_HEADS, LANES), zeros(2 * ML_HEADS, LANES)]
    st_ss = [zeros(2, N_PAIR, LANES, LANES)]
    t_idx = jnp.arange(CH).reshape(1, CH, 1)
    widths = (HALF >> jnp.arange(N_LEVEL + 1)).reshape(N_LEVEL + 1, 1, 1)
    sgn = jnp.broadcast_to(jnp.where((t_idx & widths) != 0, 1.0, -1.0), (N_LEVEL + 1, CH, LANES)).astype(F32)
    ctx = dict(is_ctx=True, seq_len=seq, row0=0, nseq=bsz, depth=depth)
    lat = dict(is_ctx=False, seq_len=dseq, row0=t_ctx, nseq=dbsz, depth=depth)
    for l in range(depth):
        x = _ffn_call(x, mod4, ffn_w_gu, ffn_w_down, ln_g4, ln_b4, l, 0, n_ctx_tiles)
        ub = _inproj_call(x, mod4, w_a, l, INPROJ_TILE_A, BF16, n_ctx_tiles, "inproj_a")
        uf = _inproj_call(x, mod4, w_b, l, INPROJ_TILE_B, F32, n_ctx_tiles, "inproj_b")

        ng_h = hg_norm_g[l].reshape(HG_HEADS, LANES)
        y_hg, *st_hg = _hgrn_call(ub, uf, lbp[l], ng_h, sgn, st_hg, layer=l, nsq=CTX_SEQS_PER_STEP, **ctx)
        y_hg_lat, = _hgrn_call(ub, uf, lbp[l], ng_h, sgn, [state_hgrn], layer=l, nsq=1, **lat)

        gate_bias = _gate_row([ml_gate_b[l, 0], ml_gate_b[l, 1], mb_dt_bias[l]])
        ng_m = ml_norm_g[l].reshape(ML_HEADS, LANES)
        y_ml, *st_ml = _mlstm_call(ub, uf, gate_bias, ng_m, sel, st_ml, layer=l, nsq=CTX_SEQS_PER_STEP_SCALAR, **ctx)
        y_ml_lat, = _mlstm_call(ub, uf, gate_bias, ng_m, sel, [state_mlstm_C, lat_n0, lat_m0], layer=l, nsq=LAT_SEQS_PER_STEP_SCALAR, **lat)

        nega = jnp.broadcast_to(-jnp.exp(mb_a_log[l].astype(F32)).reshape(2 * MB_HEADS, 1), (2 * MB_HEADS, CH))
        dskip = jnp.repeat(mb_d[l], MB_HEADDIM).reshape(1, MIX_W)
        ng_s = mb_norm_g[l].reshape(1, MIX_W)
        y_mb, *st_ss = _ssd_call(ub, uf, cw[l], gate_bias, nega, dskip, ng_s, sel, st_ss, layer=l, nsq=CTX_SEQS_PER_STEP_SCALAR, **ctx)
        y_mb_lat, = _ssd_call(ub, uf, cw[l], gate_bias, nega, dskip, ng_s, sel, [lat_h0], layer=l, nsq=LAT_SEQS_PER_STEP_SCALAR, **lat)

        x = _mixout_call(x, mod4, (y_hg, y_ml, y_mb), (y_hg_lat, y_ml_lat, y_mb_lat), ub, w_branch_b, w_out_b,
                         ln_g4, ln_b4, l, n_ctx_tiles)
        x = _ffn_call(x, mod4, ffn_w_gu, ffn_w_down, ln_g4, ln_b4, l, 1, n_ctx_tiles)

    y_prompt = x[:t_ctx].reshape(bsz, seq, d)
    y_sample = x[t_ctx:].reshape(dbsz, dseq, d)
    c_fin, n_fin, m_fin = st_ml
    h_fin = _ssd_pairs_to_state(st_ss[0].reshape((bsz * depth,) + st_ss[0].shape[2:]))
    return (y_prompt, y_sample, st_hg[0], c_fin, n_fin.reshape(bsz, depth, 2, ML_HEADS, ML_DK),
            m_fin[:, :, :, 0].reshape(bsz, depth, 2, ML_HEADS),
            h_fin.reshape((bsz, depth) + h_fin.shape[1:]))
```

```python
import functools
import math

import jax
import jax.numpy as jnp
from jax import lax
from jax.experimental import pallas as pl
from jax.experimental.pallas import tpu as pltpu

F32 = jnp.float32
BF16 = jnp.bfloat16

D_MODEL = 1024
DEPTH = 4
GRID_W = 64
MIX_W = 512
HG_HEADS = 4
HG_DK = 128
ML_HEADS = 4
ML_DK = 128
MB_HEADS = 8
MB_HEADDIM = 64
MB_GROUPS = 2
MB_DSTATE = 64
MB_XBC = MIX_W + 2 * MB_GROUPS * MB_DSTATE
D_FF = 2816
N_MOD = 9
ALPHA = (2 * DEPTH) ** 0.25
LN_EPS = 1e-5
RMS_EPS = 1e-6
LOG2E = 1.4426950408889634

LANES = 128
CH = 256
HALF = CH // 2
ROW_TILE = 1024
FFN_ROWS = 2048
MOD_LAT_ROW = 2
FF_TILE = 256
INPROJ_TILE_A = 1792
INPROJ_TILE_B = 1920
LAT_HEADS_PER_ITER = 2
CTX_SEQS_PER_STEP_SCALAR = 4
LAT_SEQS_PER_STEP_SCALAR = 2
CTX_SEQS_PER_STEP = 4
VMEM_LIMIT = 56 * 1024 * 1024

UB_SLABS = 56
UF_SLABS = 15
GATE_SLAB = 14
N_SEL = 32
LOG2_TINY = -150.0


def _dot(a, b):
    return jnp.dot(a, b, preferred_element_type=F32)


def _dot_nt(a, b):
    return lax.dot_general(a, b, (((1,), (1,)), ((), ())), preferred_element_type=F32)


def _dot_tn(a, b):
    return lax.dot_general(a, b, (((0,), (0,)), ((), ())), preferred_element_type=F32)


def _sigmoid(x):
    return 1.0 / (1.0 + jnp.exp(-x))


def _silu(x):
    return x * _sigmoid(x)


def _log_sigmoid(x):
    return jnp.minimum(x, 0.0) - jnp.log(1.0 + jnp.exp(-jnp.abs(x)))


def _softplus(x):
    return jnp.maximum(x, 0.0) + jnp.log(1.0 + jnp.exp(-jnp.abs(x)))


def _neg_abs(x):
    return -jnp.abs(x)


def _ln(z, g, b):
    mu = jnp.mean(z, -1, keepdims=True)
    d = z - mu
    var = jnp.mean(d * d, -1, keepdims=True)
    return d * lax.rsqrt(var + LN_EPS) * g + b


def _split3(x):
    h = x.astype(BF16)
    r = x - h.astype(F32)
    m = r.astype(BF16)
    l = (r - m.astype(F32)).astype(BF16)
    return h, m, l


def _cumsum_rows(tril, x):
    h, m, l = _split3(x)
    return _dot(tril, h) + _dot(tril, m) + _dot(tril, l)


def _cumsum_lanes(x, triu):
    h, m, l = _split3(x)
    return _dot(h, triu) + _dot(m, triu) + _dot(l, triu)


def _cummax_lanes(x, reverse):
    n = x.shape[1]
    lane = lax.broadcasted_iota(jnp.int32, x.shape, 1)
    k = 1
    while k < n:
        if reverse:
            sh = jnp.where(lane < n - k, pltpu.roll(x, n - k, 1), -jnp.inf)
        else:
            sh = jnp.where(lane >= k, pltpu.roll(x, k, 1), -jnp.inf)
        x = jnp.maximum(x, sh)
        k *= 2
    return x


def _tri(c, lower):
    r = lax.broadcasted_iota(jnp.int32, (c, c), 0)
    s = lax.broadcasted_iota(jnp.int32, (c, c), 1)
    return jnp.where((s <= r) if lower else (r <= s), 1.0, 0.0).astype(BF16)


def _sel_lhs(x_rows):
    h = x_rows.astype(BF16).astype(F32)
    r = x_rows - h
    m = r.astype(BF16).astype(F32)
    rows = jnp.concatenate([h, m, r - m, jnp.ones_like(x_rows)], axis=0)
    return rows.T.astype(BF16)


def _selectors():
    r = jnp.arange(LANES)
    ch = jnp.arange(N_SEL)
    hit = ((r[None, :] % N_SEL) == ch[:, None]) & (r[None, :] < 3 * N_SEL)
    return jnp.broadcast_to(hit[:, :, None], (N_SEL, LANES, LANES)).astype(BF16)


def _cparams(sem):
    return pltpu.CompilerParams(dimension_semantics=sem, vmem_limit_bytes=VMEM_LIMIT)


def _chunk_rows(ci, base=0):
    r0 = ci * CH + base
    if not isinstance(r0, int):
        r0 = pl.multiple_of(r0, CH)
    return pl.ds(r0, CH)


def _state_io(shape_tail, nseq, nsq, layer, depth):
    nz = (0,) * len(shape_tail)
    spec = pl.BlockSpec((nsq, None) + shape_tail, lambda i: (i, layer) + nz)
    shape = jax.ShapeDtypeStruct((nseq, depth) + shape_tail, F32)
    return spec, shape


def _chunk_loop(nchunk, run):
    if nchunk == 1:
        run(0, (0, 1))
    else:
        def body(ci, carry):
            run(ci, (0,))
            run(nchunk - 1 - ci, (1,))
            return carry
        lax.fori_loop(0, nchunk, body, 0)


def _mod_kernel(c_ref, w_ref, b_ref, o_ref):
    a = _silu(c_ref[...]).astype(BF16)
    o_ref[...] = _dot(a, w_ref[...].astype(BF16)) + b_ref[...]


def _mod_call(cv, w_mod, b_mod):
    depth, d, e = w_mod.shape
    tn = 1024
    return pl.pallas_call(
        _mod_kernel,
        grid=(depth, e // tn),
        in_specs=[pl.BlockSpec((8, d), lambda l, j: (0, 0)),
                  pl.BlockSpec((None, d, tn), lambda l, j: (l, 0, j)),
                  pl.BlockSpec((None, 1, tn), lambda l, j: (l, 0, j))],
        out_specs=pl.BlockSpec((None, 8, tn), lambda l, j: (l, 0, j)),
        out_shape=jax.ShapeDtypeStruct((depth, 8, e), F32),
        compiler_params=_cparams(("arbitrary", "arbitrary")),
        name="mod",
    )(cv, w_mod, b_mod.reshape(depth, 1, e))


def _mod_row(i, n_ctx_tiles):
    return jnp.maximum(i - n_ctx_tiles + MOD_LAT_ROW, 0)


def _ffn_kernel(x_ref, mod_ref, wa_ref, wb_ref, wd_ref, g_ref, b_ref, o_ref, xm_s, *, nf):
    j = pl.program_id(1)
    d = x_ref.shape[1]
    groups = [(r, slice(r * ROW_TILE, (r + 1) * ROW_TILE)) for r in range(FFN_ROWS // ROW_TILE)]

    def ffn(xm):
        wa, wb, wd = wa_ref[...].astype(BF16), wb_ref[...].astype(BF16), wd_ref[...].astype(BF16)
        return _dot((_silu(_dot(xm, wa)) * _dot(xm, wb)).astype(BF16), wd)

    @pl.when(j == 0)
    def _():
        for r, rows in groups:
            sh = mod_ref[r:r + 1, 0:d]
            sc = mod_ref[r:r + 1, d:2 * d]
            xm = (x_ref[rows, :] * (1.0 + sc) + sh).astype(BF16)
            xm_s[rows, :] = xm
            o_ref[rows, :] = ffn(xm)

    @pl.when(jnp.logical_and(j > 0, j < nf - 1))
    def _():
        o_ref[...] += ffn(xm_s[...])

    @pl.when(j == nf - 1)
    def _():
        for r, rows in groups:
            gate = mod_ref[r:r + 1, 2 * d:3 * d]
            z = ALPHA * x_ref[rows, :] + 0.5 * gate * (o_ref[rows, :] + ffn(xm_s[rows, :]))
            o_ref[rows, :] = _ln(z, g_ref[...], b_ref[...])


def _ffn_call(x, mod4, w_gu, w_down, ln_g, ln_b, layer, which, n_ctx_tiles):
    t, d = x.shape
    nf = D_FF // FF_TILE
    per = FFN_ROWS // ROW_TILE
    assert nf >= 3 and nf * FF_TILE == D_FF and n_ctx_tiles % per == 0 and MOD_LAT_ROW % per == 0
    sub = 0 if which == 0 else 2
    mod_pairs = mod4.reshape(mod4.shape[0], mod4.shape[1] // per, per, mod4.shape[3])
    first_lat = n_ctx_tiles // per
    mod_blk = lambda i, j: (layer, jnp.maximum(i - first_lat + MOD_LAT_ROW // per, 0), 0, sub)
    return pl.pallas_call(
        functools.partial(_ffn_kernel, nf=nf),
        grid=(t // FFN_ROWS, nf),
        in_specs=[pl.BlockSpec((FFN_ROWS, d), lambda i, j: (i, 0)),
                  pl.BlockSpec((None, None, per, 3 * d), mod_blk),
                  pl.BlockSpec((None, None, d, FF_TILE), lambda i, j: (layer, which, 0, j)),
                  pl.BlockSpec((None, None, d, FF_TILE), lambda i, j: (layer, which, 0, j + nf)),
                  pl.BlockSpec((None, None, FF_TILE, d), lambda i, j: (layer, which, j, 0)),
                  pl.BlockSpec((None, None, 1, d), lambda i, j: (layer, sub, 0, 0)),
                  pl.BlockSpec((None, None, 1, d), lambda i, j: (layer, sub, 0, 0))],
        out_specs=pl.BlockSpec((FFN_ROWS, d), lambda i, j: (i, 0)),
        out_shape=jax.ShapeDtypeStruct((t, d), F32),
        scratch_shapes=[pltpu.VMEM((FFN_ROWS, d), BF16)],
        compiler_params=_cparams(("arbitrary", "arbitrary")),
        name=f"ffn{which}",
    )(x, mod_pairs, w_gu, w_gu, w_down, ln_g, ln_b)


W_IN_A = [(0, 1536), (2560, 4608), (4624, 5136), (5920, 8992)]
W_IN_B = [(1536, 2560), (5136, 5904)]
W_IN_GATES = [(4608, 4624), (5904, 5920)]
REGROUP_ROWS = 256


def _regroup_kernel(w_ref, a_ref, b_ref):
    w = w_ref[...]
    a_ref[...] = jnp.concatenate([w[:, lo:hi] for lo, hi in W_IN_A], axis=1).astype(BF16)
    gates = [w[:, lo:hi] for lo, hi in W_IN_GATES] * 3
    pad = jnp.zeros((w.shape[0], UF_SLABS * LANES - sum(hi - lo for lo, hi in W_IN_B) - 3 * N_SEL), F32)
    b_ref[...] = jnp.concatenate([w[:, lo:hi] for lo, hi in W_IN_B] + gates + [pad], axis=1).astype(BF16)


def _regroup_call(w_in):
    depth, d, n = w_in.shape
    wa, wb = UB_SLABS * LANES, UF_SLABS * LANES
    return pl.pallas_call(
        _regroup_kernel,
        grid=(depth, d // REGROUP_ROWS),
        in_specs=[pl.BlockSpec((None, REGROUP_ROWS, n), lambda l, i: (l, i, 0))],
        out_specs=[pl.BlockSpec((None, REGROUP_ROWS, wa), lambda l, i: (l, i, 0)),
                   pl.BlockSpec((None, REGROUP_ROWS, wb), lambda l, i: (l, i, 0))],
        out_shape=[jax.ShapeDtypeStruct((depth, d, wa), BF16), jax.ShapeDtypeStruct((depth, d, wb), BF16)],
        compiler_params=_cparams(("arbitrary", "arbitrary")),
        name="regroup",
    )(w_in)


def _inproj_kernel(x_ref, mod_ref, w_ref, o_ref, xm_s):
    d = x_ref.shape[1]

    @pl.when(pl.program_id(1) == 0)
    def _():
        sh = mod_ref[:, 0:d]
        sc = mod_ref[:, d:2 * d]
        xm_s[...] = (x_ref[...] * (1.0 + sc) + sh).astype(BF16)

    res = _dot(xm_s[...], w_ref[...]).astype(o_ref.dtype)
    for k in range(o_ref.shape[0]):
        o_ref[k] = res[:, k * LANES:(k + 1) * LANES]


def _inproj_call(x, mod4, w, layer, tn, out_dtype, n_ctx_tiles, name):
    t, d = x.shape
    n = w.shape[2]
    return pl.pallas_call(
        _inproj_kernel,
        grid=(t // ROW_TILE, n // tn),
        in_specs=[pl.BlockSpec((ROW_TILE, d), lambda i, j: (i, 0)),
                  pl.BlockSpec((None, None, 1, 3 * d), lambda i, j: (layer, _mod_row(i, n_ctx_tiles), 0, 1)),
                  pl.BlockSpec((None, d, tn), lambda i, j: (layer, 0, j))],
        out_specs=pl.BlockSpec((tn // LANES, ROW_TILE, LANES), lambda i, j: (j, i, 0)),
        out_shape=jax.ShapeDtypeStruct((n // LANES, t, LANES), out_dtype),
        scratch_shapes=[pltpu.VMEM((ROW_TILE, d), BF16)],
        compiler_params=_cparams(("arbitrary", "arbitrary")),
        name=name,
    )(x, mod4, w)


def _slabs(ref):
    return jnp.concatenate([ref[k] for k in range(ref.shape[0])], axis=1)


def _mixout_kernel(x_ref, mod_ref, yhc_ref, ymc_ref, ysc_ref, yhl_ref, yml_ref, ysl_ref, g0_ref, g1_ref, g2_ref,
                   wb_ref, wo_ref, lg_ref, lb_ref, o_ref, *, n_ctx_steps):
    d = x_ref.shape[1]
    is_lat = pl.program_id(0) >= n_ctx_steps

    def branch(c_ref, l_ref):
        return jnp.where(is_lat, _slabs(l_ref), _slabs(c_ref))

    p = _sigmoid(_slabs(g0_ref).astype(F32)) * _dot(branch(yhc_ref, yhl_ref), wb_ref[0])
    p += _sigmoid(_slabs(g1_ref).astype(F32)) * _dot(branch(ymc_ref, yml_ref), wb_ref[1])
    p += _sigmoid(_slabs(g2_ref).astype(F32)) * _dot(branch(ysc_ref, ysl_ref), wb_ref[2])
    y = _dot(p.astype(BF16), wo_ref[...])
    gate = mod_ref[:, 2 * d:3 * d]
    z = ALPHA * x_ref[...] + gate * y
    o_ref[...] = _ln(z, lg_ref[...], lb_ref[...])


def _mixout_call(x, mod4, y_ctx, y_lat, ub, w_branch, w_out, ln_g, ln_b, layer, n_ctx_tiles):
    t, d = x.shape
    tm = 512
    per = ROW_TILE // tm
    ns = MIX_W // LANES
    ng = d // LANES
    g0 = (UB_SLABS - 3 * ng) // ng
    n_ctx_steps = n_ctx_tiles * per
    n_lat_steps = t // tm - n_ctx_steps
    yc_spec = pl.BlockSpec((ns, tm, LANES), lambda i: (0, jnp.minimum(i, n_ctx_steps - 1), 0))
    yl_spec = pl.BlockSpec((ns, tm, LANES), lambda i: (0, jnp.clip(i - n_ctx_steps, 0, n_lat_steps - 1), 0))
    return pl.pallas_call(
        functools.partial(_mixout_kernel, n_ctx_steps=n_ctx_steps),
        grid=(t // tm,),
        in_specs=[pl.BlockSpec((tm, d), lambda i: (i, 0)),
                  pl.BlockSpec((None, None, 1, 3 * d), lambda i: (layer, _mod_row(i // per, n_ctx_tiles), 0, 1)),
                  yc_spec, yc_spec, yc_spec, yl_spec, yl_spec, yl_spec,
                  pl.BlockSpec((ng, tm, LANES), lambda i: (g0, i, 0)),
                  pl.BlockSpec((ng, tm, LANES), lambda i: (g0 + 1, i, 0)),
                  pl.BlockSpec((ng, tm, LANES), lambda i: (g0 + 2, i, 0)),
                  pl.BlockSpec((None, 3, MIX_W, d), lambda i: (layer, 0, 0, 0)),
                  pl.BlockSpec((None, d, d), lambda i: (layer, 0, 0)),
                  pl.BlockSpec((None, None, 1, d), lambda i: (layer, 1, 0, 0)),
                  pl.BlockSpec((None, None, 1, d), lambda i: (layer, 1, 0, 0))],
        out_specs=pl.BlockSpec((tm, d), lambda i: (i, 0)),
        out_shape=jax.ShapeDtypeStruct((t, d), F32),
        compiler_params=_cparams(("arbitrary",)),
        name="mixout",
    )(x, mod4, *y_ctx, *y_lat, ub, ub, ub, w_branch, w_out, ln_g, ln_b)


N_LEVEL = int(math.log2(HALF))


def _block_ref_rows(c, w):
    n_rows = c.shape[0]
    if 2 * w == n_rows:
        return jnp.broadcast_to(c[w - 1:w, :], c.shape)
    if w >= 4:
        n = n_rows // (2 * w)
        c3 = c.reshape(n, 2 * w, LANES)
        return jnp.broadcast_to(c3[:, w - 1:w, :], c3.shape).reshape(n_rows, LANES)
    c8 = c.reshape(n_rows // 8, 8, LANES)
    sub = lax.broadcasted_iota(jnp.int32, c8.shape, 1)
    if w == 2:
        r = jnp.where(sub < 4, c8[:, 1:2, :], c8[:, 5:6, :])
    else:
        r = jnp.where(sub < 2, c8[:, 0:1, :],
                      jnp.where(sub < 4, c8[:, 2:3, :], jnp.where(sub < 6, c8[:, 4:5, :], c8[:, 6:7, :])))
    return r.reshape(n_rows, LANES)


def _level_ids():
    r = lax.broadcasted_iota(jnp.int32, (HALF, HALF), 0)
    s = lax.broadcasted_iota(jnp.int32, (HALF, HALF), 1)
    lv = 31 - lax.clz(r ^ s)
    lv = jnp.where(r == s, N_LEVEL, lv)
    return jnp.where(r >= s, lv, -1), jnp.where(r <= s, lv, -1)


def _hgrn_gates(q, fpre, lb, onemlb, tril, backward):
    u = jnp.exp2(_neg_abs(fpre) * LOG2E)
    r = 1.0 / (1.0 + u)
    pos_side = fpre >= 0.0
    kk = onemlb * (jnp.where(pos_side, u, 1.0) * r)
    f = lb + onemlb * (jnp.where(pos_side, 1.0, u) * r)
    lf2 = jnp.maximum(jnp.log2(f), LOG2_TINY)
    cum = _cumsum_rows(tril, lf2)
    pos = (cum - lf2) if backward else cum
    return dict(q=q, kk=kk, qb=q.astype(BF16), kb=kk.astype(BF16), cum=cum, pos=pos, f=f)


def _hgrn_levels(p, lvl, sgn_ref, backward):
    qb, kb, cum, pos = p["qb"], p["kb"], p["cum"], p["pos"]

    def sides(w):
        if w == 1:
            odd = (lax.broadcasted_iota(jnp.int32, (CH, LANES), 0) & 1) == 1
            e = (jnp.where(odd, 1.0, p["f"]) if backward else jnp.where(odd, p["f"], 1.0)).astype(BF16)
            return qb * e, kb * e
        e = jnp.exp2((pos - _block_ref_rows(cum, w)) * sgn_ref[N_LEVEL - int(math.log2(w))]).astype(BF16)
        return qb * e, kb * e

    zq, zk = sides(HALF)
    if backward:
        cross = _dot_nt(zq[0:HALF], zk[HALF:CH])
    else:
        cross = _dot_nt(zq[HALF:CH], zk[0:HALF])
    halves = (slice(0, HALF), slice(HALF, CH))
    prods = [_dot_nt(qb[rows], kb[rows]) for rows in halves]
    zq, zk = sides(HALF // 2)
    diag = [jnp.where(lvl == N_LEVEL, prods[b], 0.0) for b in range(2)]
    w = HALF // 2
    while w >= 1:
        prods = [_dot_nt(zq[rows], zk[rows]) for rows in halves]
        if w > 1:
            zq, zk = sides(w // 2)
        lv = int(math.log2(w))
        diag = [jnp.where(lvl == lv, prods[b], diag[b]) for b in range(2)]
        w //= 2
    return diag[0].astype(BF16), diag[1].astype(BF16), cross.astype(BF16)


def _hgrn_finish(p, att, v, s_t, backward, has_init):
    q, kk, cum, pos = p["q"], p["kk"], p["cum"], p["pos"]
    d0, d1, cr = att
    if backward:
        o_lo = _dot(jnp.concatenate([d0, cr], axis=1), v)
        o_hi = _dot(d1, v[HALF:CH])
    else:
        o_lo = _dot(d0, v[0:HALF])
        o_hi = _dot(jnp.concatenate([cr, d1], axis=1), v)
    o = jnp.concatenate([o_lo, o_hi], axis=0)
    c_last = cum[CH - 1:CH, :]
    k_out = kk * jnp.exp2((pos) if backward else (c_last - cum))
    s_new = _dot_tn(v, k_out.astype(BF16))
    if has_init:
        q_in = q * jnp.exp2((c_last - pos) if backward else cum)
        o = o + _dot_nt(q_in.astype(BF16), s_t.astype(BF16))
        s_new = s_new + s_t * jnp.exp2(c_last)
    return o, s_new


def _hgrn_kernel(q_ref, v_ref, g_ref, f_ref, lbp_ref, ng_ref, sgn_ref, *rest, seq_len, nsq, is_ctx):
    if is_ctx:
        y_ref, sfin_ref, oacc, st = rest
    else:
        s0_ref, y_ref, oacc, st = rest
    nchunk = seq_len // CH
    hpi = oacc.shape[0]
    tril = _tri(CH, True)
    lvl_f, lvl_b = _level_ids()
    units = [(hh, sq) for hh in range(hpi) for sq in range(nsq)]

    def head_group(hg, carry):
        head = lambda hh: hg * hpi + hh
        slot = lambda hh, sq, d: (hh * nsq + sq) * 2 + d
        for hh, sq in units:
            base = sq * seq_len
            if not is_ctx:
                for d in range(2):
                    st[slot(hh, sq, d)] = s0_ref[sq, d, head(hh)].T
                if nchunk > 1:
                    oacc[hh, base:base + seq_len, :] = jnp.zeros((seq_len, LANES), F32)

        def gates(hh, rows, d, q=None):
            h = head(hh)
            lrow = pl.ds(d * HG_HEADS + h, 1)
            q = _silu(q_ref[h, rows, :].astype(F32)) if q is None else q
            return _hgrn_gates(q, f_ref[d * HG_HEADS + h, rows, :], lbp_ref[0, lrow, :], lbp_ref[1, lrow, :],
                               tril, d == 1)

        def levels(p, d):
            return _hgrn_levels(p, lvl_b if d else lvl_f, sgn_ref, d == 1)

        def finish(p, att, hh, rows, d, sq, first):
            o, s_new = _hgrn_finish(p, att, v_ref[head(hh), rows, :], None if is_ctx else st[slot(hh, sq, d)],
                                    d == 1, not is_ctx)
            st[slot(hh, sq, d)] = s_new
            if first:
                oacc[hh, rows, :] = o
            else:
                oacc[hh, rows, :] += o

        if nchunk == 1:
            work = [(hh, sq, d, _chunk_rows(0, sq * seq_len)) for hh, sq in units for d in range(2)]
            qs = {(hh, sq): _silu(q_ref[head(hh), _chunk_rows(0, sq * seq_len), :].astype(F32)) for hh, sq in units}
            ps = [gates(hh, rows, d, qs[hh, sq]) for hh, sq, d, rows in work]
            atts = [levels(p, d) for (_, _, d, _), p in zip(work, ps)]
            for (hh, sq, d, rows), p, att in zip(work, ps, atts):
                finish(p, att, hh, rows, d, sq, d == 0)
        else:
            def body(ci, carry):
                work = [(hh, sq, d, _chunk_rows(ci if d == 0 else nchunk - 1 - ci, sq * seq_len))
                        for hh, sq in units for d in range(2)]
                ps = [gates(hh, rows, d) for hh, _, d, rows in work]
                atts = [levels(p, d) for (_, _, d, _), p in zip(work, ps)]
                for (hh, sq, d, rows), p, att in zip(work, ps, atts):
                    finish(p, att, hh, rows, d, sq, False)
                return carry
            lax.fori_loop(0, nchunk, body, 0)

        for hh, sq in units:
            h = head(hh)
            base = sq * seq_len
            o = oacc[hh, base:base + seq_len, :]
            y = o * lax.rsqrt(jnp.mean(o * o, -1, keepdims=True) + RMS_EPS) * ng_ref[pl.ds(h, 1), :]
            gate = _silu(g_ref[h, base:base + seq_len, :].astype(F32))
            y_ref[h, base:base + seq_len, :] = (y * gate).astype(y_ref.dtype)
            if is_ctx:
                for d in range(2):
                    sfin_ref[sq, d, h] = st[slot(hh, sq, d)].T
        return carry

    lax.fori_loop(0, HG_HEADS // hpi, head_group, 0)


def _seq_call(kern, name, in_specs, args, heads, st_specs, st_shapes, states, scratch, is_ctx, *, seq_len, nseq, nsq):
    in_specs, args = list(in_specs), list(args)
    n_used = len(args)
    aliases = {}
    y_spec = pl.BlockSpec((heads, nsq * seq_len, LANES), lambda i: (0, i, 0))
    y_shape = jax.ShapeDtypeStruct((heads, nseq * seq_len, LANES), BF16)
    if is_ctx:
        out_specs, out_shape = [y_spec] + list(st_specs), [y_shape] + list(st_shapes)
        for k, s in enumerate(states):
            in_specs.append(pl.BlockSpec(memory_space=pl.ANY))
            args.append(s)
            aliases[len(args) - 1] = 1 + k
    else:
        in_specs += list(st_specs)
        args += list(states)
        n_used = len(args)
        out_specs, out_shape = [y_spec], [y_shape]
    n_args = len(args)
    steps = nseq // nsq
    kw = dict(seq_len=seq_len, nsq=nsq)

    def body(*refs):
        kern(*refs[:n_used], *refs[n_args:], is_ctx=is_ctx, **kw)

    return pl.pallas_call(
        body, grid=(steps,), in_specs=in_specs, out_specs=out_specs, out_shape=out_shape,
        input_output_aliases=aliases, scratch_shapes=scratch, compiler_params=_cparams(("arbitrary",)),
        name=name + ("_ctx" if is_ctx else "_lat"),
    )(*args)


def _hgrn_call(ub, uf, lbp, ng, sgn, states, *, is_ctx, seq_len, row0, nseq, nsq, layer, depth):
    hpi = 1 if is_ctx else LAT_HEADS_PER_ITER
    rows = nsq * seq_len
    bo = row0 // rows
    nh = HG_HEADS
    in_specs = [pl.BlockSpec((nh, rows, LANES), lambda i: (0, bo + i, 0)),
                pl.BlockSpec((nh, rows, LANES), lambda i: (1, bo + i, 0)),
                pl.BlockSpec((nh, rows, LANES), lambda i: (2, bo + i, 0)),
                pl.BlockSpec((2 * nh, rows, LANES), lambda i: (0, bo + i, 0)),
                pl.BlockSpec((2, 2 * nh, LANES), lambda i: (0, 0, 0)),
                pl.BlockSpec((nh, LANES), lambda i: (0, 0)),
                pl.BlockSpec((N_LEVEL + 1, CH, LANES), lambda i: (0, 0, 0))]
    st_spec, st_shape = _state_io((2, nh, HG_DK, LANES), nseq, nsq, layer, depth)
    return _seq_call(
        _hgrn_kernel, "hgrn", in_specs, [ub, ub, ub, uf, lbp, ng, sgn], nh, [st_spec], [st_shape], states,
        [pltpu.VMEM((hpi, rows, LANES), F32), pltpu.VMEM((2 * nsq * hpi, LANES, HG_DK), F32)],
        is_ctx, seq_len=seq_len, nseq=nseq, nsq=nsq)


def _mlstm_prep(gt_ref, bias_ref, rows, ms_s, triu):
    gates = gt_ref[rows, :] + bias_ref[...]
    g_t = gates.T
    i_rows = g_t[0:8, :]
    lf_rows = _log_sigmoid(g_t[8:16, :])
    c_rows = _cumsum_lanes(lf_rows, triu)
    is_b = lax.broadcasted_iota(jnp.int32, (8, CH), 0) >= ML_HEADS
    is_b1 = is_b[:, 0:1]
    m_in = ms_s[:, 0:1]
    c_last = c_rows[:, CH - 1:CH]
    pos = jnp.where(is_b, c_rows - lf_rows, c_rows)
    u = jnp.where(is_b, pos + i_rows, i_rows - pos)
    nu = jnp.where(is_b1, c_last + m_in, m_in)
    mu = jnp.maximum(jnp.where(is_b, _cummax_lanes(u, True), _cummax_lanes(u, False)), nu)
    m_t = jnp.where(is_b, mu - pos, pos + mu)
    mu_end = jnp.where(is_b1, mu[:, 0:1], mu[:, CH - 1:CH])
    m_new = jnp.where(is_b1, mu[:, 0:1], m_t[:, CH - 1:CH])
    x_rows = jnp.concatenate([-mu, -m_t, u, jnp.zeros_like(u)], axis=0) * LOG2E
    return dict(lhs=_sel_lhs(x_rows), u_rows=x_rows[16:24, :], nu=nu * LOG2E, mu_end=mu_end * LOG2E, m_new=m_new)


def _mlstm_main(p, q_ref, k_ref, v_ref, sel_ref, rows, dirs, cn_s, ms_s, oacc, masks, has_init, first_write):
    lhs = p["lhs"]
    scale = ML_DK ** -0.5
    ones_blk = jnp.ones((CH, LANES), BF16)
    heads = range(ML_HEADS)
    hds = [(d, h, d * ML_HEADS + h) for h in heads for d in dirs]
    q = [q_ref[h, rows, :] for h in heads]
    kf = [k_ref[h, rows, :].astype(F32) * scale for h in heads]
    vaug = [jnp.concatenate([v_ref[h, rows, :], ones_blk], axis=1) for h in heads]
    qk = [_dot_nt(q[h], kf[h].astype(BF16)) for h in heads]
    b_mu = {hd: _dot(lhs, sel_ref[hd]) for _, _, hd in hds}
    b_m = {hd: _dot(lhs, sel_ref[8 + hd]) for _, _, hd in hds}
    k_t = [kf[h].T for h in heads]
    s = {}
    for d, h, hd in hds:
        xe = jnp.concatenate([b_mu[hd], b_mu[hd]], axis=1) + p["u_rows"][hd:hd + 1, :]
        s[hd] = (jnp.where(masks[d], jnp.exp2(xe), 0.0) * qk[h]).astype(BF16)
    num = {hd: _dot(s[hd], vaug[h]) for _, h, hd in hds}
    w_end = jnp.exp2(p["u_rows"] - p["mu_end"])
    upd = {hd: _dot((k_t[h] * w_end[hd:hd + 1, :]).astype(BF16), vaug[h]) for _, h, hd in hds}
    if has_init:
        for d, h, hd in hds:
            cn = cn_s[d, h]
            w_int = jnp.exp2(p["nu"][hd:hd + 1, :] + b_mu[hd])
            num[hd] = num[hd] + jnp.concatenate([w_int, w_int], axis=1) * _dot(q[h], cn.astype(BF16))
            upd[hd] = upd[hd] + jnp.exp2(p["nu"][hd:hd + 1, :] - p["mu_end"][hd:hd + 1, :]) * cn
    for d, h, hd in hds:
        cn_s[d, h] = upd[hd]
    for h in heads:
        hsum = None
        for d in dirs:
            hd = d * ML_HEADS + h
            den = jnp.maximum(jnp.abs(num[hd][:, LANES:]), jnp.exp2(b_m[hd]))
            hout = num[hd][:, :LANES] / den
            hsum = hout if hsum is None else hsum + hout
        if first_write:
            oacc[h, rows, :] = hsum
        else:
            oacc[h, rows, :] += hsum
    m_new = jnp.broadcast_to(p["m_new"], (8, LANES))
    if len(dirs) == 2:
        ms_s[...] = m_new
    else:
        row = lax.broadcasted_iota(jnp.int32, (8, LANES), 0)
        mine = (row >= ML_HEADS) if dirs[0] == 1 else (row < ML_HEADS)
        ms_s[...] = jnp.where(mine, m_new, ms_s[...])


def _mlstm_kernel(q_ref, k_ref, v_ref, og_ref, gt_ref, bias_ref, ng_ref, sel_ref, *rest, seq_len, nsq, is_ctx):
    if is_ctx:
        y_ref, cfin_ref, nfin_ref, mfin_ref, oacc, cn_s, ms_s = rest
    else:
        c0_ref, n0_ref, m0_ref, y_ref, oacc, cn_s, ms_s = rest
    nchunk = seq_len // CH
    triu = _tri(CH, False)
    rr = lax.broadcasted_iota(jnp.int32, (CH, CH), 0)
    ss = lax.broadcasted_iota(jnp.int32, (CH, CH), 1)
    masks = (ss <= rr, ss >= rr)
    def main(p, rows, dirs, sq):
        _mlstm_main(p, q_ref, k_ref, v_ref, sel_ref, rows, dirs, cn_s.at[sq], ms_s.at[sq], oacc, masks,
                    not is_ctx, nchunk == 1)

    for sq in range(nsq):
        base = sq * seq_len
        if is_ctx:
            ms_s[sq] = jnp.zeros((2 * ML_HEADS, LANES), F32)
        else:
            ms_s[sq] = m0_ref[sq]
            for d in range(2):
                for h in range(ML_HEADS):
                    hd = d * ML_HEADS + h
                    nb = jnp.broadcast_to(n0_ref[sq, hd:hd + 1, :], (ML_DK, LANES)).T
                    cn_s[sq, d, h] = jnp.concatenate([c0_ref[sq, d, h], nb], axis=1)
            if nchunk > 1:
                oacc[:, base:base + seq_len, :] = jnp.zeros((ML_HEADS, seq_len, LANES), F32)
    if nchunk == 1:
        preps = [_mlstm_prep(gt_ref, bias_ref, _chunk_rows(0, sq * seq_len), ms_s.at[sq], triu) for sq in range(nsq)]
        for sq in range(nsq):
            main(preps[sq], _chunk_rows(0, sq * seq_len), (0, 1), sq)
    else:
        def body(ci, carry):
            work = [(sq, d, _chunk_rows(ci if d == 0 else nchunk - 1 - ci, sq * seq_len))
                    for sq in range(nsq) for d in range(2)]
            ps = [_mlstm_prep(gt_ref, bias_ref, rows, ms_s.at[sq], triu) for sq, _, rows in work]
            for (sq, d, rows), p in zip(work, ps):
                main(p, rows, (d,), sq)
            return carry
        lax.fori_loop(0, nchunk, body, 0)
    for sq in range(nsq):
        base = sq * seq_len
        for h in range(ML_HEADS):
            o = oacc[h, base:base + seq_len, :]
            y = o * lax.rsqrt(jnp.mean(o * o, -1, keepdims=True) + RMS_EPS) * ng_ref[h:h + 1, :]
            gate = _sigmoid(og_ref[h, base:base + seq_len, :].astype(F32))
            y_ref[h, base:base + seq_len, :] = (y * gate).astype(y_ref.dtype)
        if is_ctx:
            for d in range(2):
                for h in range(ML_HEADS):
                    cn = cn_s[sq, d, h]
                    cfin_ref[sq, d, h] = cn[:, :LANES]
                    nfin_ref[sq, pl.ds(d * ML_HEADS + h, 1), :] = cn[:, LANES:].T[0:1, :]
            mfin_ref[sq] = ms_s[sq]


def _mlstm_call(ub, uf, bias_row, ng, sel, states, *, is_ctx, seq_len, row0, nseq, nsq, layer, depth):
    rows = nsq * seq_len
    bo = row0 // rows
    nh = ML_HEADS
    slab = lambda k: pl.BlockSpec((nh, rows, LANES), lambda i: (k, bo + i, 0))
    in_specs = [slab(3), slab(4), slab(5), slab(6),
                pl.BlockSpec((None, rows, LANES), lambda i: (GATE_SLAB, bo + i, 0)),
                pl.BlockSpec((1, LANES), lambda i: (0, 0)),
                pl.BlockSpec((nh, LANES), lambda i: (0, 0)),
                pl.BlockSpec((N_SEL, LANES, LANES), lambda i: (0, 0, 0))]
    c_spec, c_shape = _state_io((2, nh, ML_DK, LANES), nseq, nsq, layer, depth)
    v_spec, v_shape = _state_io((2 * nh, LANES), nseq, nsq, layer, depth)
    return _seq_call(
        _mlstm_kernel, "mlstm", in_specs, [ub, ub, ub, ub, uf, bias_row, ng, sel], nh,
        [c_spec, v_spec, v_spec], [c_shape, v_shape, v_shape], states,
        [pltpu.VMEM((nh, rows, LANES), F32), pltpu.VMEM((nsq, 2, nh, ML_DK, 2 * LANES), F32),
         pltpu.VMEM((nsq, 2 * nh, LANES), F32)],
        is_ctx, seq_len=seq_len, nseq=nseq, nsq=nsq)


N_PAIR = MB_HEADS // 2
PAIRS_PER_GROUP = N_PAIR // MB_GROUPS


def _ssd_prep(gt_ref, bias_ref, nega_ref, rows, triu):
    gates = gt_ref[rows, :] + bias_ref[...]
    g_t = gates.T
    dt_rows = _softplus(g_t[16:32, :])
    la_rows = dt_rows * nega_ref[...]
    c_rows = _cumsum_lanes(la_rows, triu)
    ldt = jnp.log(dt_rows)
    is_b = lax.broadcasted_iota(jnp.int32, (16, CH), 0) >= MB_HEADS
    pos = jnp.where(is_b, c_rows - la_rows, c_rows)
    a_col = jnp.where(is_b, -pos, pos) * LOG2E
    r_row = jnp.where(is_b, pos + ldt, ldt - pos) * LOG2E
    c_last = c_rows[:, CH - 1:CH] * LOG2E
    lhs = _sel_lhs(jnp.concatenate([a_col, jnp.zeros_like(a_col)], axis=0))
    return dict(lhs=lhs, r_row=r_row, c_last=c_last)


def _ssd_main(p, xs_s, bcs_s, sel_ref, rows, dirs, hst, yacc, masks, has_init, first_write):
    lhs, r_row, c_last = p["lhs"], p["r_row"], p["c_last"]
    lane = lax.broadcasted_iota(jnp.int32, (CH, LANES), 1)
    lo = lane < MB_HEADDIM
    lo_state = lax.broadcasted_iota(jnp.int32, (LANES, LANES), 1) < MB_HEADDIM
    lo_row = lo_state[0:1, :]
    bblk = bcs_s[rows, 0:LANES]
    cblk = bcs_s[rows, LANES:2 * LANES]
    c_g, b_g, gm = [], [], []
    for grp in range(MB_GROUPS):
        in_grp = (lane >= grp * MB_DSTATE) & (lane < (grp + 1) * MB_DSTATE)
        c_g.append(jnp.where(in_grp, cblk, 0.0))
        b_g.append(jnp.where(in_grp, bblk, 0.0))
        gm.append(_dot_nt(c_g[grp].astype(BF16), b_g[grp].astype(BF16)))
    xh = []
    for j in range(N_PAIR):
        xpair = xs_s[rows, j * LANES:(j + 1) * LANES]
        xh.append((jnp.where(lo, xpair, 0.0).astype(BF16), jnp.where(lo, 0.0, xpair).astype(BF16)))
    items = [(d, j, half, d * MB_HEADS + 2 * j + half) for j in range(N_PAIR) for d in dirs for half in range(2)]
    b_a = {hd: _dot(lhs, sel_ref[hd]) for _, _, _, hd in items}
    b_t = [b_g[grp].T for grp in range(MB_GROUPS)]
    is_f = lax.broadcasted_iota(jnp.int32, (2 * MB_HEADS, 1), 0) < MB_HEADS
    w_end = jnp.exp2(r_row + jnp.where(is_f, c_last, 0.0))
    m, b_out, c_in = {}, {}, {}
    for d, j, half, hd in items:
        grp = j // PAIRS_PER_GROUP
        cl = c_last[hd:hd + 1, :]
        xe = jnp.concatenate([b_a[hd], b_a[hd]], axis=1) + r_row[hd:hd + 1, :]
        m[hd] = (jnp.where(masks[d], jnp.exp2(xe), 0.0) * gm[grp]).astype(BF16)
        b_out[hd] = (b_t[grp] * w_end[hd:hd + 1, :]).astype(BF16)
        if has_init:
            c_in[hd] = (c_g[grp] * jnp.exp2(b_a[hd] + cl if d else b_a[hd])).astype(BF16)
    yy = {hd: _dot(m[hd], xh[j][half]) for _, j, half, hd in items}
    uu = {hd: _dot(b_out[hd], xh[j][half]) for _, j, half, hd in items}
    for j in range(N_PAIR):
        ysum = None
        for d in dirs:
            hd0, hd1 = d * MB_HEADS + 2 * j, d * MB_HEADS + 2 * j + 1
            y = yy[hd0] + yy[hd1]
            upd = uu[hd0] + uu[hd1]
            if has_init:
                ht = hst[d, j]
                y = y + _dot(c_in[hd0], jnp.where(lo_state, ht, 0.0).astype(BF16))
                y = y + _dot(c_in[hd1], jnp.where(lo_state, 0.0, ht).astype(BF16))
                decay = jnp.where(lo_row, jnp.exp2(c_last[hd0:hd0 + 1, :]), jnp.exp2(c_last[hd1:hd1 + 1, :]))
                upd = upd + ht * decay
            hst[d, j] = upd
            ysum = y if ysum is None else ysum + y
        if first_write:
            yacc[rows, j * LANES:(j + 1) * LANES] = ysum
        else:
            yacc[rows, j * LANES:(j + 1) * LANES] += ysum


def _ssd_kernel(x_ref, bc_ref, z_ref, gt_ref, cw_ref, bias_ref, nega_ref, dskip_ref, ng_ref, sel_ref, *rest,
                seq_len, nsq, is_ctx):
    if is_ctx:
        y_ref, hfin_ref, xs_s, bcs_s, yacc, hst = rest
    else:
        h0_ref, y_ref, xs_s, bcs_s, yacc, hst = rest
    nchunk = seq_len // CH
    triu = _tri(CH, False)
    rr = lax.broadcasted_iota(jnp.int32, (CH, CH), 0)
    ss = lax.broadcasted_iota(jnp.int32, (CH, CH), 1)
    masks = (ss <= rr, ss >= rr)

    def conv(v, lo_col):
        row = lax.broadcasted_iota(jnp.int32, v.shape, 0)
        prev = jnp.where(row == 0, 0.0, pltpu.roll(v, 1, 0))
        nxt = jnp.where(row == seq_len - 1, 0.0, pltpu.roll(v, seq_len - 1, 0))
        cs = slice(lo_col, lo_col + v.shape[1])
        return _silu(cw_ref[0:1, cs] * prev + cw_ref[1:2, cs] * v + cw_ref[2:3, cs] * nxt + cw_ref[3:4, cs])

    for sq in range(nsq):
        base = sq * seq_len
        seq_rows = slice(base, base + seq_len)
        for k in range(MIX_W // LANES):
            xs_s[seq_rows, k * LANES:(k + 1) * LANES] = conv(x_ref[k, seq_rows, :], k * LANES)
        for k in range(2):
            bcs_s[seq_rows, k * LANES:(k + 1) * LANES] = conv(bc_ref[k, seq_rows, :], MIX_W + k * LANES)
        if not is_ctx:
            hst[sq] = h0_ref[sq]
            if nchunk > 1:
                yacc[seq_rows, :] = jnp.zeros((seq_len, MIX_W), F32)

    def prep(rows):
        return _ssd_prep(gt_ref, bias_ref, nega_ref, rows, triu)

    def main(p, rows, dirs, sq):
        _ssd_main(p, xs_s, bcs_s, sel_ref, rows, dirs, hst.at[sq], yacc, masks, not is_ctx, nchunk == 1)

    if nchunk == 1:
        preps = [prep(_chunk_rows(0, sq * seq_len)) for sq in range(nsq)]
        for sq in range(nsq):
            main(preps[sq], _chunk_rows(0, sq * seq_len), (0, 1), sq)
    else:
        def body(ci, carry):
            work = [(sq, d, _chunk_rows(ci if d == 0 else nchunk - 1 - ci, sq * seq_len))
                    for sq in range(nsq) for d in range(2)]
            ps = [prep(rows) for _, _, rows in work]
            for (sq, d, rows), p in zip(work, ps):
                main(p, rows, (d,), sq)
            return carry
        lax.fori_loop(0, nchunk, body, 0)

    for sq in range(nsq):
        base = sq * seq_len
        seq_rows = slice(base, base + seq_len)
        z = jnp.concatenate([z_ref[k, seq_rows, :] for k in range(MIX_W // LANES)], axis=1).astype(F32)
        y = (yacc[seq_rows, :] + dskip_ref[...] * xs_s[seq_rows, :]) * _silu(z)
        y = (y * lax.rsqrt(jnp.mean(y * y, -1, keepdims=True) + RMS_EPS) * ng_ref[...]).astype(y_ref.dtype)
        for k in range(MIX_W // LANES):
            y_ref[k, seq_rows, :] = y[:, k * LANES:(k + 1) * LANES]
        if is_ctx:
            hfin_ref[sq] = hst[sq]


def _ssd_call(ub, uf, cw, bias_row, nega_rows, dskip, ng, sel, states, *, is_ctx, seq_len, row0, nseq, nsq, layer,
              depth):
    rows = nsq * seq_len
    bo = row0 // rows
    ns = MIX_W // LANES
    row_spec = lambda w: pl.BlockSpec((1, w), lambda i: (0, 0))
    in_specs = [pl.BlockSpec((ns, rows, LANES), lambda i: (2, bo + i, 0)),
                pl.BlockSpec((2, rows, LANES), lambda i: (6, bo + i, 0)),
                pl.BlockSpec((ns, rows, LANES), lambda i: (7, bo + i, 0)),
                pl.BlockSpec((None, rows, LANES), lambda i: (GATE_SLAB, bo + i, 0)),
                pl.BlockSpec((8, MB_XBC), lambda i: (0, 0)),
                row_spec(LANES),
                pl.BlockSpec((2 * MB_HEADS, CH), lambda i: (0, 0)),
                row_spec(MIX_W), row_spec(MIX_W),
                pl.BlockSpec((N_SEL, LANES, LANES), lambda i: (0, 0, 0))]
    st_spec, st_shape = _state_io((2, N_PAIR, LANES, LANES), nseq, nsq, layer, depth)
    return _seq_call(
        _ssd_kernel, "ssd", in_specs, [uf, uf, ub, uf, cw, bias_row, nega_rows, dskip, ng, sel], ns,
        [st_spec], [st_shape], states,
        [pltpu.VMEM((rows, MIX_W), F32), pltpu.VMEM((rows, 2 * LANES), F32), pltpu.VMEM((rows, MIX_W), F32),
         pltpu.VMEM((nsq, 2, N_PAIR, LANES, LANES), F32)],
        is_ctx, seq_len=seq_len, nseq=nseq, nsq=nsq)


def _ssd_state_to_pairs(s):
    bsz = s.shape[0]
    st = jnp.swapaxes(s, -1, -2).reshape(bsz, 2, N_PAIR, 2, MB_DSTATE, MB_HEADDIM)
    st = jnp.moveaxis(st, 3, 4).reshape(bsz, 2, N_PAIR, MB_DSTATE, 2 * MB_HEADDIM)
    zero = jnp.zeros_like(st)
    grp = (jnp.arange(N_PAIR) // PAIRS_PER_GROUP).reshape(1, 1, N_PAIR, 1, 1)
    return jnp.concatenate([jnp.where(grp == 0, st, zero), jnp.where(grp == 1, st, zero)], axis=3)


def _ssd_pairs_to_state(hp):
    bsz = hp.shape[0]
    halves = hp.reshape(bsz, 2, N_PAIR, MB_GROUPS, MB_DSTATE, 2 * MB_HEADDIM)
    grp = (jnp.arange(N_PAIR) // PAIRS_PER_GROUP).reshape(1, 1, N_PAIR, 1, 1)
    st = jnp.where(grp == 0, halves[:, :, :, 0], halves[:, :, :, 1])
    st = st.reshape(bsz, 2, N_PAIR, MB_DSTATE, 2, MB_HEADDIM)
    st = jnp.moveaxis(st, 4, 3).reshape(bsz, 2, MB_HEADS, MB_DSTATE, MB_HEADDIM)
    return jnp.swapaxes(st, -1, -2)


def _grid_pos_embed(n_tok, d_model):
    rows = n_tok // GRID_W
    r, cidx = jnp.meshgrid(jnp.arange(rows, dtype=F32), jnp.arange(GRID_W, dtype=F32), indexing='ij')
    quarter = d_model // 4
    freq = jnp.exp(-math.log(10000.0) * jnp.arange(quarter, dtype=F32) / quarter)
    ar = r.reshape(-1, 1) * freq
    ac = cidx.reshape(-1, 1) * freq
    return jnp.concatenate([jnp.sin(ar), jnp.cos(ar), jnp.sin(ac), jnp.cos(ac)], axis=-1)


def _gate_row(pieces):
    v = jnp.concatenate([p.reshape(-1).astype(F32) for p in pieces])
    return jnp.concatenate([v, v, v, jnp.zeros((LANES - 3 * N_SEL,), F32)]).reshape(1, LANES)


def kernel(x_prompt, x_sample, state_hgrn, state_mlstm_C, state_mlstm_n, state_mlstm_m, state_ssd, c, c_ctx,
           w_mod, b_mod, ln_g, ln_b, ffn_w_gu, ffn_w_down, w_in, hg_lb, hg_norm_g, ml_gate_b, ml_norm_g,
           mb_conv_w, mb_conv_b, mb_dt_bias, mb_a_log, mb_d, mb_norm_g, w_branch, w_out):
    bsz, seq, d = x_prompt.shape
    dbsz, dseq, _ = x_sample.shape
    depth = w_mod.shape[0]
    t_ctx = bsz * seq
    n_ctx_tiles = t_ctx // ROW_TILE
    assert dseq == ROW_TILE and t_ctx % ROW_TILE == 0 and seq == CH and dseq % CH == 0

    xs0 = x_sample + _grid_pos_embed(dseq, d).astype(x_sample.dtype)[None]
    x = jnp.concatenate([x_prompt.reshape(t_ctx, d), xs0.reshape(dbsz * dseq, d)], axis=0)

    cv = jnp.concatenate([c_ctx[None]] * MOD_LAT_ROW + [c, jnp.zeros((8 - MOD_LAT_ROW - dbsz, d), F32)], axis=0)
    mod4 = _mod_call(cv, w_mod, b_mod).reshape(depth, 8, 1, N_MOD * d)

    w_branch_b = w_branch.astype(BF16)
    w_out_b = w_out.astype(BF16)

    w_a, w_b = _regroup_call(w_in)
    ln_g4 = ln_g.reshape(depth, 3, 1, d)
    ln_b4 = ln_b.reshape(depth, 3, 1, d)

    lbs = jnp.cumsum(jax.nn.softmax(hg_lb.astype(F32), axis=0), axis=0)
    lbs = (lbs - lbs[0]).reshape(depth, 2 * HG_HEADS, HG_DK)
    lbp = jnp.stack([lbs, 1.0 - lbs], axis=1)
    cw = jnp.concatenate([mb_conv_w, mb_conv_b[:, None, :], jnp.zeros((depth, 4, MB_XBC), F32)], axis=1)
    sel = _selectors()

    lat_h0 = _ssd_state_to_pairs(state_ssd.reshape((dbsz * depth,) + state_ssd.shape[2:])).reshape(
        (dbsz, depth, 2, N_PAIR, LANES, LANES))
    lat_n0 = state_mlstm_n.reshape(dbsz, depth, 2 * ML_HEADS, ML_DK)
    lat_m0 = jnp.broadcast_to(state_mlstm_m.reshape(dbsz, depth, 2 * ML_HEADS, 1), (dbsz, depth, 2 * ML_HEADS, LANES))

    zeros = lambda *s: jnp.zeros((bsz, depth) + s, F32)
    st_hg = [zeros(2, HG_HEADS, HG_DK, LANES)]
    st_ml = [zeros(2, ML_HEADS, ML_DK, LANES), zeros(2 * ML_HEADS, LANES), zeros(2 * ML_HEADS, LANES)]
    st_ss = [zeros(2, N_PAIR, LANES, LANES)]
    t_idx = jnp.arange(CH).reshape(1, CH, 1)
    widths = (HALF >> jnp.arange(N_LEVEL + 1)).reshape(N_LEVEL + 1, 1, 1)
    sgn = jnp.broadcast_to(jnp.where((t_idx & widths) != 0, 1.0, -1.0), (N_LEVEL + 1, CH, LANES)).astype(F32)
    ctx = dict(is_ctx=True, seq_len=seq, row0=0, nseq=bsz, depth=depth)
    lat = dict(is_ctx=False, seq_len=dseq, row0=t_ctx, nseq=dbsz, depth=depth)
    for l in range(depth):
        x = _ffn_call(x, mod4, ffn_w_gu, ffn_w_down, ln_g4, ln_b4, l, 0, n_ctx_tiles)
        ub = _inproj_call(x, mod4, w_a, l, INPROJ_TILE_A, BF16, n_ctx_tiles, "inproj_a")
        uf = _inproj_call(x, mod4, w_b, l, INPROJ_TILE_B, F32, n_ctx_tiles, "inproj_b")

        ng_h = hg_norm_g[l].reshape(HG_HEADS, LANES)
        y_hg, *st_hg = _hgrn_call(ub, uf, lbp[l], ng_h, sgn, st_hg, layer=l, nsq=CTX_SEQS_PER_STEP, **ctx)
        y_hg_lat, = _hgrn_call(ub, uf, lbp[l], ng_h, sgn, [state_hgrn], layer=l, nsq=1, **lat)

        gate_bias = _gate_row([ml_gate_b[l, 0], ml_gate_b[l, 1], mb_dt_bias[l]])
        ng_m = ml_norm_g[l].reshape(ML_HEADS, LANES)
        y_ml, *st_ml = _mlstm_call(ub, uf, gate_bias, ng_m, sel, st_ml, layer=l, nsq=CTX_SEQS_PER_STEP_SCALAR, **ctx)
        y_ml_lat, = _mlstm_call(ub, uf, gate_bias, ng_m, sel, [state_mlstm_C, lat_n0, lat_m0], layer=l, nsq=LAT_SEQS_PER_STEP_SCALAR, **lat)

        nega = jnp.broadcast_to(-jnp.exp(mb_a_log[l].astype(F32)).reshape(2 * MB_HEADS, 1), (2 * MB_HEADS, CH))
        dskip = jnp.repeat(mb_d[l], MB_HEADDIM).reshape(1, MIX_W)
        ng_s = mb_norm_g[l].reshape(1, MIX_W)
        y_mb, *st_ss = _ssd_call(ub, uf, cw[l], gate_bias, nega, dskip, ng_s, sel, st_ss, layer=l, nsq=CTX_SEQS_PER_STEP_SCALAR, **ctx)
        y_mb_lat, = _ssd_call(ub, uf, cw[l], gate_bias, nega, dskip, ng_s, sel, [lat_h0], layer=l, nsq=LAT_SEQS_PER_STEP_SCALAR, **lat)

        x = _mixout_call(x, mod4, (y_hg, y_ml, y_mb), (y_hg_lat, y_ml_lat, y_mb_lat), ub, w_branch_b, w_out_b,
                         ln_g4, ln_b4, l, n_ctx_tiles)
        x = _ffn_call(x, mod4, ffn_w_gu, ffn_w_down, ln_g4, ln_b4, l, 1, n_ctx_tiles)

    y_prompt = x[:t_ctx].reshape(bsz, seq, d)
    y_sample = x[t_ctx:].reshape(dbsz, dseq, d)
    c_fin, n_fin, m_fin = st_ml
    h_fin = _ssd_pairs_to_state(st_ss[0].reshape((bsz * depth,) + st_ss[0].shape[2:]))
    return (y_prompt, y_sample, st_hg[0], c_fin, n_fin.reshape(bsz, depth, 2, ML_HEADS, ML_DK),
            m_fin[:, :, :, 0].reshape(bsz, depth, 2, ML_HEADS),
            h_fin.reshape((bsz, depth) + h_fin.shape[1:]))
```

```python
import functools
import math

import jax
import jax.numpy as jnp
from jax import lax
from jax.experimental import pallas as pl
from jax.experimental.pallas import tpu as pltpu

F32 = jnp.float32
BF16 = jnp.bfloat16

D_MODEL = 1024
DEPTH = 4
GRID_W = 64
MIX_W = 512
HG_HEADS = 4
HG_DK = 128
ML_HEADS = 4
ML_DK = 128
MB_HEADS = 8
MB_HEADDIM = 64
MB_GROUPS = 2
MB_DSTATE = 64
MB_XBC = MIX_W + 2 * MB_GROUPS * MB_DSTATE
D_FF = 2816
N_MOD = 9
ALPHA = (2 * DEPTH) ** 0.25
LN_EPS = 1e-5
RMS_EPS = 1e-6
LOG2E = 1.4426950408889634

LANES = 128
CH = 256
HALF = CH // 2
ROW_TILE = 1024
FFN_ROWS = 2048
MOD_LAT_ROW = 2
FF_TILE = 256
INPROJ_TILE_A = 1792
INPROJ_TILE_B = 1920
LAT_HEADS_PER_ITER = 2
CTX_SEQS_PER_STEP_SCALAR = 4
LAT_SEQS_PER_STEP_SCALAR = 2
CTX_SEQS_PER_STEP = 4
VMEM_LIMIT = 56 * 1024 * 1024

UB_SLABS = 56
UF_SLABS = 15
GATE_SLAB = 14
N_SEL = 32
LOG2_TINY = -150.0


def _dot(a, b):
    return jnp.dot(a, b, preferred_element_type=F32)


def _dot_nt(a, b):
    return lax.dot_general(a, b, (((1,), (1,)), ((), ())), preferred_element_type=F32)


def _dot_tn(a, b):
    return lax.dot_general(a, b, (((0,), (0,)), ((), ())), preferred_element_type=F32)


def _sigmoid(x):
    return 1.0 / (1.0 + jnp.exp(-x))


def _silu(x):
    return x * _sigmoid(x)


def _log_sigmoid(x):
    return jnp.minimum(x, 0.0) - jnp.log(1.0 + jnp.exp(-jnp.abs(x)))


def _softplus(x):
    return jnp.maximum(x, 0.0) + jnp.log(1.0 + jnp.exp(-jnp.abs(x)))


def _neg_abs(x):
    return -jnp.abs(x)


def _ln(z, g, b):
    mu = jnp.mean(z, -1, keepdims=True)
    d = z - mu
    var = jnp.mean(d * d, -1, keepdims=True)
    return d * lax.rsqrt(var + LN_EPS) * g + b


def _split3(x):
    h = x.astype(BF16)
    r = x - h.astype(F32)
    m = r.astype(BF16)
    l = (r - m.astype(F32)).astype(BF16)
    return h, m, l


def _cumsum_rows(tril, x):
    h, m, l = _split3(x)
    return _dot(tril, h) + _dot(tril, m) + _dot(tril, l)


def _cumsum_lanes(x, triu):
    h, m, l = _split3(x)
    return _dot(h, triu) + _dot(m, triu) + _dot(l, triu)


def _cummax_lanes(x, reverse):
    n = x.shape[1]
    lane = lax.broadcasted_iota(jnp.int32, x.shape, 1)
    k = 1
    while k < n:
        if reverse:
            sh = jnp.where(lane < n - k, pltpu.roll(x, n - k, 1), -jnp.inf)
        else:
            sh = jnp.where(lane >= k, pltpu.roll(x, k, 1), -jnp.inf)
        x = jnp.maximum(x, sh)
        k *= 2
    return x


def _tri(c, lower):
    r = lax.broadcasted_iota(jnp.int32, (c, c), 0)
    s = lax.broadcasted_iota(jnp.int32, (c, c), 1)
    return jnp.where((s <= r) if lower else (r <= s), 1.0, 0.0).astype(BF16)


def _sel_lhs(x_rows):
    h = x_rows.astype(BF16).astype(F32)
    r = x_rows - h
    m = r.astype(BF16).astype(F32)
    rows = jnp.concatenate([h, m, r - m, jnp.ones_like(x_rows)], axis=0)
    return rows.T.astype(BF16)


def _selectors():
    r = jnp.arange(LANES)
    ch = jnp.arange(N_SEL)
    hit = ((r[None, :] % N_SEL) == ch[:, None]) & (r[None, :] < 3 * N_SEL)
    return jnp.broadcast_to(hit[:, :, None], (N_SEL, LANES, LANES)).astype(BF16)


def _cparams(sem):
    return pltpu.CompilerParams(dimension_semantics=sem, vmem_limit_bytes=VMEM_LIMIT)


def _chunk_rows(ci, base=0):
    r0 = ci * CH + base
    if not isinstance(r0, int):
        r0 = pl.multiple_of(r0, CH)
    return pl.ds(r0, CH)


def _state_io(shape_tail, nseq, nsq, layer, depth):
    nz = (0,) * len(shape_tail)
    spec = pl.BlockSpec((nsq, None) + shape_tail, lambda i: (i, layer) + nz)
    shape = jax.ShapeDtypeStruct((nseq, depth) + shape_tail, F32)
    return spec, shape


def _chunk_loop(nchunk, run):
    if nchunk == 1:
        run(0, (0, 1))
    else:
        def body(ci, carry):
            run(ci, (0,))
            run(nchunk - 1 - ci, (1,))
            return carry
        lax.fori_loop(0, nchunk, body, 0)


def _mod_kernel(c_ref, w_ref, b_ref, o_ref):
    a = _silu(c_ref[...]).astype(BF16)
    o_ref[...] = _dot(a, w_ref[...].astype(BF16)) + b_ref[...]


def _mod_call(cv, w_mod, b_mod):
    depth, d, e = w_mod.shape
    tn = 1024
    return pl.pallas_call(
        _mod_kernel,
        grid=(depth, e // tn),
        in_specs=[pl.BlockSpec((8, d), lambda l, j: (0, 0)),
                  pl.BlockSpec((None, d, tn), lambda l, j: (l, 0, j)),
                  pl.BlockSpec((None, 1, tn), lambda l, j: (l, 0, j))],
        out_specs=pl.BlockSpec((None, 8, tn), lambda l, j: (l, 0, j)),
        out_shape=jax.ShapeDtypeStruct((depth, 8, e), F32),
        compiler_params=_cparams(("arbitrary", "arbitrary")),
        name="mod",
    )(cv, w_mod, b_mod.reshape(depth, 1, e))


def _mod_row(i, n_ctx_tiles):
    return jnp.maximum(i - n_ctx_tiles + MOD_LAT_ROW, 0)


def _ffn_kernel(x_ref, mod_ref, wa_ref, wb_ref, wd_ref, g_ref, b_ref, o_ref, xm_s, *, nf):
    j = pl.program_id(1)
    d = x_ref.shape[1]
    groups = [(r, slice(r * ROW_TILE, (r + 1) * ROW_TILE)) for r in range(FFN_ROWS // ROW_TILE)]

    def ffn(xm):
        wa, wb, wd = wa_ref[...].astype(BF16), wb_ref[...].astype(BF16), wd_ref[...].astype(BF16)
        return _dot((_silu(_dot(xm, wa)) * _dot(xm, wb)).astype(BF16), wd)

    @pl.when(j == 0)
    def _():
        for r, rows in groups:
            sh = mod_ref[r:r + 1, 0:d]
            sc = mod_ref[r:r + 1, d:2 * d]
            xm = (x_ref[rows, :] * (1.0 + sc) + sh).astype(BF16)
            xm_s[rows, :] = xm
            o_ref[rows, :] = ffn(xm)

    @pl.when(jnp.logical_and(j > 0, j < nf - 1))
    def _():
        o_ref[...] += ffn(xm_s[...])

    @pl.when(j == nf - 1)
    def _():
        for r, rows in groups:
            gate = mod_ref[r:r + 1, 2 * d:3 * d]
            z = ALPHA * x_ref[rows, :] + 0.5 * gate * (o_ref[rows, :] + ffn(xm_s[rows, :]))
            o_ref[rows, :] = _ln(z, g_ref[...], b_ref[...])


def _ffn_call(x, mod4, w_gu, w_down, ln_g, ln_b, layer, which, n_ctx_tiles):
    t, d = x.shape
    nf = D_FF // FF_TILE
    per = FFN_ROWS // ROW_TILE
    assert nf >= 3 and nf * FF_TILE == D_FF and n_ctx_tiles % per == 0 and MOD_LAT_ROW % per == 0
    sub = 0 if which == 0 else 2
    mod_pairs = mod4.reshape(mod4.shape[0], mod4.shape[1] // per, per, mod4.shape[3])
    first_lat = n_ctx_tiles // per
    mod_blk = lambda i, j: (layer, jnp.maximum(i - first_lat + MOD_LAT_ROW // per, 0), 0, sub)
    return pl.pallas_call(
        functools.partial(_ffn_kernel, nf=nf),
        grid=(t // FFN_ROWS, nf),
        in_specs=[pl.BlockSpec((FFN_ROWS, d), lambda i, j: (i, 0)),
                  pl.BlockSpec((None, None, per, 3 * d), mod_blk),
                  pl.BlockSpec((None, None, d, FF_TILE), lambda i, j: (layer, which, 0, j)),
                  pl.BlockSpec((None, None, d, FF_TILE), lambda i, j: (layer, which, 0, j + nf)),
                  pl.BlockSpec((None, None, FF_TILE, d), lambda i, j: (layer, which, j, 0)),
                  pl.BlockSpec((None, None, 1, d), lambda i, j: (layer, sub, 0, 0)),
                  pl.BlockSpec((None, None, 1, d), lambda i, j: (layer, sub, 0, 0))],
        out_specs=pl.BlockSpec((FFN_ROWS, d), lambda i, j: (i, 0)),
        out_shape=jax.ShapeDtypeStruct((t, d), F32),
        scratch_shapes=[pltpu.VMEM((FFN_ROWS, d), BF16)],
        compiler_params=_cparams(("arbitrary", "arbitrary")),
        name=f"ffn{which}",
    )(x, mod_pairs, w_gu, w_gu, w_down, ln_g, ln_b)


W_IN_A = [(0, 1536), (2560, 4608), (4624, 5136), (5920, 8992)]
W_IN_B = [(1536, 2560), (5136, 5904)]
W_IN_GATES = [(4608, 4624), (5904, 5920)]
REGROUP_ROWS = 256


def _regroup_kernel(w_ref, a_ref, b_ref):
    w = w_ref[...]
    a_ref[...] = jnp.concatenate([w[:, lo:hi] for lo, hi in W_IN_A], axis=1).astype(BF16)
    gates = [w[:, lo:hi] for lo, hi in W_IN_GATES] * 3
    pad = jnp.zeros((w.shape[0], UF_SLABS * LANES - sum(hi - lo for lo, hi in W_IN_B) - 3 * N_SEL), F32)
    b_ref[...] = jnp.concatenate([w[:, lo:hi] for lo, hi in W_IN_B] + gates + [pad], axis=1).astype(BF16)


def _regroup_call(w_in):
    depth, d, n = w_in.shape
    wa, wb = UB_SLABS * LANES, UF_SLABS * LANES
    return pl.pallas_call(
        _regroup_kernel,
        grid=(depth, d // REGROUP_ROWS),
        in_specs=[pl.BlockSpec((None, REGROUP_ROWS, n), lambda l, i: (l, i, 0))],
        out_specs=[pl.BlockSpec((None, REGROUP_ROWS, wa), lambda l, i: (l, i, 0)),
                   pl.BlockSpec((None, REGROUP_ROWS, wb), lambda l, i: (l, i, 0))],
        out_shape=[jax.ShapeDtypeStruct((depth, d, wa), BF16), jax.ShapeDtypeStruct((depth, d, wb), BF16)],
        compiler_params=_cparams(("arbitrary", "arbitrary")),
        name="regroup",
    )(w_in)


def _inproj_kernel(x_ref, mod_ref, w_ref, o_ref, xm_s):
    d = x_ref.shape[1]

    @pl.when(pl.program_id(1) == 0)
    def _():
        sh = mod_ref[:, 0:d]
        sc = mod_ref[:, d:2 * d]
        xm_s[...] = (x_ref[...] * (1.0 + sc) + sh).astype(BF16)

    res = _dot(xm_s[...], w_ref[...]).astype(o_ref.dtype)
    for k in range(o_ref.shape[0]):
        o_ref[k] = res[:, k * LANES:(k + 1) * LANES]


def _inproj_call(x, mod4, w, layer, tn, out_dtype, n_ctx_tiles, name):
    t, d = x.shape
    n = w.shape[2]
    return pl.pallas_call(
        _inproj_kernel,
        grid=(t // ROW_TILE, n // tn),
        in_specs=[pl.BlockSpec((ROW_TILE, d), lambda i, j: (i, 0)),
                  pl.BlockSpec((None, None, 1, 3 * d), lambda i, j: (layer, _mod_row(i, n_ctx_tiles), 0, 1)),
                  pl.BlockSpec((None, d, tn), lambda i, j: (layer, 0, j))],
        out_specs=pl.BlockSpec((tn // LANES, ROW_TILE, LANES), lambda i, j: (j, i, 0)),
        out_shape=jax.ShapeDtypeStruct((n // LANES, t, LANES), out_dtype),
        scratch_shapes=[pltpu.VMEM((ROW_TILE, d), BF16)],
        compiler_params=_cparams(("arbitrary", "arbitrary")),
        name=name,
    )(x, mod4, w)


def _slabs(ref):
    return jnp.concatenate([ref[k] for k in range(ref.shape[0])], axis=1)


def _mixout_kernel(x_ref, mod_ref, yhc_ref, ymc_ref, ysc_ref, yhl_ref, yml_ref, ysl_ref, g0_ref, g1_ref, g2_ref,
                   wb_ref, wo_ref, lg_ref, lb_ref, o_ref, *, n_ctx_steps):
    d = x_ref.shape[1]
    is_lat = pl.program_id(0) >= n_ctx_steps

    def branch(c_ref, l_ref):
        return jnp.where(is_lat, _slabs(l_ref), _slabs(c_ref))

    p = _sigmoid(_slabs(g0_ref).astype(F32)) * _dot(branch(yhc_ref, yhl_ref), wb_ref[0])
    p += _sigmoid(_slabs(g1_ref).astype(F32)) * _dot(branch(ymc_ref, yml_ref), wb_ref[1])
    p += _sigmoid(_slabs(g2_ref).astype(F32)) * _dot(branch(ysc_ref, ysl_ref), wb_ref[2])
    y = _dot(p.astype(BF16), wo_ref[...])
    gate = mod_ref[:, 2 * d:3 * d]
    z = ALPHA * x_ref[...] + gate * y
    o_ref[...] = _ln(z, lg_ref[...], lb_ref[...])


def _mixout_call(x, mod4, y_ctx, y_lat, ub, w_branch, w_out, ln_g, ln_b, layer, n_ctx_tiles):
    t, d = x.shape
    tm = ROW_TILE
    per = ROW_TILE // tm
    ns = MIX_W // LANES
    ng = d // LANES
    g0 = (UB_SLABS - 3 * ng) // ng
    n_ctx_steps = n_ctx_tiles * per
    n_lat_steps = t // tm - n_ctx_steps
    yc_spec = pl.BlockSpec((ns, tm, LANES), lambda i: (0, jnp.minimum(i, n_ctx_steps - 1), 0))
    yl_spec = pl.BlockSpec((ns, tm, LANES), lambda i: (0, jnp.clip(i - n_ctx_steps, 0, n_lat_steps - 1), 0))
    return pl.pallas_call(
        functools.partial(_mixout_kernel, n_ctx_steps=n_ctx_steps),
        grid=(t // tm,),
        in_specs=[pl.BlockSpec((tm, d), lambda i: (i, 0)),
                  pl.BlockSpec((None, None, 1, 3 * d), lambda i: (layer, _mod_row(i // per, n_ctx_tiles), 0, 1)),
                  yc_spec, yc_spec, yc_spec, yl_spec, yl_spec, yl_spec,
                  pl.BlockSpec((ng, tm, LANES), lambda i: (g0, i, 0)),
                  pl.BlockSpec((ng, tm, LANES), lambda i: (g0 + 1, i, 0)),
                  pl.BlockSpec((ng, tm, LANES), lambda i: (g0 + 2, i, 0)),
                  pl.BlockSpec((None, 3, MIX_W, d), lambda i: (layer, 0, 0, 0)),
                  pl.BlockSpec((None, d, d), lambda i: (layer, 0, 0)),
                  pl.BlockSpec((None, None, 1, d), lambda i: (layer, 1, 0, 0)),
                  pl.BlockSpec((None, None, 1, d), lambda i: (layer, 1, 0, 0))],
        out_specs=pl.BlockSpec((tm, d), lambda i: (i, 0)),
        out_shape=jax.ShapeDtypeStruct((t, d), F32),
        compiler_params=_cparams(("arbitrary",)),
        name="mixout",
    )(x, mod4, *y_ctx, *y_lat, ub, ub, ub, w_branch, w_out, ln_g, ln_b)


N_LEVEL = int(math.log2(HALF))


def _block_ref_rows(c, w):
    n_rows = c.shape[0]
    if 2 * w == n_rows:
        return jnp.broadcast_to(c[w - 1:w, :], c.shape)
    if w >= 4:
        n = n_rows // (2 * w)
        c3 = c.reshape(n, 2 * w, LANES)
        return jnp.broadcast_to(c3[:, w - 1:w, :], c3.shape).reshape(n_rows, LANES)
    c8 = c.reshape(n_rows // 8, 8, LANES)
    sub = lax.broadcasted_iota(jnp.int32, c8.shape, 1)
    if w == 2:
        r = jnp.where(sub < 4, c8[:, 1:2, :], c8[:, 5:6, :])
    else:
        r = jnp.where(sub < 2, c8[:, 0:1, :],
                      jnp.where(sub < 4, c8[:, 2:3, :], jnp.where(sub < 6, c8[:, 4:5, :], c8[:, 6:7, :])))
    return r.reshape(n_rows, LANES)


def _level_ids():
    r = lax.broadcasted_iota(jnp.int32, (HALF, HALF), 0)
    s = lax.broadcasted_iota(jnp.int32, (HALF, HALF), 1)
    lv = 31 - lax.clz(r ^ s)
    lv = jnp.where(r == s, N_LEVEL, lv)
    return jnp.where(r >= s, lv, -1), jnp.where(r <= s, lv, -1)


def _hgrn_gates(q, fpre, lb, onemlb, tril, backward):
    u = jnp.exp2(_neg_abs(fpre) * LOG2E)
    r = 1.0 / (1.0 + u)
    pos_side = fpre >= 0.0
    kk = onemlb * (jnp.where(pos_side, u, 1.0) * r)
    f = lb + onemlb * (jnp.where(pos_side, 1.0, u) * r)
    lf2 = jnp.maximum(jnp.log2(f), LOG2_TINY)
    cum = _cumsum_rows(tril, lf2)
    pos = (cum - lf2) if backward else cum
    return dict(q=q, kk=kk, qb=q.astype(BF16), kb=kk.astype(BF16), cum=cum, pos=pos, f=f)


def _hgrn_levels(p, lvl, sgn_ref, backward):
    qb, kb, cum, pos = p["qb"], p["kb"], p["cum"], p["pos"]

    def sides(w):
        if w == 1:
            odd = (lax.broadcasted_iota(jnp.int32, (CH, LANES), 0) & 1) == 1
            e = (jnp.where(odd, 1.0, p["f"]) if backward else jnp.where(odd, p["f"], 1.0)).astype(BF16)
            return qb * e, kb * e
        e = jnp.exp2((pos - _block_ref_rows(cum, w)) * sgn_ref[N_LEVEL - int(math.log2(w))]).astype(BF16)
        return qb * e, kb * e

    zq, zk = sides(HALF)
    if backward:
        cross = _dot_nt(zq[0:HALF], zk[HALF:CH])
    else:
        cross = _dot_nt(zq[HALF:CH], zk[0:HALF])
    halves = (slice(0, HALF), slice(HALF, CH))
    prods = [_dot_nt(qb[rows], kb[rows]) for rows in halves]
    zq, zk = sides(HALF // 2)
    diag = [jnp.where(lvl == N_LEVEL, prods[b], 0.0) for b in range(2)]
    w = HALF // 2
    while w >= 1:
        prods = [_dot_nt(zq[rows], zk[rows]) for rows in halves]
        if w > 1:
            zq, zk = sides(w // 2)
        lv = int(math.log2(w))
        diag = [jnp.where(lvl == lv, prods[b], diag[b]) for b in range(2)]
        w //= 2
    return diag[0].astype(BF16), diag[1].astype(BF16), cross.astype(BF16)


def _hgrn_finish(p, att, v, s_t, backward, has_init):
    q, kk, cum, pos = p["q"], p["kk"], p["cum"], p["pos"]
    d0, d1, cr = att
    if backward:
        o_lo = _dot(jnp.concatenate([d0, cr], axis=1), v)
        o_hi = _dot(d1, v[HALF:CH])
    else:
        o_lo = _dot(d0, v[0:HALF])
        o_hi = _dot(jnp.concatenate([cr, d1], axis=1), v)
    o = jnp.concatenate([o_lo, o_hi], axis=0)
    c_last = cum[CH - 1:CH, :]
    k_out = kk * jnp.exp2((pos) if backward else (c_last - cum))
    s_new = _dot_tn(v, k_out.astype(BF16))
    if has_init:
        q_in = q * jnp.exp2((c_last - pos) if backward else cum)
        o = o + _dot_nt(q_in.astype(BF16), s_t.astype(BF16))
        s_new = s_new + s_t * jnp.exp2(c_last)
    return o, s_new


def _hgrn_kernel(q_ref, v_ref, g_ref, f_ref, lbp_ref, ng_ref, sgn_ref, *rest, seq_len, nsq, is_ctx):
    if is_ctx:
        y_ref, sfin_ref, oacc, st = rest
    else:
        s0_ref, y_ref, oacc, st = rest
    nchunk = seq_len // CH
    hpi = oacc.shape[0]
    tril = _tri(CH, True)
    lvl_f, lvl_b = _level_ids()
    units = [(hh, sq) for hh in range(hpi) for sq in range(nsq)]

    def head_group(hg, carry):
        head = lambda hh: hg * hpi + hh
        slot = lambda hh, sq, d: (hh * nsq + sq) * 2 + d
        for hh, sq in units:
            base = sq * seq_len
            if not is_ctx:
                for d in range(2):
                    st[slot(hh, sq, d)] = s0_ref[sq, d, head(hh)].T
                if nchunk > 1:
                    oacc[hh, base:base + seq_len, :] = jnp.zeros((seq_len, LANES), F32)

        def gates(hh, rows, d, q=None):
            h = head(hh)
            lrow = pl.ds(d * HG_HEADS + h, 1)
            q = _silu(q_ref[h, rows, :].astype(F32)) if q is None else q
            return _hgrn_gates(q, f_ref[d * HG_HEADS + h, rows, :], lbp_ref[0, lrow, :], lbp_ref[1, lrow, :],
                               tril, d == 1)

        def levels(p, d):
            return _hgrn_levels(p, lvl_b if d else lvl_f, sgn_ref, d == 1)

        def finish(p, att, hh, rows, d, sq, first):
            o, s_new = _hgrn_finish(p, att, v_ref[head(hh), rows, :], None if is_ctx else st[slot(hh, sq, d)],
                                    d == 1, not is_ctx)
            st[slot(hh, sq, d)] = s_new
            if first:
                oacc[hh, rows, :] = o
            else:
                oacc[hh, rows, :] += o

        if nchunk == 1:
            work = [(hh, sq, d, _chunk_rows(0, sq * seq_len)) for hh, sq in units for d in range(2)]
            qs = {(hh, sq): _silu(q_ref[head(hh), _chunk_rows(0, sq * seq_len), :].astype(F32)) for hh, sq in units}
            ps = [gates(hh, rows, d, qs[hh, sq]) for hh, sq, d, rows in work]
            atts = [levels(p, d) for (_, _, d, _), p in zip(work, ps)]
            for (hh, sq, d, rows), p, att in zip(work, ps, atts):
                finish(p, att, hh, rows, d, sq, d == 0)
        else:
            def body(ci, carry):
                work = [(hh, sq, d, _chunk_rows(ci if d == 0 else nchunk - 1 - ci, sq * seq_len))
                        for hh, sq in units for d in range(2)]
                ps = [gates(hh, rows, d) for hh, _, d, rows in work]
                atts = [levels(p, d) for (_, _, d, _), p in zip(work, ps)]
                for (hh, sq, d, rows), p, att in zip(work, ps, atts):
                    finish(p, att, hh, rows, d, sq, False)
                return carry
            lax.fori_loop(0, nchunk, body, 0)

        for hh, sq in units:
            h = head(hh)
            base = sq * seq_len
            o = oacc[hh, base:base + seq_len, :]
            y = o * lax.rsqrt(jnp.mean(o * o, -1, keepdims=True) + RMS_EPS) * ng_ref[pl.ds(h, 1), :]
            gate = _silu(g_ref[h, base:base + seq_len, :].astype(F32))
            y_ref[h, base:base + seq_len, :] = (y * gate).astype(y_ref.dtype)
            if is_ctx:
                for d in range(2):
                    sfin_ref[sq, d, h] = st[slot(hh, sq, d)].T
        return carry

    lax.fori_loop(0, HG_HEADS // hpi, head_group, 0)


def _seq_call(kern, name, in_specs, args, heads, st_specs, st_shapes, states, scratch, is_ctx, *, seq_len, nseq, nsq):
    in_specs, args = list(in_specs), list(args)
    n_used = len(args)
    aliases = {}
    y_spec = pl.BlockSpec((heads, nsq * seq_len, LANES), lambda i: (0, i, 0))
    y_shape = jax.ShapeDtypeStruct((heads, nseq * seq_len, LANES), BF16)
    if is_ctx:
        out_specs, out_shape = [y_spec] + list(st_specs), [y_shape] + list(st_shapes)
        for k, s in enumerate(states):
            in_specs.append(pl.BlockSpec(memory_space=pl.ANY))
            args.append(s)
            aliases[len(args) - 1] = 1 + k
    else:
        in_specs += list(st_specs)
        args += list(states)
        n_used = len(args)
        out_specs, out_shape = [y_spec], [y_shape]
    n_args = len(args)
    steps = nseq // nsq
    kw = dict(seq_len=seq_len, nsq=nsq)

    def body(*refs):
        kern(*refs[:n_used], *refs[n_args:], is_ctx=is_ctx, **kw)

    return pl.pallas_call(
        body, grid=(steps,), in_specs=in_specs, out_specs=out_specs, out_shape=out_shape,
        input_output_aliases=aliases, scratch_shapes=scratch, compiler_params=_cparams(("arbitrary",)),
        name=name + ("_ctx" if is_ctx else "_lat"),
    )(*args)


def _hgrn_call(ub, uf, lbp, ng, sgn, states, *, is_ctx, seq_len, row0, nseq, nsq, layer, depth):
    hpi = 1 if is_ctx else LAT_HEADS_PER_ITER
    rows = nsq * seq_len
    bo = row0 // rows
    nh = HG_HEADS
    in_specs = [pl.BlockSpec((nh, rows, LANES), lambda i: (0, bo + i, 0)),
                pl.BlockSpec((nh, rows, LANES), lambda i: (1, bo + i, 0)),
                pl.BlockSpec((nh, rows, LANES), lambda i: (2, bo + i, 0)),
                pl.BlockSpec((2 * nh, rows, LANES), lambda i: (0, bo + i, 0)),
                pl.BlockSpec((2, 2 * nh, LANES), lambda i: (0, 0, 0)),
                pl.BlockSpec((nh, LANES), lambda i: (0, 0)),
                pl.BlockSpec((N_LEVEL + 1, CH, LANES), lambda i: (0, 0, 0))]
    st_spec, st_shape = _state_io((2, nh, HG_DK, LANES), nseq, nsq, layer, depth)
    return _seq_call(
        _hgrn_kernel, "hgrn", in_specs, [ub, ub, ub, uf, lbp, ng, sgn], nh, [st_spec], [st_shape], states,
        [pltpu.VMEM((hpi, rows, LANES), F32), pltpu.VMEM((2 * nsq * hpi, LANES, HG_DK), F32)],
        is_ctx, seq_len=seq_len, nseq=nseq, nsq=nsq)


def _mlstm_prep(gt_ref, bias_ref, rows, ms_s, triu):
    gates = gt_ref[rows, :] + bias_ref[...]
    g_t = gates.T
    i_rows = g_t[0:8, :]
    lf_rows = _log_sigmoid(g_t[8:16, :])
    c_rows = _cumsum_lanes(lf_rows, triu)
    is_b = lax.broadcasted_iota(jnp.int32, (8, CH), 0) >= ML_HEADS
    is_b1 = is_b[:, 0:1]
    m_in = ms_s[:, 0:1]
    c_last = c_rows[:, CH - 1:CH]
    pos = jnp.where(is_b, c_rows - lf_rows, c_rows)
    u = jnp.where(is_b, pos + i_rows, i_rows - pos)
    nu = jnp.where(is_b1, c_last + m_in, m_in)
    mu = jnp.maximum(jnp.where(is_b, _cummax_lanes(u, True), _cummax_lanes(u, False)), nu)
    m_t = jnp.where(is_b, mu - pos, pos + mu)
    mu_end = jnp.where(is_b1, mu[:, 0:1], mu[:, CH - 1:CH])
    m_new = jnp.where(is_b1, mu[:, 0:1], m_t[:, CH - 1:CH])
    x_rows = jnp.concatenate([-mu, -m_t, u, jnp.zeros_like(u)], axis=0) * LOG2E
    return dict(lhs=_sel_lhs(x_rows), u_rows=x_rows[16:24, :], nu=nu * LOG2E, mu_end=mu_end * LOG2E, m_new=m_new)


def _mlstm_main(p, q_ref, k_ref, v_ref, sel_ref, rows, dirs, cn_s, ms_s, oacc, masks, has_init, first_write):
    lhs = p["lhs"]
    scale = ML_DK ** -0.5
    ones_blk = jnp.ones((CH, LANES), BF16)
    heads = range(ML_HEADS)
    hds = [(d, h, d * ML_HEADS + h) for h in heads for d in dirs]
    q = [q_ref[h, rows, :] for h in heads]
    kf = [k_ref[h, rows, :].astype(F32) * scale for h in heads]
    vaug = [jnp.concatenate([v_ref[h, rows, :], ones_blk], axis=1) for h in heads]
    qk = [_dot_nt(q[h], kf[h].astype(BF16)) for h in heads]
    b_mu = {hd: _dot(lhs, sel_ref[hd]) for _, _, hd in hds}
    b_m = {hd: _dot(lhs, sel_ref[8 + hd]) for _, _, hd in hds}
    k_t = [kf[h].T for h in heads]
    s = {}
    for d, h, hd in hds:
        xe = jnp.concatenate([b_mu[hd], b_mu[hd]], axis=1) + p["u_rows"][hd:hd + 1, :]
        s[hd] = (jnp.where(masks[d], jnp.exp2(xe), 0.0) * qk[h]).astype(BF16)
    num = {hd: _dot(s[hd], vaug[h]) for _, h, hd in hds}
    w_end = jnp.exp2(p["u_rows"] - p["mu_end"])
    upd = {hd: _dot((k_t[h] * w_end[hd:hd + 1, :]).astype(BF16), vaug[h]) for _, h, hd in hds}
    if has_init:
        for d, h, hd in hds:
            cn = cn_s[d, h]
            w_int = jnp.exp2(p["nu"][hd:hd + 1, :] + b_mu[hd])
            num[hd] = num[hd] + jnp.concatenate([w_int, w_int], axis=1) * _dot(q[h], cn.astype(BF16))
            upd[hd] = upd[hd] + jnp.exp2(p["nu"][hd:hd + 1, :] - p["mu_end"][hd:hd + 1, :]) * cn
    for d, h, hd in hds:
        cn_s[d, h] = upd[hd]
    for h in heads:
        hsum = None
        for d in dirs:
            hd = d * ML_HEADS + h
            den = jnp.maximum(jnp.abs(num[hd][:, LANES:]), jnp.exp2(b_m[hd]))
            hout = num[hd][:, :LANES] / den
            hsum = hout if hsum is None else hsum + hout
        if first_write:
            oacc[h, rows, :] = hsum
        else:
            oacc[h, rows, :] += hsum
    m_new = jnp.broadcast_to(p["m_new"], (8, LANES))
    if len(dirs) == 2:
        ms_s[...] = m_new
    else:
        row = lax.broadcasted_iota(jnp.int32, (8, LANES), 0)
        mine = (row >= ML_HEADS) if dirs[0] == 1 else (row < ML_HEADS)
        ms_s[...] = jnp.where(mine, m_new, ms_s[...])


def _mlstm_kernel(q_ref, k_ref, v_ref, og_ref, gt_ref, bias_ref, ng_ref, sel_ref, *rest, seq_len, nsq, is_ctx):
    if is_ctx:
        y_ref, cfin_ref, nfin_ref, mfin_ref, oacc, cn_s, ms_s = rest
    else:
        c0_ref, n0_ref, m0_ref, y_ref, oacc, cn_s, ms_s = rest
    nchunk = seq_len // CH
    triu = _tri(CH, False)
    rr = lax.broadcasted_iota(jnp.int32, (CH, CH), 0)
    ss = lax.broadcasted_iota(jnp.int32, (CH, CH), 1)
    masks = (ss <= rr, ss >= rr)
    def main(p, rows, dirs, sq):
        _mlstm_main(p, q_ref, k_ref, v_ref, sel_ref, rows, dirs, cn_s.at[sq], ms_s.at[sq], oacc, masks,
                    not is_ctx, nchunk == 1)

    for sq in range(nsq):
        base = sq * seq_len
        if is_ctx:
            ms_s[sq] = jnp.zeros((2 * ML_HEADS, LANES), F32)
        else:
            ms_s[sq] = m0_ref[sq]
            for d in range(2):
                for h in range(ML_HEADS):
                    hd = d * ML_HEADS + h
                    nb = jnp.broadcast_to(n0_ref[sq, hd:hd + 1, :], (ML_DK, LANES)).T
                    cn_s[sq, d, h] = jnp.concatenate([c0_ref[sq, d, h], nb], axis=1)
            if nchunk > 1:
                oacc[:, base:base + seq_len, :] = jnp.zeros((ML_HEADS, seq_len, LANES), F32)
    if nchunk == 1:
        preps = [_mlstm_prep(gt_ref, bias_ref, _chunk_rows(0, sq * seq_len), ms_s.at[sq], triu) for sq in range(nsq)]
        for sq in range(nsq):
            main(preps[sq], _chunk_rows(0, sq * seq_len), (0, 1), sq)
    else:
        def body(ci, carry):
            work = [(sq, d, _chunk_rows(ci if d == 0 else nchunk - 1 - ci, sq * seq_len))
                    for sq in range(nsq) for d in range(2)]
            ps = [_mlstm_prep(gt_ref, bias_ref, rows, ms_s.at[sq], triu) for sq, _, rows in work]
            for (sq, d, rows), p in zip(work, ps):
                main(p, rows, (d,), sq)
            return carry
        lax.fori_loop(0, nchunk, body, 0)
    for sq in range(nsq):
        base = sq * seq_len
        for h in range(ML_HEADS):
            o = oacc[h, base:base + seq_len, :]
            y = o * lax.rsqrt(jnp.mean(o * o, -1, keepdims=True) + RMS_EPS) * ng_ref[h:h + 1, :]
            gate = _sigmoid(og_ref[h, base:base + seq_len, :].astype(F32))
            y_ref[h, base:base + seq_len, :] = (y * gate).astype(y_ref.dtype)
        if is_ctx:
            for d in range(2):
                for h in range(ML_HEADS):
                    cn = cn_s[sq, d, h]
                    cfin_ref[sq, d, h] = cn[:, :LANES]
                    nfin_ref[sq, pl.ds(d * ML_HEADS + h, 1), :] = cn[:, LANES:].T[0:1, :]
            mfin_ref[sq] = ms_s[sq]


def _mlstm_call(ub, uf, bias_row, ng, sel, states, *, is_ctx, seq_len, row0, nseq, nsq, layer, depth):
    rows = nsq * seq_len
    bo = row0 // rows
    nh = ML_HEADS
    slab = lambda k: pl.BlockSpec((nh, rows, LANES), lambda i: (k, bo + i, 0))
    in_specs = [slab(3), slab(4), slab(5), slab(6),
                pl.BlockSpec((None, rows, LANES), lambda i: (GATE_SLAB, bo + i, 0)),
                pl.BlockSpec((1, LANES), lambda i: (0, 0)),
                pl.BlockSpec((nh, LANES), lambda i: (0, 0)),
                pl.BlockSpec((N_SEL, LANES, LANES), lambda i: (0, 0, 0))]
    c_spec, c_shape = _state_io((2, nh, ML_DK, LANES), nseq, nsq, layer, depth)
    v_spec, v_shape = _state_io((2 * nh, LANES), nseq, nsq, layer, depth)
    return _seq_call(
        _mlstm_kernel, "mlstm", in_specs, [ub, ub, ub, ub, uf, bias_row, ng, sel], nh,
        [c_spec, v_spec, v_spec], [c_shape, v_shape, v_shape], states,
        [pltpu.VMEM((nh, rows, LANES), F32), pltpu.VMEM((nsq, 2, nh, ML_DK, 2 * LANES), F32),
         pltpu.VMEM((nsq, 2 * nh, LANES), F32)],
        is_ctx, seq_len=seq_len, nseq=nseq, nsq=nsq)


N_PAIR = MB_HEADS // 2
PAIRS_PER_GROUP = N_PAIR // MB_GROUPS


def _ssd_prep(gt_ref, bias_ref, nega_ref, rows, triu):
    gates = gt_ref[rows, :] + bias_ref[...]
    g_t = gates.T
    dt_rows = _softplus(g_t[16:32, :])
    la_rows = dt_rows * nega_ref[...]
    c_rows = _cumsum_lanes(la_rows, triu)
    ldt = jnp.log(dt_rows)
    is_b = lax.broadcasted_iota(jnp.int32, (16, CH), 0) >= MB_HEADS
    pos = jnp.where(is_b, c_rows - la_rows, c_rows)
    a_col = jnp.where(is_b, -pos, pos) * LOG2E
    r_row = jnp.where(is_b, pos + ldt, ldt - pos) * LOG2E
    c_last = c_rows[:, CH - 1:CH] * LOG2E
    lhs = _sel_lhs(jnp.concatenate([a_col, jnp.zeros_like(a_col)], axis=0))
    return dict(lhs=lhs, r_row=r_row, c_last=c_last)


def _ssd_main(p, xs_s, bcs_s, sel_ref, rows, dirs, hst, yacc, masks, has_init, first_write):
    lhs, r_row, c_last = p["lhs"], p["r_row"], p["c_last"]
    lane = lax.broadcasted_iota(jnp.int32, (CH, LANES), 1)
    lo = lane < MB_HEADDIM
    lo_state = lax.broadcasted_iota(jnp.int32, (LANES, LANES), 1) < MB_HEADDIM
    lo_row = lo_state[0:1, :]
    bblk = bcs_s[rows, 0:LANES]
    cblk = bcs_s[rows, LANES:2 * LANES]
    c_g, b_g, gm = [], [], []
    for grp in range(MB_GROUPS):
        in_grp = (lane >= grp * MB_DSTATE) & (lane < (grp + 1) * MB_DSTATE)
        c_g.append(jnp.where(in_grp, cblk, 0.0))
        b_g.append(jnp.where(in_grp, bblk, 0.0))
        gm.append(_dot_nt(c_g[grp].astype(BF16), b_g[grp].astype(BF16)))
    xh = []
    for j in range(N_PAIR):
        xpair = xs_s[rows, j * LANES:(j + 1) * LANES]
        xh.append((jnp.where(lo, xpair, 0.0).astype(BF16), jnp.where(lo, 0.0, xpair).astype(BF16)))
    items = [(d, j, half, d * MB_HEADS + 2 * j + half) for j in range(N_PAIR) for d in dirs for half in range(2)]
    b_a = {hd: _dot(lhs, sel_ref[hd]) for _, _, _, hd in items}
    b_t = [b_g[grp].T for grp in range(MB_GROUPS)]
    is_f = lax.broadcasted_iota(jnp.int32, (2 * MB_HEADS, 1), 0) < MB_HEADS
    w_end = jnp.exp2(r_row + jnp.where(is_f, c_last, 0.0))
    m, b_out, c_in = {}, {}, {}
    for d, j, half, hd in items:
        grp = j // PAIRS_PER_GROUP
        cl = c_last[hd:hd + 1, :]
        xe = jnp.concatenate([b_a[hd], b_a[hd]], axis=1) + r_row[hd:hd + 1, :]
        m[hd] = (jnp.where(masks[d], jnp.exp2(xe), 0.0) * gm[grp]).astype(BF16)
        b_out[hd] = (b_t[grp] * w_end[hd:hd + 1, :]).astype(BF16)
        if has_init:
            c_in[hd] = (c_g[grp] * jnp.exp2(b_a[hd] + cl if d else b_a[hd])).astype(BF16)
    yy = {hd: _dot(m[hd], xh[j][half]) for _, j, half, hd in items}
    uu = {hd: _dot(b_out[hd], xh[j][half]) for _, j, half, hd in items}
    for j in range(N_PAIR):
        ysum = None
        for d in dirs:
            hd0, hd1 = d * MB_HEADS + 2 * j, d * MB_HEADS + 2 * j + 1
            y = yy[hd0] + yy[hd1]
            upd = uu[hd0] + uu[hd1]
            if has_init:
                ht = hst[d, j]
                y = y + _dot(c_in[hd0], jnp.where(lo_state, ht, 0.0).astype(BF16))
                y = y + _dot(c_in[hd1], jnp.where(lo_state, 0.0, ht).astype(BF16))
                decay = jnp.where(lo_row, jnp.exp2(c_last[hd0:hd0 + 1, :]), jnp.exp2(c_last[hd1:hd1 + 1, :]))
                upd = upd + ht * decay
            hst[d, j] = upd
            ysum = y if ysum is None else ysum + y
        if first_write:
            yacc[rows, j * LANES:(j + 1) * LANES] = ysum
        else:
            yacc[rows, j * LANES:(j + 1) * LANES] += ysum


def _ssd_kernel(x_ref, bc_ref, z_ref, gt_ref, cw_ref, bias_ref, nega_ref, dskip_ref, ng_ref, sel_ref, *rest,
                seq_len, nsq, is_ctx):
    if is_ctx:
        y_ref, hfin_ref, xs_s, bcs_s, yacc, hst = rest
    else:
        h0_ref, y_ref, xs_s, bcs_s, yacc, hst = rest
    nchunk = seq_len // CH
    triu = _tri(CH, False)
    rr = lax.broadcasted_iota(jnp.int32, (CH, CH), 0)
    ss = lax.broadcasted_iota(jnp.int32, (CH, CH), 1)
    masks = (ss <= rr, ss >= rr)

    def conv(v, lo_col):
        row = lax.broadcasted_iota(jnp.int32, v.shape, 0)
        prev = jnp.where(row == 0, 0.0, pltpu.roll(v, 1, 0))
        nxt = jnp.where(row == seq_len - 1, 0.0, pltpu.roll(v, seq_len - 1, 0))
        cs = slice(lo_col, lo_col + v.shape[1])
        return _silu(cw_ref[0:1, cs] * prev + cw_ref[1:2, cs] * v + cw_ref[2:3, cs] * nxt + cw_ref[3:4, cs])

    for sq in range(nsq):
        base = sq * seq_len
        seq_rows = slice(base, base + seq_len)
        for k in range(MIX_W // LANES):
            xs_s[seq_rows, k * LANES:(k + 1) * LANES] = conv(x_ref[k, seq_rows, :], k * LANES)
        for k in range(2):
            bcs_s[seq_rows, k * LANES:(k + 1) * LANES] = conv(bc_ref[k, seq_rows, :], MIX_W + k * LANES)
        if not is_ctx:
            hst[sq] = h0_ref[sq]
            if nchunk > 1:
                yacc[seq_rows, :] = jnp.zeros((seq_len, MIX_W), F32)

    def prep(rows):
        return _ssd_prep(gt_ref, bias_ref, nega_ref, rows, triu)

    def main(p, rows, dirs, sq):
        _ssd_main(p, xs_s, bcs_s, sel_ref, rows, dirs, hst.at[sq], yacc, masks, not is_ctx, nchunk == 1)

    if nchunk == 1:
        preps = [prep(_chunk_rows(0, sq * seq_len)) for sq in range(nsq)]
        for sq in range(nsq):
            main(preps[sq], _chunk_rows(0, sq * seq_len), (0, 1), sq)
    else:
        def body(ci, carry):
            work = [(sq, d, _chunk_rows(ci if d == 0 else nchunk - 1 - ci, sq * seq_len))
                    for sq in range(nsq) for d in range(2)]
            ps = [prep(rows) for _, _, rows in work]
            for (sq, d, rows), p in zip(work, ps):
                main(p, rows, (d,), sq)
            return carry
        lax.fori_loop(0, nchunk, body, 0)

    for sq in range(nsq):
        base = sq * seq_len
        seq_rows = slice(base, base + seq_len)
        z = jnp.concatenate([z_ref[k, seq_rows, :] for k in range(MIX_W // LANES)], axis=1).astype(F32)
        y = (yacc[seq_rows, :] + dskip_ref[...] * xs_s[seq_rows, :]) * _silu(z)
        y = (y * lax.rsqrt(jnp.mean(y * y, -1, keepdims=True) + RMS_EPS) * ng_ref[...]).astype(y_ref.dtype)
        for k in range(MIX_W // LANES):
            y_ref[k, seq_rows, :] = y[:, k * LANES:(k + 1) * LANES]
        if is_ctx:
            hfin_ref[sq] = hst[sq]


def _ssd_call(ub, uf, cw, bias_row, nega_rows, dskip, ng, sel, states, *, is_ctx, seq_len, row0, nseq, nsq, layer,
              depth):
    rows = nsq * seq_len
    bo = row0 // rows
    ns = MIX_W // LANES
    row_spec = lambda w: pl.BlockSpec((1, w), lambda i: (0, 0))
    in_specs = [pl.BlockSpec((ns, rows, LANES), lambda i: (2, bo + i, 0)),
                pl.BlockSpec((2, rows, LANES), lambda i: (6, bo + i, 0)),
                pl.BlockSpec((ns, rows, LANES), lambda i: (7, bo + i, 0)),
                pl.BlockSpec((None, rows, LANES), lambda i: (GATE_SLAB, bo + i, 0)),
                pl.BlockSpec((8, MB_XBC), lambda i: (0, 0)),
                row_spec(LANES),
                pl.BlockSpec((2 * MB_HEADS, CH), lambda i: (0, 0)),
                row_spec(MIX_W), row_spec(MIX_W),
                pl.BlockSpec((N_SEL, LANES, LANES), lambda i: (0, 0, 0))]
    st_spec, st_shape = _state_io((2, N_PAIR, LANES, LANES), nseq, nsq, layer, depth)
    return _seq_call(
        _ssd_kernel, "ssd", in_specs, [uf, uf, ub, uf, cw, bias_row, nega_rows, dskip, ng, sel], ns,
        [st_spec], [st_shape], states,
        [pltpu.VMEM((rows, MIX_W), F32), pltpu.VMEM((rows, 2 * LANES), F32), pltpu.VMEM((rows, MIX_W), F32),
         pltpu.VMEM((nsq, 2, N_PAIR, LANES, LANES), F32)],
        is_ctx, seq_len=seq_len, nseq=nseq, nsq=nsq)


def _ssd_state_to_pairs(s):
    bsz = s.shape[0]
    st = jnp.swapaxes(s, -1, -2).reshape(bsz, 2, N_PAIR, 2, MB_DSTATE, MB_HEADDIM)
    st = jnp.moveaxis(st, 3, 4).reshape(bsz, 2, N_PAIR, MB_DSTATE, 2 * MB_HEADDIM)
    zero = jnp.zeros_like(st)
    grp = (jnp.arange(N_PAIR) // PAIRS_PER_GROUP).reshape(1, 1, N_PAIR, 1, 1)
    return jnp.concatenate([jnp.where(grp == 0, st, zero), jnp.where(grp == 1, st, zero)], axis=3)


def _ssd_pairs_to_state(hp):
    bsz = hp.shape[0]
    halves = hp.reshape(bsz, 2, N_PAIR, MB_GROUPS, MB_DSTATE, 2 * MB_HEADDIM)
    grp = (jnp.arange(N_PAIR) // PAIRS_PER_GROUP).reshape(1, 1, N_PAIR, 1, 1)
    st = jnp.where(grp == 0, halves[:, :, :, 0], halves[:, :, :, 1])
    st = st.reshape(bsz, 2, N_PAIR, MB_DSTATE, 2, MB_HEADDIM)
    st = jnp.moveaxis(st, 4, 3).reshape(bsz, 2, MB_HEADS, MB_DSTATE, MB_HEADDIM)
    return jnp.swapaxes(st, -1, -2)


def _grid_pos_embed(n_tok, d_model):
    rows = n_tok // GRID_W
    r, cidx = jnp.meshgrid(jnp.arange(rows, dtype=F32), jnp.arange(GRID_W, dtype=F32), indexing='ij')
    quarter = d_model // 4
    freq = jnp.exp(-math.log(10000.0) * jnp.arange(quarter, dtype=F32) / quarter)
    ar = r.reshape(-1, 1) * freq
    ac = cidx.reshape(-1, 1) * freq
    return jnp.concatenate([jnp.sin(ar), jnp.cos(ar), jnp.sin(ac), jnp.cos(ac)], axis=-1)


def _gate_row(pieces):
    v = jnp.concatenate([p.reshape(-1).astype(F32) for p in pieces])
    return jnp.concatenate([v, v, v, jnp.zeros((LANES - 3 * N_SEL,), F32)]).reshape(1, LANES)


def kernel(x_prompt, x_sample, state_hgrn, state_mlstm_C, state_mlstm_n, state_mlstm_m, state_ssd, c, c_ctx,
           w_mod, b_mod, ln_g, ln_b, ffn_w_gu, ffn_w_down, w_in, hg_lb, hg_norm_g, ml_gate_b, ml_norm_g,
           mb_conv_w, mb_conv_b, mb_dt_bias, mb_a_log, mb_d, mb_norm_g, w_branch, w_out):
    bsz, seq, d = x_prompt.shape
    dbsz, dseq, _ = x_sample.shape
    depth = w_mod.shape[0]
    t_ctx = bsz * seq
    n_ctx_tiles = t_ctx // ROW_TILE
    assert dseq == ROW_TILE and t_ctx % ROW_TILE == 0 and seq == CH and dseq % CH == 0

    xs0 = x_sample + _grid_pos_embed(dseq, d).astype(x_sample.dtype)[None]
    x = jnp.concatenate([x_prompt.reshape(t_ctx, d), xs0.reshape(dbsz * dseq, d)], axis=0)

    cv = jnp.concatenate([c_ctx[None]] * MOD_LAT_ROW + [c, jnp.zeros((8 - MOD_LAT_ROW - dbsz, d), F32)], axis=0)
    mod4 = _mod_call(cv, w_mod, b_mod).reshape(depth, 8, 1, N_MOD * d)

    w_branch_b = w_branch.astype(BF16)
    w_out_b = w_out.astype(BF16)

    w_a, w_b = _regroup_call(w_in)
    ln_g4 = ln_g.reshape(depth, 3, 1, d)
    ln_b4 = ln_b.reshape(depth, 3, 1, d)

    lbs = jnp.cumsum(jax.nn.softmax(hg_lb.astype(F32), axis=0), axis=0)
    lbs = (lbs - lbs[0]).reshape(depth, 2 * HG_HEADS, HG_DK)
    lbp = jnp.stack([lbs, 1.0 - lbs], axis=1)
    cw = jnp.concatenate([mb_conv_w, mb_conv_b[:, None, :], jnp.zeros((depth, 4, MB_XBC), F32)], axis=1)
    sel = _selectors()

    lat_h0 = _ssd_state_to_pairs(state_ssd.reshape((dbsz * depth,) + state_ssd.shape[2:])).reshape(
        (dbsz, depth, 2, N_PAIR, LANES, LANES))
    lat_n0 = state_mlstm_n.reshape(dbsz, depth, 2 * ML_HEADS, ML_DK)
    lat_m0 = jnp.broadcast_to(state_mlstm_m.reshape(dbsz, depth, 2 * ML_HEADS, 1), (dbsz, depth, 2 * ML_HEADS, LANES))

    zeros = lambda *s: jnp.zeros((bsz, depth) + s, F32)
    st_hg = [zeros(2, HG_HEADS, HG_DK, LANES)]
    st_ml = [zeros(2, ML_HEADS, ML_DK, LANES), zeros(2 * ML_HEADS, LANES), zeros(2 * ML_HEADS, LANES)]
    st_ss = [zeros(2, N_PAIR, LANES, LANES)]
    t_idx = jnp.arange(CH).reshape(1, CH, 1)
    widths = (HALF >> jnp.arange(N_LEVEL + 1)).reshape(N_LEVEL + 1, 1, 1)
    sgn = jnp.broadcast_to(jnp.where((t_idx & widths) != 0, 1.0, -1.0), (N_LEVEL + 1, CH, LANES)).astype(F32)
    ctx = dict(is_ctx=True, seq_len=seq, row0=0, nseq=bsz, depth=depth)
    lat = dict(is_ctx=False, seq_len=dseq, row0=t_ctx, nseq=dbsz, depth=depth)
    for l in range(depth):
        x = _ffn_call(x, mod4, ffn_w_gu, ffn_w_down, ln_g4, ln_b4, l, 0, n_ctx_tiles)
        ub = _inproj_call(x, mod4, w_a, l, INPROJ_TILE_A, BF16, n_ctx_tiles, "inproj_a")
        uf = _inproj_call(x, mod4, w_b, l, INPROJ_TILE_B, F32, n_ctx_tiles, "inproj_b")

        ng_h = hg_norm_g[l].reshape(HG_HEADS, LANES)
        y_hg, *st_hg = _hgrn_call(ub, uf, lbp[l], ng_h, sgn, st_hg, layer=l, nsq=CTX_SEQS_PER_STEP, **ctx)
        y_hg_lat, = _hgrn_call(ub, uf, lbp[l], ng_h, sgn, [state_hgrn], layer=l, nsq=1, **lat)

        gate_bias = _gate_row([ml_gate_b[l, 0], ml_gate_b[l, 1], mb_dt_bias[l]])
        ng_m = ml_norm_g[l].reshape(ML_HEADS, LANES)
        y_ml, *st_ml = _mlstm_call(ub, uf, gate_bias, ng_m, sel, st_ml, layer=l, nsq=CTX_SEQS_PER_STEP_SCALAR, **ctx)
        y_ml_lat, = _mlstm_call(ub, uf, gate_bias, ng_m, sel, [state_mlstm_C, lat_n0, lat_m0], layer=l, nsq=LAT_SEQS_PER_STEP_SCALAR, **lat)

        nega = jnp.broadcast_to(-jnp.exp(mb_a_log[l].astype(F32)).reshape(2 * MB_HEADS, 1), (2 * MB_HEADS, CH))
        dskip = jnp.repeat(mb_d[l], MB_HEADDIM).reshape(1, MIX_W)
        ng_s = mb_norm_g[l].reshape(1, MIX_W)
        y_mb, *st_ss = _ssd_call(ub, uf, cw[l], gate_bias, nega, dskip, ng_s, sel, st_ss, layer=l, nsq=CTX_SEQS_PER_STEP_SCALAR, **ctx)
        y_mb_lat, = _ssd_call(ub, uf, cw[l], gate_bias, nega, dskip, ng_s, sel, [lat_h0], layer=l, nsq=LAT_SEQS_PER_STEP_SCALAR, **lat)

        x = _mixout_call(x, mod4, (y_hg, y_ml, y_mb), (y_hg_lat, y_ml_lat, y_mb_lat), ub, w_branch_b, w_out_b,
                         ln_g4, ln_b4, l, n_ctx_tiles)
        x = _ffn_call(x, mod4, ffn_w_gu, ffn_w_down, ln_g4, ln_b4, l, 1, n_ctx_tiles)

    y_prompt = x[:t_ctx].reshape(bsz, seq, d)
    y_sample = x[t_ctx:].reshape(dbsz, dseq, d)
    c_fin, n_fin, m_fin = st_ml
    h_fin = _ssd_pairs_to_state(st_ss[0].reshape((bsz * depth,) + st_ss[0].shape[2:]))
    return (y_prompt, y_sample, st_hg[0], c_fin, n_fin.reshape(bsz, depth, 2, ML_HEADS, ML_DK),
            m_fin[:, :, :, 0].reshape(bsz, depth, 2, ML_HEADS),
            h_fin.reshape((bsz, depth) + h_fin.shape[1:]))
```
